```python
import math
import jax, jax.numpy as jnp
from jax import lax
import numpy as np

D_MODEL = 1024
BATCH = 8
SEQ = 2048
DEPTH = 2
DEC_BATCH = 8
DEC_SEQ = 16
PAST_LEN = 1024

CHUNK = 64
Q_BLOCK = 128
N_EVEN = (DEPTH + 1) // 2
N_ODD = DEPTH // 2
FOX_HEADS = 8
FOX_HD = 64
RET_HEADS = 4
RET_DK = 64
RET_DV = 128
ROPE_BASE = 10000.0
CONV_W = 31
CONV_CH = D_MODEL
N_GROUPS = 4
EXP_PER_GROUP = 8
TOP_K = 2
D_EXPERT = 256
EPS = 1e-6

FOX_W = FOX_HEADS * FOX_HD
RET_QK_W = RET_HEADS * RET_DK
RET_V_W = RET_HEADS * RET_DV
MIX_IN = 3 * FOX_W + FOX_HEADS + 2 * RET_QK_W + 2 * RET_V_W
MIX_OUT = FOX_W + RET_V_W

kernel_name = "hybrid_stream_encoder_step"

F32 = jnp.float32


def rms_norm(x, g):
    xf = x.astype(F32)
    y = xf * lax.rsqrt(jnp.mean(xf * xf, axis=-1, keepdims=True) + EPS)
    return (y * g.astype(F32)).astype(x.dtype)


def layer_norm(x, g, b):
    xf = x.astype(F32)
    mu = jnp.mean(xf, axis=-1, keepdims=True)
    var = jnp.mean(jnp.square(xf - mu), axis=-1, keepdims=True)
    y = (xf - mu) * lax.rsqrt(var + EPS) * g.astype(F32) + b.astype(F32)
    return y.astype(x.dtype)


def rotary(x, pos):
    half = x.shape[-1] // 2
    inv_freq = ROPE_BASE ** (-jnp.arange(half, dtype=F32) / half)
    ang = pos.astype(F32)[:, None] * inv_freq[None, :]
    cos = jnp.cos(ang)[None, :, None, :]
    sin = jnp.sin(ang)[None, :, None, :]
    xf = x.astype(F32)
    x1, x2 = xf[..., :half], xf[..., half:]
    return jnp.concatenate([x1 * cos - x2 * sin, x1 * sin + x2 * cos], axis=-1).astype(x.dtype)


def retention_log_gamma():
    return jnp.log(1.0 - 2.0 ** (-5.0 - jnp.arange(RET_HEADS, dtype=F32)))


def even_projections(h, w_in, b_f, q_gain, k_gain, pos):
    B, T, _ = h.shape
    proj = jnp.einsum('btd,de->bte', h, w_in)
    sizes = [FOX_W, FOX_W, FOX_W, FOX_HEADS, RET_QK_W, RET_QK_W, RET_V_W, RET_V_W]
    cuts = [int(c) for c in np.cumsum(sizes)[:-1]]
    fq, fk, fv, ff, rq, rk, rv, rg = jnp.split(proj, cuts, axis=-1)
    fq = rms_norm(fq.reshape(B, T, FOX_HEADS, FOX_HD), q_gain)
    fk = rms_norm(fk.reshape(B, T, FOX_HEADS, FOX_HD), k_gain)
    fv = fv.reshape(B, T, FOX_HEADS, FOX_HD)
    logf = jax.nn.log_sigmoid(ff.astype(F32) + b_f.astype(F32))
    rq = rotary(rq.reshape(B, T, RET_HEADS, RET_DK), pos)
    rk = rotary(rk.reshape(B, T, RET_HEADS, RET_DK), pos) * (RET_DK ** -0.5)
    rv = rv.reshape(B, T, RET_HEADS, RET_DV)
    return fq, fk, fv, logf, rq, rk, rv, rg


def fox_attend(q, k, v, cq, ck, qpos, kpos):
    s = jnp.einsum('bqhd,bkhd->bhqk', q.astype(F32), k.astype(F32)) * (q.shape[-1] ** -0.5)
    s = s + cq[..., :, None] - ck[..., None, :]
    s = jnp.where(kpos[None, :] <= qpos[:, None], s, -jnp.inf)
    p = jax.nn.softmax(s, axis=-1)
    return jnp.einsum('bhqk,bkhd->bqhd', p, v.astype(F32)).astype(q.dtype)


def fox_prompt(q, k, v, logf):
    B, T, H, d = q.shape
    nb = T // Q_BLOCK
    cT = jnp.cumsum(logf, axis=1).transpose(0, 2, 1)
    qb = q.reshape(B, nb, Q_BLOCK, H, d).transpose(1, 0, 2, 3, 4)
    cqb = cT.reshape(B, H, nb, Q_BLOCK).transpose(2, 0, 1, 3)
    kpos = jnp.arange(T)

    def one_block(args):
        qi, cqi, bi = args
        qpos = bi * Q_BLOCK + jnp.arange(Q_BLOCK)
        return fox_attend(qi, k, v, cqi, cT, qpos, kpos)

    out = lax.map(one_block, (qb, cqb, jnp.arange(nb)))
    return out.transpose(1, 0, 2, 3, 4).reshape(B, T, H, d)


def fox_sample(q, k_all, v_all, logf_all):
    L = q.shape[1]
    P = k_all.shape[1] - L
    cT = jnp.cumsum(logf_all, axis=1).transpose(0, 2, 1)
    qpos = P + jnp.arange(L)
    kpos = jnp.arange(P + L)
    return fox_attend(q, k_all, v_all, cT[:, :, P:], cT, qpos, kpos)


def retention_block(q, k, v, s_prev, log_gamma):
    L = q.shape[1]
    idx = jnp.arange(L, dtype=F32)
    diff = idx[:, None] - idx[None, :]
    decay = jnp.where(diff >= 0, jnp.exp(log_gamma[:, None, None] * jnp.maximum(diff, 0.0)), 0.0)
    qf, kf, vf = q.astype(F32), k.astype(F32), v.astype(F32)
    scores = jnp.einsum('blhd,bmhd->bhlm', qf, kf) * decay[None]
    inner = jnp.einsum('bhlm,bmhe->blhe', scores, vf)
    cross = jnp.einsum('blhd,bhde->blhe', qf, s_prev) * jnp.exp(log_gamma[None, :] * (idx[:, None] + 1.0))[None, :, :, None]
    k_dec = kf * jnp.exp(log_gamma[None, :] * (L - 1.0 - idx)[:, None])[None, :, :, None]
    s_new = jnp.exp(log_gamma * L)[None, :, None, None] * s_prev + jnp.einsum('blhd,blhe->bhde', k_dec, vf)
    return inner + cross, s_new


def retention_prompt(q, k, v, log_gamma):
    B, T, H, dk = q.shape
    dv = v.shape[-1]
    nc = T // CHUNK

    def to_chunks(a):
        return a.reshape(B, nc, CHUNK, *a.shape[2:]).swapaxes(0, 1)

    def step(s, inp):
        qc, kc, vc = inp
        o, s = retention_block(qc, kc, vc, s, log_gamma)
        return s, o

    s0 = jnp.zeros((B, H, dk, dv), F32)
    s_fin, o = lax.scan(step, s0, (to_chunks(q), to_chunks(k), to_chunks(v)))
    return o.swapaxes(0, 1).reshape(B, T, H, dv), s_fin


def ret_head_norm(y, g):
    mu = jnp.mean(y, axis=-1, keepdims=True)
    var = jnp.mean(jnp.square(y - mu), axis=-1, keepdims=True)
    return (y - mu) * lax.rsqrt(var + EPS) * g.reshape(RET_HEADS, RET_DV).astype(F32)


def even_output(o_fox, y_ret, rg, gn_gain, w_out):
    B, T = o_fox.shape[:2]
    o_ret = jax.nn.silu(rg.astype(F32)) * ret_head_norm(y_ret, gn_gain).reshape(B, T, RET_V_W)
    o = jnp.concatenate([o_fox.reshape(B, T, FOX_W), o_ret.astype(o_fox.dtype)], axis=-1)
    return jnp.einsum('bte,ed->btd', o, w_out)


def conv_glu(h, w_pw1, b_pw1):
    a, g = jnp.split(jnp.einsum('btd,dc->btc', h, w_pw1) + b_pw1, 2, axis=-1)
    return a * jax.nn.sigmoid(g)


def conv_rest(u_ext, w_dw, b_dw, ln_g, ln_b, w_pw2):
    y = lax.conv_general_dilated(u_ext, w_dw[:, None, :], window_strides=(1,), padding='VALID',
                                 dimension_numbers=('NWC', 'WIO', 'NWC'),
                                 feature_group_count=CONV_CH) + b_dw
    y = jax.nn.silu(layer_norm(y, ln_g, ln_b))
    return jnp.einsum('btc,cd->btd', y, w_pw2)


def hier_moe(h, w_grp, b_grp, w_rexp, b_rexp, w_gate, w_up, w_down):
    B, T, D = h.shape
    x = h.reshape(B * T, D)
    g_logits = jnp.einsum('nd,dg->ng', x, w_grp).astype(F32) + b_grp.astype(F32)
    grp = jnp.argmax(g_logits, axis=-1)
    p_grp = jnp.take_along_axis(jax.nn.softmax(g_logits, axis=-1), grp[:, None], axis=-1)
    e_logits = (jnp.einsum('nd,de->ne', x, w_rexp).astype(F32) + b_rexp.astype(F32)).reshape(B * T, N_GROUPS, EXP_PER_GROUP)
    e_sel = jnp.take_along_axis(e_logits, grp[:, None, None], axis=1)[:, 0]
    top_v, top_i = lax.top_k(e_sel, TOP_K)
    top_w = jax.nn.softmax(top_v, axis=-1) * p_grp
    w_in_grp = jnp.einsum('nk,nke->ne', top_w, jax.nn.one_hot(top_i, EXP_PER_GROUP, dtype=F32))
    gates = (jax.nn.one_hot(grp, N_GROUPS, dtype=F32)[:, :, None] * w_in_grp[:, None, :]).astype(x.dtype)
    out = jnp.zeros((B * T, D), F32)
    for g in range(N_GROUPS):
        a = jnp.einsum('nd,edf->nef', x, w_gate[g])
        u = jnp.einsum('nd,edf->nef', x, w_up[g])
        hid = jax.nn.silu(a) * u * gates[:, g, :, None]
        out = out + jnp.einsum('nef,efd->nd', hid, w_down[g]).astype(F32)
    return out.astype(h.dtype).reshape(B, T, D)


def setup_inputs(seed: int = 0) -> dict:
    key = jax.random.key(seed)
    ks = list(jax.random.split(key, 32))

    def nrm(i, shape, scale):
        return scale * jax.random.normal(ks[i], shape, F32)

    return {
        'x_prompt': nrm(0, (BATCH, SEQ, D_MODEL), 1.0),
        'x_sample': nrm(1, (DEC_BATCH, DEC_SEQ, D_MODEL), 1.0),
        'cache_fox_k': nrm(2, (N_EVEN, DEC_BATCH, PAST_LEN, FOX_HEADS, FOX_HD), 1.0),
        'cache_fox_v': nrm(3, (N_EVEN, DEC_BATCH, PAST_LEN, FOX_HEADS, FOX_HD), 1.0),
        'cache_fox_logf': jax.nn.log_sigmoid(2.0 + nrm(4, (N_EVEN, DEC_BATCH, PAST_LEN, FOX_HEADS), 1.0)),
        'state_ret': nrm(5, (N_EVEN, DEC_BATCH, RET_HEADS, RET_DK, RET_DV), 0.5),
        'cache_conv': nrm(6, (N_ODD, DEC_BATCH, CONV_W - 1, CONV_CH), 0.5),
        'norm_mix': 1.0 + nrm(7, (DEPTH, D_MODEL), 0.02),
        'norm_ffn': 1.0 + nrm(8, (DEPTH, D_MODEL), 0.02),
        'w_in_mix': nrm(9, (N_EVEN, D_MODEL, MIX_IN), D_MODEL ** -0.5),
        'b_forget': 2.0 + nrm(10, (N_EVEN, FOX_HEADS), 0.5),
        'fox_q_gain': 1.0 + nrm(11, (N_EVEN, FOX_HD), 0.02),
        'fox_k_gain': 1.0 + nrm(12, (N_EVEN, FOX_HD), 0.02),
        'ret_gn_gain': 1.0 + nrm(13, (N_EVEN, RET_V_W), 0.02),
        'w_out_mix': nrm(14, (N_EVEN, MIX_OUT, D_MODEL), MIX_OUT ** -0.5),
        'w_pw1': nrm(15, (N_ODD, D_MODEL, 2 * CONV_CH), D_MODEL ** -0.5),
        'b_pw1': nrm(16, (N_ODD, 2 * CONV_CH), 0.02),
        'w_dw': nrm(17, (N_ODD, CONV_W, CONV_CH), CONV_W ** -0.5),
        'b_dw': nrm(18, (N_ODD, CONV_CH), 0.02),
        'conv_ln_g': 1.0 + nrm(19, (N_ODD, CONV_CH), 0.02),
        'conv_ln_b': nrm(20, (N_ODD, CONV_CH), 0.02),
        'w_pw2': nrm(21, (N_ODD, CONV_CH, D_MODEL), CONV_CH ** -0.5),
        'w_router_group': nrm(22, (DEPTH, D_MODEL, N_GROUPS), D_MODEL ** -0.5),
        'b_router_group': nrm(23, (DEPTH, N_GROUPS), 0.01),
        'w_router_expert': nrm(24, (DEPTH, D_MODEL, N_GROUPS * EXP_PER_GROUP), D_MODEL ** -0.5),
        'b_router_expert': nrm(25, (DEPTH, N_GROUPS * EXP_PER_GROUP), 0.01),
        'w_exp_gate': nrm(26, (DEPTH, N_GROUPS, EXP_PER_GROUP, D_MODEL, D_EXPERT), D_MODEL ** -0.5),
        'w_exp_up': nrm(27, (DEPTH, N_GROUPS, EXP_PER_GROUP, D_MODEL, D_EXPERT), D_MODEL ** -0.5),
        'w_exp_down': nrm(28, (DEPTH, N_GROUPS, EXP_PER_GROUP, D_EXPERT, D_MODEL), D_EXPERT ** -0.5),
    }


def reference(x_prompt, x_sample, cache_fox_k, cache_fox_v, cache_fox_logf, state_ret, cache_conv,
              norm_mix, norm_ffn, w_in_mix, b_forget, fox_q_gain, fox_k_gain, ret_gn_gain, w_out_mix,
              w_pw1, b_pw1, w_dw, b_dw, conv_ln_g, conv_ln_b, w_pw2,
              w_router_group, b_router_group, w_router_expert, b_router_expert,
              w_exp_gate, w_exp_up, w_exp_down):
    xp, xs = x_prompt, x_sample
    T = xp.shape[1]
    L = xs.shape[1]
    P = cache_fox_k.shape[2]
    pos_p = jnp.arange(T)
    pos_s = P + jnp.arange(L)
    log_gamma = retention_log_gamma()

    fk_p, fv_p, lf_p, rs_p, cv_p = [], [], [], [], []
    fk_s, fv_s, lf_s, rs_s, cv_s = [], [], [], [], []

    for l in range(DEPTH):
        i = l // 2
        hp = rms_norm(xp, norm_mix[l])
        hs = rms_norm(xs, norm_mix[l])
        if l % 2 == 0:
            fq, fk, fv, logf, rq, rk, rv, rg = even_projections(hp, w_in_mix[i], b_forget[i], fox_q_gain[i], fox_k_gain[i], pos_p)
            o_fox = fox_prompt(fq, fk, fv, logf)
            y_ret, s_fin = retention_prompt(rq, rk, rv, log_gamma)
            xp = xp + even_output(o_fox, y_ret, rg, ret_gn_gain[i], w_out_mix[i])
            fk_p.append(fk); fv_p.append(fv); lf_p.append(logf); rs_p.append(s_fin)
            sq, sk, sv, slogf, srq, srk, srv, srg = even_projections(hs, w_in_mix[i], b_forget[i], fox_q_gain[i], fox_k_gain[i], pos_s)
            k_all = jnp.concatenate([cache_fox_k[i].astype(sk.dtype), sk], axis=1)
            v_all = jnp.concatenate([cache_fox_v[i].astype(sv.dtype), sv], axis=1)
            lf_all = jnp.concatenate([cache_fox_logf[i].astype(F32), slogf], axis=1)
            o_fox_s = fox_sample(sq, k_all, v_all, lf_all)
            y_ret_s, s_new = retention_block(srq, srk, srv, state_ret[i].astype(F32), log_gamma)
            xs = xs + even_output(o_fox_s, y_ret_s, srg, ret_gn_gain[i], w_out_mix[i])
            fk_s.append(sk); fv_s.append(sv); lf_s.append(slogf); rs_s.append(s_new)
        else:
            up = conv_glu(hp, w_pw1[i], b_pw1[i])
            up_ext = jnp.pad(up, ((0, 0), (CONV_W - 1, 0), (0, 0)))
            xp = xp + conv_rest(up_ext, w_dw[i], b_dw[i], conv_ln_g[i], conv_ln_b[i], w_pw2[i])
            cv_p.append(up[:, T - (CONV_W - 1):])
            us = conv_glu(hs, w_pw1[i], b_pw1[i])
            us_ext = jnp.concatenate([cache_conv[i].astype(us.dtype), us], axis=1)
            xs = xs + conv_rest(us_ext, w_dw[i], b_dw[i], conv_ln_g[i], conv_ln_b[i], w_pw2[i])
            cv_s.append(us_ext[:, L:])
        xp = xp + hier_moe(rms_norm(xp, norm_ffn[l]), w_router_group[l], b_router_group[l], w_router_expert[l],
                           b_router_expert[l], w_exp_gate[l], w_exp_up[l], w_exp_down[l])
        xs = xs + hier_moe(rms_norm(xs, norm_ffn[l]), w_router_group[l], b_router_group[l], w_router_expert[l],
                           b_router_expert[l], w_exp_gate[l], w_exp_up[l], w_exp_down[l])

    fox_k_prompt = jnp.stack(fk_p)
    fox_v_prompt = jnp.stack(fv_p)
    fox_logf_prompt = jnp.stack(lf_p)
    ret_state_prompt = jnp.stack(rs_p)
    conv_prompt = jnp.stack(cv_p)
    fox_k_sample = jnp.stack(fk_s)
    fox_v_sample = jnp.stack(fv_s)
    fox_logf_sample = jnp.stack(lf_s)
    ret_state_sample = jnp.stack(rs_s)
    conv_sample = jnp.stack(cv_s)
    return (xp, xs, fox_k_prompt, fox_v_prompt, fox_logf_prompt, ret_state_prompt, conv_prompt,
            fox_k_sample, fox_v_sample, fox_logf_sample, ret_state_sample, conv_sample)
```

```python
import functools
import math

import jax
import jax.numpy as jnp
import numpy as np
from jax import lax
from jax.experimental import pallas as pl
from jax.experimental.pallas import tpu as pltpu

F32 = jnp.float32
BF16 = jnp.bfloat16

D_MODEL = 1024
FOX_HEADS = 8
FOX_HD = 64
RET_HEADS = 4
RET_DK = 64
RET_DV = 128
ROPE_BASE = 10000.0
CONV_W = 31
N_GROUPS = 4
EXP_PER_GROUP = 8
N_EXPERTS = N_GROUPS * EXP_PER_GROUP
D_EXPERT = 256
EPS = 1e-6
FOX_W = FOX_HEADS * FOX_HD
RET_QK_W = RET_HEADS * RET_DK
RET_V_W = RET_HEADS * RET_DV

LANES = 128
HIST_ROWS = 32
ROUTER_LANE0 = N_GROUPS
MASKED = -1e30
VMEM_LIMIT = 56 * 1024 * 1024

C_FQ, C_FK, C_FV = 0, FOX_W, 2 * FOX_W
C_RQ = 3 * FOX_W
C_RK = C_RQ + RET_QK_W
C_RV = C_RK + RET_QK_W
C_RG = C_RV + RET_V_W
C_FF = C_RG + RET_V_W
MIX_COLS = C_FF + LANES


def _cparams(*sem):
    return pltpu.CompilerParams(dimension_semantics=sem, vmem_limit_bytes=VMEM_LIMIT)


def _full(shape):
    n = len(shape)
    return pl.BlockSpec(shape, lambda *_: (0,) * n)


def _rms(x, g):
    ms = jnp.mean(x * x, axis=-1, keepdims=True)
    return x * lax.rsqrt(ms + EPS) * g


def _sigmoid(x):
    return 1.0 / (1.0 + jnp.exp(-x))


def _dot(a, b):
    return jnp.dot(a, b, preferred_element_type=F32)


def _dot_nt(a, b):
    return lax.dot_general(a, b, (((1,), (1,)), ((), ())), preferred_element_type=F32)


def _dot_tn(a, b):
    return lax.dot_general(a, b, (((0,), (0,)), ((), ())), preferred_element_type=F32)


def _proj0_kernel(x_ref, g_ref, w_ref, bf_ref, qg_ref, kg_ref, gbd_ref, cos_ref, sin_ref,
                  q_ref, k_ref, v_ref, lf_ref, rq_ref, rk_ref, rv_ref, rg_ref):
    h = _rms(x_ref[...], g_ref[...]).astype(BF16)

    def seg(a, b):
        return _dot(h, w_ref[:, a:b])

    gbd = gbd_ref[...]

    def head_rms(y, gain):
        ms = _dot((y * y).astype(BF16), gbd)
        return y * lax.rsqrt(ms + EPS) * gain

    q_ref[...] = (head_rms(seg(C_FQ, C_FK), qg_ref[...]) * (FOX_HD ** -0.5)).astype(BF16)
    k_ref[...] = head_rms(seg(C_FK, C_FV), kg_ref[...])
    v_ref[...] = seg(C_FV, C_RQ)

    z = seg(C_FF, MIX_COLS) + bf_ref[...]
    logf = jnp.minimum(z, 0.0) - jnp.log(1.0 + jnp.exp(-jnp.abs(z)))
    lane = lax.broadcasted_iota(jnp.int32, (1, LANES), 1)
    lf_ref[...] = jnp.where(lane < FOX_HEADS, logf, 0.0)

    cos = cos_ref[...]
    sin = sin_ref[...]
    first_half = (lane % RET_DK) < (RET_DK // 2)

    def rotary(y):
        outs = []
        for s in range(y.shape[1] // LANES):
            ys = y[:, s * LANES:(s + 1) * LANES]
            rot = jnp.where(first_half, pltpu.roll(ys, LANES - RET_DK // 2, 1),
                            pltpu.roll(ys, RET_DK // 2, 1))
            outs.append(ys * cos + rot * sin)
        return jnp.concatenate(outs, axis=1)

    rq_ref[...] = rotary(seg(C_RQ, C_RK)).astype(BF16)
    rk_ref[...] = (rotary(seg(C_RK, C_RV)) * (RET_DK ** -0.5)).astype(BF16)
    rv_ref[...] = seg(C_RV, C_RG).astype(BF16)
    rg_ref[...] = seg(C_RG, C_FF).astype(BF16)


def _proj0(x, g, w, bf, qg, kg, gbd, cos, sin, tm):
    n = x.shape[0]
    nper = cos.shape[0] // tm
    row = lambda width: pl.BlockSpec((tm, width), lambda i: (i, 0))
    tab = pl.BlockSpec((tm, LANES), lambda i: (i % nper, 0))
    outs = [(FOX_W, BF16), (FOX_W, F32), (FOX_W, F32), (LANES, F32),
            (RET_QK_W, BF16), (RET_QK_W, BF16), (RET_V_W, BF16), (RET_V_W, BF16)]
    return pl.pallas_call(
        _proj0_kernel,
        grid=(n // tm,),
        in_specs=[row(D_MODEL), _full((1, D_MODEL)), _full((D_MODEL, MIX_COLS)), _full((1, LANES)),
                  _full((1, FOX_W)), _full((1, FOX_W)), _full((FOX_W, FOX_W)), tab, tab],
        out_specs=[row(wd) for wd, _ in outs],
        out_shape=[jax.ShapeDtypeStruct((n, wd), dt) for wd, dt in outs],
        compiler_params=_cparams("parallel"),
        name="proj0",
    )(x, g, w, bf, qg, kg, gbd, cos, sin)


def _cum_kernel(lf_ref, ccol_ref, crow_ref, *, t, cb):
    r = lax.broadcasted_iota(jnp.int32, (cb, cb), 0)
    c = lax.broadcasted_iota(jnp.int32, (cb, cb), 1)
    tri = jnp.where(r >= c, 1.0, 0.0).astype(BF16)
    carry = jnp.zeros((1, LANES), F32)
    for blk in range(t // cb):
        a = lf_ref[0, blk * cb:(blk + 1) * cb, :]
        a1 = a.astype(BF16)
        r1 = a - a1.astype(F32)
        a2 = r1.astype(BF16)
        a3 = (r1 - a2.astype(F32)).astype(BF16)
        cc = (_dot(tri, a1) + _dot(tri, a2)) + _dot(tri, a3) + carry
        ccol_ref[0, blk * cb:(blk + 1) * cb, :] = cc
        crow_ref[0, :, blk * cb:(blk + 1) * cb] = cc.T[0:FOX_HEADS, :]
        carry = cc[cb - 1:cb, :]


def _cumsum_logf(lf, cb):
    b, t, _ = lf.shape
    return pl.pallas_call(
        functools.partial(_cum_kernel, t=t, cb=cb),
        grid=(b,),
        in_specs=[pl.BlockSpec((1, t, LANES), lambda i: (i, 0, 0))],
        out_specs=[pl.BlockSpec((1, t, LANES), lambda i: (i, 0, 0)),
                   pl.BlockSpec((1, FOX_HEADS, t), lambda i: (i, 0, 0))],
        out_shape=[jax.ShapeDtypeStruct((b, t, LANES), F32),
                   jax.ShapeDtypeStruct((b, FOX_HEADS, t), F32)],
        compiler_params=_cparams("parallel"),
        name="cumsum_logf",
    )(lf)


def _fox_kernel(q_ref, k_ref, v_ref, cq_ref, ck_ref, o_ref, *, tq, tk, q_off):
    hp = pl.program_id(1)
    i = pl.program_id(2)
    nch = (q_off + (i + 1) * tq + tk - 1) // tk
    lane = lax.broadcasted_iota(jnp.int32, (1, LANES), 1)
    q = q_ref[0]
    cq_all = cq_ref[0]
    qpos = q_off + i * tq + lax.broadcasted_iota(jnp.int32, (tq, 1), 0)
    kiota = lax.broadcasted_iota(jnp.int32, (1, tk), 1)
    outs = []
    for h2 in range(2):
        head = 2 * hp + h2
        qm = jnp.where((lane // FOX_HD) == h2, q, jnp.zeros_like(q))
        cq = jnp.sum(jnp.where(lane == head, cq_all, 0.0), axis=-1, keepdims=True)

        def body(j, carry, head=head, qm=qm, cq=cq):
            m, l, acc = carry
            start = pl.multiple_of(j * tk, tk)
            kj = k_ref[0, pl.ds(start, tk), :].astype(BF16)
            vj = v_ref[0, pl.ds(start, tk), :].astype(BF16)
            s = _dot_nt(qm, kj)
            s = s + cq - ck_ref[0, head, j]
            s = jnp.where(j * tk + kiota <= qpos, s, MASKED)
            m_new = jnp.maximum(m, jnp.max(s, axis=-1, keepdims=True))
            alpha = jnp.exp(m - m_new)
            p = jnp.exp(s - m_new)
            l = alpha * l + jnp.sum(p, axis=-1, keepdims=True)
            acc = alpha * acc + _dot(p.astype(BF16), vj)
            return m_new, l, acc

        init = (jnp.full((tq, 1), MASKED, F32), jnp.zeros((tq, 1), F32), jnp.zeros((tq, LANES), F32))
        _, l, acc = lax.fori_loop(0, nch, body, init)
        outs.append(acc / l)
    o_ref[0] = jnp.where(lane < FOX_HD, outs[0], outs[1]).astype(BF16)


def _fox_attention(q, k, v, ccol, crow, tq, tk, q_off):
    b, tqs, _ = q.shape
    tks = k.shape[1]
    nck = tks // tk
    crow5 = crow.reshape(b, FOX_HEADS, nck, 1, tk)
    qb0 = q_off // tq
    return pl.pallas_call(
        functools.partial(_fox_kernel, tq=tq, tk=tk, q_off=q_off),
        grid=(b, FOX_HEADS // 2, tqs // tq),
        in_specs=[pl.BlockSpec((1, tq, LANES), lambda bi, hp, i: (bi, i, hp)),
                  pl.BlockSpec((1, tks, LANES), lambda bi, hp, i: (bi, 0, hp)),
                  pl.BlockSpec((1, tks, LANES), lambda bi, hp, i: (bi, 0, hp)),
                  pl.BlockSpec((1, tq, LANES), lambda bi, hp, i: (bi, qb0 + i, 0)),
                  pl.BlockSpec((1, FOX_HEADS, nck, 1, tk), lambda bi, hp, i: (bi, 0, 0, 0, 0))],
        out_specs=pl.BlockSpec((1, tq, LANES), lambda bi, hp, i: (bi, i, hp)),
        out_shape=jax.ShapeDtypeStruct((b, tqs, FOX_W), BF16),
        compiler_params=_cparams("parallel", "parallel", "parallel"),
        name="fox_attention",
    )(q, k, v, ccol, crow5)


def _ret_kernel(lg_ref, rq_ref, rk_ref, rv_ref, rg_ref, s0_ref, gn_ref, o_ref, sfin_ref, s_scr, *, c):
    h = pl.program_id(1)
    ci = pl.program_id(2)
    odd = (h % 2) == 1

    @pl.when(ci == 0)
    def _():
        s0 = s0_ref[0, 0]
        s_scr[0:RET_DK, :] = jnp.where(odd, 0.0, s0)
        s_scr[RET_DK:2 * RET_DK, :] = jnp.where(odd, s0, 0.0)

    lg = lg_ref[0][:, 0:1]
    lane = lax.broadcasted_iota(jnp.int32, (1, LANES), 1)
    inhead = (lane // RET_DK) == (h % 2)
    q = rq_ref[0]
    k = rk_ref[0]
    qm = jnp.where(inhead, q, jnp.zeros_like(q))
    km = jnp.where(inhead, k, jnp.zeros_like(k))
    v = rv_ref[0]
    ii = lax.broadcasted_iota(jnp.int32, (c, 1), 0).astype(F32)
    jj = lax.broadcasted_iota(jnp.int32, (1, c), 1).astype(F32)
    diff = ii - jj
    decay = jnp.where(diff >= 0.0, jnp.exp(lg * jnp.maximum(diff, 0.0)), 0.0)
    scores = _dot_nt(qm, km) * decay
    inner = _dot(scores.astype(BF16), v)
    s_prev = s_scr[...]
    cross = _dot(qm, s_prev.astype(BF16)) * jnp.exp(lg * (ii + 1.0))
    y = inner + cross
    k_dec = (km.astype(F32) * jnp.exp(lg * (c - 1.0 - ii))).astype(BF16)
    s_new = jnp.exp(lg * float(c)) * s_prev + _dot_tn(k_dec, v)
    s_scr[...] = s_new

    mu = jnp.mean(y, axis=-1, keepdims=True)
    yc = y - mu
    var = jnp.mean(yc * yc, axis=-1, keepdims=True)
    yn = yc * lax.rsqrt(var + EPS) * gn_ref[...]
    g = rg_ref[0].astype(F32)
    o_ref[0] = ((g * _sigmoid(g)) * yn).astype(BF16)

    @pl.when(ci == pl.num_programs(2) - 1)
    def _():
        sfin_ref[0, 0] = jnp.where(odd, s_new[RET_DK:2 * RET_DK, :], s_new[0:RET_DK, :])


def _retention(lg_tab, rq, rk, rv, rg, s0, gn, c):
    b, t, _ = rq.shape
    qk_spec = pl.BlockSpec((1, c, LANES), lambda bi, h, ci: (bi, ci, h // 2))
    v_spec = pl.BlockSpec((1, c, LANES), lambda bi, h, ci: (bi, ci, h))
    st_spec = pl.BlockSpec((1, 1, RET_DK, RET_DV), lambda bi, h, ci: (bi, h, 0, 0))
    return pl.pallas_call(
        functools.partial(_ret_kernel, c=c),
        grid=(b, RET_HEADS, t // c),
        in_specs=[pl.BlockSpec((1, 1, LANES), lambda bi, h, ci: (h, 0, 0)),
                  qk_spec, qk_spec, v_spec, v_spec, st_spec,
                  pl.BlockSpec((1, LANES), lambda bi, h, ci: (0, h))],
        out_specs=[v_spec, st_spec],
        out_shape=[jax.ShapeDtypeStruct((b, t, RET_V_W), BF16),
                   jax.ShapeDtypeStruct((b, RET_HEADS, RET_DK, RET_DV), F32)],
        scratch_shapes=[pltpu.VMEM((2 * RET_DK, RET_DV), F32)],
        compiler_params=_cparams("parallel", "parallel", "arbitrary"),
        name="retention",
    )(lg_tab, rq, rk, rv, rg, s0, gn)


def _out0_kernel(x_ref, of_ref, or_ref, w_ref, y_ref):
    mix = _dot(of_ref[...], w_ref[0:FOX_W, :]) + _dot(or_ref[...], w_ref[FOX_W:FOX_W + RET_V_W, :])
    y_ref[...] = x_ref[...] + mix


def _out0(x, o_fox, o_ret, w, tm):
    n = x.shape[0]
    row = lambda width: pl.BlockSpec((tm, width), lambda i: (i, 0))
    return pl.pallas_call(
        _out0_kernel,
        grid=(n // tm,),
        in_specs=[row(D_MODEL), row(FOX_W), row(RET_V_W), _full((FOX_W + RET_V_W, D_MODEL))],
        out_specs=row(D_MODEL),
        out_shape=jax.ShapeDtypeStruct((n, D_MODEL), F32),
        compiler_params=_cparams("parallel"),
        name="out_proj0",
    )(x, o_fox, o_ret, w)


def _router_kernel(x_ref, g_ref, wh_ref, wl_ref, b_ref, h_ref, gates_ref):
    hf = _rms(x_ref[...], g_ref[...])
    hh = hf.astype(BF16)
    hl = (hf - hh.astype(F32)).astype(BF16)
    h_ref[...] = hh
    wh = wh_ref[...]
    logits = _dot(hh, wh) + (_dot(hl, wh) + _dot(hh, wl_ref[...])) + b_ref[...]

    lane = lax.broadcasted_iota(jnp.int32, (1, LANES), 1)
    lanef = lane.astype(F32)
    ninf = -jnp.inf
    is_grp = lane < N_GROUPS
    gl = jnp.where(is_grp, logits, ninf)
    gmax = jnp.max(gl, axis=-1, keepdims=True)
    grp = jnp.min(jnp.where(gl == gmax, lanef, 1e9), axis=-1, keepdims=True)
    p_grp = 1.0 / jnp.sum(jnp.exp(gl - gmax), axis=-1, keepdims=True)

    is_exp = (lane >= ROUTER_LANE0) & (lane < ROUTER_LANE0 + N_EXPERTS)
    lane_grp = ((lane - ROUTER_LANE0) // EXP_PER_GROUP).astype(F32)
    em = jnp.where(is_exp & (lane_grp == grp), logits, ninf)
    v1 = jnp.max(em, axis=-1, keepdims=True)
    i1 = jnp.min(jnp.where(em == v1, lanef, 1e9), axis=-1, keepdims=True)
    em2 = jnp.where(lanef == i1, ninf, em)
    v2 = jnp.max(em2, axis=-1, keepdims=True)
    i2 = jnp.min(jnp.where(em2 == v2, lanef, 1e9), axis=-1, keepdims=True)
    t = jnp.exp(v2 - v1)
    w1 = (1.0 / (1.0 + t)) * p_grp
    w2 = (t / (1.0 + t)) * p_grp
    gates_ref[...] = jnp.where(lanef == i1, w1, 0.0) + jnp.where(lanef == i2, w2, 0.0)


def _router(x, g, wh, wl, b, tm):
    n = x.shape[0]
    row = lambda width: pl.BlockSpec((tm, width), lambda i: (i, 0))
    return pl.pallas_call(
        _router_kernel,
        grid=(n // tm,),
        in_specs=[row(D_MODEL), _full((1, D_MODEL)), _full((D_MODEL, LANES)), _full((D_MODEL, LANES)),
                  _full((1, LANES))],
        out_specs=[row(D_MODEL), row(LANES)],
        out_shape=[jax.ShapeDtypeStruct((n, D_MODEL), BF16), jax.ShapeDtypeStruct((n, LANES), F32)],
        compiler_params=_cparams("parallel"),
        name="router",
    )(x, g, wh, wl, b)


def _moe_kernel(h_ref, gates_ref, x_ref, wg_ref, wu_ref, wd_ref, y_ref):
    e = pl.program_id(1)

    @pl.when(e == 0)
    def _():
        y_ref[...] = jnp.zeros_like(y_ref)

    h = h_ref[...]
    a = _dot(h, wg_ref[0])
    u = _dot(h, wu_ref[0])
    lane = lax.broadcasted_iota(jnp.int32, (1, LANES), 1)
    gate = jnp.sum(jnp.where(lane == e + ROUTER_LANE0, gates_ref[...], 0.0), axis=-1, keepdims=True)
    hid = (a * _sigmoid(a)) * u * gate
    y_ref[...] += _dot(hid.astype(BF16), wd_ref[0])

    @pl.when(e == pl.num_programs(1) - 1)
    def _():
        y_ref[...] += x_ref[...]


def _moe(h, gates, x, wg, wu, wd, tm):
    n = x.shape[0]
    row = lambda width: pl.BlockSpec((tm, width), lambda i, e: (i, 0))
    return pl.pallas_call(
        _moe_kernel,
        grid=(n // tm, N_EXPERTS),
        in_specs=[row(D_MODEL), row(LANES), row(D_MODEL),
                  pl.BlockSpec((1, D_MODEL, D_EXPERT), lambda i, e: (e, 0, 0)),
                  pl.BlockSpec((1, D_MODEL, D_EXPERT), lambda i, e: (e, 0, 0)),
                  pl.BlockSpec((1, D_EXPERT, D_MODEL), lambda i, e: (e, 0, 0))],
        out_specs=row(D_MODEL),
        out_shape=jax.ShapeDtypeStruct((n, D_MODEL), F32),
        compiler_params=_cparams("parallel", "arbitrary"),
        name="moe_experts",
    )(h, gates, x, wg, wu, wd)


def _pw1_kernel(x_ref, g_ref, w_ref, b_ref, u_ref):
    h = _rms(x_ref[...], g_ref[...]).astype(BF16)
    a = _dot(h, w_ref[:, 0:D_MODEL]) + b_ref[:, 0:D_MODEL]
    g = _dot(h, w_ref[:, D_MODEL:2 * D_MODEL]) + b_ref[:, D_MODEL:2 * D_MODEL]
    u_ref[...] = a * _sigmoid(g)


def _pw1(x, g, w, b, tm):
    n = x.shape[0]
    row = lambda width: pl.BlockSpec((tm, width), lambda i: (i, 0))
    return pl.pallas_call(
        _pw1_kernel,
        grid=(n // tm,),
        in_specs=[row(D_MODEL), _full((1, D_MODEL)), _full((D_MODEL, 2 * D_MODEL)), _full((1, 2 * D_MODEL))],
        out_specs=row(D_MODEL),
        out_shape=jax.ShapeDtypeStruct((n, D_MODEL), F32),
        compiler_params=_cparams("parallel"),
        name="conv_pw1_glu",
    )(x, g, w, b)


def _conv_kernel(u_ref, hist_ref, x_ref, wdw_ref, bdw_ref, lng_ref, lnb_ref, w2_ref, y_ref, ext_scr,
                 *, tt, rs):
    ti = pl.program_id(1)

    @pl.when(ti == 0)
    def _():
        ext_scr[0:HIST_ROWS, :] = hist_ref[0]

    ext_scr[HIST_ROWS:HIST_ROWS + tt, :] = u_ref[0]
    pad = HIST_ROWS - (CONV_W - 1)
    parts = []
    for r0 in range(0, tt, rs):
        acc = jnp.zeros((rs, D_MODEL), F32)
        for kk in range(CONV_W):
            acc = acc + wdw_ref[kk:kk + 1, :] * ext_scr[r0 + kk + pad:r0 + kk + pad + rs, :]
        parts.append(acc)
    y = jnp.concatenate(parts, axis=0) + bdw_ref[...]
    mu = jnp.mean(y, axis=-1, keepdims=True)
    yc = y - mu
    var = jnp.mean(yc * yc, axis=-1, keepdims=True)
    yn = yc * lax.rsqrt(var + EPS) * lng_ref[...] + lnb_ref[...]
    z = yn * _sigmoid(yn)
    y_ref[0] = x_ref[0] + _dot(z.astype(BF16), w2_ref[...])
    if tt >= HIST_ROWS:
        ext_scr[0:HIST_ROWS, :] = ext_scr[tt:tt + HIST_ROWS, :]


def _conv_module(u, hist, x, wdw, bdw, lng, lnb, w2, tt):
    b, t, _ = u.shape
    rs = min(tt, 32)
    blk = pl.BlockSpec((1, tt, D_MODEL), lambda bi, ti: (bi, ti, 0))
    return pl.pallas_call(
        functools.partial(_conv_kernel, tt=tt, rs=rs),
        grid=(b, t // tt),
        in_specs=[blk, pl.BlockSpec((1, HIST_ROWS, D_MODEL), lambda bi, ti: (bi, 0, 0)), blk,
                  _full((HIST_ROWS, D_MODEL)), _full((1, D_MODEL)), _full((1, D_MODEL)),
                  _full((1, D_MODEL)), _full((D_MODEL, D_MODEL))],
        out_specs=blk,
        out_shape=jax.ShapeDtypeStruct((b, t, D_MODEL), F32),
        scratch_shapes=[pltpu.VMEM((HIST_ROWS + tt, D_MODEL), F32)],
        compiler_params=_cparams("parallel", "arbitrary"),
        name="conv_module",
    )(u, hist, x, wdw, bdw, lng, lnb, w2)


def _rope_tables(pos):
    half = RET_DK // 2
    inv_freq = ROPE_BASE ** (-jnp.arange(half, dtype=F32) / half)
    ang = pos.astype(F32)[:, None] * inv_freq[None, :]
    cos, sin = jnp.cos(ang), jnp.sin(ang)
    reps = LANES // RET_DK
    cos_t = jnp.tile(jnp.concatenate([cos, cos], axis=1), (1, reps))
    sin_t = jnp.tile(jnp.concatenate([-sin, sin], axis=1), (1, reps))
    return cos_t, sin_t


def _pad_lanes(v, width=LANES):
    v = v.reshape(1, -1)
    return jnp.pad(v, ((0, 0), (0, width - v.shape[1])))


def _moe_layer(x, p, l, tm):
    h, gates = _router(x, p["norm_ffn"][l], p["wr_hi"][l], p["wr_lo"][l], p["br"][l], tm)
    return _moe(h, gates, x, p["wg"][l], p["wu"][l], p["wd"][l], tm)


def _run_group(x, p, *, seq, tm, fox_hist, ret_state, conv_hist, pos, tq, tk, ret_c, conv_tt):
    b = x.shape[0] // seq
    q_off = 0 if fox_hist is None else fox_hist[0].shape[1]

    cos_t, sin_t = _rope_tables(pos)
    q, k, v, lf, rq, rk, rv, rg = _proj0(x, p["norm_mix"][0], p["w_in"], p["b_f"], p["q_gain"], p["k_gain"],
                                          p["gbd"], cos_t, sin_t, tm)
    k3 = k.reshape(b, seq, FOX_W)
    v3 = v.reshape(b, seq, FOX_W)
    lf3 = lf.reshape(b, seq, LANES)
    if fox_hist is None:
        k_all, v_all, lf_all = k3, v3, lf3
    else:
        ck_, cv_, clf_ = fox_hist
        tot = q_off + seq
        padded = -(-tot // tk) * tk
        tail = padded - tot
        k_all = jnp.concatenate([ck_, k3, jnp.zeros((b, tail, FOX_W), F32)], axis=1)
        v_all = jnp.concatenate([cv_, v3, jnp.zeros((b, tail, FOX_W), F32)], axis=1)
        clf_ = jnp.pad(clf_, ((0, 0), (0, 0), (0, LANES - FOX_HEADS)))
        lf_all = jnp.concatenate([clf_, lf3, jnp.zeros((b, tail, LANES), F32)], axis=1)
    ccol, crow = _cumsum_logf(lf_all, tk)
    o_fox = _fox_attention(q.reshape(b, seq, FOX_W), k_all, v_all, ccol, crow, tq, tk, q_off)
    o_ret, s_fin = _retention(p["lg_tab"], rq.reshape(b, seq, RET_QK_W), rk.reshape(b, seq, RET_QK_W),
                              rv.reshape(b, seq, RET_V_W), rg.reshape(b, seq, RET_V_W), ret_state,
                              p["gn_gain"], ret_c)
    x = _out0(x, o_fox.reshape(-1, FOX_W), o_ret.reshape(-1, RET_V_W), p["w_out"], tm)
    x = _moe_layer(x, p, 0, tm)

    u = _pw1(x, p["norm_mix"][1], p["w_pw1"], p["b_pw1"], tm)
    u3 = u.reshape(b, seq, D_MODEL)
    x = _conv_module(u3, conv_hist, x.reshape(b, seq, D_MODEL), p["w_dw"], p["b_dw"], p["ln_g"], p["ln_b"],
                     p["w_pw2"], conv_tt).reshape(-1, D_MODEL)
    x = _moe_layer(x, p, 1, tm)

    fox_k = k3.reshape(1, b, seq, FOX_HEADS, FOX_HD)
    fox_v = v3.reshape(1, b, seq, FOX_HEADS, FOX_HD)
    fox_lf = lf3[:, :, :FOX_HEADS].reshape(1, b, seq, FOX_HEADS)
    return x.reshape(b, seq, D_MODEL), fox_k, fox_v, fox_lf, s_fin[None], u3


def kernel(x_prompt, x_sample, cache_fox_k, cache_fox_v, cache_fox_logf, state_ret, cache_conv, norm_mix, norm_ffn, w_in_mix, b_forget, fox_q_gain, fox_k_gain, ret_gn_gain, w_out_mix, w_pw1, b_pw1, w_dw, b_dw, conv_ln_g, conv_ln_b, w_pw2, w_router_group, b_router_group, w_router_expert, b_router_expert, w_exp_gate, w_exp_up, w_exp_down):
    bp, t, d = x_prompt.shape
    bs, l, _ = x_sample.shape
    past = cache_fox_k.shape[2]
    depth = norm_mix.shape[0]
    assert d == D_MODEL and depth == 2 and w_in_mix.shape[0] == 1 and w_pw1.shape[0] == 1

    w_in = w_in_mix[0]
    n_pre = 3 * FOX_W
    w_in_r = jnp.concatenate(
        [w_in[:, :n_pre], w_in[:, n_pre + FOX_HEADS:], w_in[:, n_pre:n_pre + FOX_HEADS],
         jnp.zeros((D_MODEL, LANES - FOX_HEADS), F32)], axis=1).astype(BF16)
    hid = jnp.arange(FOX_W) // FOX_HD
    gbd = jnp.where(hid[:, None] == hid[None, :], 1.0 / FOX_HD, 0.0).astype(BF16)
    w_r = jnp.concatenate([w_router_group, w_router_expert], axis=-1)
    w_r = jnp.pad(w_r, ((0, 0), (0, 0), (0, LANES - w_r.shape[-1])))
    wr_hi = w_r.astype(BF16)
    wr_lo = (w_r - wr_hi.astype(F32)).astype(BF16)
    b_r = jnp.concatenate([b_router_group, b_router_expert], axis=-1)
    b_r = jnp.pad(b_r, ((0, 0), (0, LANES - b_r.shape[-1])))
    log_gamma = jnp.log(1.0 - 2.0 ** (-5.0 - jnp.arange(RET_HEADS, dtype=F32)))
    p = {
        "norm_mix": norm_mix.reshape(depth, 1, D_MODEL),
        "norm_ffn": norm_ffn.reshape(depth, 1, D_MODEL),
        "w_in": w_in_r,
        "b_f": _pad_lanes(b_forget[0]),
        "q_gain": jnp.tile(fox_q_gain[0], FOX_HEADS).reshape(1, FOX_W),
        "k_gain": jnp.tile(fox_k_gain[0], FOX_HEADS).reshape(1, FOX_W),
        "gbd": gbd,
        "gn_gain": ret_gn_gain[0].reshape(1, RET_V_W),
        "lg_tab": jnp.broadcast_to(log_gamma[:, None, None], (RET_HEADS, 1, LANES)),
        "w_out": w_out_mix[0].astype(BF16),
        "w_pw1": w_pw1[0].astype(BF16),
        "b_pw1": b_pw1[0].reshape(1, -1),
        "w_dw": jnp.pad(w_dw[0], ((0, HIST_ROWS - CONV_W), (0, 0))),
        "b_dw": b_dw[0].reshape(1, -1),
        "ln_g": conv_ln_g[0].reshape(1, -1),
        "ln_b": conv_ln_b[0].reshape(1, -1),
        "w_pw2": w_pw2[0].astype(BF16),
        "wr_hi": wr_hi,
        "wr_lo": wr_lo,
        "br": b_r.reshape(depth, 1, LANES),
        "wg": w_exp_gate.reshape(depth, N_EXPERTS, D_MODEL, D_EXPERT).astype(BF16),
        "wu": w_exp_up.reshape(depth, N_EXPERTS, D_MODEL, D_EXPERT).astype(BF16),
        "wd": w_exp_down.reshape(depth, N_EXPERTS, D_EXPERT, D_MODEL).astype(BF16),
    }

    hist_pad = HIST_ROWS - (CONV_W - 1)
    yp, fk_p, fv_p, lf_p, rs_p, u_p = _run_group(
        x_prompt.reshape(bp * t, D_MODEL), p, seq=t, tm=512, fox_hist=None,
        ret_state=jnp.zeros((bp, RET_HEADS, RET_DK, RET_DV), F32),
        conv_hist=jnp.zeros((bp, HIST_ROWS, D_MODEL), F32),
        pos=jnp.arange(t), tq=256, tk=256, ret_c=256, conv_tt=256)

    ns = bs * l
    ys, fk_s, fv_s, lf_s, rs_s, u_s = _run_group(
        x_sample.reshape(ns, D_MODEL), p, seq=l, tm=ns,
        fox_hist=(cache_fox_k[0].reshape(bs, past, FOX_W), cache_fox_v[0].reshape(bs, past, FOX_W),
                  cache_fox_logf[0]),
        ret_state=state_ret[0],
        conv_hist=jnp.pad(cache_conv[0], ((0, 0), (hist_pad, 0), (0, 0))),
        pos=past + (jnp.arange(ns) % l), tq=l, tk=256, ret_c=l, conv_tt=l)

    conv_p = u_p[:, t - (CONV_W - 1):][None]
    conv_s = jnp.concatenate([cache_conv[0], u_s], axis=1)[:, l:][None]
    return (yp, ys, fk_p, fv_p, lf_p, rs_p, conv_p, fk_s, fv_s, lf_s, rs_s, conv_s)
```

```python
import functools
import math

import jax
import jax.numpy as jnp
import numpy as np
from jax import lax
from jax.experimental import pallas as pl
from jax.experimental.pallas import tpu as pltpu

F32 = jnp.float32
BF16 = jnp.bfloat16

D_MODEL = 1024
FOX_HEADS = 8
FOX_HD = 64
RET_HEADS = 4
RET_DK = 64
RET_DV = 128
ROPE_BASE = 10000.0
CONV_W = 31
N_GROUPS = 4
EXP_PER_GROUP = 8
N_EXPERTS = N_GROUPS * EXP_PER_GROUP
D_EXPERT = 256
EPS = 1e-6
FOX_W = FOX_HEADS * FOX_HD
RET_QK_W = RET_HEADS * RET_DK
RET_V_W = RET_HEADS * RET_DV

LANES = 128
HIST_ROWS = 32
ROUTER_LANE0 = N_GROUPS
MASKED = -1e30
VMEM_LIMIT = 56 * 1024 * 1024
CUM_BLOCK = 256
MOE_TM = 256
PLAN_TB = 512
ROUTE_ROWS = 48

C_FQ, C_FK, C_FV = 0, FOX_W, 2 * FOX_W
C_RQ = 3 * FOX_W
C_RK = C_RQ + RET_QK_W
C_RV = C_RK + RET_QK_W
C_RG = C_RV + RET_V_W
C_FF = C_RG + RET_V_W
MIX_COLS = C_FF + LANES


def _cparams(*sem):
    return pltpu.CompilerParams(dimension_semantics=sem, vmem_limit_bytes=VMEM_LIMIT)


def _full(shape):
    n = len(shape)
    return pl.BlockSpec(shape, lambda *_: (0,) * n)


def _rms(x, g):
    ms = jnp.mean(x * x, axis=-1, keepdims=True)
    return x * lax.rsqrt(ms + EPS) * g


def _sigmoid(x):
    return 1.0 / (1.0 + jnp.exp(-x))


def _dot(a, b):
    return jnp.dot(a, b, preferred_element_type=F32)


def _dot_nt(a, b):
    return lax.dot_general(a, b, (((1,), (1,)), ((), ())), preferred_element_type=F32)


def _dot_tn(a, b):
    return lax.dot_general(a, b, (((0,), (0,)), ((), ())), preferred_element_type=F32)


def _proj0_kernel(x_ref, g_ref, w_ref, bf_ref, qg_ref, kg_ref, gbd_ref, cos_ref, sin_ref,
                  q_ref, k_ref, v_ref, lf_ref, rq_ref, rk_ref, rv_ref, rg_ref):
    h = _rms(x_ref[...], g_ref[...]).astype(BF16)

    def seg(a, b):
        return _dot(h, w_ref[:, a:b])

    gbd = gbd_ref[...]

    def head_rms(y, gain):
        ms = _dot((y * y).astype(BF16), gbd)
        return y * lax.rsqrt(ms + EPS) * gain

    q_ref[...] = (head_rms(seg(C_FQ, C_FK), qg_ref[...]) * (FOX_HD ** -0.5)).astype(BF16)
    k_ref[...] = head_rms(seg(C_FK, C_FV), kg_ref[...])
    v_ref[...] = seg(C_FV, C_RQ)

    z = seg(C_FF, MIX_COLS) + bf_ref[...]
    logf = jnp.minimum(z, 0.0) - jnp.log(1.0 + jnp.exp(-jnp.abs(z)))
    lane = lax.broadcasted_iota(jnp.int32, (1, LANES), 1)
    lf_ref[...] = jnp.where(lane < FOX_HEADS, logf, 0.0)

    cos = cos_ref[...]
    sin = sin_ref[...]
    first_half = (lane % RET_DK) < (RET_DK // 2)

    def rotary(y):
        outs = []
        for s in range(y.shape[1] // LANES):
            ys = y[:, s * LANES:(s + 1) * LANES]
            rot = jnp.where(first_half, pltpu.roll(ys, LANES - RET_DK // 2, 1),
                            pltpu.roll(ys, RET_DK // 2, 1))
            outs.append(ys * cos + rot * sin)
        return jnp.concatenate(outs, axis=1)

    rq_ref[...] = rotary(seg(C_RQ, C_RK)).astype(BF16)
    rk_ref[...] = (rotary(seg(C_RK, C_RV)) * (RET_DK ** -0.5)).astype(BF16)
    rv_ref[...] = seg(C_RV, C_RG).astype(BF16)
    rg_ref[...] = seg(C_RG, C_FF).astype(BF16)


def _proj0(x, g, w, bf, qg, kg, gbd, cos, sin, tm):
    n = x.shape[0]
    nper = cos.shape[0] // tm
    row = lambda width: pl.BlockSpec((tm, width), lambda i: (i, 0))
    tab = pl.BlockSpec((tm, LANES), lambda i: (i % nper, 0))
    outs = [(FOX_W, BF16), (FOX_W, F32), (FOX_W, F32), (LANES, F32),
            (RET_QK_W, BF16), (RET_QK_W, BF16), (RET_V_W, BF16), (RET_V_W, BF16)]
    return pl.pallas_call(
        _proj0_kernel,
        grid=(n // tm,),
        in_specs=[row(D_MODEL), _full((1, D_MODEL)), _full((D_MODEL, MIX_COLS)), _full((1, LANES)),
                  _full((1, FOX_W)), _full((1, FOX_W)), _full((FOX_W, FOX_W)), tab, tab],
        out_specs=[row(wd) for wd, _ in outs],
        out_shape=[jax.ShapeDtypeStruct((n, wd), dt) for wd, dt in outs],
        compiler_params=_cparams("parallel"),
        name="proj0",
    )(x, g, w, bf, qg, kg, gbd, cos, sin)


def _cum_kernel(lf_ref, ccol_ref, crow_ref, *, t, cb):
    r = lax.broadcasted_iota(jnp.int32, (cb, cb), 0)
    c = lax.broadcasted_iota(jnp.int32, (cb, cb), 1)
    tri = jnp.where(r >= c, 1.0, 0.0).astype(BF16)
    carry = jnp.zeros((1, LANES), F32)
    for blk in range(t // cb):
        a = lf_ref[0, blk * cb:(blk + 1) * cb, :]
        a1 = a.astype(BF16)
        r1 = a - a1.astype(F32)
        a2 = r1.astype(BF16)
        a3 = (r1 - a2.astype(F32)).astype(BF16)
        cc = (_dot(tri, a1) + _dot(tri, a2)) + _dot(tri, a3) + carry
        ccol_ref[0, blk * cb:(blk + 1) * cb, :] = cc
        crow_ref[0, :, blk * cb:(blk + 1) * cb] = cc.T[0:FOX_HEADS, :]
        carry = cc[cb - 1:cb, :]


def _cumsum_logf(lf, cb):
    b, t, _ = lf.shape
    return pl.pallas_call(
        functools.partial(_cum_kernel, t=t, cb=cb),
        grid=(b,),
        in_specs=[pl.BlockSpec((1, t, LANES), lambda i: (i, 0, 0))],
        out_specs=[pl.BlockSpec((1, t, LANES), lambda i: (i, 0, 0)),
                   pl.BlockSpec((1, FOX_HEADS, t), lambda i: (i, 0, 0))],
        out_shape=[jax.ShapeDtypeStruct((b, t, LANES), F32),
                   jax.ShapeDtypeStruct((b, FOX_HEADS, t), F32)],
        compiler_params=_cparams("parallel"),
        name="cumsum_logf",
    )(lf)


def _fox_kernel(q_ref, k_ref, v_ref, cq_ref, ck_ref, o_ref, *, tq, tk, q_off):
    hp = pl.program_id(1)
    i = pl.program_id(2)
    q0 = q_off + i * tq
    nfull = (q0 + 1) // tk
    nch = (q0 + tq + tk - 1) // tk
    lane = lax.broadcasted_iota(jnp.int32, (1, LANES), 1)
    q = q_ref[0]
    cq_all = cq_ref[0]
    qpos = q0 + lax.broadcasted_iota(jnp.int32, (tq, 1), 0)
    kiota = lax.broadcasted_iota(jnp.int32, (1, tk), 1)
    qms = [jnp.where((lane // FOX_HD) == h2, q, jnp.zeros_like(q)) for h2 in range(2)]
    cqs = [jnp.sum(jnp.where(lane == 2 * hp + h2, cq_all, 0.0), axis=-1, keepdims=True) for h2 in range(2)]

    def step(j, carry, masked):
        start = pl.multiple_of(j * tk, tk)
        kj = k_ref[0, pl.ds(start, tk), :].astype(BF16)
        vj = v_ref[0, pl.ds(start, tk), :].astype(BF16)
        new = []
        for h2 in range(2):
            m, l, acc = carry[h2]
            s = _dot_nt(qms[h2], kj)
            s = s + cqs[h2] - ck_ref[0, 2 * hp + h2, j]
            if masked:
                s = jnp.where(j * tk + kiota <= qpos, s, MASKED)
            m_new = jnp.maximum(m, jnp.max(s, axis=-1, keepdims=True))
            alpha = jnp.exp(m - m_new)
            p = jnp.exp(s - m_new)
            l = alpha * l + jnp.sum(p, axis=-1, keepdims=True)
            acc = alpha * acc + _dot(p.astype(BF16), vj)
            new.append((m_new, l, acc))
        return tuple(new)

    one = (jnp.full((tq, 1), MASKED, F32), jnp.zeros((tq, 1), F32), jnp.zeros((tq, LANES), F32))
    carry = lax.fori_loop(0, nfull, functools.partial(step, masked=False), (one, one))
    carry = lax.fori_loop(nfull, nch, functools.partial(step, masked=True), carry)
    outs = [acc / l for _, l, acc in carry]
    o_ref[0] = jnp.where(lane < FOX_HD, outs[0], outs[1]).astype(BF16)


def _fox_attention(q, k, v, ccol, crow, tq, tk, q_off):
    b, tqs, _ = q.shape
    tks = k.shape[1]
    nck = tks // tk
    crow5 = crow.reshape(b, FOX_HEADS, nck, 1, tk)
    qb0 = q_off // tq
    return pl.pallas_call(
        functools.partial(_fox_kernel, tq=tq, tk=tk, q_off=q_off),
        grid=(b, FOX_HEADS // 2, tqs // tq),
        in_specs=[pl.BlockSpec((1, tq, LANES), lambda bi, hp, i: (bi, i, hp)),
                  pl.BlockSpec((1, tks, LANES), lambda bi, hp, i: (bi, 0, hp)),
                  pl.BlockSpec((1, tks, LANES), lambda bi, hp, i: (bi, 0, hp)),
                  pl.BlockSpec((1, tq, LANES), lambda bi, hp, i: (bi, qb0 + i, 0)),
                  pl.BlockSpec((1, FOX_HEADS, nck, 1, tk), lambda bi, hp, i: (bi, 0, 0, 0, 0))],
        out_specs=pl.BlockSpec((1, tq, LANES), lambda bi, hp, i: (bi, i, hp)),
        out_shape=jax.ShapeDtypeStruct((b, tqs, FOX_W), BF16),
        compiler_params=_cparams("parallel", "parallel", "parallel"),
        name="fox_attention",
    )(q, k, v, ccol, crow5)


def _ret_kernel(lg_ref, rq_ref, rk_ref, rv_ref, rg_ref, s0_ref, gn_ref, o_ref, sfin_ref, s_scr, d_scr, *, c):
    h = pl.program_id(1)
    ci = pl.program_id(2)
    odd = (h % 2) == 1

    lg = lg_ref[0][:, 0:1]
    ii = lax.broadcasted_iota(jnp.int32, (c, 1), 0).astype(F32)

    @pl.when(ci == 0)
    def _():
        s0 = s0_ref[0, 0]
        s_scr[0:RET_DK, :] = jnp.where(odd, 0.0, s0)
        s_scr[RET_DK:2 * RET_DK, :] = jnp.where(odd, s0, 0.0)
        jj = lax.broadcasted_iota(jnp.int32, (1, c), 1).astype(F32)
        diff = ii - jj
        d_scr[...] = jnp.where(diff >= 0.0, jnp.exp(lg * jnp.maximum(diff, 0.0)), 0.0)

    lane = lax.broadcasted_iota(jnp.int32, (1, LANES), 1)
    inhead = (lane // RET_DK) == (h % 2)
    q = rq_ref[0]
    k = rk_ref[0]
    qm = jnp.where(inhead, q, jnp.zeros_like(q))
    km = jnp.where(inhead, k, jnp.zeros_like(k))
    v = rv_ref[0]
    scores = _dot_nt(qm, km) * d_scr[...]
    inner = _dot(scores.astype(BF16), v)
    s_prev = s_scr[...]
    cross = _dot(qm, s_prev.astype(BF16)) * jnp.exp(lg * (ii + 1.0))
    y = inner + cross
    k_dec = (km.astype(F32) * jnp.exp(lg * (c - 1.0 - ii))).astype(BF16)
    s_new = jnp.exp(lg * float(c)) * s_prev + _dot_tn(k_dec, v)
    s_scr[...] = s_new

    mu = jnp.mean(y, axis=-1, keepdims=True)
    yc = y - mu
    var = jnp.mean(yc * yc, axis=-1, keepdims=True)
    yn = yc * lax.rsqrt(var + EPS) * gn_ref[...]
    g = rg_ref[0].astype(F32)
    o_ref[0] = ((g * _sigmoid(g)) * yn).astype(BF16)

    @pl.when(ci == pl.num_programs(2) - 1)
    def _():
        sfin_ref[0, 0] = jnp.where(odd, s_new[RET_DK:2 * RET_DK, :], s_new[0:RET_DK, :])


def _retention(lg_tab, rq, rk, rv, rg, s0, gn, c):
    b, t, _ = rq.shape
    qk_spec = pl.BlockSpec((1, c, LANES), lambda bi, h, ci: (bi, ci, h // 2))
    v_spec = pl.BlockSpec((1, c, LANES), lambda bi, h, ci: (bi, ci, h))
    st_spec = pl.BlockSpec((1, 1, RET_DK, RET_DV), lambda bi, h, ci: (bi, h, 0, 0))
    return pl.pallas_call(
        functools.partial(_ret_kernel, c=c),
        grid=(b, RET_HEADS, t // c),
        in_specs=[pl.BlockSpec((1, 1, LANES), lambda bi, h, ci: (h, 0, 0)),
                  qk_spec, qk_spec, v_spec, v_spec, st_spec,
                  pl.BlockSpec((1, LANES), lambda bi, h, ci: (0, h))],
        out_specs=[v_spec, st_spec],
        out_shape=[jax.ShapeDtypeStruct((b, t, RET_V_W), BF16),
                   jax.ShapeDtypeStruct((b, RET_HEADS, RET_DK, RET_DV), F32)],
        scratch_shapes=[pltpu.VMEM((2 * RET_DK, RET_DV), F32), pltpu.VMEM((c, c), F32)],
        compiler_params=_cparams("arbitrary", "arbitrary", "arbitrary"),
        name="retention",
    )(lg_tab, rq, rk, rv, rg, s0, gn)


def _out0_kernel(x_ref, of_ref, or_ref, w_ref, y_ref):
    mix = _dot(of_ref[...], w_ref[0:FOX_W, :]) + _dot(or_ref[...], w_ref[FOX_W:FOX_W + RET_V_W, :])
    y_ref[...] = x_ref[...] + mix


def _out0(x, o_fox, o_ret, w, tm):
    n = x.shape[0]
    row = lambda width: pl.BlockSpec((tm, width), lambda i: (i, 0))
    return pl.pallas_call(
        _out0_kernel,
        grid=(n // tm,),
        in_specs=[row(D_MODEL), row(FOX_W), row(RET_V_W), _full((FOX_W + RET_V_W, D_MODEL))],
        out_specs=row(D_MODEL),
        out_shape=jax.ShapeDtypeStruct((n, D_MODEL), F32),
        compiler_params=_cparams("parallel"),
        name="out_proj0",
    )(x, o_fox, o_ret, w)


def _router_kernel(x_ref, g_ref, wh_ref, wl_ref, b_ref, h_ref, gates_ref):
    hf = _rms(x_ref[...], g_ref[...])
    hh = hf.astype(BF16)
    hl = (hf - hh.astype(F32)).astype(BF16)
    h_ref[...] = hh
    wh = wh_ref[...]
    logits = _dot(hh, wh) + (_dot(hl, wh) + _dot(hh, wl_ref[...])) + b_ref[...]

    lane = lax.broadcasted_iota(jnp.int32, (1, LANES), 1)
    lanef = lane.astype(F32)
    ninf = -jnp.inf
    is_grp = lane < N_GROUPS
    gl = jnp.where(is_grp, logits, ninf)
    gmax = jnp.max(gl, axis=-1, keepdims=True)
    grp = jnp.min(jnp.where(gl == gmax, lanef, 1e9), axis=-1, keepdims=True)
    p_grp = 1.0 / jnp.sum(jnp.exp(gl - gmax), axis=-1, keepdims=True)

    is_exp = (lane >= ROUTER_LANE0) & (lane < ROUTER_LANE0 + N_EXPERTS)
    lane_grp = ((lane - ROUTER_LANE0) // EXP_PER_GROUP).astype(F32)
    em = jnp.where(is_exp & (lane_grp == grp), logits, ninf)
    v1 = jnp.max(em, axis=-1, keepdims=True)
    i1 = jnp.min(jnp.where(em == v1, lanef, 1e9), axis=-1, keepdims=True)
    em2 = jnp.where(lanef == i1, ninf, em)
    v2 = jnp.max(em2, axis=-1, keepdims=True)
    i2 = jnp.min(jnp.where(em2 == v2, lanef, 1e9), axis=-1, keepdims=True)
    t = jnp.exp(v2 - v1)
    w1 = (1.0 / (1.0 + t)) * p_grp
    w2 = (t / (1.0 + t)) * p_grp
    gates_ref[...] = jnp.where(lanef == i1, w1, 0.0) + jnp.where(lanef == i2, w2, 0.0)


def _router(x, g, wh, wl, b, tm):
    n = x.shape[0]
    row = lambda width: pl.BlockSpec((tm, width), lambda i: (i, 0))
    return pl.pallas_call(
        _router_kernel,
        grid=(n // tm,),
        in_specs=[row(D_MODEL), _full((1, D_MODEL)), _full((D_MODEL, LANES)), _full((D_MODEL, LANES)),
                  _full((1, LANES))],
        out_specs=[row(D_MODEL), row(LANES)],
        out_shape=[jax.ShapeDtypeStruct((n, D_MODEL), BF16), jax.ShapeDtypeStruct((n, LANES), F32)],
        compiler_params=_cparams("parallel"),
        name="router",
    )(x, g, wh, wl, b)


def _moe_kernel(h_ref, gates_ref, x_ref, wg_ref, wu_ref, wd_ref, y_ref):
    e = pl.program_id(1)

    @pl.when(e == 0)
    def _():
        y_ref[...] = jnp.zeros_like(y_ref)

    h = h_ref[...]
    a = _dot(h, wg_ref[0])
    u = _dot(h, wu_ref[0])
    lane = lax.broadcasted_iota(jnp.int32, (1, LANES), 1)
    gate = jnp.sum(jnp.where(lane == e + ROUTER_LANE0, gates_ref[...], 0.0), axis=-1, keepdims=True)
    hid = (a * _sigmoid(a)) * u * gate
    y_ref[...] += _dot(hid.astype(BF16), wd_ref[0])

    @pl.when(e == pl.num_programs(1) - 1)
    def _():
        y_ref[...] += x_ref[...]


def _moe(h, gates, x, wg, wu, wd, tm):
    n = x.shape[0]
    row = lambda width: pl.BlockSpec((tm, width), lambda i, e: (i, 0))
    return pl.pallas_call(
        _moe_kernel,
        grid=(n // tm, N_EXPERTS),
        in_specs=[row(D_MODEL), row(LANES), row(D_MODEL),
                  pl.BlockSpec((1, D_MODEL, D_EXPERT), lambda i, e: (e, 0, 0)),
                  pl.BlockSpec((1, D_MODEL, D_EXPERT), lambda i, e: (e, 0, 0)),
                  pl.BlockSpec((1, D_EXPERT, D_MODEL), lambda i, e: (e, 0, 0))],
        out_specs=row(D_MODEL),
        out_shape=jax.ShapeDtypeStruct((n, D_MODEL), F32),
        compiler_params=_cparams("parallel", "arbitrary"),
        name="moe_experts",
    )(h, gates, x, wg, wu, wd)


def _router_t_kernel(x_ref, g_ref, wth_ref, wtl_ref, bt_ref, h_ref, route_ref, wcol_ref):
    hf = _rms(x_ref[...], g_ref[...])
    h_ref[...] = hf
    hh = hf.astype(BF16)
    hl = (hf - hh.astype(F32)).astype(BF16)
    wth = wth_ref[...]
    logits = _dot_nt(wth, hh) + (_dot_nt(wth, hl) + _dot_nt(wtl_ref[...], hh)) + bt_ref[...]

    row = lax.broadcasted_iota(jnp.int32, (ROUTE_ROWS, 1), 0)
    rowf = row.astype(F32)
    ninf = -jnp.inf
    is_grp = (row >= N_EXPERTS) & (row < N_EXPERTS + N_GROUPS)
    gl = jnp.where(is_grp, logits, ninf)
    gmax = jnp.max(gl, axis=0, keepdims=True)
    grp = jnp.min(jnp.where(gl == gmax, rowf, 1e9), axis=0, keepdims=True) - float(N_EXPERTS)
    p_grp = 1.0 / jnp.sum(jnp.exp(gl - gmax), axis=0, keepdims=True)

    row_grp = (row // EXP_PER_GROUP).astype(F32)
    em = jnp.where((row < N_EXPERTS) & (row_grp == grp), logits, ninf)
    v1 = jnp.max(em, axis=0, keepdims=True)
    i1 = jnp.min(jnp.where(em == v1, rowf, 1e9), axis=0, keepdims=True)
    em2 = jnp.where(rowf == i1, ninf, em)
    v2 = jnp.max(em2, axis=0, keepdims=True)
    i2 = jnp.min(jnp.where(em2 == v2, rowf, 1e9), axis=0, keepdims=True)
    t = jnp.exp(v2 - v1)
    w1 = (1.0 / (1.0 + t)) * p_grp
    w2 = (t / (1.0 + t)) * p_grp

    tm = logits.shape[1]
    r128 = lax.broadcasted_iota(jnp.int32, (LANES, 1), 0)
    rt = (jnp.where(r128 == 0, i1, 0.0) + jnp.where(r128 == 1, i2, 0.0)
          + jnp.where(r128 == 2, w1, 0.0) + jnp.where(r128 == 3, w2, 0.0))
    route_ref[...] = rt[0:8, :]
    wcol_ref[...] = rt.T


def _router_t(x, g, wth, wtl, bt, tm):
    n = x.shape[0]
    row = lambda width: pl.BlockSpec((tm, width), lambda i: (i, 0))
    return pl.pallas_call(
        _router_t_kernel,
        grid=(n // tm,),
        in_specs=[row(D_MODEL), _full((1, D_MODEL)), _full((ROUTE_ROWS, D_MODEL)),
                  _full((ROUTE_ROWS, D_MODEL)), _full((ROUTE_ROWS, 1))],
        out_specs=[row(D_MODEL), pl.BlockSpec((8, tm), lambda i: (0, i)), row(LANES)],
        out_shape=[jax.ShapeDtypeStruct((n, D_MODEL), F32), jax.ShapeDtypeStruct((8, n), F32),
                   jax.ShapeDtypeStruct((n, LANES), F32)],
        compiler_params=_cparams("parallel"),
        name="router_t",
    )(x, g, wth, wtl, bt)


def _plan_kernel(route_ref, pos_ref, te_ref, nt_ref, *, n, tb, tm):
    nb = n // tb
    erow = lax.broadcasted_iota(jnp.int32, (N_EXPERTS, 1), 0).astype(F32)

    def picks(blk):
        i1 = route_ref[0:1, blk * tb:(blk + 1) * tb]
        i2 = route_ref[1:2, blk * tb:(blk + 1) * tb]
        return erow == i1, erow == i2

    acc = jnp.zeros((N_EXPERTS, tb), F32)
    for blk in range(nb):
        e1, e2 = picks(blk)
        acc = acc + jnp.where(e1, 1.0, 0.0) + jnp.where(e2, 1.0, 0.0)
    counts = jnp.sum(acc, axis=1, keepdims=True).astype(jnp.int32)
    assert tm & (tm - 1) == 0
    ntile = jnp.right_shift(counts + (tm - 1), tm.bit_length() - 1).astype(F32)
    lane = lax.broadcasted_iota(jnp.int32, (1, LANES), 1).astype(F32)
    ntile_row = jnp.sum(jnp.where(lane == erow, ntile, 0.0), axis=0, keepdims=True)
    tend = jnp.sum(jnp.where(lane <= erow, ntile_row, 0.0), axis=1, keepdims=True)
    off = (tend - ntile) * float(tm)
    ntot = jnp.sum(ntile, axis=0, keepdims=True)
    kk = jnp.minimum(lax.broadcasted_iota(jnp.int32, (1, 2 * LANES), 1).astype(F32), ntot - 1.0)
    te_ref[...] = jnp.sum(jnp.where(tend <= kk, 1.0, 0.0), axis=0, keepdims=True).astype(jnp.int32)
    nt_ref[...] = jnp.broadcast_to(ntot, (1, LANES)).astype(jnp.int32)

    r = lax.broadcasted_iota(jnp.int32, (tb, tb), 0)
    c = lax.broadcasted_iota(jnp.int32, (tb, tb), 1)
    before = jnp.where(r < c, 1.0, 0.0).astype(BF16)
    carry = jnp.zeros((N_EXPERTS, 1), F32)
    for blk in range(nb):
        e1, e2 = picks(blk)
        mt = jnp.where(e1, 1.0, 0.0) + jnp.where(e2, 1.0, 0.0)
        slot = _dot(mt.astype(BF16), before) + (carry + off)
        pos_ref[blk, 0:1, :] = jnp.sum(jnp.where(e1, slot, 0.0), axis=0, keepdims=True).astype(jnp.int32)
        pos_ref[blk, 1:2, :] = jnp.sum(jnp.where(e2, slot, 0.0), axis=0, keepdims=True).astype(jnp.int32)
        carry = carry + jnp.sum(mt, axis=1, keepdims=True)


def _plan(route, tb, tm):
    n = route.shape[1]
    vm = pl.BlockSpec(memory_space=pltpu.VMEM)
    return pl.pallas_call(
        functools.partial(_plan_kernel, n=n, tb=tb, tm=tm),
        in_specs=[vm],
        out_specs=[vm, vm, vm],
        out_shape=[jax.ShapeDtypeStruct((n // tb, 2, tb), jnp.int32),
                   jax.ShapeDtypeStruct((1, 2 * LANES), jnp.int32),
                   jax.ShapeDtypeStruct((1, LANES), jnp.int32)],
        compiler_params=pltpu.CompilerParams(vmem_limit_bytes=VMEM_LIMIT),
        name="moe_plan",
    )(route)


def _row_copy(src, src_row, dst, dst_row, sem):
    return pltpu.make_async_copy(src.at[pl.ds(src_row, 1)], dst.at[pl.ds(dst_row, 1)], sem)


DMA_UNROLL = 8


def _dispatch_kernel(pos_ref, h_hbm, xs_in, xs_hbm, sem, *, ts):
    del xs_in
    base = pl.program_id(0) * ts

    def issue(g, carry):
        for u in range(DMA_UNROLL):
            r = g * DMA_UNROLL + u
            for ch in range(2):
                _row_copy(h_hbm, base + r, xs_hbm, pos_ref[0, ch, r], sem).start(priority=ch)
        return carry

    def drain(g, carry):
        for _ in range(2 * DMA_UNROLL):
            _row_copy(h_hbm, 0, xs_hbm, 0, sem).wait()
        return carry

    lax.fori_loop(0, ts // DMA_UNROLL, issue, 0)
    lax.fori_loop(0, ts // DMA_UNROLL, drain, 0)


def _dispatch(pos, h, xs0, ts):
    nb = pos.shape[0]
    anyspec = pl.BlockSpec(memory_space=pl.ANY)
    return pl.pallas_call(
        functools.partial(_dispatch_kernel, ts=ts),
        grid=(nb,),
        in_specs=[pl.BlockSpec((1, 2, ts), lambda i: (i, 0, 0), memory_space=pltpu.SMEM), anyspec, anyspec],
        out_specs=anyspec,
        out_shape=jax.ShapeDtypeStruct(xs0.shape, xs0.dtype),
        scratch_shapes=[pltpu.SemaphoreType.DMA],
        input_output_aliases={2: 0},
        compiler_params=_cparams("arbitrary"),
        name="moe_dispatch",
    )(pos, h, xs0)


def _experts_kernel(te_ref, nt_ref, xs_ref, wg_ref, wu_ref, wd_ref, ys_ref):
    used = pl.program_id(0) < nt_ref[0]

    @pl.when(used)
    def _():
        x = xs_ref[...].astype(BF16)
        a = _dot(x, wg_ref[0])
        u = _dot(x, wu_ref[0])
        hid = (a * _sigmoid(a)) * u
        ys_ref[...] = _dot(hid.astype(BF16), wd_ref[0])

    @pl.when(jnp.logical_not(used))
    def _():
        ys_ref[...] = jnp.zeros_like(ys_ref)


def _experts(te, nt, xs, wg, wu, wd, tm):
    ntmax = xs.shape[0] // tm
    tile = pl.BlockSpec((tm, D_MODEL), lambda k, te, nt: (jnp.minimum(k, nt[0] - 1), 0))
    out_tile = pl.BlockSpec((tm, D_MODEL), lambda k, te, nt: (k, 0))
    return pl.pallas_call(
        _experts_kernel,
        grid_spec=pltpu.PrefetchScalarGridSpec(
            num_scalar_prefetch=2,
            grid=(ntmax,),
            in_specs=[tile,
                      pl.BlockSpec((1, D_MODEL, D_EXPERT), lambda k, te, nt: (te[k], 0, 0)),
                      pl.BlockSpec((1, D_MODEL, D_EXPERT), lambda k, te, nt: (te[k], 0, 0)),
                      pl.BlockSpec((1, D_EXPERT, D_MODEL), lambda k, te, nt: (te[k], 0, 0))],
            out_specs=out_tile),
        out_shape=jax.ShapeDtypeStruct(xs.shape, F32),
        compiler_params=_cparams("arbitrary"),
        name="moe_experts_sorted",
    )(te, nt, xs, wg, wu, wd)


def _combine_kernel(pos_ref, x_ref, w_ref, ys_hbm, y_ref, buf, sem, *, tc):
    def issue(g, carry):
        for u in range(DMA_UNROLL):
            r = g * DMA_UNROLL + u
            for ch in range(2):
                _row_copy(ys_hbm, pos_ref[0, ch, r], buf.at[ch], r, sem).start(priority=ch)
        return carry

    def drain(g, carry):
        for _ in range(2 * DMA_UNROLL):
            _row_copy(ys_hbm, 0, buf.at[0], 0, sem).wait()
        return carry

    lax.fori_loop(0, tc // DMA_UNROLL, issue, 0)
    lax.fori_loop(0, tc // DMA_UNROLL, drain, 0)
    w = w_ref[...]
    y_ref[...] = x_ref[...] + (w[:, 2:3] * buf[0] + w[:, 3:4] * buf[1])


def _combine(pos, x, wcol, ys, tc):
    n = x.shape[0]
    row = lambda width: pl.BlockSpec((tc, width), lambda i: (i, 0))
    return pl.pallas_call(
        functools.partial(_combine_kernel, tc=tc),
        grid=(n // tc,),
        in_specs=[pl.BlockSpec((1, 2, tc), lambda i: (i, 0, 0), memory_space=pltpu.SMEM),
                  row(D_MODEL), row(LANES), pl.BlockSpec(memory_space=pl.ANY)],
        out_specs=row(D_MODEL),
        out_shape=jax.ShapeDtypeStruct((n, D_MODEL), F32),
        scratch_shapes=[pltpu.VMEM((2, tc, D_MODEL), F32), pltpu.SemaphoreType.DMA],
        compiler_params=_cparams("arbitrary"),
        name="moe_combine",
    )(pos, x, wcol, ys)


def _pw1_kernel(x_ref, g_ref, w_ref, b_ref, u_ref):
    h = _rms(x_ref[...], g_ref[...]).astype(BF16)
    a = _dot(h, w_ref[:, 0:D_MODEL]) + b_ref[:, 0:D_MODEL]
    g = _dot(h, w_ref[:, D_MODEL:2 * D_MODEL]) + b_ref[:, D_MODEL:2 * D_MODEL]
    u_ref[...] = a * _sigmoid(g)


def _pw1(x, g, w, b, tm):
    n = x.shape[0]
    row = lambda width: pl.BlockSpec((tm, width), lambda i: (i, 0))
    return pl.pallas_call(
        _pw1_kernel,
        grid=(n // tm,),
        in_specs=[row(D_MODEL), _full((1, D_MODEL)), _full((D_MODEL, 2 * D_MODEL)), _full((1, 2 * D_MODEL))],
        out_specs=row(D_MODEL),
        out_shape=jax.ShapeDtypeStruct((n, D_MODEL), F32),
        compiler_params=_cparams("parallel"),
        name="conv_pw1_glu",
    )(x, g, w, b)


def _conv_kernel(u_ref, hist_ref, x_ref, wdw_ref, bdw_ref, lng_ref, lnb_ref, w2_ref, y_ref, ext_scr,
                 *, tt, rs):
    ti = pl.program_id(1)

    @pl.when(ti == 0)
    def _():
        ext_scr[0:HIST_ROWS, :] = hist_ref[0]

    ext_scr[HIST_ROWS:HIST_ROWS + tt, :] = u_ref[0]
    pad = HIST_ROWS - (CONV_W - 1)
    parts = []
    for r0 in range(0, tt, rs):
        acc = jnp.zeros((rs, D_MODEL), F32)
        for kk in range(CONV_W):
            acc = acc + wdw_ref[kk:kk + 1, :] * ext_scr[r0 + kk + pad:r0 + kk + pad + rs, :]
        parts.append(acc)
    y = jnp.concatenate(parts, axis=0) + bdw_ref[...]
    mu = jnp.mean(y, axis=-1, keepdims=True)
    yc = y - mu
    var = jnp.mean(yc * yc, axis=-1, keepdims=True)
    yn = yc * lax.rsqrt(var + EPS) * lng_ref[...] + lnb_ref[...]
    z = yn * _sigmoid(yn)
    y_ref[0] = x_ref[0] + _dot(z.astype(BF16), w2_ref[...])
    if tt >= HIST_ROWS:
        ext_scr[0:HIST_ROWS, :] = ext_scr[tt:tt + HIST_ROWS, :]


def _conv_module(u, hist, x, wdw, bdw, lng, lnb, w2, tt):
    b, t, _ = u.shape
    rs = min(tt, 32)
    blk = pl.BlockSpec((1, tt, D_MODEL), lambda bi, ti: (bi, ti, 0))
    return pl.pallas_call(
        functools.partial(_conv_kernel, tt=tt, rs=rs),
        grid=(b, t // tt),
        in_specs=[blk, pl.BlockSpec((1, HIST_ROWS, D_MODEL), lambda bi, ti: (bi, 0, 0)), blk,
                  _full((HIST_ROWS, D_MODEL)), _full((1, D_MODEL)), _full((1, D_MODEL)),
                  _full((1, D_MODEL)), _full((D_MODEL, D_MODEL))],
        out_specs=blk,
        out_shape=jax.ShapeDtypeStruct((b, t, D_MODEL), F32),
        scratch_shapes=[pltpu.VMEM((HIST_ROWS + tt, D_MODEL), F32)],
        compiler_params=_cparams("parallel", "arbitrary"),
        name="conv_module",
    )(u, hist, x, wdw, bdw, lng, lnb, w2)


def _rope_tables(pos):
    half = RET_DK // 2
    inv_freq = ROPE_BASE ** (-jnp.arange(half, dtype=F32) / half)
    ang = pos.astype(F32)[:, None] * inv_freq[None, :]
    cos, sin = jnp.cos(ang), jnp.sin(ang)
    reps = LANES // RET_DK
    cos_t = jnp.tile(jnp.concatenate([cos, cos], axis=1), (1, reps))
    sin_t = jnp.tile(jnp.concatenate([-sin, sin], axis=1), (1, reps))
    return cos_t, sin_t


def _pad_lanes(v, width=LANES):
    v = v.reshape(1, -1)
    return jnp.pad(v, ((0, 0), (0, width - v.shape[1])))


def _moe_dense(x, p, l, tm):
    h, gates = _router(x, p["norm_ffn"][l], p["wr_hi"][l], p["wr_lo"][l], p["br"][l], tm)
    return _moe(h, gates, x, p["wg"][l], p["wu"][l], p["wd"][l], tm)


def _sorted_rows(n_tokens):
    return (2 * n_tokens // MOE_TM + N_EXPERTS) * MOE_TM


class _SparseMoe:
    def __init__(self, p, n_tokens, tm):
        self.p, self.tm = p, tm
        self.xs = jnp.zeros((_sorted_rows(n_tokens), D_MODEL), F32)

    def __call__(self, x, l):
        p = self.p
        h, route, wcol = _router_t(x, p["norm_ffn"][l], p["wt_hi"][l], p["wt_lo"][l], p["bt"][l], self.tm)
        pos, te, nt = _plan(route, PLAN_TB, MOE_TM)
        self.xs = _dispatch(pos, h, self.xs, PLAN_TB)
        ys = _experts(te.reshape(-1), nt[0, :1], self.xs, p["wg"][l], p["wu"][l], p["wd"][l], MOE_TM)
        return _combine(pos, x, wcol, ys, PLAN_TB)


def _run_group(x, p, moe, *, seq, tm, fox_hist, ret_state, conv_hist, pos, tq, tk, ret_c, conv_tt):
    b = x.shape[0] // seq
    q_off = 0 if fox_hist is None else fox_hist[0].shape[1]

    cos_t, sin_t = _rope_tables(pos)
    q, k, v, lf, rq, rk, rv, rg = _proj0(x, p["norm_mix"][0], p["w_in"], p["b_f"], p["q_gain"], p["k_gain"],
                                          p["gbd"], cos_t, sin_t, tm)
    k3 = k.reshape(b, seq, FOX_W)
    v3 = v.reshape(b, seq, FOX_W)
    lf3 = lf.reshape(b, seq, LANES)
    if fox_hist is None:
        k_all, v_all, lf_all = k3, v3, lf3
    else:
        ck_, cv_, clf_ = fox_hist
        tot = q_off + seq
        padded = -(-tot // tk) * tk
        tail = padded - tot
        k_all = jnp.concatenate([ck_, k3, jnp.zeros((b, tail, FOX_W), F32)], axis=1)
        v_all = jnp.concatenate([cv_, v3, jnp.zeros((b, tail, FOX_W), F32)], axis=1)
        clf_ = jnp.pad(clf_, ((0, 0), (0, 0), (0, LANES - FOX_HEADS)))
        lf_all = jnp.concatenate([clf_, lf3, jnp.zeros((b, tail, LANES), F32)], axis=1)
    ccol, crow = _cumsum_logf(lf_all, CUM_BLOCK)
    o_fox = _fox_attention(q.reshape(b, seq, FOX_W), k_all, v_all, ccol, crow, tq, tk, q_off)
    o_ret, s_fin = _retention(p["lg_tab"], rq.reshape(b, seq, RET_QK_W), rk.reshape(b, seq, RET_QK_W),
                              rv.reshape(b, seq, RET_V_W), rg.reshape(b, seq, RET_V_W), ret_state,
                              p["gn_gain"], ret_c)
    x = _out0(x, o_fox.reshape(-1, FOX_W), o_ret.reshape(-1, RET_V_W), p["w_out"], tm)
    x = moe(x, 0)

    u = _pw1(x, p["norm_mix"][1], p["w_pw1"], p["b_pw1"], tm)
    u3 = u.reshape(b, seq, D_MODEL)
    x = _conv_module(u3, conv_hist, x.reshape(b, seq, D_MODEL), p["w_dw"], p["b_dw"], p["ln_g"], p["ln_b"],
                     p["w_pw2"], conv_tt).reshape(-1, D_MODEL)
    x = moe(x, 1)

    fox_k = k3.reshape(1, b, seq, FOX_HEADS, FOX_HD)
    fox_v = v3.reshape(1, b, seq, FOX_HEADS, FOX_HD)
    fox_lf = lf3[:, :, :FOX_HEADS].reshape(1, b, seq, FOX_HEADS)
    return x.reshape(b, seq, D_MODEL), fox_k, fox_v, fox_lf, s_fin[None], u3


def kernel(x_prompt, x_sample, cache_fox_k, cache_fox_v, cache_fox_logf, state_ret, cache_conv, norm_mix, norm_ffn, w_in_mix, b_forget, fox_q_gain, fox_k_gain, ret_gn_gain, w_out_mix, w_pw1, b_pw1, w_dw, b_dw, conv_ln_g, conv_ln_b, w_pw2, w_router_group, b_router_group, w_router_expert, b_router_expert, w_exp_gate, w_exp_up, w_exp_down):
    bp, t, d = x_prompt.shape
    bs, l, _ = x_sample.shape
    past = cache_fox_k.shape[2]
    depth = norm_mix.shape[0]
    assert d == D_MODEL and depth == 2 and w_in_mix.shape[0] == 1 and w_pw1.shape[0] == 1

    w_in = w_in_mix[0]
    n_pre = 3 * FOX_W
    w_in_r = jnp.concatenate(
        [w_in[:, :n_pre], w_in[:, n_pre + FOX_HEADS:], w_in[:, n_pre:n_pre + FOX_HEADS],
         jnp.zeros((D_MODEL, LANES - FOX_HEADS), F32)], axis=1).astype(BF16)
    hid = jnp.arange(FOX_W) // FOX_HD
    gbd = jnp.where(hid[:, None] == hid[None, :], 1.0 / FOX_HD, 0.0).astype(BF16)
    w_r = jnp.concatenate([w_router_group, w_router_expert], axis=-1)
    w_r = jnp.pad(w_r, ((0, 0), (0, 0), (0, LANES - w_r.shape[-1])))
    wr_hi = w_r.astype(BF16)
    wr_lo = (w_r - wr_hi.astype(F32)).astype(BF16)
    b_r = jnp.concatenate([b_router_group, b_router_expert], axis=-1)
    b_r = jnp.pad(b_r, ((0, 0), (0, LANES - b_r.shape[-1])))
    w_t = jnp.swapaxes(jnp.concatenate([w_router_expert, w_router_group], axis=-1), 1, 2)
    w_t = jnp.pad(w_t, ((0, 0), (0, ROUTE_ROWS - w_t.shape[1]), (0, 0)))
    wt_hi = w_t.astype(BF16)
    wt_lo = (w_t - wt_hi.astype(F32)).astype(BF16)
    b_t = jnp.concatenate([b_router_expert, b_router_group], axis=-1)
    b_t = jnp.pad(b_t, ((0, 0), (0, ROUTE_ROWS - b_t.shape[-1])))[:, :, None]
    log_gamma = jnp.log(1.0 - 2.0 ** (-5.0 - jnp.arange(RET_HEADS, dtype=F32)))
    p = {
        "norm_mix": norm_mix.reshape(depth, 1, D_MODEL),
        "norm_ffn": norm_ffn.reshape(depth, 1, D_MODEL),
        "w_in": w_in_r,
        "b_f": _pad_lanes(b_forget[0]),
        "q_gain": jnp.tile(fox_q_gain[0], FOX_HEADS).reshape(1, FOX_W),
        "k_gain": jnp.tile(fox_k_gain[0], FOX_HEADS).reshape(1, FOX_W),
        "gbd": gbd,
        "gn_gain": ret_gn_gain[0].reshape(1, RET_V_W),
        "lg_tab": jnp.broadcast_to(log_gamma[:, None, None], (RET_HEADS, 1, LANES)),
        "w_out": w_out_mix[0].astype(BF16),
        "w_pw1": w_pw1[0].astype(BF16),
        "b_pw1": b_pw1[0].reshape(1, -1),
        "w_dw": jnp.pad(w_dw[0], ((0, HIST_ROWS - CONV_W), (0, 0))),
        "b_dw": b_dw[0].reshape(1, -1),
        "ln_g": conv_ln_g[0].reshape(1, -1),
        "ln_b": conv_ln_b[0].reshape(1, -1),
        "w_pw2": w_pw2[0].astype(BF16),
        "wr_hi": wr_hi,
        "wr_lo": wr_lo,
        "br": b_r.reshape(depth, 1, LANES),
        "wt_hi": wt_hi,
        "wt_lo": wt_lo,
        "bt": b_t,
        "wg": w_exp_gate.reshape(depth, N_EXPERTS, D_MODEL, D_EXPERT).astype(BF16),
        "wu": w_exp_up.reshape(depth, N_EXPERTS, D_MODEL, D_EXPERT).astype(BF16),
        "wd": w_exp_down.reshape(depth, N_EXPERTS, D_EXPERT, D_MODEL).astype(BF16),
    }

    hist_pad = HIST_ROWS - (CONV_W - 1)
    yp, fk_p, fv_p, lf_p, rs_p, u_p = _run_group(
        x_prompt.reshape(bp * t, D_MODEL), p, _SparseMoe(p, bp * t, 512), seq=t, tm=512, fox_hist=None,
        ret_state=jnp.zeros((bp, RET_HEADS, RET_DK, RET_DV), F32),
        conv_hist=jnp.zeros((bp, HIST_ROWS, D_MODEL), F32),
        pos=jnp.arange(t), tq=256, tk=256, ret_c=256, conv_tt=256)

    ns = bs * l
    ys, fk_s, fv_s, lf_s, rs_s, u_s = _run_group(
        x_sample.reshape(ns, D_MODEL), p, lambda x, layer: _moe_dense(x, p, layer, ns), seq=l, tm=ns,
        fox_hist=(cache_fox_k[0].reshape(bs, past, FOX_W), cache_fox_v[0].reshape(bs, past, FOX_W),
                  cache_fox_logf[0]),
        ret_state=state_ret[0],
        conv_hist=jnp.pad(cache_conv[0], ((0, 0), (hist_pad, 0), (0, 0))),
        pos=past + (jnp.arange(ns) % l), tq=l, tk=-(-(past + l) // CUM_BLOCK) * CUM_BLOCK, ret_c=l, conv_tt=l)

    conv_p = u_p[:, t - (CONV_W - 1):][None]
    conv_s = jnp.concatenate([cache_conv[0], u_s], axis=1)[:, l:][None]
    return (yp, ys, fk_p, fv_p, lf_p, rs_p, conv_p, fk_s, fv_s, lf_s, rs_s, conv_s)
```

```python
import functools
import math

import jax
import jax.numpy as jnp
import numpy as np
from jax import lax
from jax.experimental import pallas as pl
from jax.experimental.pallas import tpu as pltpu

F32 = jnp.float32
BF16 = jnp.bfloat16

D_MODEL = 1024
FOX_HEADS = 8
FOX_HD = 64
RET_HEADS = 4
RET_DK = 64
RET_DV = 128
ROPE_BASE = 10000.0
CONV_W = 31
N_GROUPS = 4
EXP_PER_GROUP = 8
N_EXPERTS = N_GROUPS * EXP_PER_GROUP
D_EXPERT = 256
EPS = 1e-6
FOX_W = FOX_HEADS * FOX_HD
RET_QK_W = RET_HEADS * RET_DK
RET_V_W = RET_HEADS * RET_DV

LANES = 128
HIST_ROWS = 32
ROUTER_LANE0 = N_GROUPS
MASKED = -1e30
VMEM_LIMIT = 56 * 1024 * 1024
CUM_BLOCK = 256
MOE_TM = 256
PLAN_TB = 512
ROUTE_ROWS = 48

C_FQ, C_FK, C_FV = 0, FOX_W, 2 * FOX_W
C_RQ = 3 * FOX_W
C_RK = C_RQ + RET_QK_W
C_RV = C_RK + RET_QK_W
C_RG = C_RV + RET_V_W
C_FF = C_RG + RET_V_W
MIX_COLS = C_FF + LANES


def _cparams(*sem):
    return pltpu.CompilerParams(dimension_semantics=sem, vmem_limit_bytes=VMEM_LIMIT)


def _full(shape):
    n = len(shape)
    return pl.BlockSpec(shape, lambda *_: (0,) * n)


def _rms(x, g):
    ms = jnp.mean(x * x, axis=-1, keepdims=True)
    return x * lax.rsqrt(ms + EPS) * g


def _sigmoid(x):
    return 1.0 / (1.0 + jnp.exp(-x))


def _dot(a, b):
    return jnp.dot(a, b, preferred_element_type=F32)


def _dot_nt(a, b):
    return lax.dot_general(a, b, (((1,), (1,)), ((), ())), preferred_element_type=F32)


def _dot_tn(a, b):
    return lax.dot_general(a, b, (((0,), (0,)), ((), ())), preferred_element_type=F32)


def _proj0_kernel(x_ref, g_ref, w_ref, bf_ref, qg_ref, kg_ref, gbd_ref, cos_ref, sin_ref,
                  q_ref, k_ref, v_ref, lf_ref, rq_ref, rk_ref, rv_ref, rg_ref):
    h = _rms(x_ref[...], g_ref[...]).astype(BF16)

    def seg(a, b):
        return _dot(h, w_ref[:, a:b])

    gbd = gbd_ref[...]

    def head_rms(y, gain):
        ms = _dot((y * y).astype(BF16), gbd)
        return y * lax.rsqrt(ms + EPS) * gain

    q_ref[...] = (head_rms(seg(C_FQ, C_FK), qg_ref[...]) * (FOX_HD ** -0.5)).astype(BF16)
    k_ref[...] = head_rms(seg(C_FK, C_FV), kg_ref[...])
    v_ref[...] = seg(C_FV, C_RQ)

    z = seg(C_FF, MIX_COLS) + bf_ref[...]
    logf = jnp.minimum(z, 0.0) - jnp.log(1.0 + jnp.exp(-jnp.abs(z)))
    lane = lax.broadcasted_iota(jnp.int32, (1, LANES), 1)
    lf_ref[...] = jnp.where(lane < FOX_HEADS, logf, 0.0)

    cos = cos_ref[...]
    sin = sin_ref[...]
    first_half = (lane % RET_DK) < (RET_DK // 2)

    def rotary(y):
        outs = []
        for s in range(y.shape[1] // LANES):
            ys = y[:, s * LANES:(s + 1) * LANES]
            rot = jnp.where(first_half, pltpu.roll(ys, LANES - RET_DK // 2, 1),
                            pltpu.roll(ys, RET_DK // 2, 1))
            outs.append(ys * cos + rot * sin)
        return jnp.concatenate(outs, axis=1)

    rq_ref[...] = rotary(seg(C_RQ, C_RK)).astype(BF16)
    rk_ref[...] = (rotary(seg(C_RK, C_RV)) * (RET_DK ** -0.5)).astype(BF16)
    rv_ref[...] = seg(C_RV, C_RG).astype(BF16)
    rg_ref[...] = seg(C_RG, C_FF).astype(BF16)


def _proj0(x, g, w, bf, qg, kg, gbd, cos, sin, tm):
    n = x.shape[0]
    nper = cos.shape[0] // tm
    row = lambda width: pl.BlockSpec((tm, width), lambda i: (i, 0))
    tab = pl.BlockSpec((tm, LANES), lambda i: (i % nper, 0))
    outs = [(FOX_W, BF16), (FOX_W, F32), (FOX_W, F32), (LANES, F32),
            (RET_QK_W, BF16), (RET_QK_W, BF16), (RET_V_W, BF16), (RET_V_W, BF16)]
    return pl.pallas_call(
        _proj0_kernel,
        grid=(n // tm,),
        in_specs=[row(D_MODEL), _full((1, D_MODEL)), _full((D_MODEL, MIX_COLS)), _full((1, LANES)),
                  _full((1, FOX_W)), _full((1, FOX_W)), _full((FOX_W, FOX_W)), tab, tab],
        out_specs=[row(wd) for wd, _ in outs],
        out_shape=[jax.ShapeDtypeStruct((n, wd), dt) for wd, dt in outs],
        compiler_params=_cparams("parallel"),
        name="proj0",
    )(x, g, w, bf, qg, kg, gbd, cos, sin)


def _cum_kernel(lf_ref, ccol_ref, crow_ref, *, t, cb):
    r = lax.broadcasted_iota(jnp.int32, (cb, cb), 0)
    c = lax.broadcasted_iota(jnp.int32, (cb, cb), 1)
    tri = jnp.where(r >= c, 1.0, 0.0).astype(BF16)
    carry = jnp.zeros((1, LANES), F32)
    for blk in range(t // cb):
        a = lf_ref[0, blk * cb:(blk + 1) * cb, :]
        a1 = a.astype(BF16)
        r1 = a - a1.astype(F32)
        a2 = r1.astype(BF16)
        a3 = (r1 - a2.astype(F32)).astype(BF16)
        cc = (_dot(tri, a1) + _dot(tri, a2)) + _dot(tri, a3) + carry
        ccol_ref[0, blk * cb:(blk + 1) * cb, :] = cc
        crow_ref[0, :, blk * cb:(blk + 1) * cb] = cc.T[0:FOX_HEADS, :]
        carry = cc[cb - 1:cb, :]


def _cumsum_logf(lf, cb):
    b, t, _ = lf.shape
    return pl.pallas_call(
        functools.partial(_cum_kernel, t=t, cb=cb),
        grid=(b,),
        in_specs=[pl.BlockSpec((1, t, LANES), lambda i: (i, 0, 0))],
        out_specs=[pl.BlockSpec((1, t, LANES), lambda i: (i, 0, 0)),
                   pl.BlockSpec((1, FOX_HEADS, t), lambda i: (i, 0, 0))],
        out_shape=[jax.ShapeDtypeStruct((b, t, LANES), F32),
                   jax.ShapeDtypeStruct((b, FOX_HEADS, t), F32)],
        compiler_params=_cparams("parallel"),
        name="cumsum_logf",
    )(lf)


def _fox_kernel(q_ref, k_ref, v_ref, cq_ref, ck_ref, o_ref, *, tq, tk, q_off):
    hp = pl.program_id(1)
    i = pl.program_id(2)
    q0 = q_off + i * tq
    nfull = (q0 + 1) // tk
    nch = (q0 + tq + tk - 1) // tk
    lane = lax.broadcasted_iota(jnp.int32, (1, LANES), 1)
    q = q_ref[0]
    cq_all = cq_ref[0]
    qpos = q0 + lax.broadcasted_iota(jnp.int32, (tq, 1), 0)
    kiota = lax.broadcasted_iota(jnp.int32, (1, tk), 1)
    qms = [jnp.where((lane // FOX_HD) == h2, q, jnp.zeros_like(q)) for h2 in range(2)]
    cqs = [jnp.sum(jnp.where(lane == 2 * hp + h2, cq_all, 0.0), axis=-1, keepdims=True) for h2 in range(2)]

    def step(j, carry, masked):
        start = pl.multiple_of(j * tk, tk)
        kj = k_ref[0, pl.ds(start, tk), :].astype(BF16)
        vj = v_ref[0, pl.ds(start, tk), :].astype(BF16)
        new = []
        for h2 in range(2):
            m, l, acc = carry[h2]
            s = _dot_nt(qms[h2], kj)
            s = s + cqs[h2] - ck_ref[0, 2 * hp + h2, j]
            if masked:
                s = jnp.where(j * tk + kiota <= qpos, s, MASKED)
            m_new = jnp.maximum(m, jnp.max(s, axis=-1, keepdims=True))
            alpha = jnp.exp(m - m_new)
            p = jnp.exp(s - m_new)
            l = alpha * l + jnp.sum(p, axis=-1, keepdims=True)
            acc = alpha * acc + _dot(p.astype(BF16), vj)
            new.append((m_new, l, acc))
        return tuple(new)

    one = (jnp.full((tq, 1), MASKED, F32), jnp.zeros((tq, 1), F32), jnp.zeros((tq, LANES), F32))
    carry = lax.fori_loop(0, nfull, functools.partial(step, masked=False), (one, one))
    carry = lax.fori_loop(nfull, nch, functools.partial(step, masked=True), carry)
    outs = [acc / l for _, l, acc in carry]
    o_ref[0] = jnp.where(lane < FOX_HD, outs[0], outs[1]).astype(BF16)


def _fox_attention(q, k, v, ccol, crow, tq, tk, q_off):
    b, tqs, _ = q.shape
    tks = k.shape[1]
    nck = tks // tk
    crow5 = crow.reshape(b, FOX_HEADS, nck, 1, tk)
    qb0 = q_off // tq
    return pl.pallas_call(
        functools.partial(_fox_kernel, tq=tq, tk=tk, q_off=q_off),
        grid=(b, FOX_HEADS // 2, tqs // tq),
        in_specs=[pl.BlockSpec((1, tq, LANES), lambda bi, hp, i: (bi, i, hp)),
                  pl.BlockSpec((1, tks, LANES), lambda bi, hp, i: (bi, 0, hp)),
                  pl.BlockSpec((1, tks, LANES), lambda bi, hp, i: (bi, 0, hp)),
                  pl.BlockSpec((1, tq, LANES), lambda bi, hp, i: (bi, qb0 + i, 0)),
                  pl.BlockSpec((1, FOX_HEADS, nck, 1, tk), lambda bi, hp, i: (bi, 0, 0, 0, 0))],
        out_specs=pl.BlockSpec((1, tq, LANES), lambda bi, hp, i: (bi, i, hp)),
        out_shape=jax.ShapeDtypeStruct((b, tqs, FOX_W), BF16),
        compiler_params=_cparams("parallel", "parallel", "parallel"),
        name="fox_attention",
    )(q, k, v, ccol, crow5)


FOX_VROWS = 80


def _bf16_pieces(c):
    hi = c.astype(BF16).astype(F32)
    r = c - hi
    mid = r.astype(BF16).astype(F32)
    lo = (r - mid).astype(BF16).astype(F32)
    return hi, mid, lo


def _fox_prep_kernel(q_ref, k_ref, v_ref, c_ref, qa_ref, ka_ref, vt_ref, *, t, rc):
    hp = pl.program_id(1)
    lane = lax.broadcasted_iota(jnp.int32, (1, LANES), 1)
    ones = (1.0, 1.0, 1.0)

    def augment(xh, own, other):
        out = jnp.where(lane < FOX_HD, xh, 0.0)
        for piece in range(3):
            out = jnp.where(lane == FOX_HD + piece, own[piece], out)
            out = jnp.where(lane == FOX_HD + 3 + piece, other[piece], out)
        return out.astype(BF16)

    extra = jnp.where(lax.broadcasted_iota(jnp.int32, (FOX_VROWS - FOX_HD, rc), 0) == 0, 1.0, 0.0)
    for ch in range(t // rc):
        rows = slice(ch * rc, (ch + 1) * rc)
        qf = q_ref[0, rows, :].astype(F32)
        kf = k_ref[0, rows, :]
        call = c_ref[0, rows, :]
        vt = v_ref[0, rows, :].T
        for h2 in range(2):
            c = jnp.sum(jnp.where(lane == 2 * hp + h2, call, 0.0), axis=-1, keepdims=True)
            hi, mid, lo = _bf16_pieces(c)
            qh = qf if h2 == 0 else pltpu.roll(qf, FOX_HD, 1)
            kh = kf if h2 == 0 else pltpu.roll(kf, FOX_HD, 1)
            qa_ref[0, h2, rows, :] = augment(qh, (hi, mid, lo), ones)
            ka_ref[0, h2, rows, :] = augment(kh, ones, (-hi, -mid, -lo))
            vt_ref[0, h2, ch] = jnp.concatenate([vt[h2 * FOX_HD:(h2 + 1) * FOX_HD, :], extra], axis=0).astype(BF16)


def _fox_prep(q, k, v, ccol, rc):
    b, t, _ = q.shape
    pair = pl.BlockSpec((1, t, LANES), lambda bi, hp: (bi, 0, hp))
    aug = pl.BlockSpec((1, 2, t, LANES), lambda bi, hp: (bi, hp, 0, 0))
    return pl.pallas_call(
        functools.partial(_fox_prep_kernel, t=t, rc=rc),
        grid=(b, FOX_HEADS // 2),
        in_specs=[pair, pair, pair, pl.BlockSpec((1, t, LANES), lambda bi, hp: (bi, 0, 0))],
        out_specs=[aug, aug, pl.BlockSpec((1, 2, t // rc, FOX_VROWS, rc), lambda bi, hp: (bi, hp, 0, 0, 0))],
        out_shape=[jax.ShapeDtypeStruct((b, FOX_HEADS, t, LANES), BF16),
                   jax.ShapeDtypeStruct((b, FOX_HEADS, t, LANES), BF16),
                   jax.ShapeDtypeStruct((b, FOX_HEADS, t // rc, FOX_VROWS, rc), BF16)],
        compiler_params=_cparams("parallel", "parallel"),
        name="fox_prep",
    )(q, k, v, ccol)


def _fox_t_kernel(qa_ref, ka_ref, vt_ref, o_ref, s_scr, acc_scr, *, tq, tk):
    i = pl.program_id(1)
    q0 = i * tq
    nfull = (q0 + 1) // tk
    nch = (q0 + tq + tk - 1) // tk
    qpos = q0 + lax.broadcasted_iota(jnp.int32, (1, tq), 1)
    kiota = lax.broadcasted_iota(jnp.int32, (tk, 1), 0)
    heads = range(FOX_HEADS)

    def scores(j, mx, masked):
        start = pl.multiple_of(j * tk, tk)
        out = []
        for h in heads:
            st = _dot_nt(ka_ref[0, h, pl.ds(start, tk), :], qa_ref[0, h])
            if masked:
                st = jnp.where(start + kiota <= qpos, st, MASKED)
            s_scr[h, j] = st
            out.append(jnp.maximum(mx[h], jnp.max(st, axis=0, keepdims=True)))
        return tuple(out)

    mx = tuple(jnp.full((1, tq), MASKED, F32) for _ in heads)
    mx = lax.fori_loop(0, nfull, functools.partial(scores, masked=False), mx)
    mx = lax.fori_loop(nfull, nch, functools.partial(scores, masked=True), mx)

    acc_scr[...] = jnp.zeros_like(acc_scr)

    def weigh(j, carry):
        for h in heads:
            p = jnp.exp(s_scr[h, j] - mx[h]).astype(BF16)
            acc_scr[h] += _dot(vt_ref[0, h, j], p)
        return carry

    lax.fori_loop(0, nch, weigh, 0)
    for hp in range(FOX_HEADS // 2):
        outs = []
        for h in (2 * hp, 2 * hp + 1):
            acc = acc_scr[h]
            outs.append(acc[0:FOX_HD, :] / acc[FOX_HD:FOX_HD + 1, :])
        o_ref[0, :, hp * LANES:(hp + 1) * LANES] = jnp.concatenate(outs, axis=0).T.astype(BF16)


def _fox_attention_t(qa, ka, vt, tq):
    b, _, t, _ = qa.shape
    nck, tk = vt.shape[2], vt.shape[4]
    return pl.pallas_call(
        functools.partial(_fox_t_kernel, tq=tq, tk=tk),
        grid=(b, t // tq),
        in_specs=[pl.BlockSpec((1, FOX_HEADS, tq, LANES), lambda bi, i: (bi, 0, i, 0)),
                  pl.BlockSpec((1, FOX_HEADS, t, LANES), lambda bi, i: (bi, 0, 0, 0)),
                  pl.BlockSpec((1, FOX_HEADS, nck, FOX_VROWS, tk), lambda bi, i: (bi, 0, 0, 0, 0))],
        out_specs=pl.BlockSpec((1, tq, FOX_W), lambda bi, i: (bi, i, 0)),
        out_shape=jax.ShapeDtypeStruct((b, t, FOX_W), BF16),
        scratch_shapes=[pltpu.VMEM((FOX_HEADS, nck, tk, tq), F32),
                        pltpu.VMEM((FOX_HEADS, FOX_VROWS, tq), F32)],
        compiler_params=_cparams("parallel", "arbitrary"),
        name="fox_attention_t",
    )(qa, ka, vt)


def _ret_kernel(lg_ref, rq_ref, rk_ref, rv_ref, rg_ref, s0_ref, gn_ref, o_ref, sfin_ref, s_scr, d_scr, *, c):
    h = pl.program_id(1)
    ci = pl.program_id(2)
    odd = (h % 2) == 1

    lg = lg_ref[0][:, 0:1]
    ii = lax.broadcasted_iota(jnp.int32, (c, 1), 0).astype(F32)

    @pl.when(ci == 0)
    def _():
        s0 = s0_ref[0, 0]
        s_scr[0:RET_DK, :] = jnp.where(odd, 0.0, s0)
        s_scr[RET_DK:2 * RET_DK, :] = jnp.where(odd, s0, 0.0)
        jj = lax.broadcasted_iota(jnp.int32, (1, c), 1).astype(F32)
        diff = ii - jj
        d_scr[...] = jnp.where(diff >= 0.0, jnp.exp(lg * jnp.maximum(diff, 0.0)), 0.0)

    lane = lax.broadcasted_iota(jnp.int32, (1, LANES), 1)
    inhead = (lane // RET_DK) == (h % 2)
    q = rq_ref[0]
    k = rk_ref[0]
    qm = jnp.where(inhead, q, jnp.zeros_like(q))
    km = jnp.where(inhead, k, jnp.zeros_like(k))
    v = rv_ref[0]
    scores = _dot_nt(qm, km) * d_scr[...]
    inner = _dot(scores.astype(BF16), v)
    s_prev = s_scr[...]
    cross = _dot(qm, s_prev.astype(BF16)) * jnp.exp(lg * (ii + 1.0))
    y = inner + cross
    k_dec = (km.astype(F32) * jnp.exp(lg * (c - 1.0 - ii))).astype(BF16)
    s_new = jnp.exp(lg * float(c)) * s_prev + _dot_tn(k_dec, v)
    s_scr[...] = s_new

    mu = jnp.mean(y, axis=-1, keepdims=True)
    yc = y - mu
    var = jnp.mean(yc * yc, axis=-1, keepdims=True)
    yn = yc * lax.rsqrt(var + EPS) * gn_ref[...]
    g = rg_ref[0].astype(F32)
    o_ref[0] = ((g * _sigmoid(g)) * yn).astype(BF16)

    @pl.when(ci == pl.num_programs(2) - 1)
    def _():
        sfin_ref[0, 0] = jnp.where(odd, s_new[RET_DK:2 * RET_DK, :], s_new[0:RET_DK, :])


def _retention(lg_tab, rq, rk, rv, rg, s0, gn, c):
    b, t, _ = rq.shape
    qk_spec = pl.BlockSpec((1, c, LANES), lambda bi, h, ci: (bi, ci, h // 2))
    v_spec = pl.BlockSpec((1, c, LANES), lambda bi, h, ci: (bi, ci, h))
    st_spec = pl.BlockSpec((1, 1, RET_DK, RET_DV), lambda bi, h, ci: (bi, h, 0, 0))
    return pl.pallas_call(
        functools.partial(_ret_kernel, c=c),
        grid=(b, RET_HEADS, t // c),
        in_specs=[pl.BlockSpec((1, 1, LANES), lambda bi, h, ci: (h, 0, 0)),
                  qk_spec, qk_spec, v_spec, v_spec, st_spec,
                  pl.BlockSpec((1, LANES), lambda bi, h, ci: (0, h))],
        out_specs=[v_spec, st_spec],
        out_shape=[jax.ShapeDtypeStruct((b, t, RET_V_W), BF16),
                   jax.ShapeDtypeStruct((b, RET_HEADS, RET_DK, RET_DV), F32)],
        scratch_shapes=[pltpu.VMEM((2 * RET_DK, RET_DV), F32), pltpu.VMEM((c, c), F32)],
        compiler_params=_cparams("arbitrary", "arbitrary", "arbitrary"),
        name="retention",
    )(lg_tab, rq, rk, rv, rg, s0, gn)


def _out0_kernel(x_ref, of_ref, or_ref, w_ref, y_ref):
    mix = _dot(of_ref[...], w_ref[0:FOX_W, :]) + _dot(or_ref[...], w_ref[FOX_W:FOX_W + RET_V_W, :])
    y_ref[...] = x_ref[...] + mix


def _out0(x, o_fox, o_ret, w, tm):
    n = x.shape[0]
    row = lambda width: pl.BlockSpec((tm, width), lambda i: (i, 0))
    return pl.pallas_call(
        _out0_kernel,
        grid=(n // tm,),
        in_specs=[row(D_MODEL), row(FOX_W), row(RET_V_W), _full((FOX_W + RET_V_W, D_MODEL))],
        out_specs=row(D_MODEL),
        out_shape=jax.ShapeDtypeStruct((n, D_MODEL), F32),
        compiler_params=_cparams("parallel"),
        name="out_proj0",
    )(x, o_fox, o_ret, w)


def _router_kernel(x_ref, g_ref, wh_ref, wl_ref, b_ref, h_ref, gates_ref):
    hf = _rms(x_ref[...], g_ref[...])
    hh = hf.astype(BF16)
    hl = (hf - hh.astype(F32)).astype(BF16)
    h_ref[...] = hh
    wh = wh_ref[...]
    logits = _dot(hh, wh) + (_dot(hl, wh) + _dot(hh, wl_ref[...])) + b_ref[...]

    lane = lax.broadcasted_iota(jnp.int32, (1, LANES), 1)
    lanef = lane.astype(F32)
    ninf = -jnp.inf
    is_grp = lane < N_GROUPS
    gl = jnp.where(is_grp, logits, ninf)
    gmax = jnp.max(gl, axis=-1, keepdims=True)
    grp = jnp.min(jnp.where(gl == gmax, lanef, 1e9), axis=-1, keepdims=True)
    p_grp = 1.0 / jnp.sum(jnp.exp(gl - gmax), axis=-1, keepdims=True)

    is_exp = (lane >= ROUTER_LANE0) & (lane < ROUTER_LANE0 + N_EXPERTS)
    lane_grp = ((lane - ROUTER_LANE0) // EXP_PER_GROUP).astype(F32)
    em = jnp.where(is_exp & (lane_grp == grp), logits, ninf)
    v1 = jnp.max(em, axis=-1, keepdims=True)
    i1 = jnp.min(jnp.where(em == v1, lanef, 1e9), axis=-1, keepdims=True)
    em2 = jnp.where(lanef == i1, ninf, em)
    v2 = jnp.max(em2, axis=-1, keepdims=True)
    i2 = jnp.min(jnp.where(em2 == v2, lanef, 1e9), axis=-1, keepdims=True)
    t = jnp.exp(v2 - v1)
    w1 = (1.0 / (1.0 + t)) * p_grp
    w2 = (t / (1.0 + t)) * p_grp
    gates_ref[...] = jnp.where(lanef == i1, w1, 0.0) + jnp.where(lanef == i2, w2, 0.0)


def _router(x, g, wh, wl, b, tm):
    n = x.shape[0]
    row = lambda width: pl.BlockSpec((tm, width), lambda i: (i, 0))
    return pl.pallas_call(
        _router_kernel,
        grid=(n // tm,),
        in_specs=[row(D_MODEL), _full((1, D_MODEL)), _full((D_MODEL, LANES)), _full((D_MODEL, LANES)),
                  _full((1, LANES))],
        out_specs=[row(D_MODEL), row(LANES)],
        out_shape=[jax.ShapeDtypeStruct((n, D_MODEL), BF16), jax.ShapeDtypeStruct((n, LANES), F32)],
        compiler_params=_cparams("parallel"),
        name="router",
    )(x, g, wh, wl, b)


def _moe_kernel(h_ref, gates_ref, x_ref, wg_ref, wu_ref, wd_ref, y_ref):
    e = pl.program_id(1)

    @pl.when(e == 0)
    def _():
        y_ref[...] = jnp.zeros_like(y_ref)

    h = h_ref[...]
    a = _dot(h, wg_ref[0])
    u = _dot(h, wu_ref[0])
    lane = lax.broadcasted_iota(jnp.int32, (1, LANES), 1)
    gate = jnp.sum(jnp.where(lane == e + ROUTER_LANE0, gates_ref[...], 0.0), axis=-1, keepdims=True)
    hid = (a * _sigmoid(a)) * u * gate
    y_ref[...] += _dot(hid.astype(BF16), wd_ref[0])

    @pl.when(e == pl.num_programs(1) - 1)
    def _():
        y_ref[...] += x_ref[...]


def _moe(h, gates, x, wg, wu, wd, tm):
    n = x.shape[0]
    row = lambda width: pl.BlockSpec((tm, width), lambda i, e: (i, 0))
    return pl.pallas_call(
        _moe_kernel,
        grid=(n // tm, N_EXPERTS),
        in_specs=[row(D_MODEL), row(LANES), row(D_MODEL),
                  pl.BlockSpec((1, D_MODEL, D_EXPERT), lambda i, e: (e, 0, 0)),
                  pl.BlockSpec((1, D_MODEL, D_EXPERT), lambda i, e: (e, 0, 0)),
                  pl.BlockSpec((1, D_EXPERT, D_MODEL), lambda i, e: (e, 0, 0))],
        out_specs=row(D_MODEL),
        out_shape=jax.ShapeDtypeStruct((n, D_MODEL), F32),
        compiler_params=_cparams("parallel", "arbitrary"),
        name="moe_experts",
    )(h, gates, x, wg, wu, wd)


def _router_t_kernel(x_ref, g_ref, wth_ref, wtl_ref, bt_ref, h_ref, route_ref, wcol_ref):
    hf = _rms(x_ref[...], g_ref[...])
    h_ref[...] = hf
    hh = hf.astype(BF16)
    hl = (hf - hh.astype(F32)).astype(BF16)
    wth = wth_ref[...]
    logits = _dot_nt(wth, hh) + (_dot_nt(wth, hl) + _dot_nt(wtl_ref[...], hh)) + bt_ref[...]

    row = lax.broadcasted_iota(jnp.int32, (ROUTE_ROWS, 1), 0)
    rowf = row.astype(F32)
    ninf = -jnp.inf
    is_grp = (row >= N_EXPERTS) & (row < N_EXPERTS + N_GROUPS)
    gl = jnp.where(is_grp, logits, ninf)
    gmax = jnp.max(gl, axis=0, keepdims=True)
    grp = jnp.min(jnp.where(gl == gmax, rowf, 1e9), axis=0, keepdims=True) - float(N_EXPERTS)
    p_grp = 1.0 / jnp.sum(jnp.exp(gl - gmax), axis=0, keepdims=True)

    row_grp = (row // EXP_PER_GROUP).astype(F32)
    em = jnp.where((row < N_EXPERTS) & (row_grp == grp), logits, ninf)
    v1 = jnp.max(em, axis=0, keepdims=True)
    i1 = jnp.min(jnp.where(em == v1, rowf, 1e9), axis=0, keepdims=True)
    em2 = jnp.where(rowf == i1, ninf, em)
    v2 = jnp.max(em2, axis=0, keepdims=True)
    i2 = jnp.min(jnp.where(em2 == v2, rowf, 1e9), axis=0, keepdims=True)
    t = jnp.exp(v2 - v1)
    w1 = (1.0 / (1.0 + t)) * p_grp
    w2 = (t / (1.0 + t)) * p_grp

    tm = logits.shape[1]
    r128 = lax.broadcasted_iota(jnp.int32, (LANES, 1), 0)
    rt = (jnp.where(r128 == 0, i1, 0.0) + jnp.where(r128 == 1, i2, 0.0)
          + jnp.where(r128 == 2, w1, 0.0) + jnp.where(r128 == 3, w2, 0.0))
    route_ref[...] = rt[0:8, :]
    wcol_ref[...] = rt.T


def _router_t(x, g, wth, wtl, bt, tm):
    n = x.shape[0]
    row = lambda width: pl.BlockSpec((tm, width), lambda i: (i, 0))
    return pl.pallas_call(
        _router_t_kernel,
        grid=(n // tm,),
        in_specs=[row(D_MODEL), _full((1, D_MODEL)), _full((ROUTE_ROWS, D_MODEL)),
                  _full((ROUTE_ROWS, D_MODEL)), _full((ROUTE_ROWS, 1))],
        out_specs=[row(D_MODEL), pl.BlockSpec((8, tm), lambda i: (0, i)), row(LANES)],
        out_shape=[jax.ShapeDtypeStruct((n, D_MODEL), F32), jax.ShapeDtypeStruct((8, n), F32),
                   jax.ShapeDtypeStruct((n, LANES), F32)],
        compiler_params=_cparams("parallel"),
        name="router_t",
    )(x, g, wth, wtl, bt)


def _plan_kernel(route_ref, pos_ref, te_ref, nt_ref, *, n, tb, tm):
    nb = n // tb
    erow = lax.broadcasted_iota(jnp.int32, (N_EXPERTS, 1), 0).astype(F32)

    def picks(blk):
        i1 = route_ref[0:1, blk * tb:(blk + 1) * tb]
        i2 = route_ref[1:2, blk * tb:(blk + 1) * tb]
        return erow == i1, erow == i2

    acc = jnp.zeros((N_EXPERTS, tb), F32)
    for blk in range(nb):
        e1, e2 = picks(blk)
        acc = acc + jnp.where(e1, 1.0, 0.0) + jnp.where(e2, 1.0, 0.0)
    counts = jnp.sum(acc, axis=1, keepdims=True).astype(jnp.int32)
    assert tm & (tm - 1) == 0
    ntile = jnp.right_shift(counts + (tm - 1), tm.bit_length() - 1).astype(F32)
    lane = lax.broadcasted_iota(jnp.int32, (1, LANES), 1).astype(F32)
    ntile_row = jnp.sum(jnp.where(lane == erow, ntile, 0.0), axis=0, keepdims=True)
    tend = jnp.sum(jnp.where(lane <= erow, ntile_row, 0.0), axis=1, keepdims=True)
    off = (tend - ntile) * float(tm)
    ntot = jnp.sum(ntile, axis=0, keepdims=True)
    kk = jnp.minimum(lax.broadcasted_iota(jnp.int32, (1, 2 * LANES), 1).astype(F32), ntot - 1.0)
    te_ref[...] = jnp.sum(jnp.where(tend <= kk, 1.0, 0.0), axis=0, keepdims=True).astype(jnp.int32)
    nt_ref[...] = jnp.broadcast_to(ntot, (1, LANES)).astype(jnp.int32)

    r = lax.broadcasted_iota(jnp.int32, (tb, tb), 0)
    c = lax.broadcasted_iota(jnp.int32, (tb, tb), 1)
    before = jnp.where(r < c, 1.0, 0.0).astype(BF16)
    carry = jnp.zeros((N_EXPERTS, 1), F32)
    for blk in range(nb):
        e1, e2 = picks(blk)
        mt = jnp.where(e1, 1.0, 0.0) + jnp.where(e2, 1.0, 0.0)
        slot = _dot(mt.astype(BF16), before) + (carry + off)
        pos_ref[blk, 0:1, :] = jnp.sum(jnp.where(e1, slot, 0.0), axis=0, keepdims=True).astype(jnp.int32)
        pos_ref[blk, 1:2, :] = jnp.sum(jnp.where(e2, slot, 0.0), axis=0, keepdims=True).astype(jnp.int32)
        carry = carry + jnp.sum(mt, axis=1, keepdims=True)


def _plan(route, tb, tm):
    n = route.shape[1]
    vm = pl.BlockSpec(memory_space=pltpu.VMEM)
    return pl.pallas_call(
        functools.partial(_plan_kernel, n=n, tb=tb, tm=tm),
        in_specs=[vm],
        out_specs=[vm, vm, vm],
        out_shape=[jax.ShapeDtypeStruct((n // tb, 2, tb), jnp.int32),
                   jax.ShapeDtypeStruct((1, 2 * LANES), jnp.int32),
                   jax.ShapeDtypeStruct((1, LANES), jnp.int32)],
        compiler_params=pltpu.CompilerParams(vmem_limit_bytes=VMEM_LIMIT),
        name="moe_plan",
    )(route)


def _row_copy(src, src_row, dst, dst_row, sem):
    return pltpu.make_async_copy(src.at[pl.ds(src_row, 1)], dst.at[pl.ds(dst_row, 1)], sem)


DMA_UNROLL = 8


def _dispatch_kernel(pos_ref, h_ref, xs_in, xs_hbm, sem, *, ts):
    del xs_in

    def issue(g, carry):
        for u in range(DMA_UNROLL):
            r = g * DMA_UNROLL + u
            for ch in range(2):
                _row_copy(h_ref, r, xs_hbm, pos_ref[0, ch, r], sem).start(priority=ch)
        return carry

    def drain(g, carry):
        for _ in range(2 * DMA_UNROLL):
            _row_copy(h_ref, 0, xs_hbm, 0, sem).wait()
        return carry

    lax.fori_loop(0, ts // DMA_UNROLL, issue, 0)
    lax.fori_loop(0, ts // DMA_UNROLL, drain, 0)


def _dispatch(pos, h, xs0, ts):
    nb = pos.shape[0]
    anyspec = pl.BlockSpec(memory_space=pl.ANY)
    return pl.pallas_call(
        functools.partial(_dispatch_kernel, ts=ts),
        grid=(nb,),
        in_specs=[pl.BlockSpec((1, 2, ts), lambda i: (i, 0, 0), memory_space=pltpu.SMEM),
                  pl.BlockSpec((ts, D_MODEL), lambda i: (i, 0)), anyspec],
        out_specs=anyspec,
        out_shape=jax.ShapeDtypeStruct(xs0.shape, xs0.dtype),
        scratch_shapes=[pltpu.SemaphoreType.DMA],
        input_output_aliases={2: 0},
        compiler_params=_cparams("arbitrary"),
        name="moe_dispatch",
    )(pos, h, xs0)


def _experts_kernel(te_ref, nt_ref, xs_ref, wg_ref, wu_ref, wd_ref, ys_ref):
    used = pl.program_id(0) < nt_ref[0]

    @pl.when(used)
    def _():
        x = xs_ref[...].astype(BF16)
        a = _dot(x, wg_ref[0])
        u = _dot(x, wu_ref[0])
        hid = (a * _sigmoid(a)) * u
        ys_ref[...] = _dot(hid.astype(BF16), wd_ref[0])

    @pl.when(jnp.logical_not(used))
    def _():
        ys_ref[...] = jnp.zeros_like(ys_ref)


def _experts(te, nt, xs, wg, wu, wd, tm):
    ntmax = xs.shape[0] // tm
    tile = pl.BlockSpec((tm, D_MODEL), lambda k, te, nt: (jnp.minimum(k, nt[0] - 1), 0))
    out_tile = pl.BlockSpec((tm, D_MODEL), lambda k, te, nt: (k, 0))
    return pl.pallas_call(
        _experts_kernel,
        grid_spec=pltpu.PrefetchScalarGridSpec(
            num_scalar_prefetch=2,
            grid=(ntmax,),
            in_specs=[tile,
                      pl.BlockSpec((1, D_MODEL, D_EXPERT), lambda k, te, nt: (te[k], 0, 0)),
                      pl.BlockSpec((1, D_MODEL, D_EXPERT), lambda k, te, nt: (te[k], 0, 0)),
                      pl.BlockSpec((1, D_EXPERT, D_MODEL), lambda k, te, nt: (te[k], 0, 0))],
            out_specs=out_tile),
        out_shape=jax.ShapeDtypeStruct(xs.shape, F32),
        compiler_params=_cparams("arbitrary"),
        name="moe_experts_sorted",
    )(te, nt, xs, wg, wu, wd)


def _combine_kernel(pos_ref, x_ref, w_ref, ys_hbm, y_ref, buf, sem, *, tc):
    def issue(g, carry):
        for u in range(DMA_UNROLL):
            r = g * DMA_UNROLL + u
            for ch in range(2):
                _row_copy(ys_hbm, pos_ref[0, ch, r], buf.at[ch], r, sem).start(priority=ch)
        return carry

    def drain(g, carry):
        for _ in range(2 * DMA_UNROLL):
            _row_copy(ys_hbm, 0, buf.at[0], 0, sem).wait()
        return carry

    lax.fori_loop(0, tc // DMA_UNROLL, issue, 0)
    lax.fori_loop(0, tc // DMA_UNROLL, drain, 0)
    w = w_ref[...]
    y_ref[...] = x_ref[...] + (w[:, 2:3] * buf[0] + w[:, 3:4] * buf[1])


def _combine(pos, x, wcol, ys, tc):
    n = x.shape[0]
    row = lambda width: pl.BlockSpec((tc, width), lambda i: (i, 0))
    return pl.pallas_call(
        functools.partial(_combine_kernel, tc=tc),
        grid=(n // tc,),
        in_specs=[pl.BlockSpec((1, 2, tc), lambda i: (i, 0, 0), memory_space=pltpu.SMEM),
                  row(D_MODEL), row(LANES), pl.BlockSpec(memory_space=pl.ANY)],
        out_specs=row(D_MODEL),
        out_shape=jax.ShapeDtypeStruct((n, D_MODEL), F32),
        scratch_shapes=[pltpu.VMEM((2, tc, D_MODEL), F32), pltpu.SemaphoreType.DMA],
        compiler_params=_cparams("arbitrary"),
        name="moe_combine",
    )(pos, x, wcol, ys)


def _pw1_kernel(x_ref, g_ref, w_ref, b_ref, u_ref):
    h = _rms(x_ref[...], g_ref[...]).astype(BF16)
    a = _dot(h, w_ref[:, 0:D_MODEL]) + b_ref[:, 0:D_MODEL]
    g = _dot(h, w_ref[:, D_MODEL:2 * D_MODEL]) + b_ref[:, D_MODEL:2 * D_MODEL]
    u_ref[...] = a * _sigmoid(g)


def _pw1(x, g, w, b, tm):
    n = x.shape[0]
    row = lambda width: pl.BlockSpec((tm, width), lambda i: (i, 0))
    return pl.pallas_call(
        _pw1_kernel,
        grid=(n // tm,),
        in_specs=[row(D_MODEL), _full((1, D_MODEL)), _full((D_MODEL, 2 * D_MODEL)), _full((1, 2 * D_MODEL))],
        out_specs=row(D_MODEL),
        out_shape=jax.ShapeDtypeStruct((n, D_MODEL), F32),
        compiler_params=_cparams("parallel"),
        name="conv_pw1_glu",
    )(x, g, w, b)


def _conv_kernel(u_ref, hist_ref, x_ref, wdw_ref, bdw_ref, lng_ref, lnb_ref, w2_ref, y_ref, ext_scr, sh_scr,
                 *, tt, rs):
    ti = pl.program_id(1)

    @pl.when(ti == 0)
    def _():
        ext_scr[0:HIST_ROWS, :] = hist_ref[0]

    ext_scr[HIST_ROWS:HIST_ROWS + tt, :] = u_ref[0]
    pad = HIST_ROWS - (CONV_W - 1)
    rows = tt + HIST_ROWS - 8
    for r in range(1, 8):
        sh_scr[r - 1, :, :] = ext_scr[r:r + rows, :]
    parts = []
    for r0 in range(0, tt, rs):
        acc = jnp.zeros((rs, D_MODEL), F32)
        for kk in range(CONV_W):
            a, r = divmod(kk + pad, 8)
            lo = r0 + 8 * a
            src = ext_scr[lo:lo + rs, :] if r == 0 else sh_scr[r - 1, lo:lo + rs, :]
            acc = acc + wdw_ref[kk:kk + 1, :] * src
        parts.append(acc)
    y = jnp.concatenate(parts, axis=0) + bdw_ref[...]
    mu = jnp.mean(y, axis=-1, keepdims=True)
    yc = y - mu
    var = jnp.mean(yc * yc, axis=-1, keepdims=True)
    yn = yc * lax.rsqrt(var + EPS) * lng_ref[...] + lnb_ref[...]
    z = yn * _sigmoid(yn)
    y_ref[0] = x_ref[0] + _dot(z.astype(BF16), w2_ref[...])
    if tt >= HIST_ROWS:
        ext_scr[0:HIST_ROWS, :] = ext_scr[tt:tt + HIST_ROWS, :]


def _conv_module(u, hist, x, wdw, bdw, lng, lnb, w2, tt):
    b, t, _ = u.shape
    rs = min(tt, 32)
    blk = pl.BlockSpec((1, tt, D_MODEL), lambda bi, ti: (bi, ti, 0))
    return pl.pallas_call(
        functools.partial(_conv_kernel, tt=tt, rs=rs),
        grid=(b, t // tt),
        in_specs=[blk, pl.BlockSpec((1, HIST_ROWS, D_MODEL), lambda bi, ti: (bi, 0, 0)), blk,
                  _full((HIST_ROWS, D_MODEL)), _full((1, D_MODEL)), _full((1, D_MODEL)),
                  _full((1, D_MODEL)), _full((D_MODEL, D_MODEL))],
        out_specs=blk,
        out_shape=jax.ShapeDtypeStruct((b, t, D_MODEL), F32),
        scratch_shapes=[pltpu.VMEM((HIST_ROWS + tt, D_MODEL), F32),
                        pltpu.VMEM((7, tt + HIST_ROWS - 8, D_MODEL), F32)],
        compiler_params=_cparams("parallel", "arbitrary"),
        name="conv_module",
    )(u, hist, x, wdw, bdw, lng, lnb, w2)


def _rope_tables(pos):
    half = RET_DK // 2
    inv_freq = ROPE_BASE ** (-jnp.arange(half, dtype=F32) / half)
    ang = pos.astype(F32)[:, None] * inv_freq[None, :]
    cos, sin = jnp.cos(ang), jnp.sin(ang)
    reps = LANES // RET_DK
    cos_t = jnp.tile(jnp.concatenate([cos, cos], axis=1), (1, reps))
    sin_t = jnp.tile(jnp.concatenate([-sin, sin], axis=1), (1, reps))
    return cos_t, sin_t


def _pad_lanes(v, width=LANES):
    v = v.reshape(1, -1)
    return jnp.pad(v, ((0, 0), (0, width - v.shape[1])))


def _moe_dense(x, p, l, tm):
    h, gates = _router(x, p["norm_ffn"][l], p["wr_hi"][l], p["wr_lo"][l], p["br"][l], tm)
    return _moe(h, gates, x, p["wg"][l], p["wu"][l], p["wd"][l], tm)


def _sorted_rows(n_tokens):
    return (2 * n_tokens // MOE_TM + N_EXPERTS) * MOE_TM


class _SparseMoe:
    def __init__(self, p, n_tokens, tm):
        self.p, self.tm = p, tm
        self.xs = jnp.zeros((_sorted_rows(n_tokens), D_MODEL), F32)

    def __call__(self, x, l):
        p = self.p
        h, route, wcol = _router_t(x, p["norm_ffn"][l], p["wt_hi"][l], p["wt_lo"][l], p["bt"][l], self.tm)
        pos, te, nt = _plan(route, PLAN_TB, MOE_TM)
        self.xs = _dispatch(pos, h, self.xs, PLAN_TB)
        ys = _experts(te.reshape(-1), nt[0, :1], self.xs, p["wg"][l], p["wu"][l], p["wd"][l], MOE_TM)
        return _combine(pos, x, wcol, ys, PLAN_TB)


def _run_group(x, p, moe, *, seq, tm, fox_hist, ret_state, conv_hist, pos, tq, tk, ret_c, conv_tt):
    b = x.shape[0] // seq
    q_off = 0 if fox_hist is None else fox_hist[0].shape[1]

    cos_t, sin_t = _rope_tables(pos)
    q, k, v, lf, rq, rk, rv, rg = _proj0(x, p["norm_mix"][0], p["w_in"], p["b_f"], p["q_gain"], p["k_gain"],
                                          p["gbd"], cos_t, sin_t, tm)
    k3 = k.reshape(b, seq, FOX_W)
    v3 = v.reshape(b, seq, FOX_W)
    lf3 = lf.reshape(b, seq, LANES)
    if fox_hist is None:
        k_all, v_all, lf_all = k3, v3, lf3
    else:
        ck_, cv_, clf_ = fox_hist
        tot = q_off + seq
        padded = -(-tot // tk) * tk
        tail = padded - tot
        k_all = jnp.concatenate([ck_, k3, jnp.zeros((b, tail, FOX_W), F32)], axis=1)
        v_all = jnp.concatenate([cv_, v3, jnp.zeros((b, tail, FOX_W), F32)], axis=1)
        clf_ = jnp.pad(clf_, ((0, 0), (0, 0), (0, LANES - FOX_HEADS)))
        lf_all = jnp.concatenate([clf_, lf3, jnp.zeros((b, tail, LANES), F32)], axis=1)
    ccol, crow = _cumsum_logf(lf_all, CUM_BLOCK)
    if fox_hist is None:
        qa, ka, vt = _fox_prep(q.reshape(b, seq, FOX_W), k_all, v_all, ccol, tk)
        o_fox = _fox_attention_t(qa, ka, vt, tq)
    else:
        o_fox = _fox_attention(q.reshape(b, seq, FOX_W), k_all, v_all, ccol, crow, tq, tk, q_off)
    o_ret, s_fin = _retention(p["lg_tab"], rq.reshape(b, seq, RET_QK_W), rk.reshape(b, seq, RET_QK_W),
                              rv.reshape(b, seq, RET_V_W), rg.reshape(b, seq, RET_V_W), ret_state,
                              p["gn_gain"], ret_c)
    x = _out0(x, o_fox.reshape(-1, FOX_W), o_ret.reshape(-1, RET_V_W), p["w_out"], tm)
    x = moe(x, 0)

    u = _pw1(x, p["norm_mix"][1], p["w_pw1"], p["b_pw1"], tm)
    u3 = u.reshape(b, seq, D_MODEL)
    x = _conv_module(u3, conv_hist, x.reshape(b, seq, D_MODEL), p["w_dw"], p["b_dw"], p["ln_g"], p["ln_b"],
                     p["w_pw2"], conv_tt).reshape(-1, D_MODEL)
    x = moe(x, 1)

    fox_k = k3.reshape(1, b, seq, FOX_HEADS, FOX_HD)
    fox_v = v3.reshape(1, b, seq, FOX_HEADS, FOX_HD)
    fox_lf = lf3[:, :, :FOX_HEADS].reshape(1, b, seq, FOX_HEADS)
    return x.reshape(b, seq, D_MODEL), fox_k, fox_v, fox_lf, s_fin[None], u3


def kernel(x_prompt, x_sample, cache_fox_k, cache_fox_v, cache_fox_logf, state_ret, cache_conv, norm_mix, norm_ffn, w_in_mix, b_forget, fox_q_gain, fox_k_gain, ret_gn_gain, w_out_mix, w_pw1, b_pw1, w_dw, b_dw, conv_ln_g, conv_ln_b, w_pw2, w_router_group, b_router_group, w_router_expert, b_router_expert, w_exp_gate, w_exp_up, w_exp_down):
    bp, t, d = x_prompt.shape
    bs, l, _ = x_sample.shape
    past = cache_fox_k.shape[2]
    depth = norm_mix.shape[0]
    assert d == D_MODEL and depth == 2 and w_in_mix.shape[0] == 1 and w_pw1.shape[0] == 1

    w_in = w_in_mix[0]
    n_pre = 3 * FOX_W
    w_in_r = jnp.concatenate(
        [w_in[:, :n_pre], w_in[:, n_pre + FOX_HEADS:], w_in[:, n_pre:n_pre + FOX_HEADS],
         jnp.zeros((D_MODEL, LANES - FOX_HEADS), F32)], axis=1).astype(BF16)
    hid = jnp.arange(FOX_W) // FOX_HD
    gbd = jnp.where(hid[:, None] == hid[None, :], 1.0 / FOX_HD, 0.0).astype(BF16)
    w_r = jnp.concatenate([w_router_group, w_router_expert], axis=-1)
    w_r = jnp.pad(w_r, ((0, 0), (0, 0), (0, LANES - w_r.shape[-1])))
    wr_hi = w_r.astype(BF16)
    wr_lo = (w_r - wr_hi.astype(F32)).astype(BF16)
    b_r = jnp.concatenate([b_router_group, b_router_expert], axis=-1)
    b_r = jnp.pad(b_r, ((0, 0), (0, LANES - b_r.shape[-1])))
    w_t = jnp.swapaxes(jnp.concatenate([w_router_expert, w_router_group], axis=-1), 1, 2)
    w_t = jnp.pad(w_t, ((0, 0), (0, ROUTE_ROWS - w_t.shape[1]), (0, 0)))
    wt_hi = w_t.astype(BF16)
    wt_lo = (w_t - wt_hi.astype(F32)).astype(BF16)
    b_t = jnp.concatenate([b_router_expert, b_router_group], axis=-1)
    b_t = jnp.pad(b_t, ((0, 0), (0, ROUTE_ROWS - b_t.shape[-1])))[:, :, None]
    log_gamma = jnp.log(1.0 - 2.0 ** (-5.0 - jnp.arange(RET_HEADS, dtype=F32)))
    p = {
        "norm_mix": norm_mix.reshape(depth, 1, D_MODEL),
        "norm_ffn": norm_ffn.reshape(depth, 1, D_MODEL),
        "w_in": w_in_r,
        "b_f": _pad_lanes(b_forget[0]),
        "q_gain": jnp.tile(fox_q_gain[0], FOX_HEADS).reshape(1, FOX_W),
        "k_gain": jnp.tile(fox_k_gain[0], FOX_HEADS).reshape(1, FOX_W),
        "gbd": gbd,
        "gn_gain": ret_gn_gain[0].reshape(1, RET_V_W),
        "lg_tab": jnp.broadcast_to(log_gamma[:, None, None], (RET_HEADS, 1, LANES)),
        "w_out": w_out_mix[0].astype(BF16),
        "w_pw1": w_pw1[0].astype(BF16),
        "b_pw1": b_pw1[0].reshape(1, -1),
        "w_dw": jnp.pad(w_dw[0], ((0, HIST_ROWS - CONV_W), (0, 0))),
        "b_dw": b_dw[0].reshape(1, -1),
        "ln_g": conv_ln_g[0].reshape(1, -1),
        "ln_b": conv_ln_b[0].reshape(1, -1),
        "w_pw2": w_pw2[0].astype(BF16),
        "wr_hi": wr_hi,
        "wr_lo": wr_lo,
        "br": b_r.reshape(depth, 1, LANES),
        "wt_hi": wt_hi,
        "wt_lo": wt_lo,
        "bt": b_t,
        "wg": w_exp_gate.reshape(depth, N_EXPERTS, D_MODEL, D_EXPERT).astype(BF16),
        "wu": w_exp_up.reshape(depth, N_EXPERTS, D_MODEL, D_EXPERT).astype(BF16),
        "wd": w_exp_down.reshape(depth, N_EXPERTS, D_EXPERT, D_MODEL).astype(BF16),
    }

    hist_pad = HIST_ROWS - (CONV_W - 1)
    yp, fk_p, fv_p, lf_p, rs_p, u_p = _run_group(
        x_prompt.reshape(bp * t, D_MODEL), p, _SparseMoe(p, bp * t, 512), seq=t, tm=512, fox_hist=None,
        ret_state=jnp.zeros((bp, RET_HEADS, RET_DK, RET_DV), F32),
        conv_hist=jnp.zeros((bp, HIST_ROWS, D_MODEL), F32),
        pos=jnp.arange(t), tq=256, tk=256, ret_c=256, conv_tt=256)

    ns = bs * l
    ys, fk_s, fv_s, lf_s, rs_s, u_s = _run_group(
        x_sample.reshape(ns, D_MODEL), p, lambda x, layer: _moe_dense(x, p, layer, ns), seq=l, tm=ns,
        fox_hist=(cache_fox_k[0].reshape(bs, past, FOX_W), cache_fox_v[0].reshape(bs, past, FOX_W),
                  cache_fox_logf[0]),
        ret_state=state_ret[0],
        conv_hist=jnp.pad(cache_conv[0], ((0, 0), (hist_pad, 0), (0, 0))),
        pos=past + (jnp.arange(ns) % l), tq=l, tk=-(-(past + l) // CUM_BLOCK) * CUM_BLOCK, ret_c=l, conv_tt=l)

    conv_p = u_p[:, t - (CONV_W - 1):][None]
    conv_s = jnp.concatenate([cache_conv[0], u_s], axis=1)[:, l:][None]
    return (yp, ys, fk_p, fv_p, lf_p, rs_p, conv_p, fk_s, fv_s, lf_s, rs_s, conv_s)
```

```python
import functools
import math

import jax
import jax.numpy as jnp
import numpy as np
from jax import lax
from jax.experimental import pallas as pl
from jax.experimental.pallas import tpu as pltpu

F32 = jnp.float32
BF16 = jnp.bfloat16

D_MODEL = 1024
FOX_HEADS = 8
FOX_HD = 64
RET_HEADS = 4
RET_DK = 64
RET_DV = 128
ROPE_BASE = 10000.0
CONV_W = 31
N_GROUPS = 4
EXP_PER_GROUP = 8
N_EXPERTS = N_GROUPS * EXP_PER_GROUP
D_EXPERT = 256
EPS = 1e-6
FOX_W = FOX_HEADS * FOX_HD
RET_QK_W = RET_HEADS * RET_DK
RET_V_W = RET_HEADS * RET_DV

LANES = 128
HIST_ROWS = 32
ROUTER_LANE0 = N_GROUPS
MASKED = -1e30
VMEM_LIMIT = 56 * 1024 * 1024
CUM_BLOCK = 256
MOE_TM = 256
PLAN_TB = 512
ROUTE_ROWS = 48

C_FQ, C_FK, C_FV = 0, FOX_W, 2 * FOX_W
C_RQ = 3 * FOX_W
C_RK = C_RQ + RET_QK_W
C_RV = C_RK + RET_QK_W
C_RG = C_RV + RET_V_W
C_FF = C_RG + RET_V_W
MIX_COLS = C_FF + LANES


def _cparams(*sem):
    return pltpu.CompilerParams(dimension_semantics=sem, vmem_limit_bytes=VMEM_LIMIT)


def _full(shape):
    n = len(shape)
    return pl.BlockSpec(shape, lambda *_: (0,) * n)


def _rms(x, g):
    ms = jnp.mean(x * x, axis=-1, keepdims=True)
    return x * lax.rsqrt(ms + EPS) * g


def _sigmoid(x):
    return 1.0 / (1.0 + jnp.exp(-x))


def _dot(a, b):
    return jnp.dot(a, b, preferred_element_type=F32)


def _dot_nt(a, b):
    return lax.dot_general(a, b, (((1,), (1,)), ((), ())), preferred_element_type=F32)


def _dot_tn(a, b):
    return lax.dot_general(a, b, (((0,), (0,)), ((), ())), preferred_element_type=F32)


def _proj0_kernel(x_ref, g_ref, w_ref, bf_ref, qg_ref, kg_ref, gbd_ref, cos_ref, sin_ref,
                  q_ref, k_ref, v_ref, lf_ref, rq_ref, rk_ref, rv_ref, rg_ref):
    h = _rms(x_ref[...], g_ref[...]).astype(BF16)

    def seg(a, b):
        return _dot(h, w_ref[:, a:b])

    gbd = gbd_ref[...]

    def head_rms(y, gain):
        ms = _dot((y * y).astype(BF16), gbd)
        return y * lax.rsqrt(ms + EPS) * gain

    q_ref[...] = (head_rms(seg(C_FQ, C_FK), qg_ref[...]) * (FOX_HD ** -0.5)).astype(BF16)
    k_ref[...] = head_rms(seg(C_FK, C_FV), kg_ref[...])
    v_ref[...] = seg(C_FV, C_RQ)

    z = seg(C_FF, MIX_COLS) + bf_ref[...]
    logf = jnp.minimum(z, 0.0) - jnp.log(1.0 + jnp.exp(-jnp.abs(z)))
    lane = lax.broadcasted_iota(jnp.int32, (1, LANES), 1)
    lf_ref[...] = jnp.where(lane < FOX_HEADS, logf, 0.0)

    cos = cos_ref[...]
    sin = sin_ref[...]
    first_half = (lane % RET_DK) < (RET_DK // 2)

    def rotary(y):
        outs = []
        for s in range(y.shape[1] // LANES):
            ys = y[:, s * LANES:(s + 1) * LANES]
            rot = jnp.where(first_half, pltpu.roll(ys, LANES - RET_DK // 2, 1),
                            pltpu.roll(ys, RET_DK // 2, 1))
            outs.append(ys * cos + rot * sin)
        return jnp.concatenate(outs, axis=1)

    rq_ref[...] = rotary(seg(C_RQ, C_RK)).astype(BF16)
    rk_ref[...] = (rotary(seg(C_RK, C_RV)) * (RET_DK ** -0.5)).astype(BF16)
    rv_ref[...] = seg(C_RV, C_RG).astype(BF16)
    rg_ref[...] = seg(C_RG, C_FF).astype(BF16)


def _proj0(x, g, w, bf, qg, kg, gbd, cos, sin, tm):
    n = x.shape[0]
    nper = cos.shape[0] // tm
    row = lambda width: pl.BlockSpec((tm, width), lambda i: (i, 0))
    tab = pl.BlockSpec((tm, LANES), lambda i: (i % nper, 0))
    outs = [(FOX_W, BF16), (FOX_W, F32), (FOX_W, F32), (LANES, F32),
            (RET_QK_W, BF16), (RET_QK_W, BF16), (RET_V_W, BF16), (RET_V_W, BF16)]
    return pl.pallas_call(
        _proj0_kernel,
        grid=(n // tm,),
        in_specs=[row(D_MODEL), _full((1, D_MODEL)), _full((D_MODEL, MIX_COLS)), _full((1, LANES)),
                  _full((1, FOX_W)), _full((1, FOX_W)), _full((FOX_W, FOX_W)), tab, tab],
        out_specs=[row(wd) for wd, _ in outs],
        out_shape=[jax.ShapeDtypeStruct((n, wd), dt) for wd, dt in outs],
        compiler_params=_cparams("parallel"),
        name="proj0",
    )(x, g, w, bf, qg, kg, gbd, cos, sin)


def _cum_kernel(lf_ref, ccol_ref, crow_ref, *, t, cb):
    r = lax.broadcasted_iota(jnp.int32, (cb, cb), 0)
    c = lax.broadcasted_iota(jnp.int32, (cb, cb), 1)
    tri = jnp.where(r >= c, 1.0, 0.0).astype(BF16)
    carry = jnp.zeros((1, LANES), F32)
    for blk in range(t // cb):
        a = lf_ref[0, blk * cb:(blk + 1) * cb, :]
        a1 = a.astype(BF16)
        r1 = a - a1.astype(F32)
        a2 = r1.astype(BF16)
        a3 = (r1 - a2.astype(F32)).astype(BF16)
        cc = (_dot(tri, a1) + _dot(tri, a2)) + _dot(tri, a3) + carry
        ccol_ref[0, blk * cb:(blk + 1) * cb, :] = cc
        crow_ref[0, :, blk * cb:(blk + 1) * cb] = cc.T[0:FOX_HEADS, :]
        carry = cc[cb - 1:cb, :]


def _cumsum_logf(lf, cb):
    b, t, _ = lf.shape
    return pl.pallas_call(
        functools.partial(_cum_kernel, t=t, cb=cb),
        grid=(b,),
        in_specs=[pl.BlockSpec((1, t, LANES), lambda i: (i, 0, 0))],
        out_specs=[pl.BlockSpec((1, t, LANES), lambda i: (i, 0, 0)),
                   pl.BlockSpec((1, FOX_HEADS, t), lambda i: (i, 0, 0))],
        out_shape=[jax.ShapeDtypeStruct((b, t, LANES), F32),
                   jax.ShapeDtypeStruct((b, FOX_HEADS, t), F32)],
        compiler_params=_cparams("parallel"),
        name="cumsum_logf",
    )(lf)


def _fox_kernel(q_ref, k_ref, v_ref, cq_ref, ck_ref, o_ref, *, tq, tk, q_off):
    hp = pl.program_id(1)
    i = pl.program_id(2)
    q0 = q_off + i * tq
    nfull = (q0 + 1) // tk
    nch = (q0 + tq + tk - 1) // tk
    lane = lax.broadcasted_iota(jnp.int32, (1, LANES), 1)
    q = q_ref[0]
    cq_all = cq_ref[0]
    qpos = q0 + lax.broadcasted_iota(jnp.int32, (tq, 1), 0)
    kiota = lax.broadcasted_iota(jnp.int32, (1, tk), 1)
    qms = [jnp.where((lane // FOX_HD) == h2, q, jnp.zeros_like(q)) for h2 in range(2)]
    cqs = [jnp.sum(jnp.where(lane == 2 * hp + h2, cq_all, 0.0), axis=-1, keepdims=True) for h2 in range(2)]

    def step(j, carry, masked):
        start = pl.multiple_of(j * tk, tk)
        kj = k_ref[0, pl.ds(start, tk), :].astype(BF16)
        vj = v_ref[0, pl.ds(start, tk), :].astype(BF16)
        new = []
        for h2 in range(2):
            m, l, acc = carry[h2]
            s = _dot_nt(qms[h2], kj)
            s = s + cqs[h2] - ck_ref[0, 2 * hp + h2, j]
            if masked:
                s = jnp.where(j * tk + kiota <= qpos, s, MASKED)
            m_new = jnp.maximum(m, jnp.max(s, axis=-1, keepdims=True))
            alpha = jnp.exp(m - m_new)
            p = jnp.exp(s - m_new)
            l = alpha * l + jnp.sum(p, axis=-1, keepdims=True)
            acc = alpha * acc + _dot(p.astype(BF16), vj)
            new.append((m_new, l, acc))
        return tuple(new)

    one = (jnp.full((tq, 1), MASKED, F32), jnp.zeros((tq, 1), F32), jnp.zeros((tq, LANES), F32))
    carry = lax.fori_loop(0, nfull, functools.partial(step, masked=False), (one, one))
    carry = lax.fori_loop(nfull, nch, functools.partial(step, masked=True), carry)
    outs = [acc / l for _, l, acc in carry]
    o_ref[0] = jnp.where(lane < FOX_HD, outs[0], outs[1]).astype(BF16)


def _fox_attention(q, k, v, ccol, crow, tq, tk, q_off):
    b, tqs, _ = q.shape
    tks = k.shape[1]
    nck = tks // tk
    crow5 = crow.reshape(b, FOX_HEADS, nck, 1, tk)
    qb0 = q_off // tq
    return pl.pallas_call(
        functools.partial(_fox_kernel, tq=tq, tk=tk, q_off=q_off),
        grid=(b, FOX_HEADS // 2, tqs // tq),
        in_specs=[pl.BlockSpec((1, tq, LANES), lambda bi, hp, i: (bi, i, hp)),
                  pl.BlockSpec((1, tks, LANES), lambda bi, hp, i: (bi, 0, hp)),
                  pl.BlockSpec((1, tks, LANES), lambda bi, hp, i: (bi, 0, hp)),
                  pl.BlockSpec((1, tq, LANES), lambda bi, hp, i: (bi, qb0 + i, 0)),
                  pl.BlockSpec((1, FOX_HEADS, nck, 1, tk), lambda bi, hp, i: (bi, 0, 0, 0, 0))],
        out_specs=pl.BlockSpec((1, tq, LANES), lambda bi, hp, i: (bi, i, hp)),
        out_shape=jax.ShapeDtypeStruct((b, tqs, FOX_W), BF16),
        compiler_params=_cparams("parallel", "parallel", "parallel"),
        name="fox_attention",
    )(q, k, v, ccol, crow5)


FOX_VROWS = 80


def _bf16_pieces(c):
    hi = c.astype(BF16).astype(F32)
    r = c - hi
    mid = r.astype(BF16).astype(F32)
    lo = (r - mid).astype(BF16).astype(F32)
    return hi, mid, lo


def _fox_prep_kernel(q_ref, k_ref, v_ref, c_ref, qa_ref, ka_ref, vt_ref, *, t, rc):
    hp = pl.program_id(1)
    lane = lax.broadcasted_iota(jnp.int32, (1, LANES), 1)
    ones = (1.0, 1.0, 1.0)

    def augment(xh, own, other):
        out = jnp.where(lane < FOX_HD, xh, 0.0)
        for piece in range(3):
            out = jnp.where(lane == FOX_HD + piece, own[piece], out)
            out = jnp.where(lane == FOX_HD + 3 + piece, other[piece], out)
        return out.astype(BF16)

    extra = jnp.where(lax.broadcasted_iota(jnp.int32, (FOX_VROWS - FOX_HD, rc), 0) == 0, 1.0, 0.0)
    for ch in range(t // rc):
        rows = slice(ch * rc, (ch + 1) * rc)
        qf = q_ref[0, rows, :].astype(F32)
        kf = k_ref[0, rows, :]
        call = c_ref[0, rows, :]
        vt = v_ref[0, rows, :].T
        for h2 in range(2):
            c = jnp.sum(jnp.where(lane == 2 * hp + h2, call, 0.0), axis=-1, keepdims=True)
            hi, mid, lo = _bf16_pieces(c)
            qh = qf if h2 == 0 else pltpu.roll(qf, FOX_HD, 1)
            kh = kf if h2 == 0 else pltpu.roll(kf, FOX_HD, 1)
            qa_ref[0, h2, rows, :] = augment(qh, (hi, mid, lo), ones)
            ka_ref[0, h2, rows, :] = augment(kh, ones, (-hi, -mid, -lo))
            vt_ref[0, h2, ch] = jnp.concatenate([vt[h2 * FOX_HD:(h2 + 1) * FOX_HD, :], extra], axis=0).astype(BF16)


def _fox_prep(q, k, v, ccol, rc):
    b, t, _ = q.shape
    pair = pl.BlockSpec((1, t, LANES), lambda bi, hp: (bi, 0, hp))
    aug = pl.BlockSpec((1, 2, t, LANES), lambda bi, hp: (bi, hp, 0, 0))
    return pl.pallas_call(
        functools.partial(_fox_prep_kernel, t=t, rc=rc),
        grid=(b, FOX_HEADS // 2),
        in_specs=[pair, pair, pair, pl.BlockSpec((1, t, LANES), lambda bi, hp: (bi, 0, 0))],
        out_specs=[aug, aug, pl.BlockSpec((1, 2, t // rc, FOX_VROWS, rc), lambda bi, hp: (bi, hp, 0, 0, 0))],
        out_shape=[jax.ShapeDtypeStruct((b, FOX_HEADS, t, LANES), BF16),
                   jax.ShapeDtypeStruct((b, FOX_HEADS, t, LANES), BF16),
                   jax.ShapeDtypeStruct((b, FOX_HEADS, t // rc, FOX_VROWS, rc), BF16)],
        compiler_params=_cparams("parallel", "parallel"),
        name="fox_prep",
    )(q, k, v, ccol)


def _fox_t_kernel(qa_ref, ka_ref, vt_ref, o_ref, s_scr, acc_scr, *, tq, tk):
    i = pl.program_id(1)
    q0 = i * tq
    nfull = (q0 + 1) // tk
    nch = (q0 + tq + tk - 1) // tk
    qpos = q0 + lax.broadcasted_iota(jnp.int32, (1, tq), 1)
    kiota = lax.broadcasted_iota(jnp.int32, (tk, 1), 0)
    heads = range(FOX_HEADS)

    def scores(j, mx, masked):
        start = pl.multiple_of(j * tk, tk)
        out = []
        for h in heads:
            st = _dot_nt(ka_ref[0, h, pl.ds(start, tk), :], qa_ref[0, h])
            if masked:
                st = jnp.where(start + kiota <= qpos, st, MASKED)
            s_scr[h, j] = st
            out.append(jnp.maximum(mx[h], jnp.max(st, axis=0, keepdims=True)))
        return tuple(out)

    mx = tuple(jnp.full((1, tq), MASKED, F32) for _ in heads)
    mx = lax.fori_loop(0, nfull, functools.partial(scores, masked=False), mx)
    mx = lax.fori_loop(nfull, nch, functools.partial(scores, masked=True), mx)

    acc_scr[...] = jnp.zeros_like(acc_scr)

    def weigh(j, carry):
        for h in heads:
            p = jnp.exp(s_scr[h, j] - mx[h]).astype(BF16)
            acc_scr[h] += _dot(vt_ref[0, h, j], p)
        return carry

    lax.fori_loop(0, nch, weigh, 0)
    for hp in range(FOX_HEADS // 2):
        outs = []
        for h in (2 * hp, 2 * hp + 1):
            acc = acc_scr[h]
            outs.append(acc[0:FOX_HD, :] / acc[FOX_HD:FOX_HD + 1, :])
        o_ref[0, :, hp * LANES:(hp + 1) * LANES] = jnp.concatenate(outs, axis=0).T.astype(BF16)


def _fox_attention_t(qa, ka, vt, tq):
    b, _, t, _ = qa.shape
    nck, tk = vt.shape[2], vt.shape[4]
    return pl.pallas_call(
        functools.partial(_fox_t_kernel, tq=tq, tk=tk),
        grid=(b, t // tq),
        in_specs=[pl.BlockSpec((1, FOX_HEADS, tq, LANES), lambda bi, i: (bi, 0, i, 0)),
                  pl.BlockSpec((1, FOX_HEADS, t, LANES), lambda bi, i: (bi, 0, 0, 0)),
                  pl.BlockSpec((1, FOX_HEADS, nck, FOX_VROWS, tk), lambda bi, i: (bi, 0, 0, 0, 0))],
        out_specs=pl.BlockSpec((1, tq, FOX_W), lambda bi, i: (bi, i, 0)),
        out_shape=jax.ShapeDtypeStruct((b, t, FOX_W), BF16),
        scratch_shapes=[pltpu.VMEM((FOX_HEADS, nck, tk, tq), F32),
                        pltpu.VMEM((FOX_HEADS, FOX_VROWS, tq), F32)],
        compiler_params=_cparams("parallel", "arbitrary"),
        name="fox_attention_t",
    )(qa, ka, vt)


def _ret_kernel(lg_ref, rq_ref, rk_ref, rv_ref, rg_ref, s0_ref, gn_ref, o_ref, sfin_ref, s_scr, d_scr, *, c):
    first = (pl.program_id(0) == 0) & (pl.program_id(1) == 0)
    ci = pl.program_id(1)
    ii = lax.broadcasted_iota(jnp.int32, (c, 1), 0).astype(F32)
    lane = lax.broadcasted_iota(jnp.int32, (1, LANES), 1)

    @pl.when(first)
    def _():
        jj = lax.broadcasted_iota(jnp.int32, (1, c), 1).astype(F32)
        diff = ii - jj
        for h in range(RET_HEADS):
            lg = lg_ref[h][:, 0:1]
            d_scr[h] = jnp.where(diff >= 0.0, jnp.exp(lg * jnp.maximum(diff, 0.0)), 0.0)

    @pl.when(ci == 0)
    def _():
        zero = jnp.zeros((RET_DK, RET_DV), F32)
        for h in range(RET_HEADS):
            lo, hi = (s0_ref[0, h], zero) if h % 2 == 0 else (zero, s0_ref[0, h])
            s_scr[h, 0:RET_DK, :] = lo
            s_scr[h, RET_DK:2 * RET_DK, :] = hi

    for h in range(RET_HEADS):
        lg = lg_ref[h][:, 0:1]
        pair = slice((h // 2) * LANES, (h // 2 + 1) * LANES)
        mine = slice(h * RET_DV, (h + 1) * RET_DV)
        inhead = (lane // RET_DK) == (h % 2)
        q = rq_ref[0, :, pair]
        k = rk_ref[0, :, pair]
        qm = jnp.where(inhead, q, jnp.zeros_like(q))
        km = jnp.where(inhead, k, jnp.zeros_like(k))
        v = rv_ref[0, :, mine]
        scores = _dot_nt(qm, km) * d_scr[h]
        inner = _dot(scores.astype(BF16), v)
        s_prev = s_scr[h]
        cross = _dot(qm, s_prev.astype(BF16)) * jnp.exp(lg * (ii + 1.0))
        y = inner + cross
        k_dec = (km.astype(F32) * jnp.exp(lg * (c - 1.0 - ii))).astype(BF16)
        s_new = jnp.exp(lg * float(c)) * s_prev + _dot_tn(k_dec, v)
        s_scr[h] = s_new

        mu = jnp.mean(y, axis=-1, keepdims=True)
        yc = y - mu
        var = jnp.mean(yc * yc, axis=-1, keepdims=True)
        yn = yc * lax.rsqrt(var + EPS) * gn_ref[:, mine]
        g = rg_ref[0, :, mine].astype(F32)
        o_ref[0, :, mine] = ((g * _sigmoid(g)) * yn).astype(BF16)
        off = (h % 2) * RET_DK
        sfin_ref[0, h] = s_new[off:off + RET_DK, :]


def _retention(lg_tab, rq, rk, rv, rg, s0, gn, c):
    b, t, _ = rq.shape
    qk_spec = pl.BlockSpec((1, c, RET_QK_W), lambda bi, ci: (bi, ci, 0))
    v_spec = pl.BlockSpec((1, c, RET_V_W), lambda bi, ci: (bi, ci, 0))
    st_spec = pl.BlockSpec((1, RET_HEADS, RET_DK, RET_DV), lambda bi, ci: (bi, 0, 0, 0))
    return pl.pallas_call(
        functools.partial(_ret_kernel, c=c),
        grid=(b, t // c),
        in_specs=[_full((RET_HEADS, 1, LANES)), qk_spec, qk_spec, v_spec, v_spec, st_spec,
                  _full((1, RET_V_W))],
        out_specs=[v_spec, st_spec],
        out_shape=[jax.ShapeDtypeStruct((b, t, RET_V_W), BF16),
                   jax.ShapeDtypeStruct((b, RET_HEADS, RET_DK, RET_DV), F32)],
        scratch_shapes=[pltpu.VMEM((RET_HEADS, 2 * RET_DK, RET_DV), F32), pltpu.VMEM((RET_HEADS, c, c), F32)],
        compiler_params=_cparams("arbitrary", "arbitrary"),
        name="retention",
    )(lg_tab, rq, rk, rv, rg, s0, gn)


def _out0_kernel(x_ref, of_ref, or_ref, w_ref, y_ref):
    mix = _dot(of_ref[...], w_ref[0:FOX_W, :]) + _dot(or_ref[...], w_ref[FOX_W:FOX_W + RET_V_W, :])
    y_ref[...] = x_ref[...] + mix


def _out0(x, o_fox, o_ret, w, tm):
    n = x.shape[0]
    row = lambda width: pl.BlockSpec((tm, width), lambda i: (i, 0))
    return pl.pallas_call(
        _out0_kernel,
        grid=(n // tm,),
        in_specs=[row(D_MODEL), row(FOX_W), row(RET_V_W), _full((FOX_W + RET_V_W, D_MODEL))],
        out_specs=row(D_MODEL),
        out_shape=jax.ShapeDtypeStruct((n, D_MODEL), F32),
        compiler_params=_cparams("parallel"),
        name="out_proj0",
    )(x, o_fox, o_ret, w)


def _router_kernel(x_ref, g_ref, wh_ref, wl_ref, b_ref, h_ref, gates_ref):
    hf = _rms(x_ref[...], g_ref[...])
    hh = hf.astype(BF16)
    hl = (hf - hh.astype(F32)).astype(BF16)
    h_ref[...] = hh
    wh = wh_ref[...]
    logits = _dot(hh, wh) + (_dot(hl, wh) + _dot(hh, wl_ref[...])) + b_ref[...]

    lane = lax.broadcasted_iota(jnp.int32, (1, LANES), 1)
    lanef = lane.astype(F32)
    ninf = -jnp.inf
    is_grp = lane < N_GROUPS
    gl = jnp.where(is_grp, logits, ninf)
    gmax = jnp.max(gl, axis=-1, keepdims=True)
    grp = jnp.min(jnp.where(gl == gmax, lanef, 1e9), axis=-1, keepdims=True)
    p_grp = 1.0 / jnp.sum(jnp.exp(gl - gmax), axis=-1, keepdims=True)

    is_exp = (lane >= ROUTER_LANE0) & (lane < ROUTER_LANE0 + N_EXPERTS)
    lane_grp = ((lane - ROUTER_LANE0) // EXP_PER_GROUP).astype(F32)
    em = jnp.where(is_exp & (lane_grp == grp), logits, ninf)
    v1 = jnp.max(em, axis=-1, keepdims=True)
    i1 = jnp.min(jnp.where(em == v1, lanef, 1e9), axis=-1, keepdims=True)
    em2 = jnp.where(lanef == i1, ninf, em)
    v2 = jnp.max(em2, axis=-1, keepdims=True)
    i2 = jnp.min(jnp.where(em2 == v2, lanef, 1e9), axis=-1, keepdims=True)
    t = jnp.exp(v2 - v1)
    w1 = (1.0 / (1.0 + t)) * p_grp
    w2 = (t / (1.0 + t)) * p_grp
    gates_ref[...] = jnp.where(lanef == i1, w1, 0.0) + jnp.where(lanef == i2, w2, 0.0)


def _router(x, g, wh, wl, b, tm):
    n = x.shape[0]
    row = lambda width: pl.BlockSpec((tm, width), lambda i: (i, 0))
    return pl.pallas_call(
        _router_kernel,
        grid=(n // tm,),
        in_specs=[row(D_MODEL), _full((1, D_MODEL)), _full((D_MODEL, LANES)), _full((D_MODEL, LANES)),
                  _full((1, LANES))],
        out_specs=[row(D_MODEL), row(LANES)],
        out_shape=[jax.ShapeDtypeStruct((n, D_MODEL), BF16), jax.ShapeDtypeStruct((n, LANES), F32)],
        compiler_params=_cparams("parallel"),
        name="router",
    )(x, g, wh, wl, b)


def _moe_kernel(h_ref, gates_ref, x_ref, wg_ref, wu_ref, wd_ref, y_ref):
    e = pl.program_id(1)

    @pl.when(e == 0)
    def _():
        y_ref[...] = jnp.zeros_like(y_ref)

    h = h_ref[...]
    a = _dot(h, wg_ref[0].astype(BF16))
    u = _dot(h, wu_ref[0].astype(BF16))
    lane = lax.broadcasted_iota(jnp.int32, (1, LANES), 1)
    gate = jnp.sum(jnp.where(lane == e + ROUTER_LANE0, gates_ref[...], 0.0), axis=-1, keepdims=True)
    hid = (a * _sigmoid(a)) * u * gate
    y_ref[...] += _dot(hid.astype(BF16), wd_ref[0].astype(BF16))

    @pl.when(e == pl.num_programs(1) - 1)
    def _():
        y_ref[...] += x_ref[...]


def _moe(h, gates, x, wg, wu, wd, tm):
    n = x.shape[0]
    row = lambda width: pl.BlockSpec((tm, width), lambda i, e: (i, 0))
    return pl.pallas_call(
        _moe_kernel,
        grid=(n // tm, N_EXPERTS),
        in_specs=[row(D_MODEL), row(LANES), row(D_MODEL),
                  pl.BlockSpec((1, D_MODEL, D_EXPERT), lambda i, e: (e, 0, 0)),
                  pl.BlockSpec((1, D_MODEL, D_EXPERT), lambda i, e: (e, 0, 0)),
                  pl.BlockSpec((1, D_EXPERT, D_MODEL), lambda i, e: (e, 0, 0))],
        out_specs=row(D_MODEL),
        out_shape=jax.ShapeDtypeStruct((n, D_MODEL), F32),
        compiler_params=_cparams("parallel", "arbitrary"),
        name="moe_experts",
    )(h, gates, x, wg, wu, wd)


def _router_t_kernel(x_ref, g_ref, wth_ref, wtl_ref, bt_ref, h_ref, route_ref, wcol_ref):
    hf = _rms(x_ref[...], g_ref[...])
    h_ref[...] = hf
    hh = hf.astype(BF16)
    hl = (hf - hh.astype(F32)).astype(BF16)
    wth = wth_ref[...]
    logits = _dot_nt(wth, hh) + (_dot_nt(wth, hl) + _dot_nt(wtl_ref[...], hh)) + bt_ref[...]

    row = lax.broadcasted_iota(jnp.int32, (ROUTE_ROWS, 1), 0)
    rowf = row.astype(F32)
    ninf = -jnp.inf
    is_grp = (row >= N_EXPERTS) & (row < N_EXPERTS + N_GROUPS)
    gl = jnp.where(is_grp, logits, ninf)
    gmax = jnp.max(gl, axis=0, keepdims=True)
    grp = jnp.min(jnp.where(gl == gmax, rowf, 1e9), axis=0, keepdims=True) - float(N_EXPERTS)
    p_grp = 1.0 / jnp.sum(jnp.exp(gl - gmax), axis=0, keepdims=True)

    row_grp = (row // EXP_PER_GROUP).astype(F32)
    em = jnp.where((row < N_EXPERTS) & (row_grp == grp), logits, ninf)
    v1 = jnp.max(em, axis=0, keepdims=True)
    i1 = jnp.min(jnp.where(em == v1, rowf, 1e9), axis=0, keepdims=True)
    em2 = jnp.where(rowf == i1, ninf, em)
    v2 = jnp.max(em2, axis=0, keepdims=True)
    i2 = jnp.min(jnp.where(em2 == v2, rowf, 1e9), axis=0, keepdims=True)
    t = jnp.exp(v2 - v1)
    w1 = (1.0 / (1.0 + t)) * p_grp
    w2 = (t / (1.0 + t)) * p_grp

    tm = logits.shape[1]
    r128 = lax.broadcasted_iota(jnp.int32, (LANES, 1), 0)
    rt = (jnp.where(r128 == 0, i1, 0.0) + jnp.where(r128 == 1, i2, 0.0)
          + jnp.where(r128 == 2, w1, 0.0) + jnp.where(r128 == 3, w2, 0.0))
    route_ref[...] = rt[0:8, :]
    wcol_ref[...] = rt.T


def _router_t(x, g, wth, wtl, bt, tm):
    n = x.shape[0]
    row = lambda width: pl.BlockSpec((tm, width), lambda i: (i, 0))
    return pl.pallas_call(
        _router_t_kernel,
        grid=(n // tm,),
        in_specs=[row(D_MODEL), _full((1, D_MODEL)), _full((ROUTE_ROWS, D_MODEL)),
                  _full((ROUTE_ROWS, D_MODEL)), _full((ROUTE_ROWS, 1))],
        out_specs=[row(D_MODEL), pl.BlockSpec((8, tm), lambda i: (0, i)), row(LANES)],
        out_shape=[jax.ShapeDtypeStruct((n, D_MODEL), F32), jax.ShapeDtypeStruct((8, n), F32),
                   jax.ShapeDtypeStruct((n, LANES), F32)],
        compiler_params=_cparams("parallel"),
        name="router_t",
    )(x, g, wth, wtl, bt)


def _plan_kernel(route_ref, pos_ref, te_ref, nt_ref, pad_ref, *, n, tb, tm):
    nb = n // tb
    erow = lax.broadcasted_iota(jnp.int32, (N_EXPERTS, 1), 0).astype(F32)

    def picks(blk):
        i1 = route_ref[0:1, blk * tb:(blk + 1) * tb]
        i2 = route_ref[1:2, blk * tb:(blk + 1) * tb]
        return erow == i1, erow == i2

    acc = jnp.zeros((N_EXPERTS, tb), F32)
    for blk in range(nb):
        e1, e2 = picks(blk)
        acc = acc + jnp.where(e1, 1.0, 0.0) + jnp.where(e2, 1.0, 0.0)
    counts = jnp.sum(acc, axis=1, keepdims=True).astype(jnp.int32)
    assert tm & (tm - 1) == 0
    ntile = jnp.right_shift(counts + (tm - 1), tm.bit_length() - 1).astype(F32)
    lane = lax.broadcasted_iota(jnp.int32, (1, LANES), 1).astype(F32)
    ntile_row = jnp.sum(jnp.where(lane == erow, ntile, 0.0), axis=0, keepdims=True)
    tend = jnp.sum(jnp.where(lane <= erow, ntile_row, 0.0), axis=1, keepdims=True)
    off = (tend - ntile) * float(tm)
    ntot = jnp.sum(ntile, axis=0, keepdims=True)
    kk = jnp.minimum(lax.broadcasted_iota(jnp.int32, (1, 2 * LANES), 1).astype(F32), ntot - 1.0)
    te_ref[...] = jnp.sum(jnp.where(tend <= kk, 1.0, 0.0), axis=0, keepdims=True).astype(jnp.int32)
    nt_ref[...] = jnp.broadcast_to(ntot, (1, LANES)).astype(jnp.int32)
    pad0 = jnp.left_shift(jnp.right_shift(off.astype(jnp.int32) + counts, 3), 3).astype(F32)
    pad_ref[...] = jnp.sum(jnp.where(lane == erow, pad0, 0.0), axis=0, keepdims=True).astype(jnp.int32)

    r = lax.broadcasted_iota(jnp.int32, (tb, tb), 0)
    c = lax.broadcasted_iota(jnp.int32, (tb, tb), 1)
    before = jnp.where(r < c, 1.0, 0.0).astype(BF16)
    carry = jnp.zeros((N_EXPERTS, 1), F32)
    for blk in range(nb):
        e1, e2 = picks(blk)
        mt = jnp.where(e1, 1.0, 0.0) + jnp.where(e2, 1.0, 0.0)
        slot = _dot(mt.astype(BF16), before) + (carry + off)
        pos_ref[blk, 0:1, :] = jnp.sum(jnp.where(e1, slot, 0.0), axis=0, keepdims=True).astype(jnp.int32)
        pos_ref[blk, 1:2, :] = jnp.sum(jnp.where(e2, slot, 0.0), axis=0, keepdims=True).astype(jnp.int32)
        carry = carry + jnp.sum(mt, axis=1, keepdims=True)


def _plan(route, tb, tm):
    n = route.shape[1]
    vm = pl.BlockSpec(memory_space=pltpu.VMEM)
    return pl.pallas_call(
        functools.partial(_plan_kernel, n=n, tb=tb, tm=tm),
        in_specs=[vm],
        out_specs=[vm, vm, vm, vm],
        out_shape=[jax.ShapeDtypeStruct((n // tb, 2, tb), jnp.int32),
                   jax.ShapeDtypeStruct((1, 2 * LANES), jnp.int32),
                   jax.ShapeDtypeStruct((1, LANES), jnp.int32),
                   jax.ShapeDtypeStruct((1, LANES), jnp.int32)],
        compiler_params=pltpu.CompilerParams(vmem_limit_bytes=VMEM_LIMIT),
        name="moe_plan",
    )(route)


def _row_copy(src, src_row, dst, dst_row, sem):
    return pltpu.make_async_copy(src.at[pl.ds(src_row, 1)], dst.at[pl.ds(dst_row, 1)], sem)


DMA_UNROLL = 8


def _dispatch_kernel(pos_ref, pad_ref, h_ref, xs_hbm, zero_scr, sem, zsem, *, ts):
    @pl.when(pl.program_id(0) == 0)
    def _():
        zero_scr[...] = jnp.zeros_like(zero_scr)

        def fill(e):
            return pltpu.make_async_copy(
                zero_scr, xs_hbm.at[pl.ds(pl.multiple_of(pad_ref[0, e], 8), zero_scr.shape[0])], zsem)

        for e in range(N_EXPERTS):
            fill(e).start()
        for e in range(N_EXPERTS):
            fill(e).wait()

    def issue(g, carry):
        for u in range(DMA_UNROLL):
            r = g * DMA_UNROLL + u
            for ch in range(2):
                _row_copy(h_ref, r, xs_hbm, pos_ref[0, ch, r], sem).start(priority=ch)
        return carry

    def drain(g, carry):
        for _ in range(2 * DMA_UNROLL):
            _row_copy(h_ref, 0, xs_hbm, 0, sem).wait()
        return carry

    lax.fori_loop(0, ts // DMA_UNROLL, issue, 0)
    lax.fori_loop(0, ts // DMA_UNROLL, drain, 0)


def _dispatch(pos, pad, h, ts, tm):
    nb, n = pos.shape[0], h.shape[0]
    fill_rows = tm + 8
    rows = _sorted_rows(n) + 2 * tm
    return pl.pallas_call(
        functools.partial(_dispatch_kernel, ts=ts),
        grid=(nb,),
        in_specs=[pl.BlockSpec((1, 2, ts), lambda i: (i, 0, 0), memory_space=pltpu.SMEM),
                  pl.BlockSpec(memory_space=pltpu.SMEM),
                  pl.BlockSpec((ts, D_MODEL), lambda i: (i, 0))],
        out_specs=pl.BlockSpec(memory_space=pl.ANY),
        out_shape=jax.ShapeDtypeStruct((rows, D_MODEL), F32),
        scratch_shapes=[pltpu.VMEM((fill_rows, D_MODEL), F32), pltpu.SemaphoreType.DMA,
                        pltpu.SemaphoreType.DMA],
        compiler_params=_cparams("arbitrary"),
        name="moe_dispatch",
    )(pos, pad, h)


def _experts_kernel(te_ref, nt_ref, xs_ref, wg_ref, wu_ref, wd_ref, ys_ref):
    used = pl.program_id(0) < nt_ref[0]

    @pl.when(used)
    def _():
        x = xs_ref[...].astype(BF16)
        a = _dot(x, wg_ref[0].astype(BF16))
        u = _dot(x, wu_ref[0].astype(BF16))
        hid = (a * _sigmoid(a)) * u
        ys_ref[...] = _dot(hid.astype(BF16), wd_ref[0].astype(BF16))

    @pl.when(jnp.logical_not(used))
    def _():
        ys_ref[...] = jnp.zeros_like(ys_ref)


def _experts(te, nt, xs, wg, wu, wd, tm):
    ntmax = xs.shape[0] // tm
    tile = pl.BlockSpec((tm, D_MODEL), lambda k, te, nt: (jnp.minimum(k, nt[0] - 1), 0))
    out_tile = pl.BlockSpec((tm, D_MODEL), lambda k, te, nt: (k, 0))
    return pl.pallas_call(
        _experts_kernel,
        grid_spec=pltpu.PrefetchScalarGridSpec(
            num_scalar_prefetch=2,
            grid=(ntmax,),
            in_specs=[tile,
                      pl.BlockSpec((1, D_MODEL, D_EXPERT), lambda k, te, nt: (te[k], 0, 0)),
                      pl.BlockSpec((1, D_MODEL, D_EXPERT), lambda k, te, nt: (te[k], 0, 0)),
                      pl.BlockSpec((1, D_EXPERT, D_MODEL), lambda k, te, nt: (te[k], 0, 0))],
            out_specs=out_tile),
        out_shape=jax.ShapeDtypeStruct(xs.shape, F32),
        compiler_params=_cparams("arbitrary"),
        name="moe_experts_sorted",
    )(te, nt, xs, wg, wu, wd)


def _combine_kernel(pos_ref, x_ref, w_ref, ys_hbm, y_ref, buf, sem, *, tc):
    def issue(g, carry):
        for u in range(DMA_UNROLL):
            r = g * DMA_UNROLL + u
            for ch in range(2):
                _row_copy(ys_hbm, pos_ref[0, ch, r], buf.at[ch], r, sem).start(priority=ch)
        return carry

    def drain(g, carry):
        for _ in range(2 * DMA_UNROLL):
            _row_copy(ys_hbm, 0, buf.at[0], 0, sem).wait()
        return carry

    lax.fori_loop(0, tc // DMA_UNROLL, issue, 0)
    lax.fori_loop(0, tc // DMA_UNROLL, drain, 0)
    w = w_ref[...]
    y_ref[...] = x_ref[...] + (w[:, 2:3] * buf[0] + w[:, 3:4] * buf[1])


def _combine(pos, x, wcol, ys, tc):
    n = x.shape[0]
    row = lambda width: pl.BlockSpec((tc, width), lambda i: (i, 0))
    return pl.pallas_call(
        functools.partial(_combine_kernel, tc=tc),
        grid=(n // tc,),
        in_specs=[pl.BlockSpec((1, 2, tc), lambda i: (i, 0, 0), memory_space=pltpu.SMEM),
                  row(D_MODEL), row(LANES), pl.BlockSpec(memory_space=pl.ANY)],
        out_specs=row(D_MODEL),
        out_shape=jax.ShapeDtypeStruct((n, D_MODEL), F32),
        scratch_shapes=[pltpu.VMEM((2, tc, D_MODEL), F32), pltpu.SemaphoreType.DMA],
        compiler_params=_cparams("arbitrary"),
        name="moe_combine",
    )(pos, x, wcol, ys)


def _pw1_kernel(x_ref, g_ref, w_ref, b_ref, u_ref):
    h = _rms(x_ref[...], g_ref[...]).astype(BF16)
    a = _dot(h, w_ref[:, 0:D_MODEL]) + b_ref[:, 0:D_MODEL]
    g = _dot(h, w_ref[:, D_MODEL:2 * D_MODEL]) + b_ref[:, D_MODEL:2 * D_MODEL]
    u_ref[...] = a * _sigmoid(g)


def _pw1(x, g, w, b, tm):
    n = x.shape[0]
    row = lambda width: pl.BlockSpec((tm, width), lambda i: (i, 0))
    return pl.pallas_call(
        _pw1_kernel,
        grid=(n // tm,),
        in_specs=[row(D_MODEL), _full((1, D_MODEL)), _full((D_MODEL, 2 * D_MODEL)), _full((1, 2 * D_MODEL))],
        out_specs=row(D_MODEL),
        out_shape=jax.ShapeDtypeStruct((n, D_MODEL), F32),
        compiler_params=_cparams("parallel"),
        name="conv_pw1_glu",
    )(x, g, w, b)


def _conv_kernel(u_ref, hist_ref, x_ref, wdw_ref, bdw_ref, lng_ref, lnb_ref, w2_ref, y_ref, ext_scr, sh_scr,
                 *, tt, rs):
    ti = pl.program_id(1)

    @pl.when(ti == 0)
    def _():
        ext_scr[0:HIST_ROWS, :] = hist_ref[0]

    ext_scr[HIST_ROWS:HIST_ROWS + tt, :] = u_ref[0]
    pad = HIST_ROWS - (CONV_W - 1)
    rows = tt + HIST_ROWS - 8
    for r in range(1, 8):
        sh_scr[r - 1, :, :] = ext_scr[r:r + rows, :]
    parts = []
    for r0 in range(0, tt, rs):
        acc = jnp.zeros((rs, D_MODEL), F32)
        for kk in range(CONV_W):
            a, r = divmod(kk + pad, 8)
            lo = r0 + 8 * a
            src = ext_scr[lo:lo + rs, :] if r == 0 else sh_scr[r - 1, lo:lo + rs, :]
            acc = acc + wdw_ref[kk:kk + 1, :] * src
        parts.append(acc)
    y = jnp.concatenate(parts, axis=0) + bdw_ref[...]
    mu = jnp.mean(y, axis=-1, keepdims=True)
    yc = y - mu
    var = jnp.mean(yc * yc, axis=-1, keepdims=True)
    yn = yc * lax.rsqrt(var + EPS) * lng_ref[...] + lnb_ref[...]
    z = yn * _sigmoid(yn)
    y_ref[0] = x_ref[0] + _dot(z.astype(BF16), w2_ref[...])
    if tt >= HIST_ROWS:
        ext_scr[0:HIST_ROWS, :] = ext_scr[tt:tt + HIST_ROWS, :]


def _conv_module(u, hist, x, wdw, bdw, lng, lnb, w2, tt):
    b, t, _ = u.shape
    rs = min(tt, 32)
    blk = pl.BlockSpec((1, tt, D_MODEL), lambda bi, ti: (bi, ti, 0))
    return pl.pallas_call(
        functools.partial(_conv_kernel, tt=tt, rs=rs),
        grid=(b, t // tt),
        in_specs=[blk, pl.BlockSpec((1, HIST_ROWS, D_MODEL), lambda bi, ti: (bi, 0, 0)), blk,
                  _full((HIST_ROWS, D_MODEL)), _full((1, D_MODEL)), _full((1, D_MODEL)),
                  _full((1, D_MODEL)), _full((D_MODEL, D_MODEL))],
        out_specs=blk,
        out_shape=jax.ShapeDtypeStruct((b, t, D_MODEL), F32),
        scratch_shapes=[pltpu.VMEM((HIST_ROWS + tt, D_MODEL), F32),
                        pltpu.VMEM((7, tt + HIST_ROWS - 8, D_MODEL), F32)],
        compiler_params=_cparams("parallel", "arbitrary"),
        name="conv_module",
    )(u, hist, x, wdw, bdw, lng, lnb, w2)


def _rope_tables(pos):
    half = RET_DK // 2
    inv_freq = ROPE_BASE ** (-jnp.arange(half, dtype=F32) / half)
    ang = pos.astype(F32)[:, None] * inv_freq[None, :]
    cos, sin = jnp.cos(ang), jnp.sin(ang)
    reps = LANES // RET_DK
    cos_t = jnp.tile(jnp.concatenate([cos, cos], axis=1), (1, reps))
    sin_t = jnp.tile(jnp.concatenate([-sin, sin], axis=1), (1, reps))
    return cos_t, sin_t


def _pad_lanes(v, width=LANES):
    v = v.reshape(1, -1)
    return jnp.pad(v, ((0, 0), (0, width - v.shape[1])))


def _moe_dense(x, p, l, tm):
    h, gates = _router(x, p["norm_ffn"][l], p["wr_hi"][l], p["wr_lo"][l], p["br"][l], tm)
    return _moe(h, gates, x, p["wg"][l], p["wu"][l], p["wd"][l], tm)


def _sorted_rows(n_tokens):
    return (2 * n_tokens // MOE_TM + N_EXPERTS) * MOE_TM


def _moe_sparse(x, p, l, tm):
    h, route, wcol = _router_t(x, p["norm_ffn"][l], p["wt_hi"][l], p["wt_lo"][l], p["bt"][l], tm)
    pos, te, nt, pad = _plan(route, PLAN_TB, MOE_TM)
    xs = _dispatch(pos, pad, h, PLAN_TB, MOE_TM)
    ys = _experts(te.reshape(-1), nt[0, :1], xs, p["wg"][l], p["wu"][l], p["wd"][l], MOE_TM)
    return _combine(pos, x, wcol, ys, PLAN_TB)


def _run_group(x, p, moe, *, seq, tm, fox_hist, ret_state, conv_hist, pos, tq, tk, ret_c, conv_tt):
    b = x.shape[0] // seq
    q_off = 0 if fox_hist is None else fox_hist[0].shape[1]

    cos_t, sin_t = _rope_tables(pos)
    q, k, v, lf, rq, rk, rv, rg = _proj0(x, p["norm_mix"][0], p["w_in"], p["b_f"], p["q_gain"], p["k_gain"],
                                          p["gbd"], cos_t, sin_t, tm)
    k3 = k.reshape(b, seq, FOX_W)
    v3 = v.reshape(b, seq, FOX_W)
    lf3 = lf.reshape(b, seq, LANES)
    if fox_hist is None:
        k_all, v_all, lf_all = k3, v3, lf3
    else:
        ck_, cv_, clf_ = fox_hist
        tot = q_off + seq
        padded = -(-tot // tk) * tk
        tail = padded - tot
        k_all = jnp.concatenate([ck_, k3, jnp.zeros((b, tail, FOX_W), F32)], axis=1)
        v_all = jnp.concatenate([cv_, v3, jnp.zeros((b, tail, FOX_W), F32)], axis=1)
        clf_ = jnp.pad(clf_, ((0, 0), (0, 0), (0, LANES - FOX_HEADS)))
        lf_all = jnp.concatenate([clf_, lf3, jnp.zeros((b, tail, LANES), F32)], axis=1)
    ccol, crow = _cumsum_logf(lf_all, CUM_BLOCK)
    if fox_hist is None:
        qa, ka, vt = _fox_prep(q.reshape(b, seq, FOX_W), k_all, v_all, ccol, tk)
        o_fox = _fox_attention_t(qa, ka, vt, tq)
    else:
        o_fox = _fox_attention(q.reshape(b, seq, FOX_W), k_all, v_all, ccol, crow, tq, tk, q_off)
    o_ret, s_fin = _retention(p["lg_tab"], rq.reshape(b, seq, RET_QK_W), rk.reshape(b, seq, RET_QK_W),
                              rv.reshape(b, seq, RET_V_W), rg.reshape(b, seq, RET_V_W), ret_state,
                              p["gn_gain"], ret_c)
    x = _out0(x, o_fox.reshape(-1, FOX_W), o_ret.reshape(-1, RET_V_W), p["w_out"], tm)
    x = moe(x, 0)

    u = _pw1(x, p["norm_mix"][1], p["w_pw1"], p["b_pw1"], tm)
    u3 = u.reshape(b, seq, D_MODEL)
    x = _conv_module(u3, conv_hist, x.reshape(b, seq, D_MODEL), p["w_dw"], p["b_dw"], p["ln_g"], p["ln_b"],
                     p["w_pw2"], conv_tt).reshape(-1, D_MODEL)
    x = moe(x, 1)

    fox_k = k3.reshape(1, b, seq, FOX_HEADS, FOX_HD)
    fox_v = v3.reshape(1, b, seq, FOX_HEADS, FOX_HD)
    fox_lf = lf3[:, :, :FOX_HEADS].reshape(1, b, seq, FOX_HEADS)
    return x.reshape(b, seq, D_MODEL), fox_k, fox_v, fox_lf, s_fin[None], u3


def kernel(x_prompt, x_sample, cache_fox_k, cache_fox_v, cache_fox_logf, state_ret, cache_conv, norm_mix, norm_ffn, w_in_mix, b_forget, fox_q_gain, fox_k_gain, ret_gn_gain, w_out_mix, w_pw1, b_pw1, w_dw, b_dw, conv_ln_g, conv_ln_b, w_pw2, w_router_group, b_router_group, w_router_expert, b_router_expert, w_exp_gate, w_exp_up, w_exp_down):
    bp, t, d = x_prompt.shape
    bs, l, _ = x_sample.shape
    past = cache_fox_k.shape[2]
    depth = norm_mix.shape[0]
    assert d == D_MODEL and depth == 2 and w_in_mix.shape[0] == 1 and w_pw1.shape[0] == 1

    w_in = w_in_mix[0]
    n_pre = 3 * FOX_W
    w_in_r = jnp.concatenate(
        [w_in[:, :n_pre], w_in[:, n_pre + FOX_HEADS:], w_in[:, n_pre:n_pre + FOX_HEADS],
         jnp.zeros((D_MODEL, LANES - FOX_HEADS), F32)], axis=1).astype(BF16)
    hid = jnp.arange(FOX_W) // FOX_HD
    gbd = jnp.where(hid[:, None] == hid[None, :], 1.0 / FOX_HD, 0.0).astype(BF16)
    w_r = jnp.concatenate([w_router_group, w_router_expert], axis=-1)
    w_r = jnp.pad(w_r, ((0, 0), (0, 0), (0, LANES - w_r.shape[-1])))
    wr_hi = w_r.astype(BF16)
    wr_lo = (w_r - wr_hi.astype(F32)).astype(BF16)
    b_r = jnp.concatenate([b_router_group, b_router_expert], axis=-1)
    b_r = jnp.pad(b_r, ((0, 0), (0, LANES - b_r.shape[-1])))
    w_t = jnp.swapaxes(jnp.concatenate([w_router_expert, w_router_group], axis=-1), 1, 2)
    w_t = jnp.pad(w_t, ((0, 0), (0, ROUTE_ROWS - w_t.shape[1]), (0, 0)))
    wt_hi = w_t.astype(BF16)
    wt_lo = (w_t - wt_hi.astype(F32)).astype(BF16)
    b_t = jnp.concatenate([b_router_expert, b_router_group], axis=-1)
    b_t = jnp.pad(b_t, ((0, 0), (0, ROUTE_ROWS - b_t.shape[-1])))[:, :, None]
    log_gamma = jnp.log(1.0 - 2.0 ** (-5.0 - jnp.arange(RET_HEADS, dtype=F32)))
    p = {
        "norm_mix": norm_mix.reshape(depth, 1, D_MODEL),
        "norm_ffn": norm_ffn.reshape(depth, 1, D_MODEL),
        "w_in": w_in_r,
        "b_f": _pad_lanes(b_forget[0]),
        "q_gain": jnp.tile(fox_q_gain[0], FOX_HEADS).reshape(1, FOX_W),
        "k_gain": jnp.tile(fox_k_gain[0], FOX_HEADS).reshape(1, FOX_W),
        "gbd": gbd,
        "gn_gain": ret_gn_gain[0].reshape(1, RET_V_W),
        "lg_tab": jnp.broadcast_to(log_gamma[:, None, None], (RET_HEADS, 1, LANES)),
        "w_out": w_out_mix[0].astype(BF16),
        "w_pw1": w_pw1[0].astype(BF16),
        "b_pw1": b_pw1[0].reshape(1, -1),
        "w_dw": jnp.pad(w_dw[0], ((0, HIST_ROWS - CONV_W), (0, 0))),
        "b_dw": b_dw[0].reshape(1, -1),
        "ln_g": conv_ln_g[0].reshape(1, -1),
        "ln_b": conv_ln_b[0].reshape(1, -1),
        "w_pw2": w_pw2[0].astype(BF16),
        "wr_hi": wr_hi,
        "wr_lo": wr_lo,
        "br": b_r.reshape(depth, 1, LANES),
        "wt_hi": wt_hi,
        "wt_lo": wt_lo,
        "bt": b_t,
        "wg": w_exp_gate.reshape(depth, N_EXPERTS, D_MODEL, D_EXPERT),
        "wu": w_exp_up.reshape(depth, N_EXPERTS, D_MODEL, D_EXPERT),
        "wd": w_exp_down.reshape(depth, N_EXPERTS, D_EXPERT, D_MODEL),
    }

    hist_pad = HIST_ROWS - (CONV_W - 1)
    yp, fk_p, fv_p, lf_p, rs_p, u_p = _run_group(
        x_prompt.reshape(bp * t, D_MODEL), p, lambda x, layer: _moe_sparse(x, p, layer, 512), seq=t, tm=512,
        fox_hist=None,
        ret_state=jnp.zeros((bp, RET_HEADS, RET_DK, RET_DV), F32),
        conv_hist=jnp.zeros((bp, HIST_ROWS, D_MODEL), F32),
        pos=jnp.arange(t), tq=256, tk=256, ret_c=256, conv_tt=256)

    ns = bs * l
    ys, fk_s, fv_s, lf_s, rs_s, u_s = _run_group(
        x_sample.reshape(ns, D_MODEL), p, lambda x, layer: _moe_dense(x, p, layer, ns), seq=l, tm=ns,
        fox_hist=(cache_fox_k[0].reshape(bs, past, FOX_W), cache_fox_v[0].reshape(bs, past, FOX_W),
                  cache_fox_logf[0]),
        ret_state=state_ret[0],
        conv_hist=jnp.pad(cache_conv[0], ((0, 0), (hist_pad, 0), (0, 0))),
        pos=past + (jnp.arange(ns) % l), tq=l, tk=-(-(past + l) // CUM_BLOCK) * CUM_BLOCK, ret_c=l, conv_tt=l)

    conv_p = u_p[:, t - (CONV_W - 1):][None]
    conv_s = jnp.concatenate([cache_conv[0], u_s], axis=1)[:, l:][None]
    return (yp, ys, fk_p, fv_p, lf_p, rs_p, conv_p, fk_s, fv_s, lf_s, rs_s, conv_s)
```

```python
import functools
import math

import jax
import jax.numpy as jnp
import numpy as np
from jax import lax
from jax.experimental import pallas as pl
from jax.experimental.pallas import tpu as pltpu

F32 = jnp.float32
BF16 = jnp.bfloat16

D_MODEL = 1024
FOX_HEADS = 8
FOX_HD = 64
RET_HEADS = 4
RET_DK = 64
RET_DV = 128
ROPE_BASE = 10000.0
CONV_W = 31
N_GROUPS = 4
EXP_PER_GROUP = 8
N_EXPERTS = N_GROUPS * EXP_PER_GROUP
D_EXPERT = 256
EPS = 1e-6
FOX_W = FOX_HEADS * FOX_HD
RET_QK_W = RET_HEADS * RET_DK
RET_V_W = RET_HEADS * RET_DV

LANES = 128
HIST_ROWS = 32
ROUTER_LANE0 = N_GROUPS
MASKED = -1e30
VMEM_LIMIT = 56 * 1024 * 1024
CUM_BLOCK = 256
MOE_TM = 256
PLAN_TB = 512
ROUTE_ROWS = 48

C_FQ, C_FK, C_FV = 0, FOX_W, 2 * FOX_W
C_RQ = 3 * FOX_W
C_RK = C_RQ + RET_QK_W
C_RV = C_RK + RET_QK_W
C_RG = C_RV + RET_V_W
C_FF = C_RG + RET_V_W
MIX_COLS = C_FF + LANES


def _cparams(*sem):
    return pltpu.CompilerParams(dimension_semantics=sem, vmem_limit_bytes=VMEM_LIMIT)


def _full(shape):
    n = len(shape)
    return pl.BlockSpec(shape, lambda *_: (0,) * n)


def _rms(x, g):
    ms = jnp.mean(x * x, axis=-1, keepdims=True)
    return x * lax.rsqrt(ms + EPS) * g


def _sigmoid(x):
    return 1.0 / (1.0 + jnp.exp(-x))


def _dot(a, b):
    return jnp.dot(a, b, preferred_element_type=F32)


def _dot_nt(a, b):
    return lax.dot_general(a, b, (((1,), (1,)), ((), ())), preferred_element_type=F32)


def _dot_tn(a, b):
    return lax.dot_general(a, b, (((0,), (0,)), ((), ())), preferred_element_type=F32)


def _proj0_kernel(x_ref, g_ref, w_ref, bf_ref, qg_ref, kg_ref, gbd_ref, cos_ref, sin_ref,
                  q_ref, k_ref, v_ref, lf_ref, rq_ref, rk_ref, rv_ref, rg_ref):
    h = _rms(x_ref[...], g_ref[...]).astype(BF16)

    def seg(a, b):
        return _dot(h, w_ref[:, a:b])

    gbd = gbd_ref[...]

    def head_rms(y, gain):
        ms = _dot((y * y).astype(BF16), gbd)
        return y * lax.rsqrt(ms + EPS) * gain

    q_ref[...] = (head_rms(seg(C_FQ, C_FK), qg_ref[...]) * (FOX_HD ** -0.5)).astype(BF16)
    k_ref[...] = head_rms(seg(C_FK, C_FV), kg_ref[...])
    v_ref[...] = seg(C_FV, C_RQ)

    z = seg(C_FF, MIX_COLS) + bf_ref[...]
    logf = jnp.minimum(z, 0.0) - jnp.log(1.0 + jnp.exp(-jnp.abs(z)))
    lane = lax.broadcasted_iota(jnp.int32, (1, LANES), 1)
    lf_ref[...] = jnp.where(lane < FOX_HEADS, logf, 0.0)

    cos = cos_ref[...]
    sin = sin_ref[...]
    first_half = (lane % RET_DK) < (RET_DK // 2)

    def rotary(y):
        outs = []
        for s in range(y.shape[1] // LANES):
            ys = y[:, s * LANES:(s + 1) * LANES]
            rot = jnp.where(first_half, pltpu.roll(ys, LANES - RET_DK // 2, 1),
                            pltpu.roll(ys, RET_DK // 2, 1))
            outs.append(ys * cos + rot * sin)
        return jnp.concatenate(outs, axis=1)

    rq_ref[...] = rotary(seg(C_RQ, C_RK)).astype(BF16)
    rk_ref[...] = (rotary(seg(C_RK, C_RV)) * (RET_DK ** -0.5)).astype(BF16)
    rv_ref[...] = seg(C_RV, C_RG).astype(BF16)
    rg_ref[...] = seg(C_RG, C_FF).astype(BF16)


def _proj0(x, g, w, bf, qg, kg, gbd, cos, sin, tm):
    n = x.shape[0]
    nper = cos.shape[0] // tm
    row = lambda width: pl.BlockSpec((tm, width), lambda i: (i, 0))
    tab = pl.BlockSpec((tm, LANES), lambda i: (i % nper, 0))
    outs = [(FOX_W, BF16), (FOX_W, F32), (FOX_W, F32), (LANES, F32),
            (RET_QK_W, BF16), (RET_QK_W, BF16), (RET_V_W, BF16), (RET_V_W, BF16)]
    return pl.pallas_call(
        _proj0_kernel,
        grid=(n // tm,),
        in_specs=[row(D_MODEL), _full((1, D_MODEL)), _full((D_MODEL, MIX_COLS)), _full((1, LANES)),
                  _full((1, FOX_W)), _full((1, FOX_W)), _full((FOX_W, FOX_W)), tab, tab],
        out_specs=[row(wd) for wd, _ in outs],
        out_shape=[jax.ShapeDtypeStruct((n, wd), dt) for wd, dt in outs],
        compiler_params=_cparams("parallel"),
        name="proj0",
    )(x, g, w, bf, qg, kg, gbd, cos, sin)


def _cum_kernel(lf_ref, ccol_ref, crow_ref, *, t, cb):
    r = lax.broadcasted_iota(jnp.int32, (cb, cb), 0)
    c = lax.broadcasted_iota(jnp.int32, (cb, cb), 1)
    tri = jnp.where(r >= c, 1.0, 0.0).astype(BF16)
    carry = jnp.zeros((1, LANES), F32)
    for blk in range(t // cb):
        a = lf_ref[0, blk * cb:(blk + 1) * cb, :]
        a1 = a.astype(BF16)
        r1 = a - a1.astype(F32)
        a2 = r1.astype(BF16)
        a3 = (r1 - a2.astype(F32)).astype(BF16)
        cc = (_dot(tri, a1) + _dot(tri, a2)) + _dot(tri, a3) + carry
        ccol_ref[0, blk * cb:(blk + 1) * cb, :] = cc
        crow_ref[0, :, blk * cb:(blk + 1) * cb] = cc.T[0:FOX_HEADS, :]
        carry = cc[cb - 1:cb, :]


def _cumsum_logf(lf, cb):
    b, t, _ = lf.shape
    return pl.pallas_call(
        functools.partial(_cum_kernel, t=t, cb=cb),
        grid=(b,),
        in_specs=[pl.BlockSpec((1, t, LANES), lambda i: (i, 0, 0))],
        out_specs=[pl.BlockSpec((1, t, LANES), lambda i: (i, 0, 0)),
                   pl.BlockSpec((1, FOX_HEADS, t), lambda i: (i, 0, 0))],
        out_shape=[jax.ShapeDtypeStruct((b, t, LANES), F32),
                   jax.ShapeDtypeStruct((b, FOX_HEADS, t), F32)],
        compiler_params=_cparams("parallel"),
        name="cumsum_logf",
    )(lf)


def _fox_kernel(q_ref, k_ref, v_ref, cq_ref, ck_ref, o_ref, *, tq, tk, q_off):
    hp = pl.program_id(1)
    i = pl.program_id(2)
    q0 = q_off + i * tq
    nfull = (q0 + 1) // tk
    nch = (q0 + tq + tk - 1) // tk
    lane = lax.broadcasted_iota(jnp.int32, (1, LANES), 1)
    q = q_ref[0]
    cq_all = cq_ref[0]
    qpos = q0 + lax.broadcasted_iota(jnp.int32, (tq, 1), 0)
    kiota = lax.broadcasted_iota(jnp.int32, (1, tk), 1)
    qms = [jnp.where((lane // FOX_HD) == h2, q, jnp.zeros_like(q)) for h2 in range(2)]
    cqs = [jnp.sum(jnp.where(lane == 2 * hp + h2, cq_all, 0.0), axis=-1, keepdims=True) for h2 in range(2)]

    def step(j, carry, masked):
        start = pl.multiple_of(j * tk, tk)
        kj = k_ref[0, pl.ds(start, tk), :].astype(BF16)
        vj = v_ref[0, pl.ds(start, tk), :].astype(BF16)
        new = []
        for h2 in range(2):
            m, l, acc = carry[h2]
            s = _dot_nt(qms[h2], kj)
            s = s + cqs[h2] - ck_ref[0, 2 * hp + h2, j]
            if masked:
                s = jnp.where(j * tk + kiota <= qpos, s, MASKED)
            m_new = jnp.maximum(m, jnp.max(s, axis=-1, keepdims=True))
            alpha = jnp.exp(m - m_new)
            p = jnp.exp(s - m_new)
            l = alpha * l + jnp.sum(p, axis=-1, keepdims=True)
            acc = alpha * acc + _dot(p.astype(BF16), vj)
            new.append((m_new, l, acc))
        return tuple(new)

    one = (jnp.full((tq, 1), MASKED, F32), jnp.zeros((tq, 1), F32), jnp.zeros((tq, LANES), F32))
    carry = lax.fori_loop(0, nfull, functools.partial(step, masked=False), (one, one))
    carry = lax.fori_loop(nfull, nch, functools.partial(step, masked=True), carry)
    outs = [acc / l for _, l, acc in carry]
    o_ref[0] = jnp.where(lane < FOX_HD, outs[0], outs[1]).astype(BF16)


def _fox_attention(q, k, v, ccol, crow, tq, tk, q_off):
    b, tqs, _ = q.shape
    tks = k.shape[1]
    nck = tks // tk
    crow5 = crow.reshape(b, FOX_HEADS, nck, 1, tk)
    qb0 = q_off // tq
    return pl.pallas_call(
        functools.partial(_fox_kernel, tq=tq, tk=tk, q_off=q_off),
        grid=(b, FOX_HEADS // 2, tqs // tq),
        in_specs=[pl.BlockSpec((1, tq, LANES), lambda bi, hp, i: (bi, i, hp)),
                  pl.BlockSpec((1, tks, LANES), lambda bi, hp, i: (bi, 0, hp)),
                  pl.BlockSpec((1, tks, LANES), lambda bi, hp, i: (bi, 0, hp)),
                  pl.BlockSpec((1, tq, LANES), lambda bi, hp, i: (bi, qb0 + i, 0)),
                  pl.BlockSpec((1, FOX_HEADS, nck, 1, tk), lambda bi, hp, i: (bi, 0, 0, 0, 0))],
        out_specs=pl.BlockSpec((1, tq, LANES), lambda bi, hp, i: (bi, i, hp)),
        out_shape=jax.ShapeDtypeStruct((b, tqs, FOX_W), BF16),
        compiler_params=_cparams("parallel", "parallel", "parallel"),
        name="fox_attention",
    )(q, k, v, ccol, crow5)


FOX_VROWS = 80


def _bf16_pieces(c):
    hi = c.astype(BF16).astype(F32)
    r = c - hi
    mid = r.astype(BF16).astype(F32)
    lo = (r - mid).astype(BF16).astype(F32)
    return hi, mid, lo


def _fox_prep_kernel(q_ref, k_ref, v_ref, c_ref, qa_ref, ka_ref, vt_ref, *, t, rc):
    hp = pl.program_id(1)
    lane = lax.broadcasted_iota(jnp.int32, (1, LANES), 1)
    ones = (1.0, 1.0, 1.0)

    def augment(xh, own, other):
        out = jnp.where(lane < FOX_HD, xh, 0.0)
        for piece in range(3):
            out = jnp.where(lane == FOX_HD + piece, own[piece], out)
            out = jnp.where(lane == FOX_HD + 3 + piece, other[piece], out)
        return out.astype(BF16)

    extra = jnp.where(lax.broadcasted_iota(jnp.int32, (FOX_VROWS - FOX_HD, rc), 0) == 0, 1.0, 0.0)
    for ch in range(t // rc):
        rows = slice(ch * rc, (ch + 1) * rc)
        qf = q_ref[0, rows, :].astype(F32)
        kf = k_ref[0, rows, :]
        call = c_ref[0, rows, :]
        vt = v_ref[0, rows, :].T
        for h2 in range(2):
            c = jnp.sum(jnp.where(lane == 2 * hp + h2, call, 0.0), axis=-1, keepdims=True)
            hi, mid, lo = _bf16_pieces(c)
            qh = qf if h2 == 0 else pltpu.roll(qf, FOX_HD, 1)
            kh = kf if h2 == 0 else pltpu.roll(kf, FOX_HD, 1)
            qa_ref[0, h2, rows, :] = augment(qh, (hi, mid, lo), ones)
            ka_ref[0, h2, rows, :] = augment(kh, ones, (-hi, -mid, -lo))
            vt_ref[0, h2, ch] = jnp.concatenate([vt[h2 * FOX_HD:(h2 + 1) * FOX_HD, :], extra], axis=0).astype(BF16)


def _fox_prep(q, k, v, ccol, rc):
    b, t, _ = q.shape
    pair = pl.BlockSpec((1, t, LANES), lambda bi, hp: (bi, 0, hp))
    aug = pl.BlockSpec((1, 2, t, LANES), lambda bi, hp: (bi, hp, 0, 0))
    return pl.pallas_call(
        functools.partial(_fox_prep_kernel, t=t, rc=rc),
        grid=(b, FOX_HEADS // 2),
        in_specs=[pair, pair, pair, pl.BlockSpec((1, t, LANES), lambda bi, hp: (bi, 0, 0))],
        out_specs=[aug, aug, pl.BlockSpec((1, 2, t // rc, FOX_VROWS, rc), lambda bi, hp: (bi, hp, 0, 0, 0))],
        out_shape=[jax.ShapeDtypeStruct((b, FOX_HEADS, t, LANES), BF16),
                   jax.ShapeDtypeStruct((b, FOX_HEADS, t, LANES), BF16),
                   jax.ShapeDtypeStruct((b, FOX_HEADS, t // rc, FOX_VROWS, rc), BF16)],
        compiler_params=_cparams("parallel", "parallel"),
        name="fox_prep",
    )(q, k, v, ccol)


def _fox_t_kernel(qa_ref, ka_ref, vt_ref, o_ref, s_scr, acc_scr, *, tq, tk):
    i = pl.program_id(1)
    q0 = i * tq
    nfull = (q0 + 1) // tk
    nch = (q0 + tq + tk - 1) // tk
    qpos = q0 + lax.broadcasted_iota(jnp.int32, (1, tq), 1)
    kiota = lax.broadcasted_iota(jnp.int32, (tk, 1), 0)
    heads = range(FOX_HEADS)

    def scores(j, mx, masked):
        start = pl.multiple_of(j * tk, tk)
        out = []
        for h in heads:
            st = _dot_nt(ka_ref[0, h, pl.ds(start, tk), :], qa_ref[0, h])
            if masked:
                st = jnp.where(start + kiota <= qpos, st, MASKED)
            s_scr[h, j] = st
            out.append(jnp.maximum(mx[h], jnp.max(st, axis=0, keepdims=True)))
        return tuple(out)

    mx = tuple(jnp.full((1, tq), MASKED, F32) for _ in heads)
    mx = lax.fori_loop(0, nfull, functools.partial(scores, masked=False), mx)
    mx = lax.fori_loop(nfull, nch, functools.partial(scores, masked=True), mx)

    acc_scr[...] = jnp.zeros_like(acc_scr)

    def weigh(j, carry):
        for h in heads:
            p = jnp.exp(s_scr[h, j] - mx[h]).astype(BF16)
            acc_scr[h] += _dot(vt_ref[0, h, j], p)
        return carry

    lax.fori_loop(0, nch, weigh, 0)
    for hp in range(FOX_HEADS // 2):
        outs = []
        for h in (2 * hp, 2 * hp + 1):
            acc = acc_scr[h]
            outs.append(acc[0:FOX_HD, :] / acc[FOX_HD:FOX_HD + 1, :])
        o_ref[0, :, hp * LANES:(hp + 1) * LANES] = jnp.concatenate(outs, axis=0).T.astype(BF16)


def _fox_attention_t(qa, ka, vt, tq):
    b, _, t, _ = qa.shape
    nck, tk = vt.shape[2], vt.shape[4]
    return pl.pallas_call(
        functools.partial(_fox_t_kernel, tq=tq, tk=tk),
        grid=(b, t // tq),
        in_specs=[pl.BlockSpec((1, FOX_HEADS, tq, LANES), lambda bi, i: (bi, 0, i, 0)),
                  pl.BlockSpec((1, FOX_HEADS, t, LANES), lambda bi, i: (bi, 0, 0, 0)),
                  pl.BlockSpec((1, FOX_HEADS, nck, FOX_VROWS, tk), lambda bi, i: (bi, 0, 0, 0, 0))],
        out_specs=pl.BlockSpec((1, tq, FOX_W), lambda bi, i: (bi, i, 0)),
        out_shape=jax.ShapeDtypeStruct((b, t, FOX_W), BF16),
        scratch_shapes=[pltpu.VMEM((FOX_HEADS, nck, tk, tq), F32),
                        pltpu.VMEM((FOX_HEADS, FOX_VROWS, tq), F32)],
        compiler_params=_cparams("parallel", "arbitrary"),
        name="fox_attention_t",
    )(qa, ka, vt)


def _ret_kernel(lg_ref, rq_ref, rk_ref, rv_ref, rg_ref, s0_ref, gn_ref, o_ref, sfin_ref, s_scr, d_scr, *, c):
    first = (pl.program_id(0) == 0) & (pl.program_id(1) == 0)
    ci = pl.program_id(1)
    ii = lax.broadcasted_iota(jnp.int32, (c, 1), 0).astype(F32)
    lane = lax.broadcasted_iota(jnp.int32, (1, LANES), 1)

    @pl.when(first)
    def _():
        jj = lax.broadcasted_iota(jnp.int32, (1, c), 1).astype(F32)
        diff = ii - jj
        for h in range(RET_HEADS):
            lg = lg_ref[h][:, 0:1]
            d_scr[h] = jnp.where(diff >= 0.0, jnp.exp(lg * jnp.maximum(diff, 0.0)), 0.0)

    @pl.when(ci == 0)
    def _():
        zero = jnp.zeros((RET_DK, RET_DV), F32)
        for h in range(RET_HEADS):
            lo, hi = (s0_ref[0, h], zero) if h % 2 == 0 else (zero, s0_ref[0, h])
            s_scr[h, 0:RET_DK, :] = lo
            s_scr[h, RET_DK:2 * RET_DK, :] = hi

    for h in range(RET_HEADS):
        lg = lg_ref[h][:, 0:1]
        pair = slice((h // 2) * LANES, (h // 2 + 1) * LANES)
        mine = slice(h * RET_DV, (h + 1) * RET_DV)
        inhead = (lane // RET_DK) == (h % 2)
        q = rq_ref[0, :, pair]
        k = rk_ref[0, :, pair]
        qm = jnp.where(inhead, q, jnp.zeros_like(q))
        km = jnp.where(inhead, k, jnp.zeros_like(k))
        v = rv_ref[0, :, mine]
        scores = _dot_nt(qm, km) * d_scr[h]
        inner = _dot(scores.astype(BF16), v)
        s_prev = s_scr[h]
        cross = _dot(qm, s_prev.astype(BF16)) * jnp.exp(lg * (ii + 1.0))
        y = inner + cross
        k_dec = (km.astype(F32) * jnp.exp(lg * (c - 1.0 - ii))).astype(BF16)
        s_new = jnp.exp(lg * float(c)) * s_prev + _dot_tn(k_dec, v)
        s_scr[h] = s_new

        mu = jnp.mean(y, axis=-1, keepdims=True)
        yc = y - mu
        var = jnp.mean(yc * yc, axis=-1, keepdims=True)
        yn = yc * lax.rsqrt(var + EPS) * gn_ref[:, mine]
        g = rg_ref[0, :, mine].astype(F32)
        o_ref[0, :, mine] = ((g * _sigmoid(g)) * yn).astype(BF16)
        off = (h % 2) * RET_DK
        sfin_ref[0, h] = s_new[off:off + RET_DK, :]


def _retention(lg_tab, rq, rk, rv, rg, s0, gn, c):
    b, t, _ = rq.shape
    qk_spec = pl.BlockSpec((1, c, RET_QK_W), lambda bi, ci: (bi, ci, 0))
    v_spec = pl.BlockSpec((1, c, RET_V_W), lambda bi, ci: (bi, ci, 0))
    st_spec = pl.BlockSpec((1, RET_HEADS, RET_DK, RET_DV), lambda bi, ci: (bi, 0, 0, 0))
    return pl.pallas_call(
        functools.partial(_ret_kernel, c=c),
        grid=(b, t // c),
        in_specs=[_full((RET_HEADS, 1, LANES)), qk_spec, qk_spec, v_spec, v_spec, st_spec,
                  _full((1, RET_V_W))],
        out_specs=[v_spec, st_spec],
        out_shape=[jax.ShapeDtypeStruct((b, t, RET_V_W), BF16),
                   jax.ShapeDtypeStruct((b, RET_HEADS, RET_DK, RET_DV), F32)],
        scratch_shapes=[pltpu.VMEM((RET_HEADS, 2 * RET_DK, RET_DV), F32), pltpu.VMEM((RET_HEADS, c, c), F32)],
        compiler_params=_cparams("arbitrary", "arbitrary"),
        name="retention",
    )(lg_tab, rq, rk, rv, rg, s0, gn)


def _out0_kernel(x_ref, of_ref, or_ref, w_ref, y_ref):
    mix = _dot(of_ref[...], w_ref[0:FOX_W, :]) + _dot(or_ref[...], w_ref[FOX_W:FOX_W + RET_V_W, :])
    y_ref[...] = x_ref[...] + mix


def _out0(x, o_fox, o_ret, w, tm):
    n = x.shape[0]
    row = lambda width: pl.BlockSpec((tm, width), lambda i: (i, 0))
    return pl.pallas_call(
        _out0_kernel,
        grid=(n // tm,),
        in_specs=[row(D_MODEL), row(FOX_W), row(RET_V_W), _full((FOX_W + RET_V_W, D_MODEL))],
        out_specs=row(D_MODEL),
        out_shape=jax.ShapeDtypeStruct((n, D_MODEL), F32),
        compiler_params=_cparams("parallel"),
        name="out_proj0",
    )(x, o_fox, o_ret, w)


def _router_kernel(x_ref, g_ref, wh_ref, wl_ref, b_ref, h_ref, gates_ref):
    hf = _rms(x_ref[...], g_ref[...])
    hh = hf.astype(BF16)
    hl = (hf - hh.astype(F32)).astype(BF16)
    h_ref[...] = hh
    wh = wh_ref[...]
    logits = _dot(hh, wh) + (_dot(hl, wh) + _dot(hh, wl_ref[...])) + b_ref[...]

    lane = lax.broadcasted_iota(jnp.int32, (1, LANES), 1)
    lanef = lane.astype(F32)
    ninf = -jnp.inf
    is_grp = lane < N_GROUPS
    gl = jnp.where(is_grp, logits, ninf)
    gmax = jnp.max(gl, axis=-1, keepdims=True)
    grp = jnp.min(jnp.where(gl == gmax, lanef, 1e9), axis=-1, keepdims=True)
    p_grp = 1.0 / jnp.sum(jnp.exp(gl - gmax), axis=-1, keepdims=True)

    is_exp = (lane >= ROUTER_LANE0) & (lane < ROUTER_LANE0 + N_EXPERTS)
    lane_grp = ((lane - ROUTER_LANE0) // EXP_PER_GROUP).astype(F32)
    em = jnp.where(is_exp & (lane_grp == grp), logits, ninf)
    v1 = jnp.max(em, axis=-1, keepdims=True)
    i1 = jnp.min(jnp.where(em == v1, lanef, 1e9), axis=-1, keepdims=True)
    em2 = jnp.where(lanef == i1, ninf, em)
    v2 = jnp.max(em2, axis=-1, keepdims=True)
    i2 = jnp.min(jnp.where(em2 == v2, lanef, 1e9), axis=-1, keepdims=True)
    t = jnp.exp(v2 - v1)
    w1 = (1.0 / (1.0 + t)) * p_grp
    w2 = (t / (1.0 + t)) * p_grp
    gates_ref[...] = jnp.where(lanef == i1, w1, 0.0) + jnp.where(lanef == i2, w2, 0.0)


def _router(x, g, wh, wl, b, tm):
    n = x.shape[0]
    row = lambda width: pl.BlockSpec((tm, width), lambda i: (i, 0))
    return pl.pallas_call(
        _router_kernel,
        grid=(n // tm,),
        in_specs=[row(D_MODEL), _full((1, D_MODEL)), _full((D_MODEL, LANES)), _full((D_MODEL, LANES)),
                  _full((1, LANES))],
        out_specs=[row(D_MODEL), row(LANES)],
        out_shape=[jax.ShapeDtypeStruct((n, D_MODEL), BF16), jax.ShapeDtypeStruct((n, LANES), F32)],
        compiler_params=_cparams("parallel"),
        name="router",
    )(x, g, wh, wl, b)


def _moe_kernel(h_ref, gates_ref, x_ref, wg_ref, wu_ref, wd_ref, y_ref):
    e = pl.program_id(1)

    @pl.when(e == 0)
    def _():
        y_ref[...] = jnp.zeros_like(y_ref)

    h = h_ref[...]
    a = _dot(h, wg_ref[0].astype(BF16))
    u = _dot(h, wu_ref[0].astype(BF16))
    lane = lax.broadcasted_iota(jnp.int32, (1, LANES), 1)
    gate = jnp.sum(jnp.where(lane == e + ROUTER_LANE0, gates_ref[...], 0.0), axis=-1, keepdims=True)
    hid = (a * _sigmoid(a)) * u * gate
    y_ref[...] += _dot(hid.astype(BF16), wd_ref[0].astype(BF16))

    @pl.when(e == pl.num_programs(1) - 1)
    def _():
        y_ref[...] += x_ref[...]


def _moe(h, gates, x, wg, wu, wd, layer, tm):
    n = x.shape[0]
    first = layer * N_EXPERTS
    row = lambda width: pl.BlockSpec((tm, width), lambda i, e: (i, 0))
    return pl.pallas_call(
        _moe_kernel,
        grid=(n // tm, N_EXPERTS),
        in_specs=[row(D_MODEL), row(LANES), row(D_MODEL),
                  pl.BlockSpec((1, D_MODEL, D_EXPERT), lambda i, e: (first + e, 0, 0)),
                  pl.BlockSpec((1, D_MODEL, D_EXPERT), lambda i, e: (first + e, 0, 0)),
                  pl.BlockSpec((1, D_EXPERT, D_MODEL), lambda i, e: (first + e, 0, 0))],
        out_specs=row(D_MODEL),
        out_shape=jax.ShapeDtypeStruct((n, D_MODEL), F32),
        compiler_params=_cparams("parallel", "arbitrary"),
        name="moe_experts",
    )(h, gates, x, wg, wu, wd)


def _router_t_kernel(x_ref, g_ref, wth_ref, wtl_ref, bt_ref, h_ref, route_ref, wcol_ref):
    hf = _rms(x_ref[...], g_ref[...])
    _to_slabs(h_ref, hf)
    hh = hf.astype(BF16)
    hl = (hf - hh.astype(F32)).astype(BF16)
    wth = wth_ref[...]
    logits = _dot_nt(wth, hh) + (_dot_nt(wth, hl) + _dot_nt(wtl_ref[...], hh)) + bt_ref[...]

    row = lax.broadcasted_iota(jnp.int32, (ROUTE_ROWS, 1), 0)
    rowf = row.astype(F32)
    ninf = -jnp.inf
    is_grp = (row >= N_EXPERTS) & (row < N_EXPERTS + N_GROUPS)
    gl = jnp.where(is_grp, logits, ninf)
    gmax = jnp.max(gl, axis=0, keepdims=True)
    grp = jnp.min(jnp.where(gl == gmax, rowf, 1e9), axis=0, keepdims=True) - float(N_EXPERTS)
    p_grp = 1.0 / jnp.sum(jnp.exp(gl - gmax), axis=0, keepdims=True)

    row_grp = (row // EXP_PER_GROUP).astype(F32)
    em = jnp.where((row < N_EXPERTS) & (row_grp == grp), logits, ninf)
    v1 = jnp.max(em, axis=0, keepdims=True)
    i1 = jnp.min(jnp.where(em == v1, rowf, 1e9), axis=0, keepdims=True)
    em2 = jnp.where(rowf == i1, ninf, em)
    v2 = jnp.max(em2, axis=0, keepdims=True)
    i2 = jnp.min(jnp.where(em2 == v2, rowf, 1e9), axis=0, keepdims=True)
    t = jnp.exp(v2 - v1)
    w1 = (1.0 / (1.0 + t)) * p_grp
    w2 = (t / (1.0 + t)) * p_grp

    tm = logits.shape[1]
    r128 = lax.broadcasted_iota(jnp.int32, (LANES, 1), 0)
    rt = (jnp.where(r128 == 0, i1, 0.0) + jnp.where(r128 == 1, i2, 0.0)
          + jnp.where(r128 == 2, w1, 0.0) + jnp.where(r128 == 3, w2, 0.0))
    route_ref[...] = rt[0:8, :]
    wcol_ref[...] = rt.T


def _router_t(x, g, wth, wtl, bt, tm):
    n = x.shape[0]
    row = lambda width: pl.BlockSpec((tm, width), lambda i: (i, 0))
    return pl.pallas_call(
        _router_t_kernel,
        grid=(n // tm,),
        in_specs=[row(D_MODEL), _full((1, D_MODEL)), _full((ROUTE_ROWS, D_MODEL)),
                  _full((ROUTE_ROWS, D_MODEL)), _full((ROUTE_ROWS, 1))],
        out_specs=[pl.BlockSpec((tm * SLAB, LANES), lambda i: (i, 0)), pl.BlockSpec((8, tm), lambda i: (0, i)),
                   row(LANES)],
        out_shape=[jax.ShapeDtypeStruct((n * SLAB, LANES), F32), jax.ShapeDtypeStruct((8, n), F32),
                   jax.ShapeDtypeStruct((n, LANES), F32)],
        compiler_params=_cparams("parallel"),
        name="router_t",
    )(x, g, wth, wtl, bt)


def _plan_kernel(route_ref, pos_ref, te_ref, nt_ref, pad_ref, *, n, tb, tm):
    nb = n // tb
    erow = lax.broadcasted_iota(jnp.int32, (N_EXPERTS, 1), 0).astype(F32)

    def picks(blk):
        i1 = route_ref[0:1, blk * tb:(blk + 1) * tb]
        i2 = route_ref[1:2, blk * tb:(blk + 1) * tb]
        return erow == i1, erow == i2

    acc = jnp.zeros((N_EXPERTS, tb), F32)
    for blk in range(nb):
        e1, e2 = picks(blk)
        acc = acc + jnp.where(e1, 1.0, 0.0) + jnp.where(e2, 1.0, 0.0)
    counts = jnp.sum(acc, axis=1, keepdims=True).astype(jnp.int32)
    assert tm & (tm - 1) == 0
    ntile = jnp.right_shift(counts + (tm - 1), tm.bit_length() - 1).astype(F32)
    lane = lax.broadcasted_iota(jnp.int32, (1, LANES), 1).astype(F32)
    ntile_row = jnp.sum(jnp.where(lane == erow, ntile, 0.0), axis=0, keepdims=True)
    tend = jnp.sum(jnp.where(lane <= erow, ntile_row, 0.0), axis=1, keepdims=True)
    off = (tend - ntile) * float(tm)
    ntot = jnp.sum(ntile, axis=0, keepdims=True)
    kk = jnp.minimum(lax.broadcasted_iota(jnp.int32, (1, 2 * LANES), 1).astype(F32), ntot - 1.0)
    te_ref[...] = jnp.sum(jnp.where(tend <= kk, 1.0, 0.0), axis=0, keepdims=True).astype(jnp.int32)
    nt_ref[...] = jnp.broadcast_to(ntot, (1, LANES)).astype(jnp.int32)
    pad0 = off + counts.astype(F32)
    pad_ref[...] = jnp.sum(jnp.where(lane == erow, pad0, 0.0), axis=0, keepdims=True).astype(jnp.int32)

    r = lax.broadcasted_iota(jnp.int32, (tb, tb), 0)
    c = lax.broadcasted_iota(jnp.int32, (tb, tb), 1)
    before = jnp.where(r < c, 1.0, 0.0).astype(BF16)
    carry = jnp.zeros((N_EXPERTS, 1), F32)
    for blk in range(nb):
        e1, e2 = picks(blk)
        mt = jnp.where(e1, 1.0, 0.0) + jnp.where(e2, 1.0, 0.0)
        slot = _dot(mt.astype(BF16), before) + (carry + off)
        pos_ref[blk, 0:1, :] = jnp.sum(jnp.where(e1, slot, 0.0), axis=0, keepdims=True).astype(jnp.int32)
        pos_ref[blk, 1:2, :] = jnp.sum(jnp.where(e2, slot, 0.0), axis=0, keepdims=True).astype(jnp.int32)
        carry = carry + jnp.sum(mt, axis=1, keepdims=True)


def _plan(route, tb, tm):
    n = route.shape[1]
    vm = pl.BlockSpec(memory_space=pltpu.VMEM)
    return pl.pallas_call(
        functools.partial(_plan_kernel, n=n, tb=tb, tm=tm),
        in_specs=[vm],
        out_specs=[vm, vm, vm, vm],
        out_shape=[jax.ShapeDtypeStruct((n // tb, 2, tb), jnp.int32),
                   jax.ShapeDtypeStruct((1, 2 * LANES), jnp.int32),
                   jax.ShapeDtypeStruct((1, LANES), jnp.int32),
                   jax.ShapeDtypeStruct((1, LANES), jnp.int32)],
        compiler_params=pltpu.CompilerParams(vmem_limit_bytes=VMEM_LIMIT),
        name="moe_plan",
    )(route)


SLAB = D_MODEL // LANES


def _to_slabs(ref, y):
    m = y.shape[0]
    for s_ in range(SLAB):
        ref[pl.ds(s_, m, stride=SLAB), :] = y[:, s_ * LANES:(s_ + 1) * LANES]


def _slab_piece(ref, s_, m):
    return ref[pl.ds(s_, m, stride=SLAB), :]


def _from_slabs(ref, m):
    return jnp.concatenate([_slab_piece(ref, s_, m) for s_ in range(SLAB)], axis=1)


def _tok_copy(src, src_tok, dst, dst_tok, sem):
    return pltpu.make_async_copy(src.at[pl.ds(pl.multiple_of(src_tok * SLAB, SLAB), SLAB)],
                                 dst.at[pl.ds(pl.multiple_of(dst_tok * SLAB, SLAB), SLAB)], sem)


DMA_UNROLL = 8


def _dispatch_kernel(pos_ref, pad_ref, nt_ref, h_ref, xs_hbm, zero_scr, sem, zsem, *, ts, tm, ntmax):
    @pl.when(pl.program_id(0) == 0)
    def _():
        zero_scr[...] = jnp.zeros_like(zero_scr)

        def fill(first_slot):
            return pltpu.make_async_copy(
                zero_scr, xs_hbm.at[pl.ds(pl.multiple_of(first_slot * SLAB, SLAB), tm * SLAB)], zsem)

        for e in range(N_EXPERTS):
            fill(pad_ref[0, e]).start()
        for e in range(N_EXPERTS):
            fill(pad_ref[0, e]).wait()

        ntot = nt_ref[0, 0]

        def tail_start(k, carry):
            fill(k * tm).start()
            return carry

        def tail_wait(k, carry):
            fill(k * tm).wait()
            return carry

        lax.fori_loop(ntot, ntmax, tail_start, 0)
        lax.fori_loop(ntot, ntmax, tail_wait, 0)

    def issue(g, carry):
        for u in range(DMA_UNROLL):
            r = g * DMA_UNROLL + u
            for ch in range(2):
                _tok_copy(h_ref, r, xs_hbm, pos_ref[0, ch, r], sem).start(priority=ch)
        return carry

    def drain(g, carry):
        for _ in range(2 * DMA_UNROLL):
            _tok_copy(h_ref, 0, xs_hbm, 0, sem).wait()
        return carry

    lax.fori_loop(0, ts // DMA_UNROLL, issue, 0)
    lax.fori_loop(0, ts // DMA_UNROLL, drain, 0)


def _dispatch(pos, pad, nt, h, ts, tm):
    nb = pos.shape[0]
    n = h.shape[0] // SLAB
    ntmax = _sorted_rows(n) // tm + 1
    smem = pl.BlockSpec(memory_space=pltpu.SMEM)
    return pl.pallas_call(
        functools.partial(_dispatch_kernel, ts=ts, tm=tm, ntmax=ntmax),
        grid=(nb,),
        in_specs=[pl.BlockSpec((1, 2, ts), lambda i: (i, 0, 0), memory_space=pltpu.SMEM), smem, smem,
                  pl.BlockSpec((ts * SLAB, LANES), lambda i: (i, 0))],
        out_specs=pl.BlockSpec(memory_space=pl.ANY),
        out_shape=jax.ShapeDtypeStruct((ntmax * tm * SLAB, LANES), F32),
        scratch_shapes=[pltpu.VMEM((tm * SLAB, LANES), F32), pltpu.SemaphoreType.DMA,
                        pltpu.SemaphoreType.DMA],
        compiler_params=_cparams("arbitrary"),
        name="moe_dispatch",
    )(pos, pad, nt, h)


def _experts_kernel(te_ref, nt_ref, xs_ref, wg_ref, wu_ref, wd_ref, ys_ref, *, tm):
    used = pl.program_id(0) < nt_ref[0]

    @pl.when(used)
    def _():
        x = _from_slabs(xs_ref, tm).astype(BF16)
        a = _dot(x, wg_ref[0].astype(BF16))
        u = _dot(x, wu_ref[0].astype(BF16))
        hid = (a * _sigmoid(a)) * u
        _to_slabs(ys_ref, _dot(hid.astype(BF16), wd_ref[0].astype(BF16)))

    @pl.when(jnp.logical_not(used))
    def _():
        ys_ref[...] = jnp.zeros_like(ys_ref)


def _experts(te, nt, xs, wg, wu, wd, layer, tm):
    ntmax = xs.shape[0] // (tm * SLAB)
    first = layer * N_EXPERTS
    tile = pl.BlockSpec((tm * SLAB, LANES), lambda k, te, nt: (jnp.minimum(k, nt[0] - 1), 0))
    out_tile = pl.BlockSpec((tm * SLAB, LANES), lambda k, te, nt: (k, 0))
    return pl.pallas_call(
        functools.partial(_experts_kernel, tm=tm),
        grid_spec=pltpu.PrefetchScalarGridSpec(
            num_scalar_prefetch=2,
            grid=(ntmax,),
            in_specs=[tile,
                      pl.BlockSpec((1, D_MODEL, D_EXPERT), lambda k, te, nt: (first + te[k], 0, 0)),
                      pl.BlockSpec((1, D_MODEL, D_EXPERT), lambda k, te, nt: (first + te[k], 0, 0)),
                      pl.BlockSpec((1, D_EXPERT, D_MODEL), lambda k, te, nt: (first + te[k], 0, 0))],
            out_specs=out_tile),
        out_shape=jax.ShapeDtypeStruct(xs.shape, F32),
        compiler_params=_cparams("arbitrary"),
        name="moe_experts_sorted",
    )(te, nt, xs, wg, wu, wd)


def _combine_kernel(pos_ref, x_ref, w_ref, ys_hbm, y_ref, buf, sem, *, tc):
    def issue(g, carry):
        for u in range(DMA_UNROLL):
            r = g * DMA_UNROLL + u
            for ch in range(2):
                _tok_copy(ys_hbm, pos_ref[0, ch, r], buf.at[ch], r, sem).start(priority=ch)
        return carry

    def drain(g, carry):
        for _ in range(2 * DMA_UNROLL):
            _tok_copy(ys_hbm, 0, buf.at[0], 0, sem).wait()
        return carry

    lax.fori_loop(0, tc // DMA_UNROLL, issue, 0)
    lax.fori_loop(0, tc // DMA_UNROLL, drain, 0)
    w = w_ref[...]
    w1, w2 = w[:, 2:3], w[:, 3:4]
    for s_ in range(SLAB):
        cols = slice(s_ * LANES, (s_ + 1) * LANES)
        y_ref[:, cols] = x_ref[:, cols] + (w1 * _slab_piece(buf.at[0], s_, tc) + w2 * _slab_piece(buf.at[1], s_, tc))


def _combine(pos, x, wcol, ys, tc):
    n = x.shape[0]
    row = lambda width: pl.BlockSpec((tc, width), lambda i: (i, 0))
    return pl.pallas_call(
        functools.partial(_combine_kernel, tc=tc),
        grid=(n // tc,),
        in_specs=[pl.BlockSpec((1, 2, tc), lambda i: (i, 0, 0), memory_space=pltpu.SMEM),
                  row(D_MODEL), row(LANES), pl.BlockSpec(memory_space=pl.ANY)],
        out_specs=row(D_MODEL),
        out_shape=jax.ShapeDtypeStruct((n, D_MODEL), F32),
        scratch_shapes=[pltpu.VMEM((2, tc * SLAB, LANES), F32), pltpu.SemaphoreType.DMA],
        compiler_params=_cparams("arbitrary"),
        name="moe_combine",
    )(pos, x, wcol, ys)


def _pw1_kernel(x_ref, g_ref, w_ref, b_ref, u_ref):
    h = _rms(x_ref[...], g_ref[...]).astype(BF16)
    a = _dot(h, w_ref[:, 0:D_MODEL]) + b_ref[:, 0:D_MODEL]
    g = _dot(h, w_ref[:, D_MODEL:2 * D_MODEL]) + b_ref[:, D_MODEL:2 * D_MODEL]
    u_ref[...] = a * _sigmoid(g)


def _pw1(x, g, w, b, tm):
    n = x.shape[0]
    row = lambda width: pl.BlockSpec((tm, width), lambda i: (i, 0))
    return pl.pallas_call(
        _pw1_kernel,
        grid=(n // tm,),
        in_specs=[row(D_MODEL), _full((1, D_MODEL)), _full((D_MODEL, 2 * D_MODEL)), _full((1, 2 * D_MODEL))],
        out_specs=row(D_MODEL),
        out_shape=jax.ShapeDtypeStruct((n, D_MODEL), F32),
        compiler_params=_cparams("parallel"),
        name="conv_pw1_glu",
    )(x, g, w, b)


def _conv_kernel(u_ref, hist_ref, x_ref, wdw_ref, bdw_ref, lng_ref, lnb_ref, w2_ref, y_ref, ext_scr, sh_scr,
                 *, tt, rs):
    ti = pl.program_id(1)

    @pl.when(ti == 0)
    def _():
        ext_scr[0:HIST_ROWS, :] = hist_ref[0]

    ext_scr[HIST_ROWS:HIST_ROWS + tt, :] = u_ref[0]
    pad = HIST_ROWS - (CONV_W - 1)
    rows = tt + HIST_ROWS - 8
    for r in range(1, 8):
        sh_scr[r - 1, :, :] = ext_scr[r:r + rows, :]
    parts = []
    for r0 in range(0, tt, rs):
        acc = jnp.zeros((rs, D_MODEL), F32)
        for kk in range(CONV_W):
            a, r = divmod(kk + pad, 8)
            lo = r0 + 8 * a
            src = ext_scr[lo:lo + rs, :] if r == 0 else sh_scr[r - 1, lo:lo + rs, :]
            acc = acc + wdw_ref[kk:kk + 1, :] * src
        parts.append(acc)
    y = jnp.concatenate(parts, axis=0) + bdw_ref[...]
    mu = jnp.mean(y, axis=-1, keepdims=True)
    yc = y - mu
    var = jnp.mean(yc * yc, axis=-1, keepdims=True)
    yn = yc * lax.rsqrt(var + EPS) * lng_ref[...] + lnb_ref[...]
    z = yn * _sigmoid(yn)
    y_ref[0] = x_ref[0] + _dot(z.astype(BF16), w2_ref[...])
    if tt >= HIST_ROWS:
        ext_scr[0:HIST_ROWS, :] = ext_scr[tt:tt + HIST_ROWS, :]


def _conv_module(u, hist, x, wdw, bdw, lng, lnb, w2, tt):
    b, t, _ = u.shape
    rs = min(tt, 32)
    blk = pl.BlockSpec((1, tt, D_MODEL), lambda bi, ti: (bi, ti, 0))
    return pl.pallas_call(
        functools.partial(_conv_kernel, tt=tt, rs=rs),
        grid=(b, t // tt),
        in_specs=[blk, pl.BlockSpec((1, HIST_ROWS, D_MODEL), lambda bi, ti: (bi, 0, 0)), blk,
                  _full((HIST_ROWS, D_MODEL)), _full((1, D_MODEL)), _full((1, D_MODEL)),
                  _full((1, D_MODEL)), _full((D_MODEL, D_MODEL))],
        out_specs=blk,
        out_shape=jax.ShapeDtypeStruct((b, t, D_MODEL), F32),
        scratch_shapes=[pltpu.VMEM((HIST_ROWS + tt, D_MODEL), F32),
                        pltpu.VMEM((7, tt + HIST_ROWS - 8, D_MODEL), F32)],
        compiler_params=_cparams("parallel", "arbitrary"),
        name="conv_module",
    )(u, hist, x, wdw, bdw, lng, lnb, w2)


def _rope_tables(pos):
    half = RET_DK // 2
    inv_freq = ROPE_BASE ** (-jnp.arange(half, dtype=F32) / half)
    ang = pos.astype(F32)[:, None] * inv_freq[None, :]
    cos, sin = jnp.cos(ang), jnp.sin(ang)
    reps = LANES // RET_DK
    cos_t = jnp.tile(jnp.concatenate([cos, cos], axis=1), (1, reps))
    sin_t = jnp.tile(jnp.concatenate([-sin, sin], axis=1), (1, reps))
    return cos_t, sin_t


def _pad_lanes(v, width=LANES):
    v = v.reshape(1, -1)
    return jnp.pad(v, ((0, 0), (0, width - v.shape[1])))


def _moe_dense(x, p, l, tm):
    h, gates = _router(x, p["norm_ffn"][l], p["wr_hi"][l], p["wr_lo"][l], p["br"][l], tm)
    return _moe(h, gates, x, p["wg"], p["wu"], p["wd"], l, tm)


def _sorted_rows(n_tokens):
    return (2 * n_tokens // MOE_TM + N_EXPERTS) * MOE_TM


def _moe_sparse(x, p, l, tm):
    h, route, wcol = _router_t(x, p["norm_ffn"][l], p["wt_hi"][l], p["wt_lo"][l], p["bt"][l], tm)
    pos, te, nt, pad = _plan(route, PLAN_TB, MOE_TM)
    xs = _dispatch(pos, pad, nt, h, PLAN_TB, MOE_TM)
    ys = _experts(te.reshape(-1), nt[0, :1], xs, p["wg"], p["wu"], p["wd"], l, MOE_TM)
    return _combine(pos, x, wcol, ys, PLAN_TB)


def _run_group(x, p, moe, *, seq, tm, fox_hist, ret_state, conv_hist, pos, tq, tk, ret_c, conv_tt):
    b = x.shape[0] // seq
    q_off = 0 if fox_hist is None else fox_hist[0].shape[1]

    cos_t, sin_t = _rope_tables(pos)
    q, k, v, lf, rq, rk, rv, rg = _proj0(x, p["norm_mix"][0], p["w_in"], p["b_f"], p["q_gain"], p["k_gain"],
                                          p["gbd"], cos_t, sin_t, tm)
    k3 = k.reshape(b, seq, FOX_W)
    v3 = v.reshape(b, seq, FOX_W)
    lf3 = lf.reshape(b, seq, LANES)
    if fox_hist is None:
        k_all, v_all, lf_all = k3, v3, lf3
    else:
        ck_, cv_, clf_ = fox_hist
        tot = q_off + seq
        padded = -(-tot // tk) * tk
        tail = padded - tot
        k_all = jnp.concatenate([ck_, k3, jnp.zeros((b, tail, FOX_W), F32)], axis=1)
        v_all = jnp.concatenate([cv_, v3, jnp.zeros((b, tail, FOX_W), F32)], axis=1)
        clf_ = jnp.pad(clf_, ((0, 0), (0, 0), (0, LANES - FOX_HEADS)))
        lf_all = jnp.concatenate([clf_, lf3, jnp.zeros((b, tail, LANES), F32)], axis=1)
    ccol, crow = _cumsum_logf(lf_all, CUM_BLOCK)
    if fox_hist is None:
        qa, ka, vt = _fox_prep(q.reshape(b, seq, FOX_W), k_all, v_all, ccol, tk)
        o_fox = _fox_attention_t(qa, ka, vt, tq)
    else:
        o_fox = _fox_attention(q.reshape(b, seq, FOX_W), k_all, v_all, ccol, crow, tq, tk, q_off)
    o_ret, s_fin = _retention(p["lg_tab"], rq.reshape(b, seq, RET_QK_W), rk.reshape(b, seq, RET_QK_W),
                              rv.reshape(b, seq, RET_V_W), rg.reshape(b, seq, RET_V_W), ret_state,
                              p["gn_gain"], ret_c)
    x = _out0(x, o_fox.reshape(-1, FOX_W), o_ret.reshape(-1, RET_V_W), p["w_out"], tm)
    x = moe(x, 0)

    u = _pw1(x, p["norm_mix"][1], p["w_pw1"], p["b_pw1"], tm)
    u3 = u.reshape(b, seq, D_MODEL)
    x = _conv_module(u3, conv_hist, x.reshape(b, seq, D_MODEL), p["w_dw"], p["b_dw"], p["ln_g"], p["ln_b"],
                     p["w_pw2"], conv_tt).reshape(-1, D_MODEL)
    x = moe(x, 1)

    fox_k = k3.reshape(1, b, seq, FOX_HEADS, FOX_HD)
    fox_v = v3.reshape(1, b, seq, FOX_HEADS, FOX_HD)
    fox_lf = lf3[:, :, :FOX_HEADS].reshape(1, b, seq, FOX_HEADS)
    return x.reshape(b, seq, D_MODEL), fox_k, fox_v, fox_lf, s_fin[None], u3


def kernel(x_prompt, x_sample, cache_fox_k, cache_fox_v, cache_fox_logf, state_ret, cache_conv, norm_mix, norm_ffn, w_in_mix, b_forget, fox_q_gain, fox_k_gain, ret_gn_gain, w_out_mix, w_pw1, b_pw1, w_dw, b_dw, conv_ln_g, conv_ln_b, w_pw2, w_router_group, b_router_group, w_router_expert, b_router_expert, w_exp_gate, w_exp_up, w_exp_down):
    bp, t, d = x_prompt.shape
    bs, l, _ = x_sample.shape
    past = cache_fox_k.shape[2]
    depth = norm_mix.shape[0]
    assert d == D_MODEL and depth == 2 and w_in_mix.shape[0] == 1 and w_pw1.shape[0] == 1

    w_in = w_in_mix[0]
    n_pre = 3 * FOX_W
    w_in_r = jnp.concatenate(
        [w_in[:, :n_pre], w_in[:, n_pre + FOX_HEADS:], w_in[:, n_pre:n_pre + FOX_HEADS],
         jnp.zeros((D_MODEL, LANES - FOX_HEADS), F32)], axis=1).astype(BF16)
    hid = jnp.arange(FOX_W) // FOX_HD
    gbd = jnp.where(hid[:, None] == hid[None, :], 1.0 / FOX_HD, 0.0).astype(BF16)
    w_r = jnp.concatenate([w_router_group, w_router_expert], axis=-1)
    w_r = jnp.pad(w_r, ((0, 0), (0, 0), (0, LANES - w_r.shape[-1])))
    wr_hi = w_r.astype(BF16)
    wr_lo = (w_r - wr_hi.astype(F32)).astype(BF16)
    b_r = jnp.concatenate([b_router_group, b_router_expert], axis=-1)
    b_r = jnp.pad(b_r, ((0, 0), (0, LANES - b_r.shape[-1])))
    w_t = jnp.swapaxes(jnp.concatenate([w_router_expert, w_router_group], axis=-1), 1, 2)
    w_t = jnp.pad(w_t, ((0, 0), (0, ROUTE_ROWS - w_t.shape[1]), (0, 0)))
    wt_hi = w_t.astype(BF16)
    wt_lo = (w_t - wt_hi.astype(F32)).astype(BF16)
    b_t = jnp.concatenate([b_router_expert, b_router_group], axis=-1)
    b_t = jnp.pad(b_t, ((0, 0), (0, ROUTE_ROWS - b_t.shape[-1])))[:, :, None]
    log_gamma = jnp.log(1.0 - 2.0 ** (-5.0 - jnp.arange(RET_HEADS, dtype=F32)))
    p = {
        "norm_mix": norm_mix.reshape(depth, 1, D_MODEL),
        "norm_ffn": norm_ffn.reshape(depth, 1, D_MODEL),
        "w_in": w_in_r,
        "b_f": _pad_lanes(b_forget[0]),
        "q_gain": jnp.tile(fox_q_gain[0], FOX_HEADS).reshape(1, FOX_W),
        "k_gain": jnp.tile(fox_k_gain[0], FOX_HEADS).reshape(1, FOX_W),
        "gbd": gbd,
        "gn_gain": ret_gn_gain[0].reshape(1, RET_V_W),
        "lg_tab": jnp.broadcast_to(log_gamma[:, None, None], (RET_HEADS, 1, LANES)),
        "w_out": w_out_mix[0].astype(BF16),
        "w_pw1": w_pw1[0].astype(BF16),
        "b_pw1": b_pw1[0].reshape(1, -1),
        "w_dw": jnp.pad(w_dw[0], ((0, HIST_ROWS - CONV_W), (0, 0))),
        "b_dw": b_dw[0].reshape(1, -1),
        "ln_g": conv_ln_g[0].reshape(1, -1),
        "ln_b": conv_ln_b[0].reshape(1, -1),
        "w_pw2": w_pw2[0].astype(BF16),
        "wr_hi": wr_hi,
        "wr_lo": wr_lo,
        "br": b_r.reshape(depth, 1, LANES),
        "wt_hi": wt_hi,
        "wt_lo": wt_lo,
        "bt": b_t,
        "wg": w_exp_gate.reshape(depth * N_EXPERTS, D_MODEL, D_EXPERT),
        "wu": w_exp_up.reshape(depth * N_EXPERTS, D_MODEL, D_EXPERT),
        "wd": w_exp_down.reshape(depth * N_EXPERTS, D_EXPERT, D_MODEL),
    }

    hist_pad = HIST_ROWS - (CONV_W - 1)
    yp, fk_p, fv_p, lf_p, rs_p, u_p = _run_group(
        x_prompt.reshape(bp * t, D_MODEL), p, lambda x, layer: _moe_sparse(x, p, layer, 512), seq=t, tm=512,
        fox_hist=None,
        ret_state=jnp.zeros((bp, RET_HEADS, RET_DK, RET_DV), F32),
        conv_hist=jnp.zeros((bp, HIST_ROWS, D_MODEL), F32),
        pos=jnp.arange(t), tq=256, tk=256, ret_c=256, conv_tt=256)

    ns = bs * l
    ys, fk_s, fv_s, lf_s, rs_s, u_s = _run_group(
        x_sample.reshape(ns, D_MODEL), p, lambda x, layer: _moe_dense(x, p, layer, ns), seq=l, tm=ns,
        fox_hist=(cache_fox_k[0].reshape(bs, past, FOX_W), cache_fox_v[0].reshape(bs, past, FOX_W),
                  cache_fox_logf[0]),
        ret_state=state_ret[0],
        conv_hist=jnp.pad(cache_conv[0], ((0, 0), (hist_pad, 0), (0, 0))),
        pos=past + (jnp.arange(ns) % l), tq=l, tk=-(-(past + l) // CUM_BLOCK) * CUM_BLOCK, ret_c=l, conv_tt=l)

    conv_p = u_p[:, t - (CONV_W - 1):][None]
    conv_s = jnp.concatenate([cache_conv[0], u_s], axis=1)[:, l:][None]
    return (yp, ys, fk_p, fv_p, lf_p, rs_p, conv_p, fk_s, fv_s, lf_s, rs_s, conv_s)
```

```python
import functools
import math

import jax
import jax.numpy as jnp
import numpy as np
from jax import lax
from jax.experimental import pallas as pl
from jax.experimental.pallas import tpu as pltpu

F32 = jnp.float32
BF16 = jnp.bfloat16

D_MODEL = 1024
FOX_HEADS = 8
FOX_HD = 64
RET_HEADS = 4
RET_DK = 64
RET_DV = 128
ROPE_BASE = 10000.0
CONV_W = 31
N_GROUPS = 4
EXP_PER_GROUP = 8
N_EXPERTS = N_GROUPS * EXP_PER_GROUP
D_EXPERT = 256
EPS = 1e-6
FOX_W = FOX_HEADS * FOX_HD
RET_QK_W = RET_HEADS * RET_DK
RET_V_W = RET_HEADS * RET_DV

LANES = 128
HIST_ROWS = 32
ROUTER_LANE0 = N_GROUPS
MASKED = -1e30
VMEM_LIMIT = 56 * 1024 * 1024
CUM_BLOCK = 256
MOE_TM = 512
PLAN_TB = 512
ROUTE_ROWS = 48

C_FQ, C_FK, C_FV = 0, FOX_W, 2 * FOX_W
C_RQ = 3 * FOX_W
C_RK = C_RQ + RET_QK_W
C_RV = C_RK + RET_QK_W
C_RG = C_RV + RET_V_W
C_FF = C_RG + RET_V_W
MIX_COLS = C_FF + LANES


def _cparams(*sem):
    return pltpu.CompilerParams(dimension_semantics=sem, vmem_limit_bytes=VMEM_LIMIT)


def _full(shape):
    n = len(shape)
    return pl.BlockSpec(shape, lambda *_: (0,) * n)


def _rms(x, g):
    ms = jnp.mean(x * x, axis=-1, keepdims=True)
    return x * lax.rsqrt(ms + EPS) * g


def _sigmoid(x):
    return 1.0 / (1.0 + jnp.exp(-x))


def _dot(a, b):
    return jnp.dot(a, b, preferred_element_type=F32)


def _dot_nt(a, b):
    return lax.dot_general(a, b, (((1,), (1,)), ((), ())), preferred_element_type=F32)


def _dot_tn(a, b):
    return lax.dot_general(a, b, (((0,), (0,)), ((), ())), preferred_element_type=F32)


def _proj0_kernel(x_ref, g_ref, w_ref, bf_ref, qg_ref, kg_ref, gbd_ref, cos_ref, sin_ref,
                  q_ref, k_ref, v_ref, lf_ref, rq_ref, rk_ref, rv_ref, rg_ref, k5_ref, v5_ref):
    h = _rms(x_ref[...], g_ref[...]).astype(BF16)

    def seg(a, b):
        return _dot(h, w_ref[:, a:b])

    gbd = gbd_ref[...]

    def head_rms(y, gain):
        ms = _dot((y * y).astype(BF16), gbd)
        return y * lax.rsqrt(ms + EPS) * gain

    q_ref[...] = (head_rms(seg(C_FQ, C_FK), qg_ref[...]) * (FOX_HD ** -0.5)).astype(BF16)
    kk = head_rms(seg(C_FK, C_FV), kg_ref[...])
    vv = seg(C_FV, C_RQ)
    k_ref[...] = kk
    v_ref[...] = vv
    k5_ref[...] = pltpu.einshape("n(hd)->nhd", kk, h=FOX_HEADS)
    v5_ref[...] = pltpu.einshape("n(hd)->nhd", vv, h=FOX_HEADS)

    z = seg(C_FF, MIX_COLS) + bf_ref[...]
    logf = jnp.minimum(z, 0.0) - jnp.log(1.0 + jnp.exp(-jnp.abs(z)))
    lane = lax.broadcasted_iota(jnp.int32, (1, LANES), 1)
    lf_ref[...] = jnp.where(lane < FOX_HEADS, logf, 0.0)

    cos = cos_ref[...]
    sin = sin_ref[...]
    first_half = (lane % RET_DK) < (RET_DK // 2)

    def rotary(y):
        outs = []
        for s in range(y.shape[1] // LANES):
            ys = y[:, s * LANES:(s + 1) * LANES]
            rot = jnp.where(first_half, pltpu.roll(ys, LANES - RET_DK // 2, 1),
                            pltpu.roll(ys, RET_DK // 2, 1))
            outs.append(ys * cos + rot * sin)
        return jnp.concatenate(outs, axis=1)

    rq_ref[...] = rotary(seg(C_RQ, C_RK)).astype(BF16)
    rk_ref[...] = (rotary(seg(C_RK, C_RV)) * (RET_DK ** -0.5)).astype(BF16)
    rv_ref[...] = seg(C_RV, C_RG).astype(BF16)
    rg_ref[...] = seg(C_RG, C_FF).astype(BF16)


def _proj0(x, g, w, bf, qg, kg, gbd, cos, sin, tm):
    n = x.shape[0]
    nper = cos.shape[0] // tm
    row = lambda width: pl.BlockSpec((tm, width), lambda i: (i, 0))
    tab = pl.BlockSpec((tm, LANES), lambda i: (i % nper, 0))
    outs = [(FOX_W, BF16), (FOX_W, F32), (FOX_W, F32), (LANES, F32),
            (RET_QK_W, BF16), (RET_QK_W, BF16), (RET_V_W, BF16), (RET_V_W, BF16)]
    return pl.pallas_call(
        _proj0_kernel,
        grid=(n // tm,),
        in_specs=[row(D_MODEL), _full((1, D_MODEL)), _full((D_MODEL, MIX_COLS)), _full((1, LANES)),
                  _full((1, FOX_W)), _full((1, FOX_W)), _full((FOX_W, FOX_W)), tab, tab],
        out_specs=[row(wd) for wd, _ in outs] + [pl.BlockSpec((tm, FOX_HEADS, FOX_HD), lambda i: (i, 0, 0))] * 2,
        out_shape=[jax.ShapeDtypeStruct((n, wd), dt) for wd, dt in outs]
        + [jax.ShapeDtypeStruct((n, FOX_HEADS, FOX_HD), F32)] * 2,
        compiler_params=_cparams("parallel"),
        name="proj0",
    )(x, g, w, bf, qg, kg, gbd, cos, sin)


def _cum_kernel(lf_ref, ccol_ref, crow_ref, *, t, cb):
    r = lax.broadcasted_iota(jnp.int32, (cb, cb), 0)
    c = lax.broadcasted_iota(jnp.int32, (cb, cb), 1)
    tri = jnp.where(r >= c, 1.0, 0.0).astype(BF16)
    carry = jnp.zeros((1, LANES), F32)
    for blk in range(t // cb):
        a = lf_ref[0, blk * cb:(blk + 1) * cb, :]
        a1 = a.astype(BF16)
        r1 = a - a1.astype(F32)
        a2 = r1.astype(BF16)
        a3 = (r1 - a2.astype(F32)).astype(BF16)
        cc = (_dot(tri, a1) + _dot(tri, a2)) + _dot(tri, a3) + carry
        ccol_ref[0, blk * cb:(blk + 1) * cb, :] = cc
        crow_ref[0, :, blk * cb:(blk + 1) * cb] = cc.T[0:FOX_HEADS, :]
        carry = cc[cb - 1:cb, :]


def _cumsum_logf(lf, cb):
    b, t, _ = lf.shape
    return pl.pallas_call(
        functools.partial(_cum_kernel, t=t, cb=cb),
        grid=(b,),
        in_specs=[pl.BlockSpec((1, t, LANES), lambda i: (i, 0, 0))],
        out_specs=[pl.BlockSpec((1, t, LANES), lambda i: (i, 0, 0)),
                   pl.BlockSpec((1, FOX_HEADS, t), lambda i: (i, 0, 0))],
        out_shape=[jax.ShapeDtypeStruct((b, t, LANES), F32),
                   jax.ShapeDtypeStruct((b, FOX_HEADS, t), F32)],
        compiler_params=_cparams("parallel"),
        name="cumsum_logf",
    )(lf)


def _fox_kernel(q_ref, k_ref, v_ref, cq_ref, ck_ref, o_ref, *, tq, tk, q_off):
    hp = pl.program_id(1)
    i = pl.program_id(2)
    q0 = q_off + i * tq
    nfull = (q0 + 1) // tk
    nch = (q0 + tq + tk - 1) // tk
    lane = lax.broadcasted_iota(jnp.int32, (1, LANES), 1)
    q = q_ref[0]
    cq_all = cq_ref[0]
    qpos = q0 + lax.broadcasted_iota(jnp.int32, (tq, 1), 0)
    kiota = lax.broadcasted_iota(jnp.int32, (1, tk), 1)
    qms = [jnp.where((lane // FOX_HD) == h2, q, jnp.zeros_like(q)) for h2 in range(2)]
    cqs = [jnp.sum(jnp.where(lane == 2 * hp + h2, cq_all, 0.0), axis=-1, keepdims=True) for h2 in range(2)]

    def step(j, carry, masked):
        start = pl.multiple_of(j * tk, tk)
        kj = k_ref[0, pl.ds(start, tk), :].astype(BF16)
        vj = v_ref[0, pl.ds(start, tk), :].astype(BF16)
        new = []
        for h2 in range(2):
            m, l, acc = carry[h2]
            s = _dot_nt(qms[h2], kj)
            s = s + cqs[h2] - ck_ref[0, 2 * hp + h2, j]
            if masked:
                s = jnp.where(j * tk + kiota <= qpos, s, MASKED)
            m_new = jnp.maximum(m, jnp.max(s, axis=-1, keepdims=True))
            alpha = jnp.exp(m - m_new)
            p = jnp.exp(s - m_new)
            l = alpha * l + jnp.sum(p, axis=-1, keepdims=True)
            acc = alpha * acc + _dot(p.astype(BF16), vj)
            new.append((m_new, l, acc))
        return tuple(new)

    one = (jnp.full((tq, 1), MASKED, F32), jnp.zeros((tq, 1), F32), jnp.zeros((tq, LANES), F32))
    carry = lax.fori_loop(0, nfull, functools.partial(step, masked=False), (one, one))
    carry = lax.fori_loop(nfull, nch, functools.partial(step, masked=True), carry)
    outs = [acc / l for _, l, acc in carry]
    o_ref[0] = jnp.where(lane < FOX_HD, outs[0], outs[1]).astype(BF16)


def _fox_attention(q, k, v, ccol, crow, tq, tk, q_off):
    b, tqs, _ = q.shape
    tks = k.shape[1]
    nck = tks // tk
    crow5 = crow.reshape(b, FOX_HEADS, nck, 1, tk)
    qb0 = q_off // tq
    return pl.pallas_call(
        functools.partial(_fox_kernel, tq=tq, tk=tk, q_off=q_off),
        grid=(b, FOX_HEADS // 2, tqs // tq),
        in_specs=[pl.BlockSpec((1, tq, LANES), lambda bi, hp, i: (bi, i, hp)),
                  pl.BlockSpec((1, tks, LANES), lambda bi, hp, i: (bi, 0, hp)),
                  pl.BlockSpec((1, tks, LANES), lambda bi, hp, i: (bi, 0, hp)),
                  pl.BlockSpec((1, tq, LANES), lambda bi, hp, i: (bi, qb0 + i, 0)),
                  pl.BlockSpec((1, FOX_HEADS, nck, 1, tk), lambda bi, hp, i: (bi, 0, 0, 0, 0))],
        out_specs=pl.BlockSpec((1, tq, LANES), lambda bi, hp, i: (bi, i, hp)),
        out_shape=jax.ShapeDtypeStruct((b, tqs, FOX_W), BF16),
        compiler_params=_cparams("parallel", "parallel", "parallel"),
        name="fox_attention",
    )(q, k, v, ccol, crow5)


FOX_VROWS = 80


def _bf16_pieces(c):
    hi = c.astype(BF16).astype(F32)
    r = c - hi
    mid = r.astype(BF16).astype(F32)
    lo = (r - mid).astype(BF16).astype(F32)
    return hi, mid, lo


def _fox_prep_kernel(q_ref, k_ref, v_ref, c_ref, qa_ref, ka_ref, vt_ref, *, t, rc):
    hp = pl.program_id(1)
    lane = lax.broadcasted_iota(jnp.int32, (1, LANES), 1)
    ones = (1.0, 1.0, 1.0)

    def augment(xh, own, other):
        out = jnp.where(lane < FOX_HD, xh, 0.0)
        for piece in range(3):
            out = jnp.where(lane == FOX_HD + piece, own[piece], out)
            out = jnp.where(lane == FOX_HD + 3 + piece, other[piece], out)
        return out.astype(BF16)

    extra = jnp.where(lax.broadcasted_iota(jnp.int32, (FOX_VROWS - FOX_HD, rc), 0) == 0, 1.0, 0.0)
    for ch in range(t // rc):
        rows = slice(ch * rc, (ch + 1) * rc)
        qf = q_ref[0, rows, :].astype(F32)
        kf = k_ref[0, rows, :]
        call = c_ref[0, rows, :]
        vt = v_ref[0, rows, :].T
        for h2 in range(2):
            c = jnp.sum(jnp.where(lane == 2 * hp + h2, call, 0.0), axis=-1, keepdims=True)
            hi, mid, lo = _bf16_pieces(c)
            qh = qf if h2 == 0 else pltpu.roll(qf, FOX_HD, 1)
            kh = kf if h2 == 0 else pltpu.roll(kf, FOX_HD, 1)
            qa_ref[0, h2, rows, :] = augment(qh, (hi, mid, lo), ones)
            ka_ref[0, h2, rows, :] = augment(kh, ones, (-hi, -mid, -lo))
            vt_ref[0, h2, ch] = jnp.concatenate([vt[h2 * FOX_HD:(h2 + 1) * FOX_HD, :], extra], axis=0).astype(BF16)


def _fox_prep(q, k, v, ccol, rc):
    b, t, _ = q.shape
    pair = pl.BlockSpec((1, t, LANES), lambda bi, hp: (bi, 0, hp))
    aug = pl.BlockSpec((1, 2, t, LANES), lambda bi, hp: (bi, hp, 0, 0))
    return pl.pallas_call(
        functools.partial(_fox_prep_kernel, t=t, rc=rc),
        grid=(b, FOX_HEADS // 2),
        in_specs=[pair, pair, pair, pl.BlockSpec((1, t, LANES), lambda bi, hp: (bi, 0, 0))],
        out_specs=[aug, aug, pl.BlockSpec((1, 2, t // rc, FOX_VROWS, rc), lambda bi, hp: (bi, hp, 0, 0, 0))],
        out_shape=[jax.ShapeDtypeStruct((b, FOX_HEADS, t, LANES), BF16),
                   jax.ShapeDtypeStruct((b, FOX_HEADS, t, LANES), BF16),
                   jax.ShapeDtypeStruct((b, FOX_HEADS, t // rc, FOX_VROWS, rc), BF16)],
        compiler_params=_cparams("parallel", "parallel"),
        name="fox_prep",
    )(q, k, v, ccol)


def _fox_t_kernel(qa_ref, ka_ref, vt_ref, o_ref, s_scr, acc_scr, *, tq, tk):
    i = pl.program_id(1)
    q0 = i * tq
    nfull = (q0 + 1) // tk
    nch = (q0 + tq + tk - 1) // tk
    qpos = q0 + lax.broadcasted_iota(jnp.int32, (1, tq), 1)
    kiota = lax.broadcasted_iota(jnp.int32, (tk, 1), 0)
    heads = range(FOX_HEADS)

    def scores(j, mx, masked):
        start = pl.multiple_of(j * tk, tk)
        out = []
        for h in heads:
            st = _dot_nt(ka_ref[0, h, pl.ds(start, tk), :], qa_ref[0, h])
            if masked:
                st = jnp.where(start + kiota <= qpos, st, MASKED)
            s_scr[h, j] = st
            out.append(jnp.maximum(mx[h], jnp.max(st, axis=0, keepdims=True)))
        return tuple(out)

    mx = tuple(jnp.full((1, tq), MASKED, F32) for _ in heads)
    mx = lax.fori_loop(0, nfull, functools.partial(scores, masked=False), mx)
    mx = lax.fori_loop(nfull, nch, functools.partial(scores, masked=True), mx)

    acc_scr[...] = jnp.zeros_like(acc_scr)

    def weigh(j, carry):
        for h in heads:
            p = jnp.exp(s_scr[h, j] - mx[h]).astype(BF16)
            acc_scr[h] += _dot(vt_ref[0, h, j], p)
        return carry

    lax.fori_loop(0, nch, weigh, 0)
    for hp in range(FOX_HEADS // 2):
        outs = []
        for h in (2 * hp, 2 * hp + 1):
            acc = acc_scr[h]
            outs.append(acc[0:FOX_HD, :] / acc[FOX_HD:FOX_HD + 1, :])
        o_ref[0, :, hp * LANES:(hp + 1) * LANES] = jnp.concatenate(outs, axis=0).T.astype(BF16)


def _fox_attention_t(qa, ka, vt, tq):
    b, _, t, _ = qa.shape
    nck, tk = vt.shape[2], vt.shape[4]
    return pl.pallas_call(
        functools.partial(_fox_t_kernel, tq=tq, tk=tk),
        grid=(b, t // tq),
        in_specs=[pl.BlockSpec((1, FOX_HEADS, tq, LANES), lambda bi, i: (bi, 0, i, 0)),
                  pl.BlockSpec((1, FOX_HEADS, t, LANES), lambda bi, i: (bi, 0, 0, 0)),
                  pl.BlockSpec((1, FOX_HEADS, nck, FOX_VROWS, tk), lambda bi, i: (bi, 0, 0, 0, 0))],
        out_specs=pl.BlockSpec((1, tq, FOX_W), lambda bi, i: (bi, i, 0)),
        out_shape=jax.ShapeDtypeStruct((b, t, FOX_W), BF16),
        scratch_shapes=[pltpu.VMEM((FOX_HEADS, nck, tk, tq), F32),
                        pltpu.VMEM((FOX_HEADS, FOX_VROWS, tq), F32)],
        compiler_params=_cparams("parallel", "arbitrary"),
        name="fox_attention_t",
    )(qa, ka, vt)


def _ret_kernel(lg_ref, rq_ref, rk_ref, rv_ref, rg_ref, s0_ref, gn_ref, o_ref, sfin_ref, s_scr, d_scr, *, c):
    first = (pl.program_id(0) == 0) & (pl.program_id(1) == 0)
    ci = pl.program_id(1)
    ii = lax.broadcasted_iota(jnp.int32, (c, 1), 0).astype(F32)
    lane = lax.broadcasted_iota(jnp.int32, (1, LANES), 1)

    @pl.when(first)
    def _():
        jj = lax.broadcasted_iota(jnp.int32, (1, c), 1).astype(F32)
        diff = ii - jj
        for h in range(RET_HEADS):
            lg = lg_ref[h][:, 0:1]
            d_scr[h] = jnp.where(diff >= 0.0, jnp.exp(lg * jnp.maximum(diff, 0.0)), 0.0)

    @pl.when(ci == 0)
    def _():
        zero = jnp.zeros((RET_DK, RET_DV), F32)
        for h in range(RET_HEADS):
            lo, hi = (s0_ref[0, h], zero) if h % 2 == 0 else (zero, s0_ref[0, h])
            s_scr[h, 0:RET_DK, :] = lo
            s_scr[h, RET_DK:2 * RET_DK, :] = hi

    for h in range(RET_HEADS):
        lg = lg_ref[h][:, 0:1]
        pair = slice((h // 2) * LANES, (h // 2 + 1) * LANES)
        mine = slice(h * RET_DV, (h + 1) * RET_DV)
        inhead = (lane // RET_DK) == (h % 2)
        q = rq_ref[0, :, pair]
        k = rk_ref[0, :, pair]
        qm = jnp.where(inhead, q, jnp.zeros_like(q))
        km = jnp.where(inhead, k, jnp.zeros_like(k))
        v = rv_ref[0, :, mine]
        scores = _dot_nt(qm, km) * d_scr[h]
        inner = _dot(scores.astype(BF16), v)
        s_prev = s_scr[h]
        cross = _dot(qm, s_prev.astype(BF16)) * jnp.exp(lg * (ii + 1.0))
        y = inner + cross
        k_dec = (km.astype(F32) * jnp.exp(lg * (c - 1.0 - ii))).astype(BF16)
        s_new = jnp.exp(lg * float(c)) * s_prev + _dot_tn(k_dec, v)
        s_scr[h] = s_new

        mu = jnp.mean(y, axis=-1, keepdims=True)
        yc = y - mu
        var = jnp.mean(yc * yc, axis=-1, keepdims=True)
        yn = yc * lax.rsqrt(var + EPS) * gn_ref[:, mine]
        g = rg_ref[0, :, mine].astype(F32)
        o_ref[0, :, mine] = ((g * _sigmoid(g)) * yn).astype(BF16)
        off = (h % 2) * RET_DK
        sfin_ref[0, h] = s_new[off:off + RET_DK, :]


def _retention(lg_tab, rq, rk, rv, rg, s0, gn, c):
    b, t, _ = rq.shape
    qk_spec = pl.BlockSpec((1, c, RET_QK_W), lambda bi, ci: (bi, ci, 0))
    v_spec = pl.BlockSpec((1, c, RET_V_W), lambda bi, ci: (bi, ci, 0))
    st_spec = pl.BlockSpec((1, RET_HEADS, RET_DK, RET_DV), lambda bi, ci: (bi, 0, 0, 0))
    return pl.pallas_call(
        functools.partial(_ret_kernel, c=c),
        grid=(b, t // c),
        in_specs=[_full((RET_HEADS, 1, LANES)), qk_spec, qk_spec, v_spec, v_spec, st_spec,
                  _full((1, RET_V_W))],
        out_specs=[v_spec, st_spec],
        out_shape=[jax.ShapeDtypeStruct((b, t, RET_V_W), BF16),
                   jax.ShapeDtypeStruct((b, RET_HEADS, RET_DK, RET_DV), F32)],
        scratch_shapes=[pltpu.VMEM((RET_HEADS, 2 * RET_DK, RET_DV), F32), pltpu.VMEM((RET_HEADS, c, c), F32)],
        compiler_params=_cparams("arbitrary", "arbitrary"),
        name="retention",
    )(lg_tab, rq, rk, rv, rg, s0, gn)


def _out0_kernel(x_ref, of_ref, or_ref, w_ref, y_ref):
    mix = _dot(of_ref[...], w_ref[0:FOX_W, :]) + _dot(or_ref[...], w_ref[FOX_W:FOX_W + RET_V_W, :])
    y_ref[...] = x_ref[...] + mix


def _out0(x, o_fox, o_ret, w, tm):
    n = x.shape[0]
    row = lambda width: pl.BlockSpec((tm, width), lambda i: (i, 0))
    return pl.pallas_call(
        _out0_kernel,
        grid=(n // tm,),
        in_specs=[row(D_MODEL), row(FOX_W), row(RET_V_W), _full((FOX_W + RET_V_W, D_MODEL))],
        out_specs=row(D_MODEL),
        out_shape=jax.ShapeDtypeStruct((n, D_MODEL), F32),
        compiler_params=_cparams("parallel"),
        name="out_proj0",
    )(x, o_fox, o_ret, w)


def _router_kernel(x_ref, g_ref, wh_ref, wl_ref, b_ref, h_ref, gates_ref):
    hf = _rms(x_ref[...], g_ref[...])
    hh = hf.astype(BF16)
    hl = (hf - hh.astype(F32)).astype(BF16)
    h_ref[...] = hh
    wh = wh_ref[...]
    logits = _dot(hh, wh) + (_dot(hl, wh) + _dot(hh, wl_ref[...])) + b_ref[...]

    lane = lax.broadcasted_iota(jnp.int32, (1, LANES), 1)
    lanef = lane.astype(F32)
    ninf = -jnp.inf
    is_grp = lane < N_GROUPS
    gl = jnp.where(is_grp, logits, ninf)
    gmax = jnp.max(gl, axis=-1, keepdims=True)
    grp = jnp.min(jnp.where(gl == gmax, lanef, 1e9), axis=-1, keepdims=True)
    p_grp = 1.0 / jnp.sum(jnp.exp(gl - gmax), axis=-1, keepdims=True)

    is_exp = (lane >= ROUTER_LANE0) & (lane < ROUTER_LANE0 + N_EXPERTS)
    lane_grp = ((lane - ROUTER_LANE0) // EXP_PER_GROUP).astype(F32)
    em = jnp.where(is_exp & (lane_grp == grp), logits, ninf)
    v1 = jnp.max(em, axis=-1, keepdims=True)
    i1 = jnp.min(jnp.where(em == v1, lanef, 1e9), axis=-1, keepdims=True)
    em2 = jnp.where(lanef == i1, ninf, em)
    v2 = jnp.max(em2, axis=-1, keepdims=True)
    i2 = jnp.min(jnp.where(em2 == v2, lanef, 1e9), axis=-1, keepdims=True)
    t = jnp.exp(v2 - v1)
    w1 = (1.0 / (1.0 + t)) * p_grp
    w2 = (t / (1.0 + t)) * p_grp
    gates_ref[...] = jnp.where(lanef == i1, w1, 0.0) + jnp.where(lanef == i2, w2, 0.0)


def _router(x, g, wh, wl, b, tm):
    n = x.shape[0]
    row = lambda width: pl.BlockSpec((tm, width), lambda i: (i, 0))
    return pl.pallas_call(
        _router_kernel,
        grid=(n // tm,),
        in_specs=[row(D_MODEL), _full((1, D_MODEL)), _full((D_MODEL, LANES)), _full((D_MODEL, LANES)),
                  _full((1, LANES))],
        out_specs=[row(D_MODEL), row(LANES)],
        out_shape=[jax.ShapeDtypeStruct((n, D_MODEL), BF16), jax.ShapeDtypeStruct((n, LANES), F32)],
        compiler_params=_cparams("parallel"),
        name="router",
    )(x, g, wh, wl, b)


def _moe_kernel(h_ref, gates_ref, x_ref, wg_ref, wu_ref, wd_ref, y_ref):
    e = pl.program_id(1)

    @pl.when(e == 0)
    def _():
        y_ref[...] = jnp.zeros_like(y_ref)

    h = h_ref[...]
    a = _dot(h, wg_ref[0].astype(BF16))
    u = _dot(h, wu_ref[0].astype(BF16))
    lane = lax.broadcasted_iota(jnp.int32, (1, LANES), 1)
    gate = jnp.sum(jnp.where(lane == e + ROUTER_LANE0, gates_ref[...], 0.0), axis=-1, keepdims=True)
    hid = (a * _sigmoid(a)) * u * gate
    y_ref[...] += _dot(hid.astype(BF16), wd_ref[0].astype(BF16))

    @pl.when(e == pl.num_programs(1) - 1)
    def _():
        y_ref[...] += x_ref[...]


def _moe(h, gates, x, wg, wu, wd, layer, tm):
    n = x.shape[0]
    first = layer * N_EXPERTS
    row = lambda width: pl.BlockSpec((tm, width), lambda i, e: (i, 0))
    return pl.pallas_call(
        _moe_kernel,
        grid=(n // tm, N_EXPERTS),
        in_specs=[row(D_MODEL), row(LANES), row(D_MODEL),
                  pl.BlockSpec((1, D_MODEL, D_EXPERT), lambda i, e: (first + e, 0, 0)),
                  pl.BlockSpec((1, D_MODEL, D_EXPERT), lambda i, e: (first + e, 0, 0)),
                  pl.BlockSpec((1, D_EXPERT, D_MODEL), lambda i, e: (first + e, 0, 0))],
        out_specs=row(D_MODEL),
        out_shape=jax.ShapeDtypeStruct((n, D_MODEL), F32),
        compiler_params=_cparams("parallel", "arbitrary"),
        name="moe_experts",
    )(h, gates, x, wg, wu, wd)


def _router_t_kernel(x_ref, g_ref, wth_ref, wtl_ref, bt_ref, h_ref, route_ref, wcol_ref):
    hf = _rms(x_ref[...], g_ref[...])
    _to_slabs(h_ref, hf)
    hh = hf.astype(BF16)
    hl = (hf - hh.astype(F32)).astype(BF16)
    wth = wth_ref[...]
    logits = _dot_nt(wth, hh) + (_dot_nt(wth, hl) + _dot_nt(wtl_ref[...], hh)) + bt_ref[...]

    row = lax.broadcasted_iota(jnp.int32, (ROUTE_ROWS, 1), 0)
    rowf = row.astype(F32)
    ninf = -jnp.inf
    is_grp = (row >= N_EXPERTS) & (row < N_EXPERTS + N_GROUPS)
    gl = jnp.where(is_grp, logits, ninf)
    gmax = jnp.max(gl, axis=0, keepdims=True)
    grp = jnp.min(jnp.where(gl == gmax, rowf, 1e9), axis=0, keepdims=True) - float(N_EXPERTS)
    p_grp = 1.0 / jnp.sum(jnp.exp(gl - gmax), axis=0, keepdims=True)

    row_grp = (row // EXP_PER_GROUP).astype(F32)
    em = jnp.where((row < N_EXPERTS) & (row_grp == grp), logits, ninf)
    v1 = jnp.max(em, axis=0, keepdims=True)
    i1 = jnp.min(jnp.where(em == v1, rowf, 1e9), axis=0, keepdims=True)
    em2 = jnp.where(rowf == i1, ninf, em)
    v2 = jnp.max(em2, axis=0, keepdims=True)
    i2 = jnp.min(jnp.where(em2 == v2, rowf, 1e9), axis=0, keepdims=True)
    t = jnp.exp(v2 - v1)
    w1 = (1.0 / (1.0 + t)) * p_grp
    w2 = (t / (1.0 + t)) * p_grp

    tm = logits.shape[1]
    r128 = lax.broadcasted_iota(jnp.int32, (LANES, 1), 0)
    rt = (jnp.where(r128 == 0, i1, 0.0) + jnp.where(r128 == 1, i2, 0.0)
          + jnp.where(r128 == 2, w1, 0.0) + jnp.where(r128 == 3, w2, 0.0))
    route_ref[...] = rt[0:8, :]
    wcol_ref[...] = rt.T


def _router_t(x, g, wth, wtl, bt, tm):
    n = x.shape[0]
    row = lambda width: pl.BlockSpec((tm, width), lambda i: (i, 0))
    return pl.pallas_call(
        _router_t_kernel,
        grid=(n // tm,),
        in_specs=[row(D_MODEL), _full((1, D_MODEL)), _full((ROUTE_ROWS, D_MODEL)),
                  _full((ROUTE_ROWS, D_MODEL)), _full((ROUTE_ROWS, 1))],
        out_specs=[pl.BlockSpec((tm * SLAB, LANES), lambda i: (i, 0)), pl.BlockSpec((8, tm), lambda i: (0, i)),
                   row(LANES)],
        out_shape=[jax.ShapeDtypeStruct((n * SLAB, LANES), F32), jax.ShapeDtypeStruct((8, n), F32),
                   jax.ShapeDtypeStruct((n, LANES), F32)],
        compiler_params=_cparams("parallel"),
        name="router_t",
    )(x, g, wth, wtl, bt)


def _plan_kernel(route_ref, pos_ref, te_ref, nt_ref, pad_ref, *, n, tb, tm):
    nb = n // tb
    erow = lax.broadcasted_iota(jnp.int32, (N_EXPERTS, 1), 0).astype(F32)

    def picks(blk):
        i1 = route_ref[0:1, blk * tb:(blk + 1) * tb]
        i2 = route_ref[1:2, blk * tb:(blk + 1) * tb]
        return erow == i1, erow == i2

    acc = jnp.zeros((N_EXPERTS, tb), F32)
    for blk in range(nb):
        e1, e2 = picks(blk)
        acc = acc + jnp.where(e1, 1.0, 0.0) + jnp.where(e2, 1.0, 0.0)
    counts = jnp.sum(acc, axis=1, keepdims=True).astype(jnp.int32)
    assert tm & (tm - 1) == 0
    ntile = jnp.right_shift(counts + (tm - 1), tm.bit_length() - 1).astype(F32)
    lane = lax.broadcasted_iota(jnp.int32, (1, LANES), 1).astype(F32)
    ntile_row = jnp.sum(jnp.where(lane == erow, ntile, 0.0), axis=0, keepdims=True)
    tend = jnp.sum(jnp.where(lane <= erow, ntile_row, 0.0), axis=1, keepdims=True)
    off = (tend - ntile) * float(tm)
    ntot = jnp.sum(ntile, axis=0, keepdims=True)
    kk = jnp.minimum(lax.broadcasted_iota(jnp.int32, (1, 2 * LANES), 1).astype(F32), ntot - 1.0)
    te_ref[...] = jnp.sum(jnp.where(tend <= kk, 1.0, 0.0), axis=0, keepdims=True).astype(jnp.int32)
    nt_ref[...] = jnp.broadcast_to(ntot, (1, LANES)).astype(jnp.int32)
    pad0 = off + counts.astype(F32)
    pad_ref[...] = jnp.sum(jnp.where(lane == erow, pad0, 0.0), axis=0, keepdims=True).astype(jnp.int32)

    r = lax.broadcasted_iota(jnp.int32, (tb, tb), 0)
    c = lax.broadcasted_iota(jnp.int32, (tb, tb), 1)
    before = jnp.where(r < c, 1.0, 0.0).astype(BF16)
    carry = jnp.zeros((N_EXPERTS, 1), F32)
    for blk in range(nb):
        e1, e2 = picks(blk)
        mt = jnp.where(e1, 1.0, 0.0) + jnp.where(e2, 1.0, 0.0)
        slot = _dot(mt.astype(BF16), before) + (carry + off)
        pos_ref[blk, 0:1, :] = jnp.sum(jnp.where(e1, slot, 0.0), axis=0, keepdims=True).astype(jnp.int32)
        pos_ref[blk, 1:2, :] = jnp.sum(jnp.where(e2, slot, 0.0), axis=0, keepdims=True).astype(jnp.int32)
        carry = carry + jnp.sum(mt, axis=1, keepdims=True)


def _plan(route, tb, tm):
    n = route.shape[1]
    vm = pl.BlockSpec(memory_space=pltpu.VMEM)
    return pl.pallas_call(
        functools.partial(_plan_kernel, n=n, tb=tb, tm=tm),
        in_specs=[vm],
        out_specs=[vm, vm, vm, vm],
        out_shape=[jax.ShapeDtypeStruct((n // tb, 2, tb), jnp.int32),
                   jax.ShapeDtypeStruct((1, 2 * LANES), jnp.int32),
                   jax.ShapeDtypeStruct((1, LANES), jnp.int32),
                   jax.ShapeDtypeStruct((1, LANES), jnp.int32)],
        compiler_params=pltpu.CompilerParams(vmem_limit_bytes=VMEM_LIMIT),
        name="moe_plan",
    )(route)


SLAB = D_MODEL // LANES


def _to_slabs(ref, y):
    m = y.shape[0]
    for s_ in range(SLAB):
        ref[pl.ds(s_, m, stride=SLAB), :] = y[:, s_ * LANES:(s_ + 1) * LANES]


def _slab_piece(ref, s_, m):
    return ref[pl.ds(s_, m, stride=SLAB), :]


def _from_slabs(ref, m):
    return jnp.concatenate([_slab_piece(ref, s_, m) for s_ in range(SLAB)], axis=1)


def _tok_copy(src, src_tok, dst, dst_tok, sem):
    return pltpu.make_async_copy(src.at[pl.ds(pl.multiple_of(src_tok * SLAB, SLAB), SLAB)],
                                 dst.at[pl.ds(pl.multiple_of(dst_tok * SLAB, SLAB), SLAB)], sem)


DMA_UNROLL = 8


def _dispatch_kernel(pos_ref, pad_ref, nt_ref, h_ref, xs_hbm, zero_scr, sem, zsem, *, ts, tm, ntmax):
    @pl.when(pl.program_id(0) == 0)
    def _():
        zero_scr[...] = jnp.zeros_like(zero_scr)

        def fill(first_slot):
            return pltpu.make_async_copy(
                zero_scr, xs_hbm.at[pl.ds(pl.multiple_of(first_slot * SLAB, SLAB), tm * SLAB)], zsem)

        for e in range(N_EXPERTS):
            fill(pad_ref[0, e]).start()
        for e in range(N_EXPERTS):
            fill(pad_ref[0, e]).wait()

        ntot = nt_ref[0, 0]

        def tail_start(k, carry):
            fill(k * tm).start()
            return carry

        def tail_wait(k, carry):
            fill(k * tm).wait()
            return carry

        lax.fori_loop(ntot, ntmax, tail_start, 0)
        lax.fori_loop(ntot, ntmax, tail_wait, 0)

    def issue(g, carry):
        for u in range(DMA_UNROLL):
            r = g * DMA_UNROLL + u
            for ch in range(2):
                _tok_copy(h_ref, r, xs_hbm, pos_ref[0, ch, r], sem).start(priority=ch)
        return carry

    def drain(g, carry):
        for _ in range(2 * DMA_UNROLL):
            _tok_copy(h_ref, 0, xs_hbm, 0, sem).wait()
        return carry

    lax.fori_loop(0, ts // DMA_UNROLL, issue, 0)
    lax.fori_loop(0, ts // DMA_UNROLL, drain, 0)


def _dispatch(pos, pad, nt, h, ts, tm):
    nb = pos.shape[0]
    n = h.shape[0] // SLAB
    ntmax = _sorted_rows(n) // tm + 1
    smem = pl.BlockSpec(memory_space=pltpu.SMEM)
    return pl.pallas_call(
        functools.partial(_dispatch_kernel, ts=ts, tm=tm, ntmax=ntmax),
        grid=(nb,),
        in_specs=[pl.BlockSpec((1, 2, ts), lambda i: (i, 0, 0), memory_space=pltpu.SMEM), smem, smem,
                  pl.BlockSpec((ts * SLAB, LANES), lambda i: (i, 0))],
        out_specs=pl.BlockSpec(memory_space=pl.ANY),
        out_shape=jax.ShapeDtypeStruct((ntmax * tm * SLAB, LANES), F32),
        scratch_shapes=[pltpu.VMEM((tm * SLAB, LANES), F32), pltpu.SemaphoreType.DMA,
                        pltpu.SemaphoreType.DMA],
        compiler_params=_cparams("arbitrary"),
        name="moe_dispatch",
    )(pos, pad, nt, h)


def _experts_kernel(te_ref, nt_ref, xs_ref, wg_ref, wu_ref, wd_ref, ys_ref, *, tm):
    used = pl.program_id(0) < nt_ref[0]

    @pl.when(used)
    def _():
        x = _from_slabs(xs_ref, tm).astype(BF16)
        a = _dot(x, wg_ref[0].astype(BF16))
        u = _dot(x, wu_ref[0].astype(BF16))
        hid = (a * _sigmoid(a)) * u
        _to_slabs(ys_ref, _dot(hid.astype(BF16), wd_ref[0].astype(BF16)))

    @pl.when(jnp.logical_not(used))
    def _():
        ys_ref[...] = jnp.zeros_like(ys_ref)


def _experts(te, nt, xs, wg, wu, wd, layer, tm):
    ntmax = xs.shape[0] // (tm * SLAB)
    first = layer * N_EXPERTS
    tile = pl.BlockSpec((tm * SLAB, LANES), lambda k, te, nt: (jnp.minimum(k, nt[0] - 1), 0))
    out_tile = pl.BlockSpec((tm * SLAB, LANES), lambda k, te, nt: (k, 0))
    return pl.pallas_call(
        functools.partial(_experts_kernel, tm=tm),
        grid_spec=pltpu.PrefetchScalarGridSpec(
            num_scalar_prefetch=2,
            grid=(ntmax,),
            in_specs=[tile,
                      pl.BlockSpec((1, D_MODEL, D_EXPERT), lambda k, te, nt: (first + te[k], 0, 0)),
                      pl.BlockSpec((1, D_MODEL, D_EXPERT), lambda k, te, nt: (first + te[k], 0, 0)),
                      pl.BlockSpec((1, D_EXPERT, D_MODEL), lambda k, te, nt: (first + te[k], 0, 0))],
            out_specs=out_tile),
        out_shape=jax.ShapeDtypeStruct(xs.shape, F32),
        compiler_params=_cparams("arbitrary"),
        name="moe_experts_sorted",
    )(te, nt, xs, wg, wu, wd)


def _combine_kernel(pos_ref, x_ref, w_ref, ys_hbm, y_ref, buf, sem, *, tc):
    def issue(g, carry):
        for u in range(DMA_UNROLL):
            r = g * DMA_UNROLL + u
            for ch in range(2):
                _tok_copy(ys_hbm, pos_ref[0, ch, r], buf.at[ch], r, sem).start(priority=ch)
        return carry

    def drain(g, carry):
        for _ in range(2 * DMA_UNROLL):
            _tok_copy(ys_hbm, 0, buf.at[0], 0, sem).wait()
        return carry

    lax.fori_loop(0, tc // DMA_UNROLL, issue, 0)
    lax.fori_loop(0, tc // DMA_UNROLL, drain, 0)
    w = w_ref[...]
    w1, w2 = w[:, 2:3], w[:, 3:4]
    for s_ in range(SLAB):
        cols = slice(s_ * LANES, (s_ + 1) * LANES)
        y_ref[:, cols] = x_ref[:, cols] + (w1 * _slab_piece(buf.at[0], s_, tc) + w2 * _slab_piece(buf.at[1], s_, tc))


def _combine(pos, x, wcol, ys, tc):
    n = x.shape[0]
    row = lambda width: pl.BlockSpec((tc, width), lambda i: (i, 0))
    return pl.pallas_call(
        functools.partial(_combine_kernel, tc=tc),
        grid=(n // tc,),
        in_specs=[pl.BlockSpec((1, 2, tc), lambda i: (i, 0, 0), memory_space=pltpu.SMEM),
                  row(D_MODEL), row(LANES), pl.BlockSpec(memory_space=pl.ANY)],
        out_specs=row(D_MODEL),
        out_shape=jax.ShapeDtypeStruct((n, D_MODEL), F32),
        scratch_shapes=[pltpu.VMEM((2, tc * SLAB, LANES), F32), pltpu.SemaphoreType.DMA],
        compiler_params=_cparams("arbitrary"),
        name="moe_combine",
    )(pos, x, wcol, ys)


def _pw1_kernel(x_ref, g_ref, w_ref, b_ref, u_ref):
    h = _rms(x_ref[...], g_ref[...]).astype(BF16)
    a = _dot(h, w_ref[:, 0:D_MODEL]) + b_ref[:, 0:D_MODEL]
    g = _dot(h, w_ref[:, D_MODEL:2 * D_MODEL]) + b_ref[:, D_MODEL:2 * D_MODEL]
    u_ref[...] = a * _sigmoid(g)


def _pw1(x, g, w, b, tm):
    n = x.shape[0]
    row = lambda width: pl.BlockSpec((tm, width), lambda i: (i, 0))
    return pl.pallas_call(
        _pw1_kernel,
        grid=(n // tm,),
        in_specs=[row(D_MODEL), _full((1, D_MODEL)), _full((D_MODEL, 2 * D_MODEL)), _full((1, 2 * D_MODEL))],
        out_specs=row(D_MODEL),
        out_shape=jax.ShapeDtypeStruct((n, D_MODEL), F32),
        compiler_params=_cparams("parallel"),
        name="conv_pw1_glu",
    )(x, g, w, b)


def _conv_kernel(u_ref, hist_ref, x_ref, wdw_ref, bdw_ref, lng_ref, lnb_ref, w2_ref, y_ref, ext_scr, sh_scr,
                 *, tt, rs):
    ti = pl.program_id(1)

    @pl.when(ti == 0)
    def _():
        ext_scr[0:HIST_ROWS, :] = hist_ref[0]

    ext_scr[HIST_ROWS:HIST_ROWS + tt, :] = u_ref[0]
    pad = HIST_ROWS - (CONV_W - 1)
    rows = tt + HIST_ROWS - 8
    for r in range(1, 8):
        sh_scr[r - 1, :, :] = ext_scr[r:r + rows, :]
    parts = []
    for r0 in range(0, tt, rs):
        acc = jnp.zeros((rs, D_MODEL), F32)
        for kk in range(CONV_W):
            a, r = divmod(kk + pad, 8)
            lo = r0 + 8 * a
            src = ext_scr[lo:lo + rs, :] if r == 0 else sh_scr[r - 1, lo:lo + rs, :]
            acc = acc + wdw_ref[kk:kk + 1, :] * src
        parts.append(acc)
    y = jnp.concatenate(parts, axis=0) + bdw_ref[...]
    mu = jnp.mean(y, axis=-1, keepdims=True)
    yc = y - mu
    var = jnp.mean(yc * yc, axis=-1, keepdims=True)
    yn = yc * lax.rsqrt(var + EPS) * lng_ref[...] + lnb_ref[...]
    z = yn * _sigmoid(yn)
    y_ref[0] = x_ref[0] + _dot(z.astype(BF16), w2_ref[...])
    if tt >= HIST_ROWS:
        ext_scr[0:HIST_ROWS, :] = ext_scr[tt:tt + HIST_ROWS, :]


def _conv_module(u, hist, x, wdw, bdw, lng, lnb, w2, tt):
    b, t, _ = u.shape
    rs = min(tt, 32)
    blk = pl.BlockSpec((1, tt, D_MODEL), lambda bi, ti: (bi, ti, 0))
    return pl.pallas_call(
        functools.partial(_conv_kernel, tt=tt, rs=rs),
        grid=(b, t // tt),
        in_specs=[blk, pl.BlockSpec((1, HIST_ROWS, D_MODEL), lambda bi, ti: (bi, 0, 0)), blk,
                  _full((HIST_ROWS, D_MODEL)), _full((1, D_MODEL)), _full((1, D_MODEL)),
                  _full((1, D_MODEL)), _full((D_MODEL, D_MODEL))],
        out_specs=blk,
        out_shape=jax.ShapeDtypeStruct((b, t, D_MODEL), F32),
        scratch_shapes=[pltpu.VMEM((HIST_ROWS + tt, D_MODEL), F32),
                        pltpu.VMEM((7, tt + HIST_ROWS - 8, D_MODEL), F32)],
        compiler_params=_cparams("parallel", "arbitrary"),
        name="conv_module",
    )(u, hist, x, wdw, bdw, lng, lnb, w2)


def _rope_tables(pos):
    half = RET_DK // 2
    inv_freq = ROPE_BASE ** (-jnp.arange(half, dtype=F32) / half)
    ang = pos.astype(F32)[:, None] * inv_freq[None, :]
    cos, sin = jnp.cos(ang), jnp.sin(ang)
    reps = LANES // RET_DK
    cos_t = jnp.tile(jnp.concatenate([cos, cos], axis=1), (1, reps))
    sin_t = jnp.tile(jnp.concatenate([-sin, sin], axis=1), (1, reps))
    return cos_t, sin_t


def _pad_lanes(v, width=LANES):
    v = v.reshape(1, -1)
    return jnp.pad(v, ((0, 0), (0, width - v.shape[1])))


def _moe_dense(x, p, l, tm):
    h, gates = _router(x, p["norm_ffn"][l], p["wr_hi"][l], p["wr_lo"][l], p["br"][l], tm)
    return _moe(h, gates, x, p["wg"], p["wu"], p["wd"], l, tm)


def _sorted_rows(n_tokens):
    return (2 * n_tokens // MOE_TM + N_EXPERTS) * MOE_TM


def _moe_sparse(x, p, l, tm):
    h, route, wcol = _router_t(x, p["norm_ffn"][l], p["wt_hi"][l], p["wt_lo"][l], p["bt"][l], tm)
    pos, te, nt, pad = _plan(route, PLAN_TB, MOE_TM)
    xs = _dispatch(pos, pad, nt, h, PLAN_TB, MOE_TM)
    ys = _experts(te.reshape(-1), nt[0, :1], xs, p["wg"], p["wu"], p["wd"], l, MOE_TM)
    return _combine(pos, x, wcol, ys, PLAN_TB)


def _run_group(x, p, moe, *, seq, tm, fox_hist, ret_state, conv_hist, pos, tq, tk, ret_c, conv_tt):
    b = x.shape[0] // seq
    q_off = 0 if fox_hist is None else fox_hist[0].shape[1]

    cos_t, sin_t = _rope_tables(pos)
    q, k, v, lf, rq, rk, rv, rg, k5, v5 = _proj0(x, p["norm_mix"][0], p["w_in"], p["b_f"], p["q_gain"], p["k_gain"],
                                          p["gbd"], cos_t, sin_t, tm)
    k3 = k.reshape(b, seq, FOX_W)
    v3 = v.reshape(b, seq, FOX_W)
    lf3 = lf.reshape(b, seq, LANES)
    if fox_hist is None:
        k_all, v_all, lf_all = k3, v3, lf3
    else:
        ck_, cv_, clf_ = fox_hist
        tot = q_off + seq
        padded = -(-tot // tk) * tk
        tail = padded - tot
        k_all = jnp.concatenate([ck_, k3, jnp.zeros((b, tail, FOX_W), F32)], axis=1)
        v_all = jnp.concatenate([cv_, v3, jnp.zeros((b, tail, FOX_W), F32)], axis=1)
        clf_ = jnp.pad(clf_, ((0, 0), (0, 0), (0, LANES - FOX_HEADS)))
        lf_all = jnp.concatenate([clf_, lf3, jnp.zeros((b, tail, LANES), F32)], axis=1)
    ccol, crow = _cumsum_logf(lf_all, CUM_BLOCK)
    if fox_hist is None:
        qa, ka, vt = _fox_prep(q.reshape(b, seq, FOX_W), k_all, v_all, ccol, tk)
        o_fox = _fox_attention_t(qa, ka, vt, tq)
    else:
        o_fox = _fox_attention(q.reshape(b, seq, FOX_W), k_all, v_all, ccol, crow, tq, tk, q_off)
    o_ret, s_fin = _retention(p["lg_tab"], rq.reshape(b, seq, RET_QK_W), rk.reshape(b, seq, RET_QK_W),
                              rv.reshape(b, seq, RET_V_W), rg.reshape(b, seq, RET_V_W), ret_state,
                              p["gn_gain"], ret_c)
    x = _out0(x, o_fox.reshape(-1, FOX_W), o_ret.reshape(-1, RET_V_W), p["w_out"], tm)
    x = moe(x, 0)

    u = _pw1(x, p["norm_mix"][1], p["w_pw1"], p["b_pw1"], tm)
    u3 = u.reshape(b, seq, D_MODEL)
    x = _conv_module(u3, conv_hist, x.reshape(b, seq, D_MODEL), p["w_dw"], p["b_dw"], p["ln_g"], p["ln_b"],
                     p["w_pw2"], conv_tt).reshape(-1, D_MODEL)
    x = moe(x, 1)

    fox_k = k5.reshape(1, b, seq, FOX_HEADS, FOX_HD)
    fox_v = v5.reshape(1, b, seq, FOX_HEADS, FOX_HD)
    fox_lf = lf3[:, :, :FOX_HEADS].reshape(1, b, seq, FOX_HEADS)
    return x.reshape(b, seq, D_MODEL), fox_k, fox_v, fox_lf, s_fin[None], u3


def kernel(x_prompt, x_sample, cache_fox_k, cache_fox_v, cache_fox_logf, state_ret, cache_conv, norm_mix, norm_ffn, w_in_mix, b_forget, fox_q_gain, fox_k_gain, ret_gn_gain, w_out_mix, w_pw1, b_pw1, w_dw, b_dw, conv_ln_g, conv_ln_b, w_pw2, w_router_group, b_router_group, w_router_expert, b_router_expert, w_exp_gate, w_exp_up, w_exp_down):
    bp, t, d = x_prompt.shape
    bs, l, _ = x_sample.shape
    past = cache_fox_k.shape[2]
    depth = norm_mix.shape[0]
    assert d == D_MODEL and depth == 2 and w_in_mix.shape[0] == 1 and w_pw1.shape[0] == 1

    w_in = w_in_mix[0]
    n_pre = 3 * FOX_W
    w_in_r = jnp.concatenate(
        [w_in[:, :n_pre], w_in[:, n_pre + FOX_HEADS:], w_in[:, n_pre:n_pre + FOX_HEADS],
         jnp.zeros((D_MODEL, LANES - FOX_HEADS), F32)], axis=1).astype(BF16)
    hid = jnp.arange(FOX_W) // FOX_HD
    gbd = jnp.where(hid[:, None] == hid[None, :], 1.0 / FOX_HD, 0.0).astype(BF16)
    w_r = jnp.concatenate([w_router_group, w_router_expert], axis=-1)
    w_r = jnp.pad(w_r, ((0, 0), (0, 0), (0, LANES - w_r.shape[-1])))
    wr_hi = w_r.astype(BF16)
    wr_lo = (w_r - wr_hi.astype(F32)).astype(BF16)
    b_r = jnp.concatenate([b_router_group, b_router_expert], axis=-1)
    b_r = jnp.pad(b_r, ((0, 0), (0, LANES - b_r.shape[-1])))
    w_t = jnp.swapaxes(jnp.concatenate([w_router_expert, w_router_group], axis=-1), 1, 2)
    w_t = jnp.pad(w_t, ((0, 0), (0, ROUTE_ROWS - w_t.shape[1]), (0, 0)))
    wt_hi = w_t.astype(BF16)
    wt_lo = (w_t - wt_hi.astype(F32)).astype(BF16)
    b_t = jnp.concatenate([b_router_expert, b_router_group], axis=-1)
    b_t = jnp.pad(b_t, ((0, 0), (0, ROUTE_ROWS - b_t.shape[-1])))[:, :, None]
    log_gamma = jnp.log(1.0 - 2.0 ** (-5.0 - jnp.arange(RET_HEADS, dtype=F32)))
    p = {
        "norm_mix": norm_mix.reshape(depth, 1, D_MODEL),
        "norm_ffn": norm_ffn.reshape(depth, 1, D_MODEL),
        "w_in": w_in_r,
        "b_f": _pad_lanes(b_forget[0]),
        "q_gain": jnp.tile(fox_q_gain[0], FOX_HEADS).reshape(1, FOX_W),
        "k_gain": jnp.tile(fox_k_gain[0], FOX_HEADS).reshape(1, FOX_W),
        "gbd": gbd,
        "gn_gain": ret_gn_gain[0].reshape(1, RET_V_W),
        "lg_tab": jnp.broadcast_to(log_gamma[:, None, None], (RET_HEADS, 1, LANES)),
        "w_out": w_out_mix[0].astype(BF16),
        "w_pw1": w_pw1[0].astype(BF16),
        "b_pw1": b_pw1[0].reshape(1, -1),
        "w_dw": jnp.pad(w_dw[0], ((0, HIST_ROWS - CONV_W), (0, 0))),
        "b_dw": b_dw[0].reshape(1, -1),
        "ln_g": conv_ln_g[0].reshape(1, -1),
        "ln_b": conv_ln_b[0].reshape(1, -1),
        "w_pw2": w_pw2[0].astype(BF16),
        "wr_hi": wr_hi,
        "wr_lo": wr_lo,
        "br": b_r.reshape(depth, 1, LANES),
        "wt_hi": wt_hi,
        "wt_lo": wt_lo,
        "bt": b_t,
        "wg": w_exp_gate.reshape(depth * N_EXPERTS, D_MODEL, D_EXPERT),
        "wu": w_exp_up.reshape(depth * N_EXPERTS, D_MODEL, D_EXPERT),
        "wd": w_exp_down.reshape(depth * N_EXPERTS, D_EXPERT, D_MODEL),
    }

    hist_pad = HIST_ROWS - (CONV_W - 1)
    yp, fk_p, fv_p, lf_p, rs_p, u_p = _run_group(
        x_prompt.reshape(bp * t, D_MODEL), p, lambda x, layer: _moe_sparse(x, p, layer, 512), seq=t, tm=512,
        fox_hist=None,
        ret_state=jnp.zeros((bp, RET_HEADS, RET_DK, RET_DV), F32),
        conv_hist=jnp.zeros((bp, HIST_ROWS, D_MODEL), F32),
        pos=jnp.arange(t), tq=256, tk=256, ret_c=256, conv_tt=256)

    ns = bs * l
    ys, fk_s, fv_s, lf_s, rs_s, u_s = _run_group(
        x_sample.reshape(ns, D_MODEL), p, lambda x, layer: _moe_dense(x, p, layer, ns), seq=l, tm=ns,
        fox_hist=(cache_fox_k[0].reshape(bs, past, FOX_W), cache_fox_v[0].reshape(bs, past, FOX_W),
                  cache_fox_logf[0]),
        ret_state=state_ret[0],
        conv_hist=jnp.pad(cache_conv[0], ((0, 0), (hist_pad, 0), (0, 0))),
        pos=past + (jnp.arange(ns) % l), tq=l, tk=-(-(past + l) // CUM_BLOCK) * CUM_BLOCK, ret_c=l, conv_tt=l)

    conv_p = u_p[:, t - (CONV_W - 1):][None]
    conv_s = jnp.concatenate([cache_conv[0], u_s], axis=1)[:, l:][None]
    return (yp, ys, fk_p, fv_p, lf_p, rs_p, conv_p, fk_s, fv_s, lf_s, rs_s, conv_s)
```

```python
import functools
import math

import jax
import jax.numpy as jnp
import numpy as np
from jax import lax
from jax.experimental import pallas as pl
from jax.experimental.pallas import tpu as pltpu

F32 = jnp.float32
BF16 = jnp.bfloat16

D_MODEL = 1024
FOX_HEADS = 8
FOX_HD = 64
RET_HEADS = 4
RET_DK = 64
RET_DV = 128
ROPE_BASE = 10000.0
CONV_W = 31
N_GROUPS = 4
EXP_PER_GROUP = 8
N_EXPERTS = N_GROUPS * EXP_PER_GROUP
D_EXPERT = 256
EPS = 1e-6
FOX_W = FOX_HEADS * FOX_HD
RET_QK_W = RET_HEADS * RET_DK
RET_V_W = RET_HEADS * RET_DV

LANES = 128
HIST_ROWS = 32
ROUTER_LANE0 = N_GROUPS
MASKED = -1e30
VMEM_LIMIT = 56 * 1024 * 1024
CUM_BLOCK = 256
MOE_TM = 512
PLAN_TB = 512
ROUTE_ROWS = 48

C_FQ, C_FK, C_FV = 0, FOX_W, 2 * FOX_W
C_RQ = 3 * FOX_W
C_RK = C_RQ + RET_QK_W
C_RV = C_RK + RET_QK_W
C_RG = C_RV + RET_V_W
C_FF = C_RG + RET_V_W
MIX_COLS = C_FF + LANES


def _cparams(*sem):
    return pltpu.CompilerParams(dimension_semantics=sem, vmem_limit_bytes=VMEM_LIMIT)


def _full(shape):
    n = len(shape)
    return pl.BlockSpec(shape, lambda *_: (0,) * n)


def _rms(x, g):
    ms = jnp.mean(x * x, axis=-1, keepdims=True)
    return x * lax.rsqrt(ms + EPS) * g


def _sigmoid(x):
    return 1.0 / (1.0 + jnp.exp(-x))


def _dot(a, b):
    return jnp.dot(a, b, preferred_element_type=F32)


def _dot_nt(a, b):
    return lax.dot_general(a, b, (((1,), (1,)), ((), ())), preferred_element_type=F32)


def _dot_tn(a, b):
    return lax.dot_general(a, b, (((0,), (0,)), ((), ())), preferred_element_type=F32)


def _proj0_kernel(x_ref, g_ref, w_ref, bf_ref, qg_ref, kg_ref, gbd_ref, cos_ref, sin_ref,
                  q_ref, k_ref, v_ref, lf_ref, rq_ref, rk_ref, rv_ref, rg_ref, k5_ref, v5_ref):
    h = _rms(x_ref[...], g_ref[...]).astype(BF16)

    def seg(a, b):
        return _dot(h, w_ref[:, a:b])

    gbd = gbd_ref[...]

    def head_rms(y, gain):
        ms = _dot((y * y).astype(BF16), gbd)
        return y * lax.rsqrt(ms + EPS) * gain

    q_ref[...] = (head_rms(seg(C_FQ, C_FK), qg_ref[...]) * (FOX_HD ** -0.5)).astype(BF16)
    kk = head_rms(seg(C_FK, C_FV), kg_ref[...])
    vv = seg(C_FV, C_RQ)
    k_ref[...] = kk
    v_ref[...] = vv
    k5_ref[...] = kk.reshape(kk.shape[0], FOX_HEADS, FOX_HD)
    v5_ref[...] = vv.reshape(vv.shape[0], FOX_HEADS, FOX_HD)

    z = seg(C_FF, MIX_COLS) + bf_ref[...]
    logf = jnp.minimum(z, 0.0) - jnp.log(1.0 + jnp.exp(-jnp.abs(z)))
    lane = lax.broadcasted_iota(jnp.int32, (1, LANES), 1)
    lf_ref[...] = jnp.where(lane < FOX_HEADS, logf, 0.0)

    cos = cos_ref[...]
    sin = sin_ref[...]
    first_half = (lane % RET_DK) < (RET_DK // 2)

    def rotary(y):
        outs = []
        for s in range(y.shape[1] // LANES):
            ys = y[:, s * LANES:(s + 1) * LANES]
            rot = jnp.where(first_half, pltpu.roll(ys, LANES - RET_DK // 2, 1),
                            pltpu.roll(ys, RET_DK // 2, 1))
            outs.append(ys * cos + rot * sin)
        return jnp.concatenate(outs, axis=1)

    rq_ref[...] = rotary(seg(C_RQ, C_RK)).astype(BF16)
    rk_ref[...] = (rotary(seg(C_RK, C_RV)) * (RET_DK ** -0.5)).astype(BF16)
    rv_ref[...] = seg(C_RV, C_RG).astype(BF16)
    rg_ref[...] = seg(C_RG, C_FF).astype(BF16)


def _proj0(x, g, w, bf, qg, kg, gbd, cos, sin, tm):
    n = x.shape[0]
    nper = cos.shape[0] // tm
    row = lambda width: pl.BlockSpec((tm, width), lambda i: (i, 0))
    tab = pl.BlockSpec((tm, LANES), lambda i: (i % nper, 0))
    outs = [(FOX_W, BF16), (FOX_W, F32), (FOX_W, F32), (LANES, F32),
            (RET_QK_W, BF16), (RET_QK_W, BF16), (RET_V_W, BF16), (RET_V_W, BF16)]
    return pl.pallas_call(
        _proj0_kernel,
        grid=(n // tm,),
        in_specs=[row(D_MODEL), _full((1, D_MODEL)), _full((D_MODEL, MIX_COLS)), _full((1, LANES)),
                  _full((1, FOX_W)), _full((1, FOX_W)), _full((FOX_W, FOX_W)), tab, tab],
        out_specs=[row(wd) for wd, _ in outs] + [pl.BlockSpec((tm, FOX_HEADS, FOX_HD), lambda i: (i, 0, 0))] * 2,
        out_shape=[jax.ShapeDtypeStruct((n, wd), dt) for wd, dt in outs]
        + [jax.ShapeDtypeStruct((n, FOX_HEADS, FOX_HD), F32)] * 2,
        compiler_params=_cparams("parallel"),
        name="proj0",
    )(x, g, w, bf, qg, kg, gbd, cos, sin)


def _cum_kernel(lf_ref, ccol_ref, crow_ref, *, t, cb):
    r = lax.broadcasted_iota(jnp.int32, (cb, cb), 0)
    c = lax.broadcasted_iota(jnp.int32, (cb, cb), 1)
    tri = jnp.where(r >= c, 1.0, 0.0).astype(BF16)
    carry = jnp.zeros((1, LANES), F32)
    for blk in range(t // cb):
        a = lf_ref[0, blk * cb:(blk + 1) * cb, :]
        a1 = a.astype(BF16)
        r1 = a - a1.astype(F32)
        a2 = r1.astype(BF16)
        a3 = (r1 - a2.astype(F32)).astype(BF16)
        cc = (_dot(tri, a1) + _dot(tri, a2)) + _dot(tri, a3) + carry
        ccol_ref[0, blk * cb:(blk + 1) * cb, :] = cc
        crow_ref[0, :, blk * cb:(blk + 1) * cb] = cc.T[0:FOX_HEADS, :]
        carry = cc[cb - 1:cb, :]


def _cumsum_logf(lf, cb):
    b, t, _ = lf.shape
    return pl.pallas_call(
        functools.partial(_cum_kernel, t=t, cb=cb),
        grid=(b,),
        in_specs=[pl.BlockSpec((1, t, LANES), lambda i: (i, 0, 0))],
        out_specs=[pl.BlockSpec((1, t, LANES), lambda i: (i, 0, 0)),
                   pl.BlockSpec((1, FOX_HEADS, t), lambda i: (i, 0, 0))],
        out_shape=[jax.ShapeDtypeStruct((b, t, LANES), F32),
                   jax.ShapeDtypeStruct((b, FOX_HEADS, t), F32)],
        compiler_params=_cparams("parallel"),
        name="cumsum_logf",
    )(lf)


def _fox_kernel(q_ref, k_ref, v_ref, cq_ref, ck_ref, o_ref, *, tq, tk, q_off):
    hp = pl.program_id(1)
    i = pl.program_id(2)
    q0 = q_off + i * tq
    nfull = (q0 + 1) // tk
    nch = (q0 + tq + tk - 1) // tk
    lane = lax.broadcasted_iota(jnp.int32, (1, LANES), 1)
    q = q_ref[0]
    cq_all = cq_ref[0]
    qpos = q0 + lax.broadcasted_iota(jnp.int32, (tq, 1), 0)
    kiota = lax.broadcasted_iota(jnp.int32, (1, tk), 1)
    qms = [jnp.where((lane // FOX_HD) == h2, q, jnp.zeros_like(q)) for h2 in range(2)]
    cqs = [jnp.sum(jnp.where(lane == 2 * hp + h2, cq_all, 0.0), axis=-1, keepdims=True) for h2 in range(2)]

    def step(j, carry, masked):
        start = pl.multiple_of(j * tk, tk)
        kj = k_ref[0, pl.ds(start, tk), :].astype(BF16)
        vj = v_ref[0, pl.ds(start, tk), :].astype(BF16)
        new = []
        for h2 in range(2):
            m, l, acc = carry[h2]
            s = _dot_nt(qms[h2], kj)
            s = s + cqs[h2] - ck_ref[0, 2 * hp + h2, j]
            if masked:
                s = jnp.where(j * tk + kiota <= qpos, s, MASKED)
            m_new = jnp.maximum(m, jnp.max(s, axis=-1, keepdims=True))
            alpha = jnp.exp(m - m_new)
            p = jnp.exp(s - m_new)
            l = alpha * l + jnp.sum(p, axis=-1, keepdims=True)
            acc = alpha * acc + _dot(p.astype(BF16), vj)
            new.append((m_new, l, acc))
        return tuple(new)

    one = (jnp.full((tq, 1), MASKED, F32), jnp.zeros((tq, 1), F32), jnp.zeros((tq, LANES), F32))
    carry = lax.fori_loop(0, nfull, functools.partial(step, masked=False), (one, one))
    carry = lax.fori_loop(nfull, nch, functools.partial(step, masked=True), carry)
    outs = [acc / l for _, l, acc in carry]
    o_ref[0] = jnp.where(lane < FOX_HD, outs[0], outs[1]).astype(BF16)


def _fox_attention(q, k, v, ccol, crow, tq, tk, q_off):
    b, tqs, _ = q.shape
    tks = k.shape[1]
    nck = tks // tk
    crow5 = crow.reshape(b, FOX_HEADS, nck, 1, tk)
    qb0 = q_off // tq
    return pl.pallas_call(
        functools.partial(_fox_kernel, tq=tq, tk=tk, q_off=q_off),
        grid=(b, FOX_HEADS // 2, tqs // tq),
        in_specs=[pl.BlockSpec((1, tq, LANES), lambda bi, hp, i: (bi, i, hp)),
                  pl.BlockSpec((1, tks, LANES), lambda bi, hp, i: (bi, 0, hp)),
                  pl.BlockSpec((1, tks, LANES), lambda bi, hp, i: (bi, 0, hp)),
                  pl.BlockSpec((1, tq, LANES), lambda bi, hp, i: (bi, qb0 + i, 0)),
                  pl.BlockSpec((1, FOX_HEADS, nck, 1, tk), lambda bi, hp, i: (bi, 0, 0, 0, 0))],
        out_specs=pl.BlockSpec((1, tq, LANES), lambda bi, hp, i: (bi, i, hp)),
        out_shape=jax.ShapeDtypeStruct((b, tqs, FOX_W), BF16),
        compiler_params=_cparams("parallel", "parallel", "parallel"),
        name="fox_attention",
    )(q, k, v, ccol, crow5)


FOX_VROWS = 80


def _bf16_pieces(c):
    hi = c.astype(BF16).astype(F32)
    r = c - hi
    mid = r.astype(BF16).astype(F32)
    lo = (r - mid).astype(BF16).astype(F32)
    return hi, mid, lo


def _fox_prep_kernel(q_ref, k_ref, v_ref, c_ref, qa_ref, ka_ref, vt_ref, *, t, rc):
    hp = pl.program_id(1)
    lane = lax.broadcasted_iota(jnp.int32, (1, LANES), 1)
    ones = (1.0, 1.0, 1.0)

    def augment(xh, own, other):
        out = jnp.where(lane < FOX_HD, xh, 0.0)
        for piece in range(3):
            out = jnp.where(lane == FOX_HD + piece, own[piece], out)
            out = jnp.where(lane == FOX_HD + 3 + piece, other[piece], out)
        return out.astype(BF16)

    extra = jnp.where(lax.broadcasted_iota(jnp.int32, (FOX_VROWS - FOX_HD, rc), 0) == 0, 1.0, 0.0)
    for ch in range(t // rc):
        rows = slice(ch * rc, (ch + 1) * rc)
        qf = q_ref[0, rows, :].astype(F32)
        kf = k_ref[0, rows, :]
        call = c_ref[0, rows, :]
        vt = v_ref[0, rows, :].T
        for h2 in range(2):
            c = jnp.sum(jnp.where(lane == 2 * hp + h2, call, 0.0), axis=-1, keepdims=True)
            hi, mid, lo = _bf16_pieces(c)
            qh = qf if h2 == 0 else pltpu.roll(qf, FOX_HD, 1)
            kh = kf if h2 == 0 else pltpu.roll(kf, FOX_HD, 1)
            qa_ref[0, h2, rows, :] = augment(qh, (hi, mid, lo), ones)
            ka_ref[0, h2, rows, :] = augment(kh, ones, (-hi, -mid, -lo))
            vt_ref[0, h2, ch] = jnp.concatenate([vt[h2 * FOX_HD:(h2 + 1) * FOX_HD, :], extra], axis=0).astype(BF16)


def _fox_prep(q, k, v, ccol, rc):
    b, t, _ = q.shape
    pair = pl.BlockSpec((1, t, LANES), lambda bi, hp: (bi, 0, hp))
    aug = pl.BlockSpec((1, 2, t, LANES), lambda bi, hp: (bi, hp, 0, 0))
    return pl.pallas_call(
        functools.partial(_fox_prep_kernel, t=t, rc=rc),
        grid=(b, FOX_HEADS // 2),
        in_specs=[pair, pair, pair, pl.BlockSpec((1, t, LANES), lambda bi, hp: (bi, 0, 0))],
        out_specs=[aug, aug, pl.BlockSpec((1, 2, t // rc, FOX_VROWS, rc), lambda bi, hp: (bi, hp, 0, 0, 0))],
        out_shape=[jax.ShapeDtypeStruct((b, FOX_HEADS, t, LANES), BF16),
                   jax.ShapeDtypeStruct((b, FOX_HEADS, t, LANES), BF16),
                   jax.ShapeDtypeStruct((b, FOX_HEADS, t // rc, FOX_VROWS, rc), BF16)],
        compiler_params=_cparams("parallel", "parallel"),
        name="fox_prep",
    )(q, k, v, ccol)


def _fox_t_kernel(qa_ref, ka_ref, vt_ref, o_ref, s_scr, acc_scr, *, tq, tk):
    i = pl.program_id(1)
    q0 = i * tq
    nfull = (q0 + 1) // tk
    nch = (q0 + tq + tk - 1) // tk
    qpos = q0 + lax.broadcasted_iota(jnp.int32, (1, tq), 1)
    kiota = lax.broadcasted_iota(jnp.int32, (tk, 1), 0)
    heads = range(FOX_HEADS)

    def scores(j, mx, masked):
        start = pl.multiple_of(j * tk, tk)
        out = []
        for h in heads:
            st = _dot_nt(ka_ref[0, h, pl.ds(start, tk), :], qa_ref[0, h])
            if masked:
                st = jnp.where(start + kiota <= qpos, st, MASKED)
            s_scr[h, j] = st
            out.append(jnp.maximum(mx[h], jnp.max(st, axis=0, keepdims=True)))
        return tuple(out)

    mx = tuple(jnp.full((1, tq), MASKED, F32) for _ in heads)
    mx = lax.fori_loop(0, nfull, functools.partial(scores, masked=False), mx)
    mx = lax.fori_loop(nfull, nch, functools.partial(scores, masked=True), mx)

    acc_scr[...] = jnp.zeros_like(acc_scr)

    def weigh(j, carry):
        for h in heads:
            p = jnp.exp(s_scr[h, j] - mx[h]).astype(BF16)
            acc_scr[h] += _dot(vt_ref[0, h, j], p)
        return carry

    lax.fori_loop(0, nch, weigh, 0)
    for hp in range(FOX_HEADS // 2):
        outs = []
        for h in (2 * hp, 2 * hp + 1):
            acc = acc_scr[h]
            outs.append(acc[0:FOX_HD, :] / acc[FOX_HD:FOX_HD + 1, :])
        o_ref[0, :, hp * LANES:(hp + 1) * LANES] = jnp.concatenate(outs, axis=0).T.astype(BF16)


def _fox_attention_t(qa, ka, vt, tq):
    b, _, t, _ = qa.shape
    nck, tk = vt.shape[2], vt.shape[4]
    return pl.pallas_call(
        functools.partial(_fox_t_kernel, tq=tq, tk=tk),
        grid=(b, t // tq),
        in_specs=[pl.BlockSpec((1, FOX_HEADS, tq, LANES), lambda bi, i: (bi, 0, i, 0)),
                  pl.BlockSpec((1, FOX_HEADS, t, LANES), lambda bi, i: (bi, 0, 0, 0)),
                  pl.BlockSpec((1, FOX_HEADS, nck, FOX_VROWS, tk), lambda bi, i: (bi, 0, 0, 0, 0))],
        out_specs=pl.BlockSpec((1, tq, FOX_W), lambda bi, i: (bi, i, 0)),
        out_shape=jax.ShapeDtypeStruct((b, t, FOX_W), BF16),
        scratch_shapes=[pltpu.VMEM((FOX_HEADS, nck, tk, tq), F32),
                        pltpu.VMEM((FOX_HEADS, FOX_VROWS, tq), F32)],
        compiler_params=_cparams("parallel", "arbitrary"),
        name="fox_attention_t",
    )(qa, ka, vt)


def _ret_kernel(lg_ref, rq_ref, rk_ref, rv_ref, rg_ref, s0_ref, gn_ref, o_ref, sfin_ref, s_scr, d_scr, *, c):
    first = (pl.program_id(0) == 0) & (pl.program_id(1) == 0)
    ci = pl.program_id(1)
    ii = lax.broadcasted_iota(jnp.int32, (c, 1), 0).astype(F32)
    lane = lax.broadcasted_iota(jnp.int32, (1, LANES), 1)

    @pl.when(first)
    def _():
        jj = lax.broadcasted_iota(jnp.int32, (1, c), 1).astype(F32)
        diff = ii - jj
        for h in range(RET_HEADS):
            lg = lg_ref[h][:, 0:1]
            d_scr[h] = jnp.where(diff >= 0.0, jnp.exp(lg * jnp.maximum(diff, 0.0)), 0.0)

    @pl.when(ci == 0)
    def _():
        zero = jnp.zeros((RET_DK, RET_DV), F32)
        for h in range(RET_HEADS):
            lo, hi = (s0_ref[0, h], zero) if h % 2 == 0 else (zero, s0_ref[0, h])
            s_scr[h, 0:RET_DK, :] = lo
            s_scr[h, RET_DK:2 * RET_DK, :] = hi

    for h in range(RET_HEADS):
        lg = lg_ref[h][:, 0:1]
        pair = slice((h // 2) * LANES, (h // 2 + 1) * LANES)
        mine = slice(h * RET_DV, (h + 1) * RET_DV)
        inhead = (lane // RET_DK) == (h % 2)
        q = rq_ref[0, :, pair]
        k = rk_ref[0, :, pair]
        qm = jnp.where(inhead, q, jnp.zeros_like(q))
        km = jnp.where(inhead, k, jnp.zeros_like(k))
        v = rv_ref[0, :, mine]
        scores = _dot_nt(qm, km) * d_scr[h]
        inner = _dot(scores.astype(BF16), v)
        s_prev = s_scr[h]
        cross = _dot(qm, s_prev.astype(BF16)) * jnp.exp(lg * (ii + 1.0))
        y = inner + cross
        k_dec = (km.astype(F32) * jnp.exp(lg * (c - 1.0 - ii))).astype(BF16)
        s_new = jnp.exp(lg * float(c)) * s_prev + _dot_tn(k_dec, v)
        s_scr[h] = s_new

        mu = jnp.mean(y, axis=-1, keepdims=True)
        yc = y - mu
        var = jnp.mean(yc * yc, axis=-1, keepdims=True)
        yn = yc * lax.rsqrt(var + EPS) * gn_ref[:, mine]
        g = rg_ref[0, :, mine].astype(F32)
        o_ref[0, :, mine] = ((g * _sigmoid(g)) * yn).astype(BF16)
        off = (h % 2) * RET_DK
        sfin_ref[0, h] = s_new[off:off + RET_DK, :]


def _retention(lg_tab, rq, rk, rv, rg, s0, gn, c):
    b, t, _ = rq.shape
    qk_spec = pl.BlockSpec((1, c, RET_QK_W), lambda bi, ci: (bi, ci, 0))
    v_spec = pl.BlockSpec((1, c, RET_V_W), lambda bi, ci: (bi, ci, 0))
    st_spec = pl.BlockSpec((1, RET_HEADS, RET_DK, RET_DV), lambda bi, ci: (bi, 0, 0, 0))
    return pl.pallas_call(
        functools.partial(_ret_kernel, c=c),
        grid=(b, t // c),
        in_specs=[_full((RET_HEADS, 1, LANES)), qk_spec, qk_spec, v_spec, v_spec, st_spec,
                  _full((1, RET_V_W))],
        out_specs=[v_spec, st_spec],
        out_shape=[jax.ShapeDtypeStruct((b, t, RET_V_W), BF16),
                   jax.ShapeDtypeStruct((b, RET_HEADS, RET_DK, RET_DV), F32)],
        scratch_shapes=[pltpu.VMEM((RET_HEADS, 2 * RET_DK, RET_DV), F32), pltpu.VMEM((RET_HEADS, c, c), F32)],
        compiler_params=_cparams("arbitrary", "arbitrary"),
        name="retention",
    )(lg_tab, rq, rk, rv, rg, s0, gn)


def _out0_kernel(x_ref, of_ref, or_ref, w_ref, *rest):
    y_ref = rest[-4] if len(rest) > 1 else rest[0]
    mix = _dot(of_ref[...], w_ref[0:FOX_W, :]) + _dot(or_ref[...], w_ref[FOX_W:FOX_W + RET_V_W, :])
    y = x_ref[...] + mix
    y_ref[...] = y
    if len(rest) > 1:
        g_ref, wth_ref, wtl_ref, bt_ref, _, h_ref, route_ref, wcol_ref = rest
        _route_tokens(y, g_ref, wth_ref, wtl_ref, bt_ref, h_ref, route_ref, wcol_ref)


def _out0(x, o_fox, o_ret, w, tm, router=None):
    n = x.shape[0]
    row = lambda width: pl.BlockSpec((tm, width), lambda i: (i, 0))
    in_specs = [row(D_MODEL), row(FOX_W), row(RET_V_W), _full((FOX_W + RET_V_W, D_MODEL))]
    out_specs = [row(D_MODEL)]
    out_shape = [jax.ShapeDtypeStruct((n, D_MODEL), F32)]
    args = [x, o_fox, o_ret, w]
    if router is not None:
        in_specs += [_full((1, D_MODEL)), _full((ROUTE_ROWS, D_MODEL)), _full((ROUTE_ROWS, D_MODEL)),
                     _full((ROUTE_ROWS, 1))]
        out_specs += [pl.BlockSpec((tm * SLAB, LANES), lambda i: (i, 0)), pl.BlockSpec((8, tm), lambda i: (0, i)),
                      row(LANES)]
        out_shape += [jax.ShapeDtypeStruct((n * SLAB, LANES), F32), jax.ShapeDtypeStruct((8, n), F32),
                      jax.ShapeDtypeStruct((n, LANES), F32)]
        args += list(router)
    return pl.pallas_call(
        _out0_kernel,
        grid=(n // tm,),
        in_specs=in_specs,
        out_specs=out_specs,
        out_shape=out_shape,
        compiler_params=_cparams("parallel"),
        name="out_proj0",
    )(*args)


def _router_kernel(x_ref, g_ref, wh_ref, wl_ref, b_ref, h_ref, gates_ref):
    hf = _rms(x_ref[...], g_ref[...])
    hh = hf.astype(BF16)
    hl = (hf - hh.astype(F32)).astype(BF16)
    h_ref[...] = hh
    wh = wh_ref[...]
    logits = _dot(hh, wh) + (_dot(hl, wh) + _dot(hh, wl_ref[...])) + b_ref[...]

    lane = lax.broadcasted_iota(jnp.int32, (1, LANES), 1)
    lanef = lane.astype(F32)
    ninf = -jnp.inf
    is_grp = lane < N_GROUPS
    gl = jnp.where(is_grp, logits, ninf)
    gmax = jnp.max(gl, axis=-1, keepdims=True)
    grp = jnp.min(jnp.where(gl == gmax, lanef, 1e9), axis=-1, keepdims=True)
    p_grp = 1.0 / jnp.sum(jnp.exp(gl - gmax), axis=-1, keepdims=True)

    is_exp = (lane >= ROUTER_LANE0) & (lane < ROUTER_LANE0 + N_EXPERTS)
    lane_grp = ((lane - ROUTER_LANE0) // EXP_PER_GROUP).astype(F32)
    em = jnp.where(is_exp & (lane_grp == grp), logits, ninf)
    v1 = jnp.max(em, axis=-1, keepdims=True)
    i1 = jnp.min(jnp.where(em == v1, lanef, 1e9), axis=-1, keepdims=True)
    em2 = jnp.where(lanef == i1, ninf, em)
    v2 = jnp.max(em2, axis=-1, keepdims=True)
    i2 = jnp.min(jnp.where(em2 == v2, lanef, 1e9), axis=-1, keepdims=True)
    t = jnp.exp(v2 - v1)
    w1 = (1.0 / (1.0 + t)) * p_grp
    w2 = (t / (1.0 + t)) * p_grp
    gates_ref[...] = jnp.where(lanef == i1, w1, 0.0) + jnp.where(lanef == i2, w2, 0.0)


def _router(x, g, wh, wl, b, tm):
    n = x.shape[0]
    row = lambda width: pl.BlockSpec((tm, width), lambda i: (i, 0))
    return pl.pallas_call(
        _router_kernel,
        grid=(n // tm,),
        in_specs=[row(D_MODEL), _full((1, D_MODEL)), _full((D_MODEL, LANES)), _full((D_MODEL, LANES)),
                  _full((1, LANES))],
        out_specs=[row(D_MODEL), row(LANES)],
        out_shape=[jax.ShapeDtypeStruct((n, D_MODEL), BF16), jax.ShapeDtypeStruct((n, LANES), F32)],
        compiler_params=_cparams("parallel"),
        name="router",
    )(x, g, wh, wl, b)


def _moe_kernel(h_ref, gates_ref, x_ref, wg_ref, wu_ref, wd_ref, y_ref):
    e = pl.program_id(1)

    @pl.when(e == 0)
    def _():
        y_ref[...] = jnp.zeros_like(y_ref)

    h = h_ref[...]
    a = _dot(h, wg_ref[0].astype(BF16))
    u = _dot(h, wu_ref[0].astype(BF16))
    lane = lax.broadcasted_iota(jnp.int32, (1, LANES), 1)
    gate = jnp.sum(jnp.where(lane == e + ROUTER_LANE0, gates_ref[...], 0.0), axis=-1, keepdims=True)
    hid = (a * _sigmoid(a)) * u * gate
    y_ref[...] += _dot(hid.astype(BF16), wd_ref[0].astype(BF16))

    @pl.when(e == pl.num_programs(1) - 1)
    def _():
        y_ref[...] += x_ref[...]


def _moe(h, gates, x, wg, wu, wd, layer, tm):
    n = x.shape[0]
    first = layer * N_EXPERTS
    row = lambda width: pl.BlockSpec((tm, width), lambda i, e: (i, 0))
    return pl.pallas_call(
        _moe_kernel,
        grid=(n // tm, N_EXPERTS),
        in_specs=[row(D_MODEL), row(LANES), row(D_MODEL),
                  pl.BlockSpec((1, D_MODEL, D_EXPERT), lambda i, e: (first + e, 0, 0)),
                  pl.BlockSpec((1, D_MODEL, D_EXPERT), lambda i, e: (first + e, 0, 0)),
                  pl.BlockSpec((1, D_EXPERT, D_MODEL), lambda i, e: (first + e, 0, 0))],
        out_specs=row(D_MODEL),
        out_shape=jax.ShapeDtypeStruct((n, D_MODEL), F32),
        compiler_params=_cparams("parallel", "arbitrary"),
        name="moe_experts",
    )(h, gates, x, wg, wu, wd)


def _router_t_kernel(x_ref, g_ref, wth_ref, wtl_ref, bt_ref, h_ref, route_ref, wcol_ref):
    _route_tokens(x_ref[...], g_ref, wth_ref, wtl_ref, bt_ref, h_ref, route_ref, wcol_ref)


def _route_tokens(x, g_ref, wth_ref, wtl_ref, bt_ref, h_ref, route_ref, wcol_ref):
    hf = _rms(x, g_ref[...])
    _to_slabs(h_ref, hf)
    hh = hf.astype(BF16)
    hl = (hf - hh.astype(F32)).astype(BF16)
    wth = wth_ref[...]
    logits = _dot_nt(wth, hh) + (_dot_nt(wth, hl) + _dot_nt(wtl_ref[...], hh)) + bt_ref[...]

    row = lax.broadcasted_iota(jnp.int32, (ROUTE_ROWS, 1), 0)
    rowf = row.astype(F32)
    ninf = -jnp.inf
    is_grp = (row >= N_EXPERTS) & (row < N_EXPERTS + N_GROUPS)
    gl = jnp.where(is_grp, logits, ninf)
    gmax = jnp.max(gl, axis=0, keepdims=True)
    grp = jnp.min(jnp.where(gl == gmax, rowf, 1e9), axis=0, keepdims=True) - float(N_EXPERTS)
    p_grp = 1.0 / jnp.sum(jnp.exp(gl - gmax), axis=0, keepdims=True)

    row_grp = (row // EXP_PER_GROUP).astype(F32)
    em = jnp.where((row < N_EXPERTS) & (row_grp == grp), logits, ninf)
    v1 = jnp.max(em, axis=0, keepdims=True)
    i1 = jnp.min(jnp.where(em == v1, rowf, 1e9), axis=0, keepdims=True)
    em2 = jnp.where(rowf == i1, ninf, em)
    v2 = jnp.max(em2, axis=0, keepdims=True)
    i2 = jnp.min(jnp.where(em2 == v2, rowf, 1e9), axis=0, keepdims=True)
    t = jnp.exp(v2 - v1)
    w1 = (1.0 / (1.0 + t)) * p_grp
    w2 = (t / (1.0 + t)) * p_grp

    tm = logits.shape[1]
    r128 = lax.broadcasted_iota(jnp.int32, (LANES, 1), 0)
    rt = (jnp.where(r128 == 0, i1, 0.0) + jnp.where(r128 == 1, i2, 0.0)
          + jnp.where(r128 == 2, w1, 0.0) + jnp.where(r128 == 3, w2, 0.0))
    route_ref[...] = rt[0:8, :]
    wcol_ref[...] = rt.T


def _router_t(x, g, wth, wtl, bt, tm):
    n = x.shape[0]
    row = lambda width: pl.BlockSpec((tm, width), lambda i: (i, 0))
    return pl.pallas_call(
        _router_t_kernel,
        grid=(n // tm,),
        in_specs=[row(D_MODEL), _full((1, D_MODEL)), _full((ROUTE_ROWS, D_MODEL)),
                  _full((ROUTE_ROWS, D_MODEL)), _full((ROUTE_ROWS, 1))],
        out_specs=[pl.BlockSpec((tm * SLAB, LANES), lambda i: (i, 0)), pl.BlockSpec((8, tm), lambda i: (0, i)),
                   row(LANES)],
        out_shape=[jax.ShapeDtypeStruct((n * SLAB, LANES), F32), jax.ShapeDtypeStruct((8, n), F32),
                   jax.ShapeDtypeStruct((n, LANES), F32)],
        compiler_params=_cparams("parallel"),
        name="router_t",
    )(x, g, wth, wtl, bt)


def _plan_kernel(route_ref, pos_ref, te_ref, nt_ref, pad_ref, *, n, tb, tm):
    nb = n // tb
    erow = lax.broadcasted_iota(jnp.int32, (N_EXPERTS, 1), 0).astype(F32)

    def picks(blk):
        i1 = route_ref[0:1, blk * tb:(blk + 1) * tb]
        i2 = route_ref[1:2, blk * tb:(blk + 1) * tb]
        return erow == i1, erow == i2

    acc = jnp.zeros((N_EXPERTS, tb), F32)
    for blk in range(nb):
        e1, e2 = picks(blk)
        acc = acc + jnp.where(e1, 1.0, 0.0) + jnp.where(e2, 1.0, 0.0)
    counts = jnp.sum(acc, axis=1, keepdims=True).astype(jnp.int32)
    assert tm & (tm - 1) == 0
    ntile = jnp.right_shift(counts + (tm - 1), tm.bit_length() - 1).astype(F32)
    lane = lax.broadcasted_iota(jnp.int32, (1, LANES), 1).astype(F32)
    ntile_row = jnp.sum(jnp.where(lane == erow, ntile, 0.0), axis=0, keepdims=True)
    tend = jnp.sum(jnp.where(lane <= erow, ntile_row, 0.0), axis=1, keepdims=True)
    off = (tend - ntile) * float(tm)
    ntot = jnp.sum(ntile, axis=0, keepdims=True)
    kk = jnp.minimum(lax.broadcasted_iota(jnp.int32, (1, 2 * LANES), 1).astype(F32), ntot - 1.0)
    te_ref[...] = jnp.sum(jnp.where(tend <= kk, 1.0, 0.0), axis=0, keepdims=True).astype(jnp.int32)
    nt_ref[...] = jnp.broadcast_to(ntot, (1, LANES)).astype(jnp.int32)
    pad0 = off + counts.astype(F32)
    pad_ref[...] = jnp.sum(jnp.where(lane == erow, pad0, 0.0), axis=0, keepdims=True).astype(jnp.int32)

    r = lax.broadcasted_iota(jnp.int32, (tb, tb), 0)
    c = lax.broadcasted_iota(jnp.int32, (tb, tb), 1)
    before = jnp.where(r < c, 1.0, 0.0).astype(BF16)
    carry = jnp.zeros((N_EXPERTS, 1), F32)
    for blk in range(nb):
        e1, e2 = picks(blk)
        mt = jnp.where(e1, 1.0, 0.0) + jnp.where(e2, 1.0, 0.0)
        slot = _dot(mt.astype(BF16), before) + (carry + off)
        pos_ref[blk, 0:1, :] = jnp.sum(jnp.where(e1, slot, 0.0), axis=0, keepdims=True).astype(jnp.int32)
        pos_ref[blk, 1:2, :] = jnp.sum(jnp.where(e2, slot, 0.0), axis=0, keepdims=True).astype(jnp.int32)
        carry = carry + jnp.sum(mt, axis=1, keepdims=True)


def _plan(route, tb, tm):
    n = route.shape[1]
    vm = pl.BlockSpec(memory_space=pltpu.VMEM)
    return pl.pallas_call(
        functools.partial(_plan_kernel, n=n, tb=tb, tm=tm),
        in_specs=[vm],
        out_specs=[vm, vm, vm, vm],
        out_shape=[jax.ShapeDtypeStruct((n // tb, 2, tb), jnp.int32),
                   jax.ShapeDtypeStruct((1, 2 * LANES), jnp.int32),
                   jax.ShapeDtypeStruct((1, LANES), jnp.int32),
                   jax.ShapeDtypeStruct((1, LANES), jnp.int32)],
        compiler_params=pltpu.CompilerParams(vmem_limit_bytes=VMEM_LIMIT),
        name="moe_plan",
    )(route)


SLAB = D_MODEL // LANES


def _to_slabs(ref, y):
    m = y.shape[0]
    for s_ in range(SLAB):
        ref[pl.ds(s_, m, stride=SLAB), :] = y[:, s_ * LANES:(s_ + 1) * LANES]


def _slab_piece(ref, s_, m):
    return ref[pl.ds(s_, m, stride=SLAB), :]


def _from_slabs(ref, m):
    return jnp.concatenate([_slab_piece(ref, s_, m) for s_ in range(SLAB)], axis=1)


def _tok_copy(src, src_tok, dst, dst_tok, sem):
    return pltpu.make_async_copy(src.at[pl.ds(pl.multiple_of(src_tok * SLAB, SLAB), SLAB)],
                                 dst.at[pl.ds(pl.multiple_of(dst_tok * SLAB, SLAB), SLAB)], sem)


DMA_UNROLL = 8


def _dispatch_kernel(pos_ref, pad_ref, nt_ref, h_ref, xs_hbm, zero_scr, sem, zsem, *, ts, tm, ntmax):
    @pl.when(pl.program_id(0) == 0)
    def _():
        zero_scr[...] = jnp.zeros_like(zero_scr)

        def fill(first_slot):
            return pltpu.make_async_copy(
                zero_scr, xs_hbm.at[pl.ds(pl.multiple_of(first_slot * SLAB, SLAB), tm * SLAB)], zsem)

        for e in range(N_EXPERTS):
            fill(pad_ref[0, e]).start()
        for e in range(N_EXPERTS):
            fill(pad_ref[0, e]).wait()

        ntot = nt_ref[0, 0]

        def tail_start(k, carry):
            fill(k * tm).start()
            return carry

        def tail_wait(k, carry):
            fill(k * tm).wait()
            return carry

        lax.fori_loop(ntot, ntmax, tail_start, 0)
        lax.fori_loop(ntot, ntmax, tail_wait, 0)

    def issue(g, carry):
        for u in range(DMA_UNROLL):
            r = g * DMA_UNROLL + u
            for ch in range(2):
                _tok_copy(h_ref, r, xs_hbm, pos_ref[0, ch, r], sem).start(priority=ch)
        return carry

    def drain(g, carry):
        for _ in range(2 * DMA_UNROLL):
            _tok_copy(h_ref, 0, xs_hbm, 0, sem).wait()
        return carry

    lax.fori_loop(0, ts // DMA_UNROLL, issue, 0)
    lax.fori_loop(0, ts // DMA_UNROLL, drain, 0)


def _dispatch(pos, pad, nt, h, ts, tm):
    nb = pos.shape[0]
    n = h.shape[0] // SLAB
    ntmax = _sorted_rows(n) // tm + 1
    smem = pl.BlockSpec(memory_space=pltpu.SMEM)
    return pl.pallas_call(
        functools.partial(_dispatch_kernel, ts=ts, tm=tm, ntmax=ntmax),
        grid=(nb,),
        in_specs=[pl.BlockSpec((1, 2, ts), lambda i: (i, 0, 0), memory_space=pltpu.SMEM), smem, smem,
                  pl.BlockSpec((ts * SLAB, LANES), lambda i: (i, 0))],
        out_specs=pl.BlockSpec(memory_space=pl.ANY),
        out_shape=jax.ShapeDtypeStruct((ntmax * tm * SLAB, LANES), F32),
        scratch_shapes=[pltpu.VMEM((tm * SLAB, LANES), F32), pltpu.SemaphoreType.DMA,
                        pltpu.SemaphoreType.DMA],
        compiler_params=_cparams("arbitrary"),
        name="moe_dispatch",
    )(pos, pad, nt, h)


def _experts_kernel(te_ref, nt_ref, xs_ref, wg_ref, wu_ref, wd_ref, ys_ref, *, tm):
    used = pl.program_id(0) < nt_ref[0]

    @pl.when(used)
    def _():
        x = _from_slabs(xs_ref, tm).astype(BF16)
        a = _dot(x, wg_ref[0].astype(BF16))
        u = _dot(x, wu_ref[0].astype(BF16))
        hid = (a * _sigmoid(a)) * u
        _to_slabs(ys_ref, _dot(hid.astype(BF16), wd_ref[0].astype(BF16)))

    @pl.when(jnp.logical_not(used))
    def _():
        ys_ref[...] = jnp.zeros_like(ys_ref)


def _experts(te, nt, xs, wg, wu, wd, layer, tm):
    ntmax = xs.shape[0] // (tm * SLAB)
    first = layer * N_EXPERTS
    tile = pl.BlockSpec((tm * SLAB, LANES), lambda k, te, nt: (jnp.minimum(k, nt[0] - 1), 0))
    out_tile = pl.BlockSpec((tm * SLAB, LANES), lambda k, te, nt: (k, 0))
    return pl.pallas_call(
        functools.partial(_experts_kernel, tm=tm),
        grid_spec=pltpu.PrefetchScalarGridSpec(
            num_scalar_prefetch=2,
            grid=(ntmax,),
            in_specs=[tile,
                      pl.BlockSpec((1, D_MODEL, D_EXPERT), lambda k, te, nt: (first + te[k], 0, 0)),
                      pl.BlockSpec((1, D_MODEL, D_EXPERT), lambda k, te, nt: (first + te[k], 0, 0)),
                      pl.BlockSpec((1, D_EXPERT, D_MODEL), lambda k, te, nt: (first + te[k], 0, 0))],
            out_specs=out_tile),
        out_shape=jax.ShapeDtypeStruct(xs.shape, F32),
        compiler_params=_cparams("arbitrary"),
        name="moe_experts_sorted",
    )(te, nt, xs, wg, wu, wd)


def _combine_kernel(pos_ref, posn_ref, x_ref, w_ref, ys_hbm, y_ref, buf, sems, *, tc):
    i = pl.program_id(0)
    nb = pl.num_programs(0)

    def start(p_ref, slot):
        def body(g, carry):
            for u in range(DMA_UNROLL):
                r = g * DMA_UNROLL + u
                for ch in range(2):
                    _tok_copy(ys_hbm, p_ref[0, ch, r], buf.at[slot, ch], r, sems.at[slot]).start(priority=ch)
            return carry
        lax.fori_loop(0, tc // DMA_UNROLL, body, 0)

    def wait(slot):
        def body(g, carry):
            for _ in range(2 * DMA_UNROLL):
                _tok_copy(ys_hbm, 0, buf.at[slot, 0], 0, sems.at[slot]).wait()
            return carry
        lax.fori_loop(0, tc // DMA_UNROLL, body, 0)

    @pl.when(i == 0)
    def _():
        start(pos_ref, 0)

    w = w_ref[...]
    w1, w2 = w[:, 2:3], w[:, 3:4]
    for slot in range(2):
        @pl.when(i % 2 == slot)
        def _(slot=slot):
            @pl.when(i + 1 < nb)
            def _():
                start(posn_ref, 1 - slot)

            wait(slot)
            for s_ in range(SLAB):
                cols = slice(s_ * LANES, (s_ + 1) * LANES)
                y_ref[:, cols] = x_ref[:, cols] + (w1 * _slab_piece(buf.at[slot, 0], s_, tc)
                                                   + w2 * _slab_piece(buf.at[slot, 1], s_, tc))


def _combine(pos, x, wcol, ys, tc):
    n = x.shape[0]
    nb = n // tc
    row = lambda width: pl.BlockSpec((tc, width), lambda i: (i, 0))
    return pl.pallas_call(
        functools.partial(_combine_kernel, tc=tc),
        grid=(nb,),
        in_specs=[pl.BlockSpec((1, 2, tc), lambda i: (i, 0, 0), memory_space=pltpu.SMEM),
                  pl.BlockSpec((1, 2, tc), lambda i: (jnp.minimum(i + 1, nb - 1), 0, 0), memory_space=pltpu.SMEM),
                  row(D_MODEL), row(LANES), pl.BlockSpec(memory_space=pl.ANY)],
        out_specs=row(D_MODEL),
        out_shape=jax.ShapeDtypeStruct((n, D_MODEL), F32),
        scratch_shapes=[pltpu.VMEM((2, 2, tc * SLAB, LANES), F32), pltpu.SemaphoreType.DMA((2,))],
        compiler_params=_cparams("arbitrary"),
        name="moe_combine",
    )(pos, pos, x, wcol, ys)


def _pw1_kernel(x_ref, g_ref, w_ref, b_ref, u_ref):
    h = _rms(x_ref[...], g_ref[...]).astype(BF16)
    a = _dot(h, w_ref[:, 0:D_MODEL]) + b_ref[:, 0:D_MODEL]
    g = _dot(h, w_ref[:, D_MODEL:2 * D_MODEL]) + b_ref[:, D_MODEL:2 * D_MODEL]
    u_ref[...] = a * _sigmoid(g)


def _pw1(x, g, w, b, tm):
    n = x.shape[0]
    row = lambda width: pl.BlockSpec((tm, width), lambda i: (i, 0))
    return pl.pallas_call(
        _pw1_kernel,
        grid=(n // tm,),
        in_specs=[row(D_MODEL), _full((1, D_MODEL)), _full((D_MODEL, 2 * D_MODEL)), _full((1, 2 * D_MODEL))],
        out_specs=row(D_MODEL),
        out_shape=jax.ShapeDtypeStruct((n, D_MODEL), F32),
        compiler_params=_cparams("parallel"),
        name="conv_pw1_glu",
    )(x, g, w, b)


def _conv_kernel(u_ref, hist_ref, x_ref, wdw_ref, bdw_ref, lng_ref, lnb_ref, w2_ref, y_ref, ext_scr, sh_scr,
                 *, tt, rs):
    ti = pl.program_id(1)

    @pl.when(ti == 0)
    def _():
        ext_scr[0:HIST_ROWS, :] = hist_ref[0]

    ext_scr[HIST_ROWS:HIST_ROWS + tt, :] = u_ref[0]
    pad = HIST_ROWS - (CONV_W - 1)
    rows = tt + HIST_ROWS - 8
    for r in range(1, 8):
        sh_scr[r - 1, :, :] = ext_scr[r:r + rows, :]
    parts = []
    for r0 in range(0, tt, rs):
        acc = jnp.zeros((rs, D_MODEL), F32)
        for kk in range(CONV_W):
            a, r = divmod(kk + pad, 8)
            lo = r0 + 8 * a
            src = ext_scr[lo:lo + rs, :] if r == 0 else sh_scr[r - 1, lo:lo + rs, :]
            acc = acc + wdw_ref[kk:kk + 1, :] * src
        parts.append(acc)
    y = jnp.concatenate(parts, axis=0) + bdw_ref[...]
    mu = jnp.mean(y, axis=-1, keepdims=True)
    yc = y - mu
    var = jnp.mean(yc * yc, axis=-1, keepdims=True)
    yn = yc * lax.rsqrt(var + EPS) * lng_ref[...] + lnb_ref[...]
    z = yn * _sigmoid(yn)
    y_ref[0] = x_ref[0] + _dot(z.astype(BF16), w2_ref[...])
    if tt >= HIST_ROWS:
        ext_scr[0:HIST_ROWS, :] = ext_scr[tt:tt + HIST_ROWS, :]


def _conv_module(u, hist, x, wdw, bdw, lng, lnb, w2, tt):
    b, t, _ = u.shape
    rs = min(tt, 32)
    blk = pl.BlockSpec((1, tt, D_MODEL), lambda bi, ti: (bi, ti, 0))
    return pl.pallas_call(
        functools.partial(_conv_kernel, tt=tt, rs=rs),
        grid=(b, t // tt),
        in_specs=[blk, pl.BlockSpec((1, HIST_ROWS, D_MODEL), lambda bi, ti: (bi, 0, 0)), blk,
                  _full((HIST_ROWS, D_MODEL)), _full((1, D_MODEL)), _full((1, D_MODEL)),
                  _full((1, D_MODEL)), _full((D_MODEL, D_MODEL))],
        out_specs=blk,
        out_shape=jax.ShapeDtypeStruct((b, t, D_MODEL), F32),
        scratch_shapes=[pltpu.VMEM((HIST_ROWS + tt, D_MODEL), F32),
                        pltpu.VMEM((7, tt + HIST_ROWS - 8, D_MODEL), F32)],
        compiler_params=_cparams("parallel", "arbitrary"),
        name="conv_module",
    )(u, hist, x, wdw, bdw, lng, lnb, w2)


def _rope_tables(pos):
    half = RET_DK // 2
    inv_freq = ROPE_BASE ** (-jnp.arange(half, dtype=F32) / half)
    ang = pos.astype(F32)[:, None] * inv_freq[None, :]
    cos, sin = jnp.cos(ang), jnp.sin(ang)
    reps = LANES // RET_DK
    cos_t = jnp.tile(jnp.concatenate([cos, cos], axis=1), (1, reps))
    sin_t = jnp.tile(jnp.concatenate([-sin, sin], axis=1), (1, reps))
    return cos_t, sin_t


def _pad_lanes(v, width=LANES):
    v = v.reshape(1, -1)
    return jnp.pad(v, ((0, 0), (0, width - v.shape[1])))


def _moe_dense(x, p, l, tm):
    h, gates = _router(x, p["norm_ffn"][l], p["wr_hi"][l], p["wr_lo"][l], p["br"][l], tm)
    return _moe(h, gates, x, p["wg"], p["wu"], p["wd"], l, tm)


def _sorted_rows(n_tokens):
    return (2 * n_tokens // MOE_TM + N_EXPERTS) * MOE_TM


def _router_args(p, l):
    return p["norm_ffn"][l], p["wt_hi"][l], p["wt_lo"][l], p["bt"][l]


def _moe_sparse(x, p, l, tm, routed=None):
    h, route, wcol = routed if routed is not None else _router_t(x, *_router_args(p, l), tm)
    pos, te, nt, pad = _plan(route, PLAN_TB, MOE_TM)
    xs = _dispatch(pos, pad, nt, h, PLAN_TB, MOE_TM)
    ys = _experts(te.reshape(-1), nt[0, :1], xs, p["wg"], p["wu"], p["wd"], l, MOE_TM)
    return _combine(pos, x, wcol, ys, PLAN_TB)


def _run_group(x, p, *, sparse, seq, tm, fox_hist, ret_state, conv_hist, pos, tq, tk, ret_c, conv_tt):
    b = x.shape[0] // seq
    q_off = 0 if fox_hist is None else fox_hist[0].shape[1]

    cos_t, sin_t = _rope_tables(pos)
    q, k, v, lf, rq, rk, rv, rg, k5, v5 = _proj0(x, p["norm_mix"][0], p["w_in"], p["b_f"], p["q_gain"], p["k_gain"],
                                          p["gbd"], cos_t, sin_t, tm)
    k3 = k.reshape(b, seq, FOX_W)
    v3 = v.reshape(b, seq, FOX_W)
    lf3 = lf.reshape(b, seq, LANES)
    if fox_hist is None:
        k_all, v_all, lf_all = k3, v3, lf3
    else:
        ck_, cv_, clf_ = fox_hist
        tot = q_off + seq
        padded = -(-tot // tk) * tk
        tail = padded - tot
        k_all = jnp.concatenate([ck_, k3, jnp.zeros((b, tail, FOX_W), F32)], axis=1)
        v_all = jnp.concatenate([cv_, v3, jnp.zeros((b, tail, FOX_W), F32)], axis=1)
        clf_ = jnp.pad(clf_, ((0, 0), (0, 0), (0, LANES - FOX_HEADS)))
        lf_all = jnp.concatenate([clf_, lf3, jnp.zeros((b, tail, LANES), F32)], axis=1)
    ccol, crow = _cumsum_logf(lf_all, CUM_BLOCK)
    if fox_hist is None:
        qa, ka, vt = _fox_prep(q.reshape(b, seq, FOX_W), k_all, v_all, ccol, tk)
        o_fox = _fox_attention_t(qa, ka, vt, tq)
    else:
        o_fox = _fox_attention(q.reshape(b, seq, FOX_W), k_all, v_all, ccol, crow, tq, tk, q_off)
    o_ret, s_fin = _retention(p["lg_tab"], rq.reshape(b, seq, RET_QK_W), rk.reshape(b, seq, RET_QK_W),
                              rv.reshape(b, seq, RET_V_W), rg.reshape(b, seq, RET_V_W), ret_state,
                              p["gn_gain"], ret_c)
    o_fox2, o_ret2 = o_fox.reshape(-1, FOX_W), o_ret.reshape(-1, RET_V_W)
    if sparse:
        x, *routed = _out0(x, o_fox2, o_ret2, p["w_out"], tm, _router_args(p, 0))
        x = _moe_sparse(x, p, 0, tm, routed)
    else:
        x = _moe_dense(_out0(x, o_fox2, o_ret2, p["w_out"], tm)[0], p, 0, tm)

    u = _pw1(x, p["norm_mix"][1], p["w_pw1"], p["b_pw1"], tm)
    u3 = u.reshape(b, seq, D_MODEL)
    x = _conv_module(u3, conv_hist, x.reshape(b, seq, D_MODEL), p["w_dw"], p["b_dw"], p["ln_g"], p["ln_b"],
                     p["w_pw2"], conv_tt).reshape(-1, D_MODEL)
    x = _moe_sparse(x, p, 1, tm) if sparse else _moe_dense(x, p, 1, tm)

    fox_k = k5.reshape(1, b, seq, FOX_HEADS, FOX_HD)
    fox_v = v5.reshape(1, b, seq, FOX_HEADS, FOX_HD)
    fox_lf = lf3[:, :, :FOX_HEADS].reshape(1, b, seq, FOX_HEADS)
    return x.reshape(b, seq, D_MODEL), fox_k, fox_v, fox_lf, s_fin[None], u3


def kernel(x_prompt, x_sample, cache_fox_k, cache_fox_v, cache_fox_logf, state_ret, cache_conv, norm_mix, norm_ffn, w_in_mix, b_forget, fox_q_gain, fox_k_gain, ret_gn_gain, w_out_mix, w_pw1, b_pw1, w_dw, b_dw, conv_ln_g, conv_ln_b, w_pw2, w_router_group, b_router_group, w_router_expert, b_router_expert, w_exp_gate, w_exp_up, w_exp_down):
    bp, t, d = x_prompt.shape
    bs, l, _ = x_sample.shape
    past = cache_fox_k.shape[2]
    depth = norm_mix.shape[0]
    assert d == D_MODEL and depth == 2 and w_in_mix.shape[0] == 1 and w_pw1.shape[0] == 1

    w_in = w_in_mix[0]
    n_pre = 3 * FOX_W
    w_in_r = jnp.concatenate(
        [w_in[:, :n_pre], w_in[:, n_pre + FOX_HEADS:], w_in[:, n_pre:n_pre + FOX_HEADS],
         jnp.zeros((D_MODEL, LANES - FOX_HEADS), F32)], axis=1).astype(BF16)
    hid = jnp.arange(FOX_W) // FOX_HD
    gbd = jnp.where(hid[:, None] == hid[None, :], 1.0 / FOX_HD, 0.0).astype(BF16)
    w_r = jnp.concatenate([w_router_group, w_router_expert], axis=-1)
    w_r = jnp.pad(w_r, ((0, 0), (0, 0), (0, LANES - w_r.shape[-1])))
    wr_hi = w_r.astype(BF16)
    wr_lo = (w_r - wr_hi.astype(F32)).astype(BF16)
    b_r = jnp.concatenate([b_router_group, b_router_expert], axis=-1)
    b_r = jnp.pad(b_r, ((0, 0), (0, LANES - b_r.shape[-1])))
    w_t = jnp.swapaxes(jnp.concatenate([w_router_expert, w_router_group], axis=-1), 1, 2)
    w_t = jnp.pad(w_t, ((0, 0), (0, ROUTE_ROWS - w_t.shape[1]), (0, 0)))
    wt_hi = w_t.astype(BF16)
    wt_lo = (w_t - wt_hi.astype(F32)).astype(BF16)
    b_t = jnp.concatenate([b_router_expert, b_router_group], axis=-1)
    b_t = jnp.pad(b_t, ((0, 0), (0, ROUTE_ROWS - b_t.shape[-1])))[:, :, None]
    log_gamma = jnp.log(1.0 - 2.0 ** (-5.0 - jnp.arange(RET_HEADS, dtype=F32)))
    p = {
        "norm_mix": norm_mix.reshape(depth, 1, D_MODEL),
        "norm_ffn": norm_ffn.reshape(depth, 1, D_MODEL),
        "w_in": w_in_r,
        "b_f": _pad_lanes(b_forget[0]),
        "q_gain": jnp.tile(fox_q_gain[0], FOX_HEADS).reshape(1, FOX_W),
        "k_gain": jnp.tile(fox_k_gain[0], FOX_HEADS).reshape(1, FOX_W),
        "gbd": gbd,
        "gn_gain": ret_gn_gain[0].reshape(1, RET_V_W),
        "lg_tab": jnp.broadcast_to(log_gamma[:, None, None], (RET_HEADS, 1, LANES)),
        "w_out": w_out_mix[0].astype(BF16),
        "w_pw1": w_pw1[0].astype(BF16),
        "b_pw1": b_pw1[0].reshape(1, -1),
        "w_dw": jnp.pad(w_dw[0], ((0, HIST_ROWS - CONV_W), (0, 0))),
        "b_dw": b_dw[0].reshape(1, -1),
        "ln_g": conv_ln_g[0].reshape(1, -1),
        "ln_b": conv_ln_b[0].reshape(1, -1),
        "w_pw2": w_pw2[0].astype(BF16),
        "wr_hi": wr_hi,
        "wr_lo": wr_lo,
        "br": b_r.reshape(depth, 1, LANES),
        "wt_hi": wt_hi,
        "wt_lo": wt_lo,
        "bt": b_t,
        "wg": w_exp_gate.reshape(depth * N_EXPERTS, D_MODEL, D_EXPERT),
        "wu": w_exp_up.reshape(depth * N_EXPERTS, D_MODEL, D_EXPERT),
        "wd": w_exp_down.reshape(depth * N_EXPERTS, D_EXPERT, D_MODEL),
    }

    hist_pad = HIST_ROWS - (CONV_W - 1)
    yp, fk_p, fv_p, lf_p, rs_p, u_p = _run_group(
        x_prompt.reshape(bp * t, D_MODEL), p, sparse=True, seq=t, tm=512, fox_hist=None,
        ret_state=jnp.zeros((bp, RET_HEADS, RET_DK, RET_DV), F32),
        conv_hist=jnp.zeros((bp, HIST_ROWS, D_MODEL), F32),
        pos=jnp.arange(t), tq=256, tk=256, ret_c=256, conv_tt=256)

    ns = bs * l
    ys, fk_s, fv_s, lf_s, rs_s, u_s = _run_group(
        x_sample.reshape(ns, D_MODEL), p, sparse=False, seq=l, tm=ns,
        fox_hist=(cache_fox_k[0].reshape(bs, past, FOX_W), cache_fox_v[0].reshape(bs, past, FOX_W),
                  cache_fox_logf[0]),
        ret_state=state_ret[0],
        conv_hist=jnp.pad(cache_conv[0], ((0, 0), (hist_pad, 0), (0, 0))),
        pos=past + (jnp.arange(ns) % l), tq=l, tk=-(-(past + l) // CUM_BLOCK) * CUM_BLOCK, ret_c=l, conv_tt=l)

    conv_p = u_p[:, t - (CONV_W - 1):][None]
    conv_s = jnp.concatenate([cache_conv[0], u_s], axis=1)[:, l:][None]
    return (yp, ys, fk_p, fv_p, lf_p, rs_p, conv_p, fk_s, fv_s, lf_s, rs_s, conv_s)
```

```python
import functools
import math

import jax
import jax.numpy as jnp
import numpy as np
from jax import lax
from jax.experimental import pallas as pl
from jax.experimental.pallas import tpu as pltpu

F32 = jnp.float32
BF16 = jnp.bfloat16

D_MODEL = 1024
FOX_HEADS = 8
FOX_HD = 64
RET_HEADS = 4
RET_DK = 64
RET_DV = 128
ROPE_BASE = 10000.0
CONV_W = 31
N_GROUPS = 4
EXP_PER_GROUP = 8
N_EXPERTS = N_GROUPS * EXP_PER_GROUP
D_EXPERT = 256
EPS = 1e-6
FOX_W = FOX_HEADS * FOX_HD
RET_QK_W = RET_HEADS * RET_DK
RET_V_W = RET_HEADS * RET_DV

LANES = 128
HIST_ROWS = 32
ROUTER_LANE0 = N_GROUPS
MASKED = -1e30
VMEM_LIMIT = 56 * 1024 * 1024
CUM_BLOCK = 256
MOE_TM = 512
PLAN_TB = 512
ROUTE_ROWS = 48

C_FQ, C_FK, C_FV = 0, FOX_W, 2 * FOX_W
C_RQ = 3 * FOX_W
C_RK = C_RQ + RET_QK_W
C_RV = C_RK + RET_QK_W
C_RG = C_RV + RET_V_W
C_FF = C_RG + RET_V_W
MIX_COLS = C_FF + LANES


def _cparams(*sem):
    return pltpu.CompilerParams(dimension_semantics=sem, vmem_limit_bytes=VMEM_LIMIT)


def _full(shape):
    n = len(shape)
    return pl.BlockSpec(shape, lambda *_: (0,) * n)


def _rms(x, g):
    ms = jnp.mean(x * x, axis=-1, keepdims=True)
    return x * lax.rsqrt(ms + EPS) * g


def _sigmoid(x):
    return 1.0 / (1.0 + jnp.exp(-x))


def _dot(a, b):
    return jnp.dot(a, b, preferred_element_type=F32)


def _dot_nt(a, b):
    return lax.dot_general(a, b, (((1,), (1,)), ((), ())), preferred_element_type=F32)


def _dot_tn(a, b):
    return lax.dot_general(a, b, (((0,), (0,)), ((), ())), preferred_element_type=F32)


def _proj0_kernel(x_ref, g_ref, w_ref, bf_ref, qg_ref, kg_ref, gbd_ref, cos_ref, sin_ref,
                  q_ref, k_ref, v_ref, lf_ref, rq_ref, rk_ref, rv_ref, rg_ref, k5_ref, v5_ref):
    h = _rms(x_ref[...], g_ref[...]).astype(BF16)

    def seg(a, b):
        return _dot(h, w_ref[:, a:b])

    gbd = gbd_ref[...]

    def head_rms(y, gain):
        ms = _dot((y * y).astype(BF16), gbd)
        return y * lax.rsqrt(ms + EPS) * gain

    q_ref[...] = (head_rms(seg(C_FQ, C_FK), qg_ref[...]) * (FOX_HD ** -0.5)).astype(BF16)
    kk = head_rms(seg(C_FK, C_FV), kg_ref[...])
    vv = seg(C_FV, C_RQ)
    k_ref[...] = kk
    v_ref[...] = vv
    k5_ref[...] = kk.reshape(kk.shape[0], FOX_HEADS, FOX_HD)
    v5_ref[...] = vv.reshape(vv.shape[0], FOX_HEADS, FOX_HD)

    z = seg(C_FF, MIX_COLS) + bf_ref[...]
    logf = jnp.minimum(z, 0.0) - jnp.log(1.0 + jnp.exp(-jnp.abs(z)))
    lane = lax.broadcasted_iota(jnp.int32, (1, LANES), 1)
    lf_ref[...] = jnp.where(lane < FOX_HEADS, logf, 0.0)

    cos = cos_ref[...]
    sin = sin_ref[...]
    first_half = (lane % RET_DK) < (RET_DK // 2)

    def rotary(y):
        outs = []
        for s in range(y.shape[1] // LANES):
            ys = y[:, s * LANES:(s + 1) * LANES]
            rot = jnp.where(first_half, pltpu.roll(ys, LANES - RET_DK // 2, 1),
                            pltpu.roll(ys, RET_DK // 2, 1))
            outs.append(ys * cos + rot * sin)
        return jnp.concatenate(outs, axis=1)

    rq_ref[...] = rotary(seg(C_RQ, C_RK)).astype(BF16)
    rk_ref[...] = (rotary(seg(C_RK, C_RV)) * (RET_DK ** -0.5)).astype(BF16)
    rv_ref[...] = seg(C_RV, C_RG).astype(BF16)
    rg_ref[...] = seg(C_RG, C_FF).astype(BF16)


def _proj0(x, g, w, bf, qg, kg, gbd, cos, sin, tm):
    n = x.shape[0]
    nper = cos.shape[0] // tm
    row = lambda width: pl.BlockSpec((tm, width), lambda i: (i, 0))
    tab = pl.BlockSpec((tm, LANES), lambda i: (i % nper, 0))
    outs = [(FOX_W, BF16), (FOX_W, F32), (FOX_W, F32), (LANES, F32),
            (RET_QK_W, BF16), (RET_QK_W, BF16), (RET_V_W, BF16), (RET_V_W, BF16)]
    return pl.pallas_call(
        _proj0_kernel,
        grid=(n // tm,),
        in_specs=[row(D_MODEL), _full((1, D_MODEL)), _full((D_MODEL, MIX_COLS)), _full((1, LANES)),
                  _full((1, FOX_W)), _full((1, FOX_W)), _full((FOX_W, FOX_W)), tab, tab],
        out_specs=[row(wd) for wd, _ in outs] + [pl.BlockSpec((tm, FOX_HEADS, FOX_HD), lambda i: (i, 0, 0))] * 2,
        out_shape=[jax.ShapeDtypeStruct((n, wd), dt) for wd, dt in outs]
        + [jax.ShapeDtypeStruct((n, FOX_HEADS, FOX_HD), F32)] * 2,
        compiler_params=_cparams("parallel"),
        name="proj0",
    )(x, g, w, bf, qg, kg, gbd, cos, sin)


def _cum_kernel(lf_ref, ccol_ref, crow_ref, *, t, cb):
    r = lax.broadcasted_iota(jnp.int32, (cb, cb), 0)
    c = lax.broadcasted_iota(jnp.int32, (cb, cb), 1)
    tri = jnp.where(r >= c, 1.0, 0.0).astype(BF16)
    carry = jnp.zeros((1, LANES), F32)
    for blk in range(t // cb):
        a = lf_ref[0, blk * cb:(blk + 1) * cb, :]
        a1 = a.astype(BF16)
        r1 = a - a1.astype(F32)
        a2 = r1.astype(BF16)
        a3 = (r1 - a2.astype(F32)).astype(BF16)
        cc = (_dot(tri, a1) + _dot(tri, a2)) + _dot(tri, a3) + carry
        ccol_ref[0, blk * cb:(blk + 1) * cb, :] = cc
        crow_ref[0, :, blk * cb:(blk + 1) * cb] = cc.T[0:FOX_HEADS, :]
        carry = cc[cb - 1:cb, :]


def _cumsum_logf(lf, cb):
    b, t, _ = lf.shape
    return pl.pallas_call(
        functools.partial(_cum_kernel, t=t, cb=cb),
        grid=(b,),
        in_specs=[pl.BlockSpec((1, t, LANES), lambda i: (i, 0, 0))],
        out_specs=[pl.BlockSpec((1, t, LANES), lambda i: (i, 0, 0)),
                   pl.BlockSpec((1, FOX_HEADS, t), lambda i: (i, 0, 0))],
        out_shape=[jax.ShapeDtypeStruct((b, t, LANES), F32),
                   jax.ShapeDtypeStruct((b, FOX_HEADS, t), F32)],
        compiler_params=_cparams("parallel"),
        name="cumsum_logf",
    )(lf)


def _fox_kernel(q_ref, k_ref, v_ref, cq_ref, ck_ref, o_ref, *, tq, tk, q_off):
    hp = pl.program_id(1)
    i = pl.program_id(2)
    q0 = q_off + i * tq
    nfull = (q0 + 1) // tk
    nch = (q0 + tq + tk - 1) // tk
    lane = lax.broadcasted_iota(jnp.int32, (1, LANES), 1)
    q = q_ref[0]
    cq_all = cq_ref[0]
    qpos = q0 + lax.broadcasted_iota(jnp.int32, (tq, 1), 0)
    kiota = lax.broadcasted_iota(jnp.int32, (1, tk), 1)
    qms = [jnp.where((lane // FOX_HD) == h2, q, jnp.zeros_like(q)) for h2 in range(2)]
    cqs = [jnp.sum(jnp.where(lane == 2 * hp + h2, cq_all, 0.0), axis=-1, keepdims=True) for h2 in range(2)]

    def step(j, carry, masked):
        start = pl.multiple_of(j * tk, tk)
        kj = k_ref[0, pl.ds(start, tk), :].astype(BF16)
        vj = v_ref[0, pl.ds(start, tk), :].astype(BF16)
        new = []
        for h2 in range(2):
            m, l, acc = carry[h2]
            s = _dot_nt(qms[h2], kj)
            s = s + cqs[h2] - ck_ref[0, 2 * hp + h2, j]
            if masked:
                s = jnp.where(j * tk + kiota <= qpos, s, MASKED)
            m_new = jnp.maximum(m, jnp.max(s, axis=-1, keepdims=True))
            alpha = jnp.exp(m - m_new)
            p = jnp.exp(s - m_new)
            l = alpha * l + jnp.sum(p, axis=-1, keepdims=True)
            acc = alpha * acc + _dot(p.astype(BF16), vj)
            new.append((m_new, l, acc))
        return tuple(new)

    one = (jnp.full((tq, 1), MASKED, F32), jnp.zeros((tq, 1), F32), jnp.zeros((tq, LANES), F32))
    carry = lax.fori_loop(0, nfull, functools.partial(step, masked=False), (one, one))
    carry = lax.fori_loop(nfull, nch, functools.partial(step, masked=True), carry)
    outs = [acc / l for _, l, acc in carry]
    o_ref[0] = jnp.where(lane < FOX_HD, outs[0], outs[1]).astype(BF16)


def _fox_attention(q, k, v, ccol, crow, tq, tk, q_off):
    b, tqs, _ = q.shape
    tks = k.shape[1]
    nck = tks // tk
    crow5 = crow.reshape(b, FOX_HEADS, nck, 1, tk)
    qb0 = q_off // tq
    return pl.pallas_call(
        functools.partial(_fox_kernel, tq=tq, tk=tk, q_off=q_off),
        grid=(b, FOX_HEADS // 2, tqs // tq),
        in_specs=[pl.BlockSpec((1, tq, LANES), lambda bi, hp, i: (bi, i, hp)),
                  pl.BlockSpec((1, tks, LANES), lambda bi, hp, i: (bi, 0, hp)),
                  pl.BlockSpec((1, tks, LANES), lambda bi, hp, i: (bi, 0, hp)),
                  pl.BlockSpec((1, tq, LANES), lambda bi, hp, i: (bi, qb0 + i, 0)),
                  pl.BlockSpec((1, FOX_HEADS, nck, 1, tk), lambda bi, hp, i: (bi, 0, 0, 0, 0))],
        out_specs=pl.BlockSpec((1, tq, LANES), lambda bi, hp, i: (bi, i, hp)),
        out_shape=jax.ShapeDtypeStruct((b, tqs, FOX_W), BF16),
        compiler_params=_cparams("parallel", "parallel", "parallel"),
        name="fox_attention",
    )(q, k, v, ccol, crow5)


FOX_VROWS = 80


def _bf16_pieces(c):
    hi = c.astype(BF16).astype(F32)
    r = c - hi
    mid = r.astype(BF16).astype(F32)
    lo = (r - mid).astype(BF16).astype(F32)
    return hi, mid, lo


def _fox_prep_kernel(q_ref, k_ref, v_ref, c_ref, qa_ref, ka_ref, vt_ref, *, t, rc):
    hp = pl.program_id(1)
    lane = lax.broadcasted_iota(jnp.int32, (1, LANES), 1)
    src = lax.broadcasted_iota(jnp.int32, (LANES, 1), 0)
    one_lane = 3 * FOX_HEADS

    def placer(h, c_to, one_to, sign):
        m = jnp.zeros((LANES, LANES), F32)
        for piece in range(3):
            m = jnp.where((src == piece * FOX_HEADS + h) & (lane == c_to + piece), sign, m)
            m = jnp.where((src == one_lane) & (lane == one_to + piece), 1.0, m)
        return m.astype(BF16)

    place_q = [placer(2 * hp + h2, FOX_HD, FOX_HD + 3, 1.0) for h2 in range(2)]
    place_k = [placer(2 * hp + h2, FOX_HD + 3, FOX_HD, -1.0) for h2 in range(2)]
    extra = jnp.where(lax.broadcasted_iota(jnp.int32, (FOX_VROWS - FOX_HD, rc), 0) == 0, 1.0, 0.0)
    for ch in range(t // rc):
        rows = slice(ch * rc, (ch + 1) * rc)
        qf = q_ref[0, rows, :].astype(F32)
        kf = k_ref[0, rows, :]
        hi, mid, lo = _bf16_pieces(c_ref[0, rows, :])
        pieces = (hi + pltpu.roll(mid, FOX_HEADS, 1) + pltpu.roll(lo, 2 * FOX_HEADS, 1)
                  + jnp.where(lane == one_lane, 1.0, 0.0)).astype(BF16)
        vt = v_ref[0, rows, :].T
        for h2 in range(2):
            qh = qf if h2 == 0 else pltpu.roll(qf, FOX_HD, 1)
            kh = kf if h2 == 0 else pltpu.roll(kf, FOX_HD, 1)
            qa_ref[0, h2, rows, :] = jnp.where(lane < FOX_HD, qh, _dot(pieces, place_q[h2])).astype(BF16)
            ka_ref[0, h2, rows, :] = jnp.where(lane < FOX_HD, kh, _dot(pieces, place_k[h2])).astype(BF16)
            vt_ref[0, h2, ch] = jnp.concatenate([vt[h2 * FOX_HD:(h2 + 1) * FOX_HD, :], extra], axis=0).astype(BF16)


def _fox_prep(q, k, v, ccol, rc):
    b, t, _ = q.shape
    pair = pl.BlockSpec((1, t, LANES), lambda bi, hp: (bi, 0, hp))
    aug = pl.BlockSpec((1, 2, t, LANES), lambda bi, hp: (bi, hp, 0, 0))
    return pl.pallas_call(
        functools.partial(_fox_prep_kernel, t=t, rc=rc),
        grid=(b, FOX_HEADS // 2),
        in_specs=[pair, pair, pair, pl.BlockSpec((1, t, LANES), lambda bi, hp: (bi, 0, 0))],
        out_specs=[aug, aug, pl.BlockSpec((1, 2, t // rc, FOX_VROWS, rc), lambda bi, hp: (bi, hp, 0, 0, 0))],
        out_shape=[jax.ShapeDtypeStruct((b, FOX_HEADS, t, LANES), BF16),
                   jax.ShapeDtypeStruct((b, FOX_HEADS, t, LANES), BF16),
                   jax.ShapeDtypeStruct((b, FOX_HEADS, t // rc, FOX_VROWS, rc), BF16)],
        compiler_params=_cparams("parallel", "parallel"),
        name="fox_prep",
    )(q, k, v, ccol)


def _fox_t_kernel(qa_ref, ka_ref, vt_ref, o_ref, s_scr, acc_scr, *, tq, tk):
    i = pl.program_id(1)
    q0 = i * tq
    nfull = (q0 + 1) // tk
    nch = (q0 + tq + tk - 1) // tk
    qpos = q0 + lax.broadcasted_iota(jnp.int32, (1, tq), 1)
    kiota = lax.broadcasted_iota(jnp.int32, (tk, 1), 0)
    heads = range(FOX_HEADS)

    def scores(j, mx, masked):
        start = pl.multiple_of(j * tk, tk)
        out = []
        for h in heads:
            st = _dot_nt(ka_ref[0, h, pl.ds(start, tk), :], qa_ref[0, h])
            if masked:
                st = jnp.where(start + kiota <= qpos, st, MASKED)
            s_scr[h, j] = st
            out.append(jnp.maximum(mx[h], jnp.max(st, axis=0, keepdims=True)))
        return tuple(out)

    mx = tuple(jnp.full((1, tq), MASKED, F32) for _ in heads)
    mx = lax.fori_loop(0, nfull, functools.partial(scores, masked=False), mx)
    mx = lax.fori_loop(nfull, nch, functools.partial(scores, masked=True), mx)

    acc_scr[...] = jnp.zeros_like(acc_scr)

    def weigh(j, carry):
        for h in heads:
            p = jnp.exp(s_scr[h, j] - mx[h]).astype(BF16)
            acc_scr[h] += _dot(vt_ref[0, h, j], p)
        return carry

    lax.fori_loop(0, nch, weigh, 0)
    for hp in range(FOX_HEADS // 2):
        outs = []
        for h in (2 * hp, 2 * hp + 1):
            acc = acc_scr[h]
            outs.append(acc[0:FOX_HD, :] / acc[FOX_HD:FOX_HD + 1, :])
        o_ref[0, :, hp * LANES:(hp + 1) * LANES] = jnp.concatenate(outs, axis=0).T.astype(BF16)


def _fox_attention_t(qa, ka, vt, tq):
    b, _, t, _ = qa.shape
    nck, tk = vt.shape[2], vt.shape[4]
    return pl.pallas_call(
        functools.partial(_fox_t_kernel, tq=tq, tk=tk),
        grid=(b, t // tq),
        in_specs=[pl.BlockSpec((1, FOX_HEADS, tq, LANES), lambda bi, i: (bi, 0, i, 0)),
                  pl.BlockSpec((1, FOX_HEADS, t, LANES), lambda bi, i: (bi, 0, 0, 0)),
                  pl.BlockSpec((1, FOX_HEADS, nck, FOX_VROWS, tk), lambda bi, i: (bi, 0, 0, 0, 0))],
        out_specs=pl.BlockSpec((1, tq, FOX_W), lambda bi, i: (bi, i, 0)),
        out_shape=jax.ShapeDtypeStruct((b, t, FOX_W), BF16),
        scratch_shapes=[pltpu.VMEM((FOX_HEADS, nck, tk, tq), F32),
                        pltpu.VMEM((FOX_HEADS, FOX_VROWS, tq), F32)],
        compiler_params=_cparams("parallel", "arbitrary"),
        name="fox_attention_t",
    )(qa, ka, vt)


def _ret_kernel(lg_ref, rq_ref, rk_ref, rv_ref, rg_ref, s0_ref, gn_ref, o_ref, sfin_ref, s_scr, d_scr, *, c, nbs):
    first = (pl.program_id(0) == 0) & (pl.program_id(1) == 0)
    ci = pl.program_id(1)
    ii = lax.broadcasted_iota(jnp.int32, (c, 1), 0).astype(F32)
    lane = lax.broadcasted_iota(jnp.int32, (1, LANES), 1)

    @pl.when(first)
    def _():
        jj = lax.broadcasted_iota(jnp.int32, (1, c), 1).astype(F32)
        diff = ii - jj
        for h in range(RET_HEADS):
            lg = lg_ref[h][:, 0:1]
            d_scr[h] = jnp.where(diff >= 0.0, jnp.exp(lg * jnp.maximum(diff, 0.0)), 0.0)

    @pl.when(ci == 0)
    def _():
        zero = jnp.zeros((RET_DK, RET_DV), F32)
        for bb in range(nbs):
            for h in range(RET_HEADS):
                lo, hi = (s0_ref[bb, h], zero) if h % 2 == 0 else (zero, s0_ref[bb, h])
                s_scr[bb * RET_HEADS + h, 0:RET_DK, :] = lo
                s_scr[bb * RET_HEADS + h, RET_DK:2 * RET_DK, :] = hi

    for bb, h in [(bb, h) for bb in range(nbs) for h in range(RET_HEADS)]:
        lg = lg_ref[h][:, 0:1]
        st = bb * RET_HEADS + h
        pair = slice((h // 2) * LANES, (h // 2 + 1) * LANES)
        mine = slice(h * RET_DV, (h + 1) * RET_DV)
        inhead = (lane // RET_DK) == (h % 2)
        q = rq_ref[bb, :, pair]
        k = rk_ref[bb, :, pair]
        qm = jnp.where(inhead, q, jnp.zeros_like(q))
        km = jnp.where(inhead, k, jnp.zeros_like(k))
        v = rv_ref[bb, :, mine]
        scores = _dot_nt(qm, km) * d_scr[h]
        inner = _dot(scores.astype(BF16), v)
        s_prev = s_scr[st]
        cross = _dot(qm, s_prev.astype(BF16)) * jnp.exp(lg * (ii + 1.0))
        y = inner + cross
        k_dec = (km.astype(F32) * jnp.exp(lg * (c - 1.0 - ii))).astype(BF16)
        s_new = jnp.exp(lg * float(c)) * s_prev + _dot_tn(k_dec, v)
        s_scr[st] = s_new

        mu = jnp.mean(y, axis=-1, keepdims=True)
        yc = y - mu
        var = jnp.mean(yc * yc, axis=-1, keepdims=True)
        yn = yc * lax.rsqrt(var + EPS) * gn_ref[:, mine]
        g = rg_ref[bb, :, mine].astype(F32)
        o_ref[bb, :, mine] = ((g * _sigmoid(g)) * yn).astype(BF16)
        off = (h % 2) * RET_DK
        sfin_ref[bb, h] = s_new[off:off + RET_DK, :]


def _retention(lg_tab, rq, rk, rv, rg, s0, gn, c):
    b, t, _ = rq.shape
    nbs = 2 if b % 2 == 0 else 1
    qk_spec = pl.BlockSpec((nbs, c, RET_QK_W), lambda bi, ci: (bi, ci, 0))
    v_spec = pl.BlockSpec((nbs, c, RET_V_W), lambda bi, ci: (bi, ci, 0))
    st_spec = pl.BlockSpec((nbs, RET_HEADS, RET_DK, RET_DV), lambda bi, ci: (bi, 0, 0, 0))
    return pl.pallas_call(
        functools.partial(_ret_kernel, c=c, nbs=nbs),
        grid=(b // nbs, t // c),
        in_specs=[_full((RET_HEADS, 1, LANES)), qk_spec, qk_spec, v_spec, v_spec, st_spec,
                  _full((1, RET_V_W))],
        out_specs=[v_spec, st_spec],
        out_shape=[jax.ShapeDtypeStruct((b, t, RET_V_W), BF16),
                   jax.ShapeDtypeStruct((b, RET_HEADS, RET_DK, RET_DV), F32)],
        scratch_shapes=[pltpu.VMEM((nbs * RET_HEADS, 2 * RET_DK, RET_DV), F32), pltpu.VMEM((RET_HEADS, c, c), F32)],
        compiler_params=_cparams("arbitrary", "arbitrary"),
        name="retention",
    )(lg_tab, rq, rk, rv, rg, s0, gn)


def _out0_kernel(x_ref, of_ref, or_ref, w_ref, *rest):
    y_ref = rest[-4] if len(rest) > 1 else rest[0]
    mix = _dot(of_ref[...], w_ref[0:FOX_W, :]) + _dot(or_ref[...], w_ref[FOX_W:FOX_W + RET_V_W, :])
    y = x_ref[...] + mix
    y_ref[...] = y
    if len(rest) > 1:
        g_ref, wth_ref, wtl_ref, bt_ref, _, h_ref, route_ref, wcol_ref = rest
        _route_tokens(y, g_ref, wth_ref, wtl_ref, bt_ref, h_ref, route_ref, wcol_ref)


def _out0(x, o_fox, o_ret, w, tm, router=None):
    n = x.shape[0]
    row = lambda width: pl.BlockSpec((tm, width), lambda i: (i, 0))
    in_specs = [row(D_MODEL), row(FOX_W), row(RET_V_W), _full((FOX_W + RET_V_W, D_MODEL))]
    out_specs = [row(D_MODEL)]
    out_shape = [jax.ShapeDtypeStruct((n, D_MODEL), F32)]
    args = [x, o_fox, o_ret, w]
    if router is not None:
        in_specs += [_full((1, D_MODEL)), _full((ROUTE_ROWS, D_MODEL)), _full((ROUTE_ROWS, D_MODEL)),
                     _full((ROUTE_ROWS, 1))]
        out_specs += [pl.BlockSpec((tm * SLAB, LANES), lambda i: (i, 0)), pl.BlockSpec((8, tm), lambda i: (0, i)),
                      row(LANES)]
        out_shape += [jax.ShapeDtypeStruct((n * SLAB, LANES), F32), jax.ShapeDtypeStruct((8, n), F32),
                      jax.ShapeDtypeStruct((n, LANES), F32)]
        args += list(router)
    return pl.pallas_call(
        _out0_kernel,
        grid=(n // tm,),
        in_specs=in_specs,
        out_specs=out_specs,
        out_shape=out_shape,
        compiler_params=_cparams("parallel"),
        name="out_proj0",
    )(*args)


def _router_kernel(x_ref, g_ref, wh_ref, wl_ref, b_ref, h_ref, gates_ref):
    hf = _rms(x_ref[...], g_ref[...])
    hh = hf.astype(BF16)
    hl = (hf - hh.astype(F32)).astype(BF16)
    h_ref[...] = hh
    wh = wh_ref[...]
    logits = _dot(hh, wh) + (_dot(hl, wh) + _dot(hh, wl_ref[...])) + b_ref[...]

    lane = lax.broadcasted_iota(jnp.int32, (1, LANES), 1)
    lanef = lane.astype(F32)
    ninf = -jnp.inf
    is_grp = lane < N_GROUPS
    gl = jnp.where(is_grp, logits, ninf)
    gmax = jnp.max(gl, axis=-1, keepdims=True)
    grp = jnp.min(jnp.where(gl == gmax, lanef, 1e9), axis=-1, keepdims=True)
    p_grp = 1.0 / jnp.sum(jnp.exp(gl - gmax), axis=-1, keepdims=True)

    is_exp = (lane >= ROUTER_LANE0) & (lane < ROUTER_LANE0 + N_EXPERTS)
    lane_grp = ((lane - ROUTER_LANE0) // EXP_PER_GROUP).astype(F32)
    em = jnp.where(is_exp & (lane_grp == grp), logits, ninf)
    v1 = jnp.max(em, axis=-1, keepdims=True)
    i1 = jnp.min(jnp.where(em == v1, lanef, 1e9), axis=-1, keepdims=True)
    em2 = jnp.where(lanef == i1, ninf, em)
    v2 = jnp.max(em2, axis=-1, keepdims=True)
    i2 = jnp.min(jnp.where(em2 == v2, lanef, 1e9), axis=-1, keepdims=True)
    t = jnp.exp(v2 - v1)
    w1 = (1.0 / (1.0 + t)) * p_grp
    w2 = (t / (1.0 + t)) * p_grp
    gates_ref[...] = jnp.where(lanef == i1, w1, 0.0) + jnp.where(lanef == i2, w2, 0.0)


def _router(x, g, wh, wl, b, tm):
    n = x.shape[0]
    row = lambda width: pl.BlockSpec((tm, width), lambda i: (i, 0))
    return pl.pallas_call(
        _router_kernel,
        grid=(n // tm,),
        in_specs=[row(D_MODEL), _full((1, D_MODEL)), _full((D_MODEL, LANES)), _full((D_MODEL, LANES)),
                  _full((1, LANES))],
        out_specs=[row(D_MODEL), row(LANES)],
        out_shape=[jax.ShapeDtypeStruct((n, D_MODEL), BF16), jax.ShapeDtypeStruct((n, LANES), F32)],
        compiler_params=_cparams("parallel"),
        name="router",
    )(x, g, wh, wl, b)


def _moe_kernel(h_ref, gates_ref, x_ref, wg_ref, wu_ref, wd_ref, y_ref):
    e = pl.program_id(1)

    @pl.when(e == 0)
    def _():
        y_ref[...] = jnp.zeros_like(y_ref)

    h = h_ref[...]
    a = _dot(h, wg_ref[0].astype(BF16))
    u = _dot(h, wu_ref[0].astype(BF16))
    lane = lax.broadcasted_iota(jnp.int32, (1, LANES), 1)
    gate = jnp.sum(jnp.where(lane == e + ROUTER_LANE0, gates_ref[...], 0.0), axis=-1, keepdims=True)
    hid = (a * _sigmoid(a)) * u * gate
    y_ref[...] += _dot(hid.astype(BF16), wd_ref[0].astype(BF16))

    @pl.when(e == pl.num_programs(1) - 1)
    def _():
        y_ref[...] += x_ref[...]


def _moe(h, gates, x, wg, wu, wd, layer, tm):
    n = x.shape[0]
    first = layer * N_EXPERTS
    row = lambda width: pl.BlockSpec((tm, width), lambda i, e: (i, 0))
    return pl.pallas_call(
        _moe_kernel,
        grid=(n // tm, N_EXPERTS),
        in_specs=[row(D_MODEL), row(LANES), row(D_MODEL),
                  pl.BlockSpec((1, D_MODEL, D_EXPERT), lambda i, e: (first + e, 0, 0)),
                  pl.BlockSpec((1, D_MODEL, D_EXPERT), lambda i, e: (first + e, 0, 0)),
                  pl.BlockSpec((1, D_EXPERT, D_MODEL), lambda i, e: (first + e, 0, 0))],
        out_specs=row(D_MODEL),
        out_shape=jax.ShapeDtypeStruct((n, D_MODEL), F32),
        compiler_params=_cparams("parallel", "arbitrary"),
        name="moe_experts",
    )(h, gates, x, wg, wu, wd)


def _router_t_kernel(x_ref, g_ref, wth_ref, wtl_ref, bt_ref, h_ref, route_ref, wcol_ref):
    _route_tokens(x_ref[...], g_ref, wth_ref, wtl_ref, bt_ref, h_ref, route_ref, wcol_ref)


def _route_tokens(x, g_ref, wth_ref, wtl_ref, bt_ref, h_ref, route_ref, wcol_ref):
    hf = _rms(x, g_ref[...])
    _to_slabs(h_ref, hf)
    hh = hf.astype(BF16)
    hl = (hf - hh.astype(F32)).astype(BF16)
    wth = wth_ref[...]
    logits = _dot_nt(wth, hh) + (_dot_nt(wth, hl) + _dot_nt(wtl_ref[...], hh)) + bt_ref[...]

    row = lax.broadcasted_iota(jnp.int32, (ROUTE_ROWS, 1), 0)
    rowf = row.astype(F32)
    ninf = -jnp.inf
    is_grp = (row >= N_EXPERTS) & (row < N_EXPERTS + N_GROUPS)
    gl = jnp.where(is_grp, logits, ninf)
    gmax = jnp.max(gl, axis=0, keepdims=True)
    grp = jnp.min(jnp.where(gl == gmax, rowf, 1e9), axis=0, keepdims=True) - float(N_EXPERTS)
    p_grp = 1.0 / jnp.sum(jnp.exp(gl - gmax), axis=0, keepdims=True)

    row_grp = (row // EXP_PER_GROUP).astype(F32)
    em = jnp.where((row < N_EXPERTS) & (row_grp == grp), logits, ninf)
    v1 = jnp.max(em, axis=0, keepdims=True)
    i1 = jnp.min(jnp.where(em == v1, rowf, 1e9), axis=0, keepdims=True)
    em2 = jnp.where(rowf == i1, ninf, em)
    v2 = jnp.max(em2, axis=0, keepdims=True)
    i2 = jnp.min(jnp.where(em2 == v2, rowf, 1e9), axis=0, keepdims=True)
    t = jnp.exp(v2 - v1)
    w1 = (1.0 / (1.0 + t)) * p_grp
    w2 = (t / (1.0 + t)) * p_grp

    tm = logits.shape[1]
    r128 = lax.broadcasted_iota(jnp.int32, (LANES, 1), 0)
    rt = (jnp.where(r128 == 0, i1, 0.0) + jnp.where(r128 == 1, i2, 0.0)
          + jnp.where(r128 == 2, w1, 0.0) + jnp.where(r128 == 3, w2, 0.0))
    route_ref[...] = rt[0:8, :]
    wcol_ref[...] = rt.T


def _router_t(x, g, wth, wtl, bt, tm):
    n = x.shape[0]
    row = lambda width: pl.BlockSpec((tm, width), lambda i: (i, 0))
    return pl.pallas_call(
        _router_t_kernel,
        grid=(n // tm,),
        in_specs=[row(D_MODEL), _full((1, D_MODEL)), _full((ROUTE_ROWS, D_MODEL)),
                  _full((ROUTE_ROWS, D_MODEL)), _full((ROUTE_ROWS, 1))],
        out_specs=[pl.BlockSpec((tm * SLAB, LANES), lambda i: (i, 0)), pl.BlockSpec((8, tm), lambda i: (0, i)),
                   row(LANES)],
        out_shape=[jax.ShapeDtypeStruct((n * SLAB, LANES), F32), jax.ShapeDtypeStruct((8, n), F32),
                   jax.ShapeDtypeStruct((n, LANES), F32)],
        compiler_params=_cparams("parallel"),
        name="router_t",
    )(x, g, wth, wtl, bt)


def _plan_kernel(route_ref, pos_ref, te_ref, nt_ref, pad_ref, *, n, tb, tm):
    nb = n // tb
    erow = lax.broadcasted_iota(jnp.int32, (N_EXPERTS, 1), 0).astype(F32)

    def picks(blk):
        i1 = route_ref[0:1, blk * tb:(blk + 1) * tb]
        i2 = route_ref[1:2, blk * tb:(blk + 1) * tb]
        return erow == i1, erow == i2

    acc = jnp.zeros((N_EXPERTS, tb), F32)
    for blk in range(nb):
        e1, e2 = picks(blk)
        acc = acc + jnp.where(e1, 1.0, 0.0) + jnp.where(e2, 1.0, 0.0)
    counts = jnp.sum(acc, axis=1, keepdims=True).astype(jnp.int32)
    assert tm & (tm - 1) == 0
    ntile = jnp.right_shift(counts + (tm - 1), tm.bit_length() - 1).astype(F32)
    lane = lax.broadcasted_iota(jnp.int32, (1, LANES), 1).astype(F32)
    ntile_row = jnp.sum(jnp.where(lane == erow, ntile, 0.0), axis=0, keepdims=True)
    tend = jnp.sum(jnp.where(lane <= erow, ntile_row, 0.0), axis=1, keepdims=True)
    off = (tend - ntile) * float(tm)
    ntot = jnp.sum(ntile, axis=0, keepdims=True)
    kk = jnp.minimum(lax.broadcasted_iota(jnp.int32, (1, 2 * LANES), 1).astype(F32), ntot - 1.0)
    te_ref[...] = jnp.sum(jnp.where(tend <= kk, 1.0, 0.0), axis=0, keepdims=True).astype(jnp.int32)
    nt_ref[...] = jnp.broadcast_to(ntot, (1, LANES)).astype(jnp.int32)
    pad0 = off + counts.astype(F32)
    pad_ref[...] = jnp.sum(jnp.where(lane == erow, pad0, 0.0), axis=0, keepdims=True).astype(jnp.int32)

    r = lax.broadcasted_iota(jnp.int32, (tb, tb), 0)
    c = lax.broadcasted_iota(jnp.int32, (tb, tb), 1)
    before = jnp.where(r < c, 1.0, 0.0).astype(BF16)
    carry = jnp.zeros((N_EXPERTS, 1), F32)
    for blk in range(nb):
        e1, e2 = picks(blk)
        mt = jnp.where(e1, 1.0, 0.0) + jnp.where(e2, 1.0, 0.0)
        slot = _dot(mt.astype(BF16), before) + (carry + off)
        pos_ref[blk, 0:1, :] = jnp.sum(jnp.where(e1, slot, 0.0), axis=0, keepdims=True).astype(jnp.int32)
        pos_ref[blk, 1:2, :] = jnp.sum(jnp.where(e2, slot, 0.0), axis=0, keepdims=True).astype(jnp.int32)
        carry = carry + jnp.sum(mt, axis=1, keepdims=True)


def _plan(route, tb, tm):
    n = route.shape[1]
    vm = pl.BlockSpec(memory_space=pltpu.VMEM)
    return pl.pallas_call(
        functools.partial(_plan_kernel, n=n, tb=tb, tm=tm),
        in_specs=[vm],
        out_specs=[vm, vm, vm, vm],
        out_shape=[jax.ShapeDtypeStruct((n // tb, 2, tb), jnp.int32),
                   jax.ShapeDtypeStruct((1, 2 * LANES), jnp.int32),
                   jax.ShapeDtypeStruct((1, LANES), jnp.int32),
                   jax.ShapeDtypeStruct((1, LANES), jnp.int32)],
        compiler_params=pltpu.CompilerParams(vmem_limit_bytes=VMEM_LIMIT),
        name="moe_plan",
    )(route)


SLAB = D_MODEL // LANES


def _to_slabs(ref, y):
    m = y.shape[0]
    for s_ in range(SLAB):
        ref[pl.ds(s_, m, stride=SLAB), :] = y[:, s_ * LANES:(s_ + 1) * LANES]


def _slab_piece(ref, s_, m):
    return ref[pl.ds(s_, m, stride=SLAB), :]


def _from_slabs(ref, m):
    return jnp.concatenate([_slab_piece(ref, s_, m) for s_ in range(SLAB)], axis=1)


def _tok_copy(src, src_tok, dst, dst_tok, sem):
    return pltpu.make_async_copy(src.at[pl.ds(pl.multiple_of(src_tok * SLAB, SLAB), SLAB)],
                                 dst.at[pl.ds(pl.multiple_of(dst_tok * SLAB, SLAB), SLAB)], sem)


DMA_UNROLL = 8


def _dispatch_kernel(pos_ref, pad_ref, nt_ref, h_ref, xs_hbm, zero_scr, sem, zsem, *, ts, tm, ntmax):
    @pl.when(pl.program_id(0) == 0)
    def _():
        zero_scr[...] = jnp.zeros_like(zero_scr)

        def fill(first_slot):
            return pltpu.make_async_copy(
                zero_scr, xs_hbm.at[pl.ds(pl.multiple_of(first_slot * SLAB, SLAB), tm * SLAB)], zsem)

        for e in range(N_EXPERTS):
            fill(pad_ref[0, e]).start()
        for e in range(N_EXPERTS):
            fill(pad_ref[0, e]).wait()

        ntot = nt_ref[0, 0]

        def tail_start(k, carry):
            fill(k * tm).start()
            return carry

        def tail_wait(k, carry):
            fill(k * tm).wait()
            return carry

        lax.fori_loop(ntot, ntmax, tail_start, 0)
        lax.fori_loop(ntot, ntmax, tail_wait, 0)

    def issue(g, carry):
        for u in range(DMA_UNROLL):
            r = g * DMA_UNROLL + u
            for ch in range(2):
                _tok_copy(h_ref, r, xs_hbm, pos_ref[0, ch, r], sem).start(priority=ch)
        return carry

    def drain(g, carry):
        for _ in range(2 * DMA_UNROLL):
            _tok_copy(h_ref, 0, xs_hbm, 0, sem).wait()
        return carry

    lax.fori_loop(0, ts // DMA_UNROLL, issue, 0)
    lax.fori_loop(0, ts // DMA_UNROLL, drain, 0)


def _dispatch(pos, pad, nt, h, ts, tm):
    nb = pos.shape[0]
    n = h.shape[0] // SLAB
    ntmax = _sorted_rows(n) // tm + 1
    smem = pl.BlockSpec(memory_space=pltpu.SMEM)
    return pl.pallas_call(
        functools.partial(_dispatch_kernel, ts=ts, tm=tm, ntmax=ntmax),
        grid=(nb,),
        in_specs=[pl.BlockSpec((1, 2, ts), lambda i: (i, 0, 0), memory_space=pltpu.SMEM), smem, smem,
                  pl.BlockSpec((ts * SLAB, LANES), lambda i: (i, 0))],
        out_specs=pl.BlockSpec(memory_space=pl.ANY),
        out_shape=jax.ShapeDtypeStruct((ntmax * tm * SLAB, LANES), F32),
        scratch_shapes=[pltpu.VMEM((tm * SLAB, LANES), F32), pltpu.SemaphoreType.DMA,
                        pltpu.SemaphoreType.DMA],
        compiler_params=_cparams("arbitrary"),
        name="moe_dispatch",
    )(pos, pad, nt, h)


def _experts_kernel(te_ref, nt_ref, xs_ref, wg_ref, wu_ref, wd_ref, ys_ref, *, tm):
    used = pl.program_id(0) < nt_ref[0]

    @pl.when(used)
    def _():
        x = _from_slabs(xs_ref, tm).astype(BF16)
        a = _dot(x, wg_ref[0].astype(BF16))
        u = _dot(x, wu_ref[0].astype(BF16))
        hid = (a * _sigmoid(a)) * u
        _to_slabs(ys_ref, _dot(hid.astype(BF16), wd_ref[0].astype(BF16)))

    @pl.when(jnp.logical_not(used))
    def _():
        ys_ref[...] = jnp.zeros_like(ys_ref)


def _experts(te, nt, xs, wg, wu, wd, layer, tm):
    ntmax = xs.shape[0] // (tm * SLAB)
    first = layer * N_EXPERTS
    tile = pl.BlockSpec((tm * SLAB, LANES), lambda k, te, nt: (jnp.minimum(k, nt[0] - 1), 0))
    out_tile = pl.BlockSpec((tm * SLAB, LANES), lambda k, te, nt: (k, 0))
    return pl.pallas_call(
        functools.partial(_experts_kernel, tm=tm),
        grid_spec=pltpu.PrefetchScalarGridSpec(
            num_scalar_prefetch=2,
            grid=(ntmax,),
            in_specs=[tile,
                      pl.BlockSpec((1, D_MODEL, D_EXPERT), lambda k, te, nt: (first + te[k], 0, 0)),
                      pl.BlockSpec((1, D_MODEL, D_EXPERT), lambda k, te, nt: (first + te[k], 0, 0)),
                      pl.BlockSpec((1, D_EXPERT, D_MODEL), lambda k, te, nt: (first + te[k], 0, 0))],
            out_specs=out_tile),
        out_shape=jax.ShapeDtypeStruct(xs.shape, F32),
        compiler_params=_cparams("arbitrary"),
        name="moe_experts_sorted",
    )(te, nt, xs, wg, wu, wd)


def _combine_kernel(pos_ref, posn_ref, x_ref, w_ref, ys_hbm, y_ref, buf, sems, *, tc):
    i = pl.program_id(0)
    nb = pl.num_programs(0)

    def start(p_ref, slot):
        def body(g, carry):
            for u in range(DMA_UNROLL):
                r = g * DMA_UNROLL + u
                for ch in range(2):
                    _tok_copy(ys_hbm, p_ref[0, ch, r], buf.at[slot, ch], r, sems.at[slot]).start(priority=ch)
            return carry
        lax.fori_loop(0, tc // DMA_UNROLL, body, 0)

    def wait(slot):
        def body(g, carry):
            for _ in range(2 * DMA_UNROLL):
                _tok_copy(ys_hbm, 0, buf.at[slot, 0], 0, sems.at[slot]).wait()
            return carry
        lax.fori_loop(0, tc // DMA_UNROLL, body, 0)

    @pl.when(i == 0)
    def _():
        start(pos_ref, 0)

    w = w_ref[...]
    w1, w2 = w[:, 2:3], w[:, 3:4]
    for slot in range(2):
        @pl.when(i % 2 == slot)
        def _(slot=slot):
            @pl.when(i + 1 < nb)
            def _():
                start(posn_ref, 1 - slot)

            wait(slot)
            for s_ in range(SLAB):
                cols = slice(s_ * LANES, (s_ + 1) * LANES)
                y_ref[:, cols] = x_ref[:, cols] + (w1 * _slab_piece(buf.at[slot, 0], s_, tc)
                                                   + w2 * _slab_piece(buf.at[slot, 1], s_, tc))


def _combine(pos, x, wcol, ys, tc):
    n = x.shape[0]
    nb = n // tc
    row = lambda width: pl.BlockSpec((tc, width), lambda i: (i, 0))
    return pl.pallas_call(
        functools.partial(_combine_kernel, tc=tc),
        grid=(nb,),
        in_specs=[pl.BlockSpec((1, 2, tc), lambda i: (i, 0, 0), memory_space=pltpu.SMEM),
                  pl.BlockSpec((1, 2, tc), lambda i: (jnp.minimum(i + 1, nb - 1), 0, 0), memory_space=pltpu.SMEM),
                  row(D_MODEL), row(LANES), pl.BlockSpec(memory_space=pl.ANY)],
        out_specs=row(D_MODEL),
        out_shape=jax.ShapeDtypeStruct((n, D_MODEL), F32),
        scratch_shapes=[pltpu.VMEM((2, 2, tc * SLAB, LANES), F32), pltpu.SemaphoreType.DMA((2,))],
        compiler_params=_cparams("arbitrary"),
        name="moe_combine",
    )(pos, pos, x, wcol, ys)


def _pw1_kernel(x_ref, g_ref, w_ref, b_ref, u_ref):
    h = _rms(x_ref[...], g_ref[...]).astype(BF16)
    a = _dot(h, w_ref[:, 0:D_MODEL]) + b_ref[:, 0:D_MODEL]
    g = _dot(h, w_ref[:, D_MODEL:2 * D_MODEL]) + b_ref[:, D_MODEL:2 * D_MODEL]
    u_ref[...] = a * _sigmoid(g)


def _pw1(x, g, w, b, tm):
    n = x.shape[0]
    row = lambda width: pl.BlockSpec((tm, width), lambda i: (i, 0))
    return pl.pallas_call(
        _pw1_kernel,
        grid=(n // tm,),
        in_specs=[row(D_MODEL), _full((1, D_MODEL)), _full((D_MODEL, 2 * D_MODEL)), _full((1, 2 * D_MODEL))],
        out_specs=row(D_MODEL),
        out_shape=jax.ShapeDtypeStruct((n, D_MODEL), F32),
        compiler_params=_cparams("parallel"),
        name="conv_pw1_glu",
    )(x, g, w, b)


def _conv_kernel(u_ref, hist_ref, x_ref, wdw_ref, bdw_ref, lng_ref, lnb_ref, w2_ref, y_ref, ext_scr, sh_scr,
                 *, tt, rs):
    ti = pl.program_id(1)

    @pl.when(ti == 0)
    def _():
        ext_scr[0:HIST_ROWS, :] = hist_ref[0]

    ext_scr[HIST_ROWS:HIST_ROWS + tt, :] = u_ref[0]
    pad = HIST_ROWS - (CONV_W - 1)
    rows = tt + HIST_ROWS - 8
    for r in range(1, 8):
        sh_scr[r - 1, :, :] = ext_scr[r:r + rows, :]
    parts = []
    for r0 in range(0, tt, rs):
        acc = jnp.zeros((rs, D_MODEL), F32)
        for kk in range(CONV_W):
            a, r = divmod(kk + pad, 8)
            lo = r0 + 8 * a
            src = ext_scr[lo:lo + rs, :] if r == 0 else sh_scr[r - 1, lo:lo + rs, :]
            acc = acc + wdw_ref[kk:kk + 1, :] * src
        parts.append(acc)
    y = jnp.concatenate(parts, axis=0) + bdw_ref[...]
    mu = jnp.mean(y, axis=-1, keepdims=True)
    yc = y - mu
    var = jnp.mean(yc * yc, axis=-1, keepdims=True)
    yn = yc * lax.rsqrt(var + EPS) * lng_ref[...] + lnb_ref[...]
    z = yn * _sigmoid(yn)
    y_ref[0] = x_ref[0] + _dot(z.astype(BF16), w2_ref[...])
    if tt >= HIST_ROWS:
        ext_scr[0:HIST_ROWS, :] = ext_scr[tt:tt + HIST_ROWS, :]


def _conv_module(u, hist, x, wdw, bdw, lng, lnb, w2, tt):
    b, t, _ = u.shape
    rs = min(tt, 32)
    blk = pl.BlockSpec((1, tt, D_MODEL), lambda bi, ti: (bi, ti, 0))
    return pl.pallas_call(
        functools.partial(_conv_kernel, tt=tt, rs=rs),
        grid=(b, t // tt),
        in_specs=[blk, pl.BlockSpec((1, HIST_ROWS, D_MODEL), lambda bi, ti: (bi, 0, 0)), blk,
                  _full((HIST_ROWS, D_MODEL)), _full((1, D_MODEL)), _full((1, D_MODEL)),
                  _full((1, D_MODEL)), _full((D_MODEL, D_MODEL))],
        out_specs=blk,
        out_shape=jax.ShapeDtypeStruct((b, t, D_MODEL), F32),
        scratch_shapes=[pltpu.VMEM((HIST_ROWS + tt, D_MODEL), F32),
                        pltpu.VMEM((7, tt + HIST_ROWS - 8, D_MODEL), F32)],
        compiler_params=_cparams("parallel", "arbitrary"),
        name="conv_module",
    )(u, hist, x, wdw, bdw, lng, lnb, w2)


def _rope_tables(pos):
    half = RET_DK // 2
    inv_freq = ROPE_BASE ** (-jnp.arange(half, dtype=F32) / half)
    ang = pos.astype(F32)[:, None] * inv_freq[None, :]
    cos, sin = jnp.cos(ang), jnp.sin(ang)
    reps = LANES // RET_DK
    cos_t = jnp.tile(jnp.concatenate([cos, cos], axis=1), (1, reps))
    sin_t = jnp.tile(jnp.concatenate([-sin, sin], axis=1), (1, reps))
    return cos_t, sin_t


def _pad_lanes(v, width=LANES):
    v = v.reshape(1, -1)
    return jnp.pad(v, ((0, 0), (0, width - v.shape[1])))


def _moe_dense(x, p, l, tm):
    h, gates = _router(x, p["norm_ffn"][l], p["wr_hi"][l], p["wr_lo"][l], p["br"][l], tm)
    return _moe(h, gates, x, p["wg"], p["wu"], p["wd"], l, tm)


def _sorted_rows(n_tokens):
    return (2 * n_tokens // MOE_TM + N_EXPERTS) * MOE_TM


def _router_args(p, l):
    return p["norm_ffn"][l], p["wt_hi"][l], p["wt_lo"][l], p["bt"][l]


def _moe_sparse(x, p, l, tm, routed=None):
    h, route, wcol = routed if routed is not None else _router_t(x, *_router_args(p, l), tm)
    pos, te, nt, pad = _plan(route, PLAN_TB, MOE_TM)
    xs = _dispatch(pos, pad, nt, h, PLAN_TB, MOE_TM)
    ys = _experts(te.reshape(-1), nt[0, :1], xs, p["wg"], p["wu"], p["wd"], l, MOE_TM)
    return _combine(pos, x, wcol, ys, PLAN_TB)


def _run_group(x, p, *, sparse, seq, tm, fox_hist, ret_state, conv_hist, pos, tq, tk, ret_c, conv_tt):
    b = x.shape[0] // seq
    q_off = 0 if fox_hist is None else fox_hist[0].shape[1]

    cos_t, sin_t = _rope_tables(pos)
    q, k, v, lf, rq, rk, rv, rg, k5, v5 = _proj0(x, p["norm_mix"][0], p["w_in"], p["b_f"], p["q_gain"], p["k_gain"],
                                          p["gbd"], cos_t, sin_t, tm)
    k3 = k.reshape(b, seq, FOX_W)
    v3 = v.reshape(b, seq, FOX_W)
    lf3 = lf.reshape(b, seq, LANES)
    if fox_hist is None:
        k_all, v_all, lf_all = k3, v3, lf3
    else:
        ck_, cv_, clf_ = fox_hist
        tot = q_off + seq
        padded = -(-tot // tk) * tk
        tail = padded - tot
        k_all = jnp.concatenate([ck_, k3, jnp.zeros((b, tail, FOX_W), F32)], axis=1)
        v_all = jnp.concatenate([cv_, v3, jnp.zeros((b, tail, FOX_W), F32)], axis=1)
        clf_ = jnp.pad(clf_, ((0, 0), (0, 0), (0, LANES - FOX_HEADS)))
        lf_all = jnp.concatenate([clf_, lf3, jnp.zeros((b, tail, LANES), F32)], axis=1)
    ccol, crow = _cumsum_logf(lf_all, CUM_BLOCK)
    if fox_hist is None:
        qa, ka, vt = _fox_prep(q.reshape(b, seq, FOX_W), k_all, v_all, ccol, tk)
        o_fox = _fox_attention_t(qa, ka, vt, tq)
    else:
        o_fox = _fox_attention(q.reshape(b, seq, FOX_W), k_all, v_all, ccol, crow, tq, tk, q_off)
    o_ret, s_fin = _retention(p["lg_tab"], rq.reshape(b, seq, RET_QK_W), rk.reshape(b, seq, RET_QK_W),
                              rv.reshape(b, seq, RET_V_W), rg.reshape(b, seq, RET_V_W), ret_state,
                              p["gn_gain"], ret_c)
    o_fox2, o_ret2 = o_fox.reshape(-1, FOX_W), o_ret.reshape(-1, RET_V_W)
    if sparse:
        x, *routed = _out0(x, o_fox2, o_ret2, p["w_out"], tm, _router_args(p, 0))
        x = _moe_sparse(x, p, 0, tm, routed)
    else:
        x = _moe_dense(_out0(x, o_fox2, o_ret2, p["w_out"], tm)[0], p, 0, tm)

    u = _pw1(x, p["norm_mix"][1], p["w_pw1"], p["b_pw1"], tm)
    u3 = u.reshape(b, seq, D_MODEL)
    x = _conv_module(u3, conv_hist, x.reshape(b, seq, D_MODEL), p["w_dw"], p["b_dw"], p["ln_g"], p["ln_b"],
                     p["w_pw2"], conv_tt).reshape(-1, D_MODEL)
    x = _moe_sparse(x, p, 1, tm) if sparse else _moe_dense(x, p, 1, tm)

    fox_k = k5.reshape(1, b, seq, FOX_HEADS, FOX_HD)
    fox_v = v5.reshape(1, b, seq, FOX_HEADS, FOX_HD)
    fox_lf = lf3[:, :, :FOX_HEADS].reshape(1, b, seq, FOX_HEADS)
    return x.reshape(b, seq, D_MODEL), fox_k, fox_v, fox_lf, s_fin[None], u3


def kernel(x_prompt, x_sample, cache_fox_k, cache_fox_v, cache_fox_logf, state_ret, cache_conv, norm_mix, norm_ffn, w_in_mix, b_forget, fox_q_gain, fox_k_gain, ret_gn_gain, w_out_mix, w_pw1, b_pw1, w_dw, b_dw, conv_ln_g, conv_ln_b, w_pw2, w_router_group, b_router_group, w_router_expert, b_router_expert, w_exp_gate, w_exp_up, w_exp_down):
    bp, t, d = x_prompt.shape
    bs, l, _ = x_sample.shape
    past = cache_fox_k.shape[2]
    depth = norm_mix.shape[0]
    assert d == D_MODEL and depth == 2 and w_in_mix.shape[0] == 1 and w_pw1.shape[0] == 1

    w_in = w_in_mix[0]
    n_pre = 3 * FOX_W
    w_in_r = jnp.concatenate(
        [w_in[:, :n_pre], w_in[:, n_pre + FOX_HEADS:], w_in[:, n_pre:n_pre + FOX_HEADS],
         jnp.zeros((D_MODEL, LANES - FOX_HEADS), F32)], axis=1).astype(BF16)
    hid = jnp.arange(FOX_W) // FOX_HD
    gbd = jnp.where(hid[:, None] == hid[None, :], 1.0 / FOX_HD, 0.0).astype(BF16)
    w_r = jnp.concatenate([w_router_group, w_router_expert], axis=-1)
    w_r = jnp.pad(w_r, ((0, 0), (0, 0), (0, LANES - w_r.shape[-1])))
    wr_hi = w_r.astype(BF16)
    wr_lo = (w_r - wr_hi.astype(F32)).astype(BF16)
    b_r = jnp.concatenate([b_router_group, b_router_expert], axis=-1)
    b_r = jnp.pad(b_r, ((0, 0), (0, LANES - b_r.shape[-1])))
    w_t = jnp.swapaxes(jnp.concatenate([w_router_expert, w_router_group], axis=-1), 1, 2)
    w_t = jnp.pad(w_t, ((0, 0), (0, ROUTE_ROWS - w_t.shape[1]), (0, 0)))
    wt_hi = w_t.astype(BF16)
    wt_lo = (w_t - wt_hi.astype(F32)).astype(BF16)
    b_t = jnp.concatenate([b_router_expert, b_router_group], axis=-1)
    b_t = jnp.pad(b_t, ((0, 0), (0, ROUTE_ROWS - b_t.shape[-1])))[:, :, None]
    log_gamma = jnp.log(1.0 - 2.0 ** (-5.0 - jnp.arange(RET_HEADS, dtype=F32)))
    p = {
        "norm_mix": norm_mix.reshape(depth, 1, D_MODEL),
        "norm_ffn": norm_ffn.reshape(depth, 1, D_MODEL),
        "w_in": w_in_r,
        "b_f": _pad_lanes(b_forget[0]),
        "q_gain": jnp.tile(fox_q_gain[0], FOX_HEADS).reshape(1, FOX_W),
        "k_gain": jnp.tile(fox_k_gain[0], FOX_HEADS).reshape(1, FOX_W),
        "gbd": gbd,
        "gn_gain": ret_gn_gain[0].reshape(1, RET_V_W),
        "lg_tab": jnp.broadcast_to(log_gamma[:, None, None], (RET_HEADS, 1, LANES)),
        "w_out": w_out_mix[0].astype(BF16),
        "w_pw1": w_pw1[0].astype(BF16),
        "b_pw1": b_pw1[0].reshape(1, -1),
        "w_dw": jnp.pad(w_dw[0], ((0, HIST_ROWS - CONV_W), (0, 0))),
        "b_dw": b_dw[0].reshape(1, -1),
        "ln_g": conv_ln_g[0].reshape(1, -1),
        "ln_b": conv_ln_b[0].reshape(1, -1),
        "w_pw2": w_pw2[0].astype(BF16),
        "wr_hi": wr_hi,
        "wr_lo": wr_lo,
        "br": b_r.reshape(depth, 1, LANES),
        "wt_hi": wt_hi,
        "wt_lo": wt_lo,
        "bt": b_t,
        "wg": w_exp_gate.reshape(depth * N_EXPERTS, D_MODEL, D_EXPERT),
        "wu": w_exp_up.reshape(depth * N_EXPERTS, D_MODEL, D_EXPERT),
        "wd": w_exp_down.reshape(depth * N_EXPERTS, D_EXPERT, D_MODEL),
    }

    hist_pad = HIST_ROWS - (CONV_W - 1)
    yp, fk_p, fv_p, lf_p, rs_p, u_p = _run_group(
        x_prompt.reshape(bp * t, D_MODEL), p, sparse=True, seq=t, tm=512, fox_hist=None,
        ret_state=jnp.zeros((bp, RET_HEADS, RET_DK, RET_DV), F32),
        conv_hist=jnp.zeros((bp, HIST_ROWS, D_MODEL), F32),
        pos=jnp.arange(t), tq=256, tk=256, ret_c=256, conv_tt=256)

    ns = bs * l
    ys, fk_s, fv_s, lf_s, rs_s, u_s = _run_group(
        x_sample.reshape(ns, D_MODEL), p, sparse=False, seq=l, tm=ns,
        fox_hist=(cache_fox_k[0].reshape(bs, past, FOX_W), cache_fox_v[0].reshape(bs, past, FOX_W),
                  cache_fox_logf[0]),
        ret_state=state_ret[0],
        conv_hist=jnp.pad(cache_conv[0], ((0, 0), (hist_pad, 0), (0, 0))),
        pos=past + (jnp.arange(ns) % l), tq=l, tk=-(-(past + l) // CUM_BLOCK) * CUM_BLOCK, ret_c=l, conv_tt=l)

    conv_p = u_p[:, t - (CONV_W - 1):][None]
    conv_s = jnp.concatenate([cache_conv[0], u_s], axis=1)[:, l:][None]
    return (yp, ys, fk_p, fv_p, lf_p, rs_p, conv_p, fk_s, fv_s, lf_s, rs_s, conv_s)
```

```python
import functools
import math

import jax
import jax.numpy as jnp
import numpy as np
from jax import lax
from jax.experimental import pallas as pl
from jax.experimental.pallas import tpu as pltpu

F32 = jnp.float32
BF16 = jnp.bfloat16

D_MODEL = 1024
FOX_HEADS = 8
FOX_HD = 64
RET_HEADS = 4
RET_DK = 64
RET_DV = 128
ROPE_BASE = 10000.0
CONV_W = 31
N_GROUPS = 4
EXP_PER_GROUP = 8
N_EXPERTS = N_GROUPS * EXP_PER_GROUP
D_EXPERT = 256
EPS = 1e-6
FOX_W = FOX_HEADS * FOX_HD
RET_QK_W = RET_HEADS * RET_DK
RET_V_W = RET_HEADS * RET_DV

LANES = 128
HIST_ROWS = 32
ROUTER_LANE0 = N_GROUPS
MASKED = -1e30
VMEM_LIMIT = 56 * 1024 * 1024
CUM_BLOCK = 256
MOE_TM = 512
PLAN_TB = 512
ROUTE_ROWS = 48

C_FQ, C_FK, C_FV = 0, FOX_W, 2 * FOX_W
C_RQ = 3 * FOX_W
C_RK = C_RQ + RET_QK_W
C_RV = C_RK + RET_QK_W
C_RG = C_RV + RET_V_W
C_FF = C_RG + RET_V_W
MIX_COLS = C_FF + LANES


def _cparams(*sem):
    return pltpu.CompilerParams(dimension_semantics=sem, vmem_limit_bytes=VMEM_LIMIT)


def _full(shape):
    n = len(shape)
    return pl.BlockSpec(shape, lambda *_: (0,) * n)


def _rms(x, g):
    ms = jnp.mean(x * x, axis=-1, keepdims=True)
    return x * lax.rsqrt(ms + EPS) * g


def _sigmoid(x):
    return 1.0 / (1.0 + jnp.exp(-x))


def _dot(a, b):
    return jnp.dot(a, b, preferred_element_type=F32)


def _dot_nt(a, b):
    return lax.dot_general(a, b, (((1,), (1,)), ((), ())), preferred_element_type=F32)


def _dot_tn(a, b):
    return lax.dot_general(a, b, (((0,), (0,)), ((), ())), preferred_element_type=F32)


def _proj0_kernel(x_ref, g_ref, w_ref, bf_ref, qg_ref, kg_ref, gbd_ref, cos_ref, sin_ref,
                  q_ref, k_ref, v_ref, lf_ref, rq_ref, rk_ref, rv_ref, rg_ref, k5_ref, v5_ref):
    h = _rms(x_ref[...], g_ref[...]).astype(BF16)

    def seg(a, b):
        return _dot(h, w_ref[:, a:b])

    gbd = gbd_ref[...]

    def head_rms(y, gain):
        ms = _dot((y * y).astype(BF16), gbd)
        return y * lax.rsqrt(ms + EPS) * gain

    q_ref[...] = (head_rms(seg(C_FQ, C_FK), qg_ref[...]) * (FOX_HD ** -0.5)).astype(BF16)
    kk = head_rms(seg(C_FK, C_FV), kg_ref[...])
    vv = seg(C_FV, C_RQ)
    k_ref[...] = kk
    v_ref[...] = vv
    k5_ref[...] = kk.reshape(kk.shape[0], FOX_HEADS, FOX_HD)
    v5_ref[...] = vv.reshape(vv.shape[0], FOX_HEADS, FOX_HD)

    z = seg(C_FF, MIX_COLS) + bf_ref[...]
    logf = jnp.minimum(z, 0.0) - jnp.log(1.0 + jnp.exp(-jnp.abs(z)))
    lane = lax.broadcasted_iota(jnp.int32, (1, LANES), 1)
    lf_ref[...] = jnp.where(lane < FOX_HEADS, logf, 0.0)

    cos = cos_ref[...]
    sin = sin_ref[...]
    first_half = (lane % RET_DK) < (RET_DK // 2)

    def rotary(y):
        outs = []
        for s in range(y.shape[1] // LANES):
            ys = y[:, s * LANES:(s + 1) * LANES]
            rot = jnp.where(first_half, pltpu.roll(ys, LANES - RET_DK // 2, 1),
                            pltpu.roll(ys, RET_DK // 2, 1))
            outs.append(ys * cos + rot * sin)
        return jnp.concatenate(outs, axis=1)

    rq_ref[...] = rotary(seg(C_RQ, C_RK)).astype(BF16)
    rk_ref[...] = (rotary(seg(C_RK, C_RV)) * (RET_DK ** -0.5)).astype(BF16)
    rv_ref[...] = seg(C_RV, C_RG).astype(BF16)
    rg_ref[...] = seg(C_RG, C_FF).astype(BF16)


def _proj0(x, g, w, bf, qg, kg, gbd, cos, sin, tm):
    n = x.shape[0]
    nper = cos.shape[0] // tm
    row = lambda width: pl.BlockSpec((tm, width), lambda i: (i, 0))
    tab = pl.BlockSpec((tm, LANES), lambda i: (i % nper, 0))
    outs = [(FOX_W, BF16), (FOX_W, F32), (FOX_W, F32), (LANES, F32),
            (RET_QK_W, BF16), (RET_QK_W, BF16), (RET_V_W, BF16), (RET_V_W, BF16)]
    return pl.pallas_call(
        _proj0_kernel,
        grid=(n // tm,),
        in_specs=[row(D_MODEL), _full((1, D_MODEL)), _full((D_MODEL, MIX_COLS)), _full((1, LANES)),
                  _full((1, FOX_W)), _full((1, FOX_W)), _full((FOX_W, FOX_W)), tab, tab],
        out_specs=[row(wd) for wd, _ in outs] + [pl.BlockSpec((tm, FOX_HEADS, FOX_HD), lambda i: (i, 0, 0))] * 2,
        out_shape=[jax.ShapeDtypeStruct((n, wd), dt) for wd, dt in outs]
        + [jax.ShapeDtypeStruct((n, FOX_HEADS, FOX_HD), F32)] * 2,
        compiler_params=_cparams("parallel"),
        name="proj0",
    )(x, g, w, bf, qg, kg, gbd, cos, sin)


def _cum_kernel(lf_ref, ccol_ref, crow_ref, *, t, cb):
    r = lax.broadcasted_iota(jnp.int32, (cb, cb), 0)
    c = lax.broadcasted_iota(jnp.int32, (cb, cb), 1)
    tri = jnp.where(r >= c, 1.0, 0.0).astype(BF16)
    carry = jnp.zeros((1, LANES), F32)
    for blk in range(t // cb):
        a = lf_ref[0, blk * cb:(blk + 1) * cb, :]
        a1 = a.astype(BF16)
        r1 = a - a1.astype(F32)
        a2 = r1.astype(BF16)
        a3 = (r1 - a2.astype(F32)).astype(BF16)
        cc = (_dot(tri, a1) + _dot(tri, a2)) + _dot(tri, a3) + carry
        ccol_ref[0, blk * cb:(blk + 1) * cb, :] = cc
        crow_ref[0, :, blk * cb:(blk + 1) * cb] = cc.T[0:FOX_HEADS, :]
        carry = cc[cb - 1:cb, :]


def _cumsum_logf(lf, cb):
    b, t, _ = lf.shape
    return pl.pallas_call(
        functools.partial(_cum_kernel, t=t, cb=cb),
        grid=(b,),
        in_specs=[pl.BlockSpec((1, t, LANES), lambda i: (i, 0, 0))],
        out_specs=[pl.BlockSpec((1, t, LANES), lambda i: (i, 0, 0)),
                   pl.BlockSpec((1, FOX_HEADS, t), lambda i: (i, 0, 0))],
        out_shape=[jax.ShapeDtypeStruct((b, t, LANES), F32),
                   jax.ShapeDtypeStruct((b, FOX_HEADS, t), F32)],
        compiler_params=_cparams("parallel"),
        name="cumsum_logf",
    )(lf)


def _fox_kernel(q_ref, k_ref, v_ref, cq_ref, ck_ref, o_ref, *, tq, tk, q_off):
    hp = pl.program_id(1)
    i = pl.program_id(2)
    q0 = q_off + i * tq
    nfull = (q0 + 1) // tk
    nch = (q0 + tq + tk - 1) // tk
    lane = lax.broadcasted_iota(jnp.int32, (1, LANES), 1)
    q = q_ref[0]
    cq_all = cq_ref[0]
    qpos = q0 + lax.broadcasted_iota(jnp.int32, (tq, 1), 0)
    kiota = lax.broadcasted_iota(jnp.int32, (1, tk), 1)
    qms = [jnp.where((lane // FOX_HD) == h2, q, jnp.zeros_like(q)) for h2 in range(2)]
    cqs = [jnp.sum(jnp.where(lane == 2 * hp + h2, cq_all, 0.0), axis=-1, keepdims=True) for h2 in range(2)]

    def step(j, carry, masked):
        start = pl.multiple_of(j * tk, tk)
        kj = k_ref[0, pl.ds(start, tk), :].astype(BF16)
        vj = v_ref[0, pl.ds(start, tk), :].astype(BF16)
        new = []
        for h2 in range(2):
            m, l, acc = carry[h2]
            s = _dot_nt(qms[h2], kj)
            s = s + cqs[h2] - ck_ref[0, 2 * hp + h2, j]
            if masked:
                s = jnp.where(j * tk + kiota <= qpos, s, MASKED)
            m_new = jnp.maximum(m, jnp.max(s, axis=-1, keepdims=True))
            alpha = jnp.exp(m - m_new)
            p = jnp.exp(s - m_new)
            l = alpha * l + jnp.sum(p, axis=-1, keepdims=True)
            acc = alpha * acc + _dot(p.astype(BF16), vj)
            new.append((m_new, l, acc))
        return tuple(new)

    one = (jnp.full((tq, 1), MASKED, F32), jnp.zeros((tq, 1), F32), jnp.zeros((tq, LANES), F32))
    carry = lax.fori_loop(0, nfull, functools.partial(step, masked=False), (one, one))
    carry = lax.fori_loop(nfull, nch, functools.partial(step, masked=True), carry)
    outs = [acc / l for _, l, acc in carry]
    o_ref[0] = jnp.where(lane < FOX_HD, outs[0], outs[1]).astype(BF16)


def _fox_attention(q, k, v, ccol, crow, tq, tk, q_off):
    b, tqs, _ = q.shape
    tks = k.shape[1]
    nck = tks // tk
    crow5 = crow.reshape(b, FOX_HEADS, nck, 1, tk)
    qb0 = q_off // tq
    return pl.pallas_call(
        functools.partial(_fox_kernel, tq=tq, tk=tk, q_off=q_off),
        grid=(b, FOX_HEADS // 2, tqs // tq),
        in_specs=[pl.BlockSpec((1, tq, LANES), lambda bi, hp, i: (bi, i, hp)),
                  pl.BlockSpec((1, tks, LANES), lambda bi, hp, i: (bi, 0, hp)),
                  pl.BlockSpec((1, tks, LANES), lambda bi, hp, i: (bi, 0, hp)),
                  pl.BlockSpec((1, tq, LANES), lambda bi, hp, i: (bi, qb0 + i, 0)),
                  pl.BlockSpec((1, FOX_HEADS, nck, 1, tk), lambda bi, hp, i: (bi, 0, 0, 0, 0))],
        out_specs=pl.BlockSpec((1, tq, LANES), lambda bi, hp, i: (bi, i, hp)),
        out_shape=jax.ShapeDtypeStruct((b, tqs, FOX_W), BF16),
        compiler_params=_cparams("parallel", "parallel", "parallel"),
        name="fox_attention",
    )(q, k, v, ccol, crow5)


FOX_VROWS = 80


def _bf16_pieces(c):
    hi = c.astype(BF16).astype(F32)
    r = c - hi
    mid = r.astype(BF16).astype(F32)
    lo = (r - mid).astype(BF16).astype(F32)
    return hi, mid, lo


def _fox_prep_kernel(q_ref, k_ref, v_ref, c_ref, qa_ref, ka_ref, vt_ref, *, t, rc):
    hp = pl.program_id(1)
    lane = lax.broadcasted_iota(jnp.int32, (1, LANES), 1)
    src = lax.broadcasted_iota(jnp.int32, (LANES, 1), 0)
    one_lane = 3 * FOX_HEADS

    def placer(h, c_to, one_to, sign):
        m = jnp.zeros((LANES, LANES), F32)
        for piece in range(3):
            m = jnp.where((src == piece * FOX_HEADS + h) & (lane == c_to + piece), sign, m)
            m = jnp.where((src == one_lane) & (lane == one_to + piece), 1.0, m)
        return m.astype(BF16)

    place_q = [placer(2 * hp + h2, FOX_HD, FOX_HD + 3, 1.0) for h2 in range(2)]
    place_k = [placer(2 * hp + h2, FOX_HD + 3, FOX_HD, -1.0) for h2 in range(2)]
    extra = jnp.where(lax.broadcasted_iota(jnp.int32, (FOX_VROWS - FOX_HD, rc), 0) == 0, 1.0, 0.0)
    for ch in range(t // rc):
        rows = slice(ch * rc, (ch + 1) * rc)
        qf = q_ref[0, rows, :].astype(F32)
        kf = k_ref[0, rows, :]
        hi, mid, lo = _bf16_pieces(c_ref[0, rows, :])
        pieces = (hi + pltpu.roll(mid, FOX_HEADS, 1) + pltpu.roll(lo, 2 * FOX_HEADS, 1)
                  + jnp.where(lane == one_lane, 1.0, 0.0)).astype(BF16)
        vt = v_ref[0, rows, :].T
        for h2 in range(2):
            qh = qf if h2 == 0 else pltpu.roll(qf, FOX_HD, 1)
            kh = kf if h2 == 0 else pltpu.roll(kf, FOX_HD, 1)
            qa_ref[0, h2, rows, :] = jnp.where(lane < FOX_HD, qh, _dot(pieces, place_q[h2])).astype(BF16)
            ka_ref[0, h2, rows, :] = jnp.where(lane < FOX_HD, kh, _dot(pieces, place_k[h2])).astype(BF16)
            vt_ref[0, h2, ch] = jnp.concatenate([vt[h2 * FOX_HD:(h2 + 1) * FOX_HD, :], extra], axis=0).astype(BF16)


def _fox_prep(q, k, v, ccol, rc):
    b, t, _ = q.shape
    pair = pl.BlockSpec((1, t, LANES), lambda bi, hp: (bi, 0, hp))
    aug = pl.BlockSpec((1, 2, t, LANES), lambda bi, hp: (bi, hp, 0, 0))
    return pl.pallas_call(
        functools.partial(_fox_prep_kernel, t=t, rc=rc),
        grid=(b, FOX_HEADS // 2),
        in_specs=[pair, pair, pair, pl.BlockSpec((1, t, LANES), lambda bi, hp: (bi, 0, 0))],
        out_specs=[aug, aug, pl.BlockSpec((1, 2, t // rc, FOX_VROWS, rc), lambda bi, hp: (bi, hp, 0, 0, 0))],
        out_shape=[jax.ShapeDtypeStruct((b, FOX_HEADS, t, LANES), BF16),
                   jax.ShapeDtypeStruct((b, FOX_HEADS, t, LANES), BF16),
                   jax.ShapeDtypeStruct((b, FOX_HEADS, t // rc, FOX_VROWS, rc), BF16)],
        compiler_params=_cparams("parallel", "parallel"),
        name="fox_prep",
    )(q, k, v, ccol)


def _fox_t_kernel(qa_ref, ka_ref, vt_ref, o_ref, s_scr, acc_scr, *, tq, tk):
    i = pl.program_id(1)
    q0 = i * tq
    nfull = (q0 + 1) // tk
    nch = (q0 + tq + tk - 1) // tk
    qpos = q0 + lax.broadcasted_iota(jnp.int32, (1, tq), 1)
    kiota = lax.broadcasted_iota(jnp.int32, (tk, 1), 0)
    heads = range(FOX_HEADS)

    def scores(j, mx, masked):
        start = pl.multiple_of(j * tk, tk)
        out = []
        for h in heads:
            st = _dot_nt(ka_ref[0, h, pl.ds(start, tk), :], qa_ref[0, h])
            if masked:
                st = jnp.where(start + kiota <= qpos, st, MASKED)
            s_scr[h, j] = st
            out.append(jnp.maximum(mx[h], jnp.max(st, axis=0, keepdims=True)))
        return tuple(out)

    mx = tuple(jnp.full((1, tq), MASKED, F32) for _ in heads)
    mx = lax.fori_loop(0, nfull, functools.partial(scores, masked=False), mx)
    mx = lax.fori_loop(nfull, nch, functools.partial(scores, masked=True), mx)

    acc_scr[...] = jnp.zeros_like(acc_scr)

    def weigh(j, carry):
        for h in heads:
            p = jnp.exp(s_scr[h, j] - mx[h]).astype(BF16)
            acc_scr[h] += _dot(vt_ref[0, h, j], p)
        return carry

    lax.fori_loop(0, nch, weigh, 0)
    for hp in range(FOX_HEADS // 2):
        outs = []
        for h in (2 * hp, 2 * hp + 1):
            acc = acc_scr[h]
            outs.append(acc[0:FOX_HD, :] / acc[FOX_HD:FOX_HD + 1, :])
        o_ref[0, :, hp * LANES:(hp + 1) * LANES] = jnp.concatenate(outs, axis=0).T.astype(BF16)


def _fox_attention_t(qa, ka, vt, tq):
    b, _, t, _ = qa.shape
    nck, tk = vt.shape[2], vt.shape[4]
    return pl.pallas_call(
        functools.partial(_fox_t_kernel, tq=tq, tk=tk),
        grid=(b, t // tq),
        in_specs=[pl.BlockSpec((1, FOX_HEADS, tq, LANES), lambda bi, i: (bi, 0, i, 0)),
                  pl.BlockSpec((1, FOX_HEADS, t, LANES), lambda bi, i: (bi, 0, 0, 0)),
                  pl.BlockSpec((1, FOX_HEADS, nck, FOX_VROWS, tk), lambda bi, i: (bi, 0, 0, 0, 0))],
        out_specs=pl.BlockSpec((1, tq, FOX_W), lambda bi, i: (bi, i, 0)),
        out_shape=jax.ShapeDtypeStruct((b, t, FOX_W), BF16),
        scratch_shapes=[pltpu.VMEM((FOX_HEADS, nck, tk, tq), F32),
                        pltpu.VMEM((FOX_HEADS, FOX_VROWS, tq), F32)],
        compiler_params=_cparams("parallel", "arbitrary"),
        name="fox_attention_t",
    )(qa, ka, vt)


def _ret_kernel(lg_ref, rq_ref, rk_ref, rv_ref, rg_ref, s0_ref, gn_ref, o_ref, sfin_ref, s_scr, d_scr, *, c, nbs):
    first = (pl.program_id(0) == 0) & (pl.program_id(1) == 0)
    ci = pl.program_id(1)
    ii = lax.broadcasted_iota(jnp.int32, (c, 1), 0).astype(F32)
    lane = lax.broadcasted_iota(jnp.int32, (1, LANES), 1)

    @pl.when(first)
    def _():
        jj = lax.broadcasted_iota(jnp.int32, (1, c), 1).astype(F32)
        diff = ii - jj
        for h in range(RET_HEADS):
            lg = lg_ref[h][:, 0:1]
            d_scr[h] = jnp.where(diff >= 0.0, jnp.exp(lg * jnp.maximum(diff, 0.0)), 0.0)

    @pl.when(ci == 0)
    def _():
        zero = jnp.zeros((RET_DK, RET_DV), F32)
        for bb in range(nbs):
            for h in range(RET_HEADS):
                lo, hi = (s0_ref[bb, h], zero) if h % 2 == 0 else (zero, s0_ref[bb, h])
                s_scr[bb * RET_HEADS + h, 0:RET_DK, :] = lo
                s_scr[bb * RET_HEADS + h, RET_DK:2 * RET_DK, :] = hi

    for bb, h in [(bb, h) for bb in range(nbs) for h in range(RET_HEADS)]:
        lg = lg_ref[h][:, 0:1]
        st = bb * RET_HEADS + h
        pair = slice((h // 2) * LANES, (h // 2 + 1) * LANES)
        mine = slice(h * RET_DV, (h + 1) * RET_DV)
        inhead = (lane // RET_DK) == (h % 2)
        q = rq_ref[bb, :, pair]
        k = rk_ref[bb, :, pair]
        qm = jnp.where(inhead, q, jnp.zeros_like(q))
        km = jnp.where(inhead, k, jnp.zeros_like(k))
        v = rv_ref[bb, :, mine]
        scores = _dot_nt(qm, km) * d_scr[h]
        inner = _dot(scores.astype(BF16), v)
        s_prev = s_scr[st]
        cross = _dot(qm, s_prev.astype(BF16)) * jnp.exp(lg * (ii + 1.0))
        y = inner + cross
        k_dec = (km.astype(F32) * jnp.exp(lg * (c - 1.0 - ii))).astype(BF16)
        s_new = jnp.exp(lg * float(c)) * s_prev + _dot_tn(k_dec, v)
        s_scr[st] = s_new

        mu = jnp.mean(y, axis=-1, keepdims=True)
        yc = y - mu
        var = jnp.mean(yc * yc, axis=-1, keepdims=True)
        yn = yc * lax.rsqrt(var + EPS) * gn_ref[:, mine]
        g = rg_ref[bb, :, mine].astype(F32)
        o_ref[bb, :, mine] = ((g * _sigmoid(g)) * yn).astype(BF16)
        off = (h % 2) * RET_DK
        sfin_ref[bb, h] = s_new[off:off + RET_DK, :]


def _retention(lg_tab, rq, rk, rv, rg, s0, gn, c):
    b, t, _ = rq.shape
    nbs = 2 if b % 2 == 0 else 1
    qk_spec = pl.BlockSpec((nbs, c, RET_QK_W), lambda bi, ci: (bi, ci, 0))
    v_spec = pl.BlockSpec((nbs, c, RET_V_W), lambda bi, ci: (bi, ci, 0))
    st_spec = pl.BlockSpec((nbs, RET_HEADS, RET_DK, RET_DV), lambda bi, ci: (bi, 0, 0, 0))
    return pl.pallas_call(
        functools.partial(_ret_kernel, c=c, nbs=nbs),
        grid=(b // nbs, t // c),
        in_specs=[_full((RET_HEADS, 1, LANES)), qk_spec, qk_spec, v_spec, v_spec, st_spec,
                  _full((1, RET_V_W))],
        out_specs=[v_spec, st_spec],
        out_shape=[jax.ShapeDtypeStruct((b, t, RET_V_W), BF16),
                   jax.ShapeDtypeStruct((b, RET_HEADS, RET_DK, RET_DV), F32)],
        scratch_shapes=[pltpu.VMEM((nbs * RET_HEADS, 2 * RET_DK, RET_DV), F32), pltpu.VMEM((RET_HEADS, c, c), F32)],
        compiler_params=_cparams("arbitrary", "arbitrary"),
        name="retention",
    )(lg_tab, rq, rk, rv, rg, s0, gn)


def _out0_kernel(x_ref, of_ref, or_ref, w_ref, *rest):
    y_ref = rest[-4] if len(rest) > 1 else rest[0]
    mix = _dot(of_ref[...], w_ref[0:FOX_W, :]) + _dot(or_ref[...], w_ref[FOX_W:FOX_W + RET_V_W, :])
    y = x_ref[...] + mix
    y_ref[...] = y
    if len(rest) > 1:
        g_ref, wth_ref, wtl_ref, bt_ref, _, h_ref, route_ref, wcol_ref = rest
        _route_tokens(y, g_ref, wth_ref, wtl_ref, bt_ref, h_ref, route_ref, wcol_ref)


def _out0(x, o_fox, o_ret, w, tm, router=None):
    n = x.shape[0]
    row = lambda width: pl.BlockSpec((tm, width), lambda i: (i, 0))
    in_specs = [row(D_MODEL), row(FOX_W), row(RET_V_W), _full((FOX_W + RET_V_W, D_MODEL))]
    out_specs = [row(D_MODEL)]
    out_shape = [jax.ShapeDtypeStruct((n, D_MODEL), F32)]
    args = [x, o_fox, o_ret, w]
    if router is not None:
        in_specs += [_full((1, D_MODEL)), _full((ROUTE_ROWS, D_MODEL)), _full((ROUTE_ROWS, D_MODEL)),
                     _full((ROUTE_ROWS, 1))]
        out_specs += [pl.BlockSpec((tm * SLAB, LANES), lambda i: (i, 0)), pl.BlockSpec((8, tm), lambda i: (0, i)),
                      row(LANES)]
        out_shape += [jax.ShapeDtypeStruct((n * SLAB, LANES), F32), jax.ShapeDtypeStruct((8, n), F32),
                      jax.ShapeDtypeStruct((n, LANES), F32)]
        args += list(router)
    return pl.pallas_call(
        _out0_kernel,
        grid=(n // tm,),
        in_specs=in_specs,
        out_specs=out_specs,
        out_shape=out_shape,
        compiler_params=_cparams("parallel"),
        name="out_proj0",
    )(*args)


def _router_kernel(x_ref, g_ref, wh_ref, wl_ref, b_ref, h_ref, gates_ref):
    hf = _rms(x_ref[...], g_ref[...])
    hh = hf.astype(BF16)
    hl = (hf - hh.astype(F32)).astype(BF16)
    h_ref[...] = hh
    wh = wh_ref[...]
    logits = _dot(hh, wh) + (_dot(hl, wh) + _dot(hh, wl_ref[...])) + b_ref[...]

    lane = lax.broadcasted_iota(jnp.int32, (1, LANES), 1)
    lanef = lane.astype(F32)
    ninf = -jnp.inf
    is_grp = lane < N_GROUPS
    gl = jnp.where(is_grp, logits, ninf)
    gmax = jnp.max(gl, axis=-1, keepdims=True)
    grp = jnp.min(jnp.where(gl == gmax, lanef, 1e9), axis=-1, keepdims=True)
    p_grp = 1.0 / jnp.sum(jnp.exp(gl - gmax), axis=-1, keepdims=True)

    is_exp = (lane >= ROUTER_LANE0) & (lane < ROUTER_LANE0 + N_EXPERTS)
    lane_grp = ((lane - ROUTER_LANE0) // EXP_PER_GROUP).astype(F32)
    em = jnp.where(is_exp & (lane_grp == grp), logits, ninf)
    v1 = jnp.max(em, axis=-1, keepdims=True)
    i1 = jnp.min(jnp.where(em == v1, lanef, 1e9), axis=-1, keepdims=True)
    em2 = jnp.where(lanef == i1, ninf, em)
    v2 = jnp.max(em2, axis=-1, keepdims=True)
    i2 = jnp.min(jnp.where(em2 == v2, lanef, 1e9), axis=-1, keepdims=True)
    t = jnp.exp(v2 - v1)
    w1 = (1.0 / (1.0 + t)) * p_grp
    w2 = (t / (1.0 + t)) * p_grp
    gates_ref[...] = jnp.where(lanef == i1, w1, 0.0) + jnp.where(lanef == i2, w2, 0.0)


def _router(x, g, wh, wl, b, tm):
    n = x.shape[0]
    row = lambda width: pl.BlockSpec((tm, width), lambda i: (i, 0))
    return pl.pallas_call(
        _router_kernel,
        grid=(n // tm,),
        in_specs=[row(D_MODEL), _full((1, D_MODEL)), _full((D_MODEL, LANES)), _full((D_MODEL, LANES)),
                  _full((1, LANES))],
        out_specs=[row(D_MODEL), row(LANES)],
        out_shape=[jax.ShapeDtypeStruct((n, D_MODEL), BF16), jax.ShapeDtypeStruct((n, LANES), F32)],
        compiler_params=_cparams("parallel"),
        name="router",
    )(x, g, wh, wl, b)


def _moe_kernel(h_ref, gates_ref, x_ref, wg_ref, wu_ref, wd_ref, y_ref):
    e = pl.program_id(1)

    @pl.when(e == 0)
    def _():
        y_ref[...] = jnp.zeros_like(y_ref)

    h = h_ref[...]
    a = _dot(h, wg_ref[0].astype(BF16))
    u = _dot(h, wu_ref[0].astype(BF16))
    lane = lax.broadcasted_iota(jnp.int32, (1, LANES), 1)
    gate = jnp.sum(jnp.where(lane == e + ROUTER_LANE0, gates_ref[...], 0.0), axis=-1, keepdims=True)
    hid = (a * _sigmoid(a)) * u * gate
    y_ref[...] += _dot(hid.astype(BF16), wd_ref[0].astype(BF16))

    @pl.when(e == pl.num_programs(1) - 1)
    def _():
        y_ref[...] += x_ref[...]


def _moe(h, gates, x, wg, wu, wd, layer, tm):
    n = x.shape[0]
    first = layer * N_EXPERTS
    row = lambda width: pl.BlockSpec((tm, width), lambda i, e: (i, 0))
    return pl.pallas_call(
        _moe_kernel,
        grid=(n // tm, N_EXPERTS),
        in_specs=[row(D_MODEL), row(LANES), row(D_MODEL),
                  pl.BlockSpec((1, D_MODEL, D_EXPERT), lambda i, e: (first + e, 0, 0)),
                  pl.BlockSpec((1, D_MODEL, D_EXPERT), lambda i, e: (first + e, 0, 0)),
                  pl.BlockSpec((1, D_EXPERT, D_MODEL), lambda i, e: (first + e, 0, 0))],
        out_specs=row(D_MODEL),
        out_shape=jax.ShapeDtypeStruct((n, D_MODEL), F32),
        compiler_params=_cparams("parallel", "arbitrary"),
        name="moe_experts",
    )(h, gates, x, wg, wu, wd)


def _router_t_kernel(x_ref, g_ref, wth_ref, wtl_ref, bt_ref, h_ref, route_ref, wcol_ref):
    _route_tokens(x_ref[...], g_ref, wth_ref, wtl_ref, bt_ref, h_ref, route_ref, wcol_ref)


def _route_tokens(x, g_ref, wth_ref, wtl_ref, bt_ref, h_ref, route_ref, wcol_ref):
    hf = _rms(x, g_ref[...])
    _to_slabs(h_ref, hf)
    hh = hf.astype(BF16)
    hl = (hf - hh.astype(F32)).astype(BF16)
    wth = wth_ref[...]
    logits = _dot_nt(wth, hh) + (_dot_nt(wth, hl) + _dot_nt(wtl_ref[...], hh)) + bt_ref[...]

    row = lax.broadcasted_iota(jnp.int32, (ROUTE_ROWS, 1), 0)
    rowf = row.astype(F32)
    ninf = -jnp.inf
    is_grp = (row >= N_EXPERTS) & (row < N_EXPERTS + N_GROUPS)
    gl = jnp.where(is_grp, logits, ninf)
    gmax = jnp.max(gl, axis=0, keepdims=True)
    grp = jnp.min(jnp.where(gl == gmax, rowf, 1e9), axis=0, keepdims=True) - float(N_EXPERTS)
    p_grp = 1.0 / jnp.sum(jnp.exp(gl - gmax), axis=0, keepdims=True)

    row_grp = (row // EXP_PER_GROUP).astype(F32)
    em = jnp.where((row < N_EXPERTS) & (row_grp == grp), logits, ninf)
    v1 = jnp.max(em, axis=0, keepdims=True)
    i1 = jnp.min(jnp.where(em == v1, rowf, 1e9), axis=0, keepdims=True)
    em2 = jnp.where(rowf == i1, ninf, em)
    v2 = jnp.max(em2, axis=0, keepdims=True)
    i2 = jnp.min(jnp.where(em2 == v2, rowf, 1e9), axis=0, keepdims=True)
    t = jnp.exp(v2 - v1)
    w1 = (1.0 / (1.0 + t)) * p_grp
    w2 = (t / (1.0 + t)) * p_grp

    tm = logits.shape[1]
    r128 = lax.broadcasted_iota(jnp.int32, (LANES, 1), 0)
    rt = (jnp.where(r128 == 0, i1, 0.0) + jnp.where(r128 == 1, i2, 0.0)
          + jnp.where(r128 == 2, w1, 0.0) + jnp.where(r128 == 3, w2, 0.0))
    route_ref[...] = rt[0:8, :]
    wcol_ref[...] = rt.T


def _router_t(x, g, wth, wtl, bt, tm):
    n = x.shape[0]
    row = lambda width: pl.BlockSpec((tm, width), lambda i: (i, 0))
    return pl.pallas_call(
        _router_t_kernel,
        grid=(n // tm,),
        in_specs=[row(D_MODEL), _full((1, D_MODEL)), _full((ROUTE_ROWS, D_MODEL)),
                  _full((ROUTE_ROWS, D_MODEL)), _full((ROUTE_ROWS, 1))],
        out_specs=[pl.BlockSpec((tm * SLAB, LANES), lambda i: (i, 0)), pl.BlockSpec((8, tm), lambda i: (0, i)),
                   row(LANES)],
        out_shape=[jax.ShapeDtypeStruct((n * SLAB, LANES), F32), jax.ShapeDtypeStruct((8, n), F32),
                   jax.ShapeDtypeStruct((n, LANES), F32)],
        compiler_params=_cparams("parallel"),
        name="router_t",
    )(x, g, wth, wtl, bt)


def _plan_kernel(route_ref, pos_ref, te_ref, nt_ref, pad_ref, *, n, tb, tm):
    nb = n // tb
    erow = lax.broadcasted_iota(jnp.int32, (N_EXPERTS, 1), 0).astype(F32)

    def picks(blk):
        i1 = route_ref[0:1, blk * tb:(blk + 1) * tb]
        i2 = route_ref[1:2, blk * tb:(blk + 1) * tb]
        return erow == i1, erow == i2

    acc = jnp.zeros((N_EXPERTS, tb), F32)
    for blk in range(nb):
        e1, e2 = picks(blk)
        acc = acc + jnp.where(e1, 1.0, 0.0) + jnp.where(e2, 1.0, 0.0)
    counts = jnp.sum(acc, axis=1, keepdims=True).astype(jnp.int32)
    assert tm & (tm - 1) == 0
    ntile = jnp.right_shift(counts + (tm - 1), tm.bit_length() - 1).astype(F32)
    lane = lax.broadcasted_iota(jnp.int32, (1, LANES), 1).astype(F32)
    ntile_row = jnp.sum(jnp.where(lane == erow, ntile, 0.0), axis=0, keepdims=True)
    tend = jnp.sum(jnp.where(lane <= erow, ntile_row, 0.0), axis=1, keepdims=True)
    off = (tend - ntile) * float(tm)
    ntot = jnp.sum(ntile, axis=0, keepdims=True)
    kk = jnp.minimum(lax.broadcasted_iota(jnp.int32, (1, 2 * LANES), 1).astype(F32), ntot - 1.0)
    te_ref[...] = jnp.sum(jnp.where(tend <= kk, 1.0, 0.0), axis=0, keepdims=True).astype(jnp.int32)
    nt_ref[...] = jnp.broadcast_to(ntot, (1, LANES)).astype(jnp.int32)
    pad0 = off + counts.astype(F32)
    pad_ref[...] = jnp.sum(jnp.where(lane == erow, pad0, 0.0), axis=0, keepdims=True).astype(jnp.int32)

    r = lax.broadcasted_iota(jnp.int32, (tb, tb), 0)
    c = lax.broadcasted_iota(jnp.int32, (tb, tb), 1)
    before = jnp.where(r < c, 1.0, 0.0).astype(BF16)
    carry = jnp.zeros((N_EXPERTS, 1), F32)
    for blk in range(nb):
        e1, e2 = picks(blk)
        mt = jnp.where(e1, 1.0, 0.0) + jnp.where(e2, 1.0, 0.0)
        slot = _dot(mt.astype(BF16), before) + (carry + off)
        pos_ref[blk, 0:1, :] = jnp.sum(jnp.where(e1, slot, 0.0), axis=0, keepdims=True).astype(jnp.int32)
        pos_ref[blk, 1:2, :] = jnp.sum(jnp.where(e2, slot, 0.0), axis=0, keepdims=True).astype(jnp.int32)
        carry = carry + jnp.sum(mt, axis=1, keepdims=True)


def _plan(route, tb, tm):
    n = route.shape[1]
    vm = pl.BlockSpec(memory_space=pltpu.VMEM)
    return pl.pallas_call(
        functools.partial(_plan_kernel, n=n, tb=tb, tm=tm),
        in_specs=[vm],
        out_specs=[vm, vm, vm, vm],
        out_shape=[jax.ShapeDtypeStruct((n // tb, 2, tb), jnp.int32),
                   jax.ShapeDtypeStruct((1, 2 * LANES), jnp.int32),
                   jax.ShapeDtypeStruct((1, LANES), jnp.int32),
                   jax.ShapeDtypeStruct((1, LANES), jnp.int32)],
        compiler_params=pltpu.CompilerParams(vmem_limit_bytes=VMEM_LIMIT),
        name="moe_plan",
    )(route)


SLAB = D_MODEL // LANES


def _to_slabs(ref, y):
    m = y.shape[0]
    for s_ in range(SLAB):
        ref[pl.ds(s_, m, stride=SLAB), :] = y[:, s_ * LANES:(s_ + 1) * LANES]


def _slab_piece(ref, s_, m):
    return ref[pl.ds(s_, m, stride=SLAB), :]


def _from_slabs(ref, m):
    return jnp.concatenate([_slab_piece(ref, s_, m) for s_ in range(SLAB)], axis=1)


def _tok_copy(src, src_tok, dst, dst_tok, sem):
    return pltpu.make_async_copy(src.at[pl.ds(pl.multiple_of(src_tok * SLAB, SLAB), SLAB)],
                                 dst.at[pl.ds(pl.multiple_of(dst_tok * SLAB, SLAB), SLAB)], sem)


DMA_UNROLL = 8


def _dispatch_kernel(pos_ref, pad_ref, nt_ref, h_ref, xs_hbm, zero_scr, sem, zsem, *, ts, tm, ntmax):
    @pl.when(pl.program_id(0) == 0)
    def _():
        zero_scr[...] = jnp.zeros_like(zero_scr)

        def fill(first_slot):
            return pltpu.make_async_copy(
                zero_scr, xs_hbm.at[pl.ds(pl.multiple_of(first_slot * SLAB, SLAB), tm * SLAB)], zsem)

        for e in range(N_EXPERTS):
            fill(pad_ref[0, e]).start()
        for e in range(N_EXPERTS):
            fill(pad_ref[0, e]).wait()

        ntot = nt_ref[0, 0]

        def tail_start(k, carry):
            fill(k * tm).start()
            return carry

        def tail_wait(k, carry):
            fill(k * tm).wait()
            return carry

        lax.fori_loop(ntot, ntmax, tail_start, 0)
        lax.fori_loop(ntot, ntmax, tail_wait, 0)

    def issue(g, carry):
        for u in range(DMA_UNROLL):
            r = g * DMA_UNROLL + u
            for ch in range(2):
                _tok_copy(h_ref, r, xs_hbm, pos_ref[0, ch, r], sem).start(priority=ch)
        return carry

    def drain(g, carry):
        for _ in range(2 * DMA_UNROLL):
            _tok_copy(h_ref, 0, xs_hbm, 0, sem).wait()
        return carry

    lax.fori_loop(0, ts // DMA_UNROLL, issue, 0)
    lax.fori_loop(0, ts // DMA_UNROLL, drain, 0)


def _dispatch(pos, pad, nt, h, ts, tm):
    nb = pos.shape[0]
    n = h.shape[0] // SLAB
    ntmax = _sorted_rows(n) // tm + 1
    smem = pl.BlockSpec(memory_space=pltpu.SMEM)
    return pl.pallas_call(
        functools.partial(_dispatch_kernel, ts=ts, tm=tm, ntmax=ntmax),
        grid=(nb,),
        in_specs=[pl.BlockSpec((1, 2, ts), lambda i: (i, 0, 0), memory_space=pltpu.SMEM), smem, smem,
                  pl.BlockSpec((ts * SLAB, LANES), lambda i: (i, 0))],
        out_specs=pl.BlockSpec(memory_space=pl.ANY),
        out_shape=jax.ShapeDtypeStruct((ntmax * tm * SLAB, LANES), F32),
        scratch_shapes=[pltpu.VMEM((tm * SLAB, LANES), F32), pltpu.SemaphoreType.DMA,
                        pltpu.SemaphoreType.DMA],
        compiler_params=_cparams("arbitrary"),
        name="moe_dispatch",
    )(pos, pad, nt, h)


def _experts_kernel(te_ref, nt_ref, xs_ref, wg_ref, wu_ref, wd_ref, ys_ref, *, tm):
    @pl.when(pl.program_id(0) < nt_ref[0])
    def _():
        x = _from_slabs(xs_ref, tm).astype(BF16)
        a = _dot(x, wg_ref[0].astype(BF16))
        u = _dot(x, wu_ref[0].astype(BF16))
        hid = (a * _sigmoid(a)) * u
        _to_slabs(ys_ref, _dot(hid.astype(BF16), wd_ref[0].astype(BF16)))


def _experts(te, nt, xs, wg, wu, wd, layer, tm):
    ntmax = xs.shape[0] // (tm * SLAB)
    first = layer * N_EXPERTS
    tile = pl.BlockSpec((tm * SLAB, LANES), lambda k, te, nt: (jnp.minimum(k, nt[0] - 1), 0))
    out_tile = tile
    return pl.pallas_call(
        functools.partial(_experts_kernel, tm=tm),
        grid_spec=pltpu.PrefetchScalarGridSpec(
            num_scalar_prefetch=2,
            grid=(ntmax,),
            in_specs=[tile,
                      pl.BlockSpec((1, D_MODEL, D_EXPERT), lambda k, te, nt: (first + te[k], 0, 0)),
                      pl.BlockSpec((1, D_MODEL, D_EXPERT), lambda k, te, nt: (first + te[k], 0, 0)),
                      pl.BlockSpec((1, D_EXPERT, D_MODEL), lambda k, te, nt: (first + te[k], 0, 0))],
            out_specs=out_tile),
        out_shape=jax.ShapeDtypeStruct(xs.shape, F32),
        input_output_aliases={2: 0},
        compiler_params=_cparams("arbitrary"),
        name="moe_experts_sorted",
    )(te, nt, xs, wg, wu, wd)


def _combine_kernel(pos_ref, posn_ref, x_ref, w_ref, ys_hbm, *rest, tc):
    if len(rest) > 3:
        pg_ref, pw_ref, pb_ref, y_ref, u_ref, buf, sems = rest
    else:
        (y_ref, buf, sems), u_ref = rest, None
    i = pl.program_id(0)
    nb = pl.num_programs(0)

    def start(p_ref, slot):
        def body(g, carry):
            for u in range(DMA_UNROLL):
                r = g * DMA_UNROLL + u
                for ch in range(2):
                    _tok_copy(ys_hbm, p_ref[0, ch, r], buf.at[slot, ch], r, sems.at[slot]).start(priority=ch)
            return carry
        lax.fori_loop(0, tc // DMA_UNROLL, body, 0)

    def wait(slot):
        def body(g, carry):
            for _ in range(2 * DMA_UNROLL):
                _tok_copy(ys_hbm, 0, buf.at[slot, 0], 0, sems.at[slot]).wait()
            return carry
        lax.fori_loop(0, tc // DMA_UNROLL, body, 0)

    @pl.when(i == 0)
    def _():
        start(pos_ref, 0)

    w = w_ref[...]
    w1, w2 = w[:, 2:3], w[:, 3:4]
    for slot in range(2):
        @pl.when(i % 2 == slot)
        def _(slot=slot):
            @pl.when(i + 1 < nb)
            def _():
                start(posn_ref, 1 - slot)

            wait(slot)
            pieces = []
            for s_ in range(SLAB):
                cols = slice(s_ * LANES, (s_ + 1) * LANES)
                piece = x_ref[:, cols] + (w1 * _slab_piece(buf.at[slot, 0], s_, tc)
                                          + w2 * _slab_piece(buf.at[slot, 1], s_, tc))
                y_ref[:, cols] = piece
                pieces.append(piece)
            if u_ref is not None:
                h = _rms(jnp.concatenate(pieces, axis=1), pg_ref[...]).astype(BF16)
                a = _dot(h, pw_ref[:, 0:D_MODEL]) + pb_ref[:, 0:D_MODEL]
                g = _dot(h, pw_ref[:, D_MODEL:2 * D_MODEL]) + pb_ref[:, D_MODEL:2 * D_MODEL]
                u_ref[...] = a * _sigmoid(g)


def _combine(pos, x, wcol, ys, tc, pointwise=None):
    n = x.shape[0]
    nb = n // tc
    row = lambda width: pl.BlockSpec((tc, width), lambda i: (i, 0))
    in_specs = [pl.BlockSpec((1, 2, tc), lambda i: (i, 0, 0), memory_space=pltpu.SMEM),
                pl.BlockSpec((1, 2, tc), lambda i: (jnp.minimum(i + 1, nb - 1), 0, 0), memory_space=pltpu.SMEM),
                row(D_MODEL), row(LANES), pl.BlockSpec(memory_space=pl.ANY)]
    out_specs = [row(D_MODEL)]
    out_shape = [jax.ShapeDtypeStruct((n, D_MODEL), F32)]
    args = [pos, pos, x, wcol, ys]
    if pointwise is not None:
        in_specs += [_full((1, D_MODEL)), _full((D_MODEL, 2 * D_MODEL)), _full((1, 2 * D_MODEL))]
        out_specs += [row(D_MODEL)]
        out_shape += [jax.ShapeDtypeStruct((n, D_MODEL), F32)]
        args += list(pointwise)
    return pl.pallas_call(
        functools.partial(_combine_kernel, tc=tc),
        grid=(nb,),
        in_specs=in_specs,
        out_specs=out_specs,
        out_shape=out_shape,
        scratch_shapes=[pltpu.VMEM((2, 2, tc * SLAB, LANES), F32), pltpu.SemaphoreType.DMA((2,))],
        compiler_params=_cparams("arbitrary"),
        name="moe_combine",
    )(*args)


def _pw1_kernel(x_ref, g_ref, w_ref, b_ref, u_ref):
    h = _rms(x_ref[...], g_ref[...]).astype(BF16)
    a = _dot(h, w_ref[:, 0:D_MODEL]) + b_ref[:, 0:D_MODEL]
    g = _dot(h, w_ref[:, D_MODEL:2 * D_MODEL]) + b_ref[:, D_MODEL:2 * D_MODEL]
    u_ref[...] = a * _sigmoid(g)


def _pw1(x, g, w, b, tm):
    n = x.shape[0]
    row = lambda width: pl.BlockSpec((tm, width), lambda i: (i, 0))
    return pl.pallas_call(
        _pw1_kernel,
        grid=(n // tm,),
        in_specs=[row(D_MODEL), _full((1, D_MODEL)), _full((D_MODEL, 2 * D_MODEL)), _full((1, 2 * D_MODEL))],
        out_specs=row(D_MODEL),
        out_shape=jax.ShapeDtypeStruct((n, D_MODEL), F32),
        compiler_params=_cparams("parallel"),
        name="conv_pw1_glu",
    )(x, g, w, b)


def _conv_kernel(u_ref, hist_ref, x_ref, wdw_ref, bdw_ref, lng_ref, lnb_ref, w2_ref, y_ref, ext_scr, sh_scr,
                 *, tt, rs):
    ti = pl.program_id(1)

    @pl.when(ti == 0)
    def _():
        ext_scr[0:HIST_ROWS, :] = hist_ref[0]

    ext_scr[HIST_ROWS:HIST_ROWS + tt, :] = u_ref[0]
    pad = HIST_ROWS - (CONV_W - 1)
    rows = tt + HIST_ROWS - 8
    for r in range(1, 8):
        sh_scr[r - 1, :, :] = ext_scr[r:r + rows, :]
    parts = []
    for r0 in range(0, tt, rs):
        acc = jnp.zeros((rs, D_MODEL), F32)
        for kk in range(CONV_W):
            a, r = divmod(kk + pad, 8)
            lo = r0 + 8 * a
            src = ext_scr[lo:lo + rs, :] if r == 0 else sh_scr[r - 1, lo:lo + rs, :]
            acc = acc + wdw_ref[kk:kk + 1, :] * src
        parts.append(acc)
    y = jnp.concatenate(parts, axis=0) + bdw_ref[...]
    mu = jnp.mean(y, axis=-1, keepdims=True)
    yc = y - mu
    var = jnp.mean(yc * yc, axis=-1, keepdims=True)
    yn = yc * lax.rsqrt(var + EPS) * lng_ref[...] + lnb_ref[...]
    z = yn * _sigmoid(yn)
    y_ref[0] = x_ref[0] + _dot(z.astype(BF16), w2_ref[...])
    if tt >= HIST_ROWS:
        ext_scr[0:HIST_ROWS, :] = ext_scr[tt:tt + HIST_ROWS, :]


def _conv_module(u, hist, x, wdw, bdw, lng, lnb, w2, tt):
    b, t, _ = u.shape
    rs = min(tt, 32)
    blk = pl.BlockSpec((1, tt, D_MODEL), lambda bi, ti: (bi, ti, 0))
    return pl.pallas_call(
        functools.partial(_conv_kernel, tt=tt, rs=rs),
        grid=(b, t // tt),
        in_specs=[blk, pl.BlockSpec((1, HIST_ROWS, D_MODEL), lambda bi, ti: (bi, 0, 0)), blk,
                  _full((HIST_ROWS, D_MODEL)), _full((1, D_MODEL)), _full((1, D_MODEL)),
                  _full((1, D_MODEL)), _full((D_MODEL, D_MODEL))],
        out_specs=blk,
        out_shape=jax.ShapeDtypeStruct((b, t, D_MODEL), F32),
        scratch_shapes=[pltpu.VMEM((HIST_ROWS + tt, D_MODEL), F32),
                        pltpu.VMEM((7, tt + HIST_ROWS - 8, D_MODEL), F32)],
        compiler_params=_cparams("parallel", "arbitrary"),
        name="conv_module",
    )(u, hist, x, wdw, bdw, lng, lnb, w2)


def _rope_tables(pos):
    half = RET_DK // 2
    inv_freq = ROPE_BASE ** (-jnp.arange(half, dtype=F32) / half)
    ang = pos.astype(F32)[:, None] * inv_freq[None, :]
    cos, sin = jnp.cos(ang), jnp.sin(ang)
    reps = LANES // RET_DK
    cos_t = jnp.tile(jnp.concatenate([cos, cos], axis=1), (1, reps))
    sin_t = jnp.tile(jnp.concatenate([-sin, sin], axis=1), (1, reps))
    return cos_t, sin_t


def _pad_lanes(v, width=LANES):
    v = v.reshape(1, -1)
    return jnp.pad(v, ((0, 0), (0, width - v.shape[1])))


def _moe_dense(x, p, l, tm):
    h, gates = _router(x, p["norm_ffn"][l], p["wr_hi"][l], p["wr_lo"][l], p["br"][l], tm)
    return _moe(h, gates, x, p["wg"], p["wu"], p["wd"], l, tm)


def _sorted_rows(n_tokens):
    return (2 * n_tokens // MOE_TM + N_EXPERTS) * MOE_TM


def _router_args(p, l):
    return p["norm_ffn"][l], p["wt_hi"][l], p["wt_lo"][l], p["bt"][l]


def _moe_sparse(x, p, l, tm, routed=None, pointwise=None):
    h, route, wcol = routed if routed is not None else _router_t(x, *_router_args(p, l), tm)
    pos, te, nt, pad = _plan(route, PLAN_TB, MOE_TM)
    xs = _dispatch(pos, pad, nt, h, PLAN_TB, MOE_TM)
    ys = _experts(te.reshape(-1), nt[0, :1], xs, p["wg"], p["wu"], p["wd"], l, MOE_TM)
    return _combine(pos, x, wcol, ys, PLAN_TB, pointwise)


def _run_group(x, p, *, sparse, seq, tm, fox_hist, ret_state, conv_hist, pos, tq, tk, ret_c, conv_tt):
    b = x.shape[0] // seq
    q_off = 0 if fox_hist is None else fox_hist[0].shape[1]

    cos_t, sin_t = _rope_tables(pos)
    q, k, v, lf, rq, rk, rv, rg, k5, v5 = _proj0(x, p["norm_mix"][0], p["w_in"], p["b_f"], p["q_gain"], p["k_gain"],
                                          p["gbd"], cos_t, sin_t, tm)
    k3 = k.reshape(b, seq, FOX_W)
    v3 = v.reshape(b, seq, FOX_W)
    lf3 = lf.reshape(b, seq, LANES)
    if fox_hist is None:
        k_all, v_all, lf_all = k3, v3, lf3
    else:
        ck_, cv_, clf_ = fox_hist
        tot = q_off + seq
        padded = -(-tot // tk) * tk
        tail = padded - tot
        k_all = jnp.concatenate([ck_, k3, jnp.zeros((b, tail, FOX_W), F32)], axis=1)
        v_all = jnp.concatenate([cv_, v3, jnp.zeros((b, tail, FOX_W), F32)], axis=1)
        clf_ = jnp.pad(clf_, ((0, 0), (0, 0), (0, LANES - FOX_HEADS)))
        lf_all = jnp.concatenate([clf_, lf3, jnp.zeros((b, tail, LANES), F32)], axis=1)
    ccol, crow = _cumsum_logf(lf_all, CUM_BLOCK)
    if fox_hist is None:
        qa, ka, vt = _fox_prep(q.reshape(b, seq, FOX_W), k_all, v_all, ccol, tk)
        o_fox = _fox_attention_t(qa, ka, vt, tq)
    else:
        o_fox = _fox_attention(q.reshape(b, seq, FOX_W), k_all, v_all, ccol, crow, tq, tk, q_off)
    o_ret, s_fin = _retention(p["lg_tab"], rq.reshape(b, seq, RET_QK_W), rk.reshape(b, seq, RET_QK_W),
                              rv.reshape(b, seq, RET_V_W), rg.reshape(b, seq, RET_V_W), ret_state,
                              p["gn_gain"], ret_c)
    o_fox2, o_ret2 = o_fox.reshape(-1, FOX_W), o_ret.reshape(-1, RET_V_W)
    pw1_args = (p["norm_mix"][1], p["w_pw1"], p["b_pw1"])
    if sparse:
        x, *routed = _out0(x, o_fox2, o_ret2, p["w_out"], tm, _router_args(p, 0))
        x, u = _moe_sparse(x, p, 0, tm, routed, pw1_args)
    else:
        x = _moe_dense(_out0(x, o_fox2, o_ret2, p["w_out"], tm)[0], p, 0, tm)
        u = _pw1(x, *pw1_args, tm)

    u3 = u.reshape(b, seq, D_MODEL)
    x = _conv_module(u3, conv_hist, x.reshape(b, seq, D_MODEL), p["w_dw"], p["b_dw"], p["ln_g"], p["ln_b"],
                     p["w_pw2"], conv_tt).reshape(-1, D_MODEL)
    x = _moe_sparse(x, p, 1, tm)[0] if sparse else _moe_dense(x, p, 1, tm)

    fox_k = k5.reshape(1, b, seq, FOX_HEADS, FOX_HD)
    fox_v = v5.reshape(1, b, seq, FOX_HEADS, FOX_HD)
    fox_lf = lf3[:, :, :FOX_HEADS].reshape(1, b, seq, FOX_HEADS)
    return x.reshape(b, seq, D_MODEL), fox_k, fox_v, fox_lf, s_fin[None], u3


def kernel(x_prompt, x_sample, cache_fox_k, cache_fox_v, cache_fox_logf, state_ret, cache_conv, norm_mix, norm_ffn, w_in_mix, b_forget, fox_q_gain, fox_k_gain, ret_gn_gain, w_out_mix, w_pw1, b_pw1, w_dw, b_dw, conv_ln_g, conv_ln_b, w_pw2, w_router_group, b_router_group, w_router_expert, b_router_expert, w_exp_gate, w_exp_up, w_exp_down):
    bp, t, d = x_prompt.shape
    bs, l, _ = x_sample.shape
    past = cache_fox_k.shape[2]
    depth = norm_mix.shape[0]
    assert d == D_MODEL and depth == 2 and w_in_mix.shape[0] == 1 and w_pw1.shape[0] == 1

    w_in = w_in_mix[0]
    n_pre = 3 * FOX_W
    w_in_r = jnp.concatenate(
        [w_in[:, :n_pre], w_in[:, n_pre + FOX_HEADS:], w_in[:, n_pre:n_pre + FOX_HEADS],
         jnp.zeros((D_MODEL, LANES - FOX_HEADS), F32)], axis=1).astype(BF16)
    hid = jnp.arange(FOX_W) // FOX_HD
    gbd = jnp.where(hid[:, None] == hid[None, :], 1.0 / FOX_HD, 0.0).astype(BF16)
    w_r = jnp.concatenate([w_router_group, w_router_expert], axis=-1)
    w_r = jnp.pad(w_r, ((0, 0), (0, 0), (0, LANES - w_r.shape[-1])))
    wr_hi = w_r.astype(BF16)
    wr_lo = (w_r - wr_hi.astype(F32)).astype(BF16)
    b_r = jnp.concatenate([b_router_group, b_router_expert], axis=-1)
    b_r = jnp.pad(b_r, ((0, 0), (0, LANES - b_r.shape[-1])))
    w_t = jnp.swapaxes(jnp.concatenate([w_router_expert, w_router_group], axis=-1), 1, 2)
    w_t = jnp.pad(w_t, ((0, 0), (0, ROUTE_ROWS - w_t.shape[1]), (0, 0)))
    wt_hi = w_t.astype(BF16)
    wt_lo = (w_t - wt_hi.astype(F32)).astype(BF16)
    b_t = jnp.concatenate([b_router_expert, b_router_group], axis=-1)
    b_t = jnp.pad(b_t, ((0, 0), (0, ROUTE_ROWS - b_t.shape[-1])))[:, :, None]
    log_gamma = jnp.log(1.0 - 2.0 ** (-5.0 - jnp.arange(RET_HEADS, dtype=F32)))
    p = {
        "norm_mix": norm_mix.reshape(depth, 1, D_MODEL),
        "norm_ffn": norm_ffn.reshape(depth, 1, D_MODEL),
        "w_in": w_in_r,
        "b_f": _pad_lanes(b_forget[0]),
        "q_gain": jnp.tile(fox_q_gain[0], FOX_HEADS).reshape(1, FOX_W),
        "k_gain": jnp.tile(fox_k_gain[0], FOX_HEADS).reshape(1, FOX_W),
        "gbd": gbd,
        "gn_gain": ret_gn_gain[0].reshape(1, RET_V_W),
        "lg_tab": jnp.broadcast_to(log_gamma[:, None, None], (RET_HEADS, 1, LANES)),
        "w_out": w_out_mix[0].astype(BF16),
        "w_pw1": w_pw1[0].astype(BF16),
        "b_pw1": b_pw1[0].reshape(1, -1),
        "w_dw": jnp.pad(w_dw[0], ((0, HIST_ROWS - CONV_W), (0, 0))),
        "b_dw": b_dw[0].reshape(1, -1),
        "ln_g": conv_ln_g[0].reshape(1, -1),
        "ln_b": conv_ln_b[0].reshape(1, -1),
        "w_pw2": w_pw2[0].astype(BF16),
        "wr_hi": wr_hi,
        "wr_lo": wr_lo,
        "br": b_r.reshape(depth, 1, LANES),
        "wt_hi": wt_hi,
        "wt_lo": wt_lo,
        "bt": b_t,
        "wg": w_exp_gate.reshape(depth * N_EXPERTS, D_MODEL, D_EXPERT),
        "wu": w_exp_up.reshape(depth * N_EXPERTS, D_MODEL, D_EXPERT),
        "wd": w_exp_down.reshape(depth * N_EXPERTS, D_EXPERT, D_MODEL),
    }

    hist_pad = HIST_ROWS - (CONV_W - 1)
    yp, fk_p, fv_p, lf_p, rs_p, u_p = _run_group(
        x_prompt.reshape(bp * t, D_MODEL), p, sparse=True, seq=t, tm=512, fox_hist=None,
        ret_state=jnp.zeros((bp, RET_HEADS, RET_DK, RET_DV), F32),
        conv_hist=jnp.zeros((bp, HIST_ROWS, D_MODEL), F32),
        pos=jnp.arange(t), tq=256, tk=256, ret_c=256, conv_tt=256)

    ns = bs * l
    ys, fk_s, fv_s, lf_s, rs_s, u_s = _run_group(
        x_sample.reshape(ns, D_MODEL), p, sparse=False, seq=l, tm=ns,
        fox_hist=(cache_fox_k[0].reshape(bs, past, FOX_W), cache_fox_v[0].reshape(bs, past, FOX_W),
                  cache_fox_logf[0]),
        ret_state=state_ret[0],
        conv_hist=jnp.pad(cache_conv[0], ((0, 0), (hist_pad, 0), (0, 0))),
        pos=past + (jnp.arange(ns) % l), tq=l, tk=-(-(past + l) // CUM_BLOCK) * CUM_BLOCK, ret_c=l, conv_tt=l)

    conv_p = u_p[:, t - (CONV_W - 1):][None]
    conv_s = jnp.concatenate([cache_conv[0], u_s], axis=1)[:, l:][None]
    return (yp, ys, fk_p, fv_p, lf_p, rs_p, conv_p, fk_s, fv_s, lf_s, rs_s, conv_s)
```

```python
import functools
import math

import jax
import jax.numpy as jnp
import numpy as np
from jax import lax
from jax.experimental import pallas as pl
from jax.experimental.pallas import tpu as pltpu

F32 = jnp.float32
BF16 = jnp.bfloat16

D_MODEL = 1024
FOX_HEADS = 8
FOX_HD = 64
RET_HEADS = 4
RET_DK = 64
RET_DV = 128
ROPE_BASE = 10000.0
CONV_W = 31
N_GROUPS = 4
EXP_PER_GROUP = 8
N_EXPERTS = N_GROUPS * EXP_PER_GROUP
D_EXPERT = 256
EPS = 1e-6
FOX_W = FOX_HEADS * FOX_HD
RET_QK_W = RET_HEADS * RET_DK
RET_V_W = RET_HEADS * RET_DV

LANES = 128
HIST_ROWS = 32
ROUTER_LANE0 = N_GROUPS
MASKED = -1e30
VMEM_LIMIT = 56 * 1024 * 1024
CUM_BLOCK = 256
MOE_TM = 512
PLAN_TB = 1024
ROUTE_ROWS = 48

C_FQ, C_FK, C_FV = 0, FOX_W, 2 * FOX_W
C_RQ = 3 * FOX_W
C_RK = C_RQ + RET_QK_W
C_RV = C_RK + RET_QK_W
C_RG = C_RV + RET_V_W
C_FF = C_RG + RET_V_W
MIX_COLS = C_FF + LANES


def _cparams(*sem):
    return pltpu.CompilerParams(dimension_semantics=sem, vmem_limit_bytes=VMEM_LIMIT)


def _full(shape):
    n = len(shape)
    return pl.BlockSpec(shape, lambda *_: (0,) * n)


def _rms(x, g):
    ms = jnp.mean(x * x, axis=-1, keepdims=True)
    return x * lax.rsqrt(ms + EPS) * g


def _sigmoid(x):
    return 1.0 / (1.0 + jnp.exp(-x))


def _dot(a, b):
    return jnp.dot(a, b, preferred_element_type=F32)


def _dot_nt(a, b):
    return lax.dot_general(a, b, (((1,), (1,)), ((), ())), preferred_element_type=F32)


def _dot_tn(a, b):
    return lax.dot_general(a, b, (((0,), (0,)), ((), ())), preferred_element_type=F32)


def _proj0_kernel(x_ref, g_ref, w_ref, bf_ref, qg_ref, kg_ref, gbd_ref, cos_ref, sin_ref,
                  q_ref, k_ref, v_ref, lf_ref, rq_ref, rk_ref, rv_ref, rg_ref, k5_ref, v5_ref):
    h = _rms(x_ref[...], g_ref[...]).astype(BF16)

    def seg(a, b):
        return _dot(h, w_ref[:, a:b])

    gbd = gbd_ref[...]

    def head_rms(y, gain):
        ms = _dot((y * y).astype(BF16), gbd)
        return y * lax.rsqrt(ms + EPS) * gain

    q_ref[...] = (head_rms(seg(C_FQ, C_FK), qg_ref[...]) * (FOX_HD ** -0.5)).astype(BF16)
    kk = head_rms(seg(C_FK, C_FV), kg_ref[...])
    vv = seg(C_FV, C_RQ)
    k_ref[...] = kk
    v_ref[...] = vv
    k5_ref[...] = kk.reshape(kk.shape[0], FOX_HEADS, FOX_HD)
    v5_ref[...] = vv.reshape(vv.shape[0], FOX_HEADS, FOX_HD)

    z = seg(C_FF, MIX_COLS) + bf_ref[...]
    logf = jnp.minimum(z, 0.0) - jnp.log(1.0 + jnp.exp(-jnp.abs(z)))
    lane = lax.broadcasted_iota(jnp.int32, (1, LANES), 1)
    lf_ref[...] = jnp.where(lane < FOX_HEADS, logf, 0.0)

    cos = cos_ref[...]
    sin = sin_ref[...]
    first_half = (lane % RET_DK) < (RET_DK // 2)

    def rotary(y):
        outs = []
        for s in range(y.shape[1] // LANES):
            ys = y[:, s * LANES:(s + 1) * LANES]
            rot = jnp.where(first_half, pltpu.roll(ys, LANES - RET_DK // 2, 1),
                            pltpu.roll(ys, RET_DK // 2, 1))
            outs.append(ys * cos + rot * sin)
        return jnp.concatenate(outs, axis=1)

    rq_ref[...] = rotary(seg(C_RQ, C_RK)).astype(BF16)
    rk_ref[...] = (rotary(seg(C_RK, C_RV)) * (RET_DK ** -0.5)).astype(BF16)
    rv_ref[...] = seg(C_RV, C_RG).astype(BF16)
    rg_ref[...] = seg(C_RG, C_FF).astype(BF16)


def _proj0(x, g, w, bf, qg, kg, gbd, cos, sin, tm):
    n = x.shape[0]
    nper = cos.shape[0] // tm
    row = lambda width: pl.BlockSpec((tm, width), lambda i: (i, 0))
    tab = pl.BlockSpec((tm, LANES), lambda i: (i % nper, 0))
    outs = [(FOX_W, BF16), (FOX_W, F32), (FOX_W, F32), (LANES, F32),
            (RET_QK_W, BF16), (RET_QK_W, BF16), (RET_V_W, BF16), (RET_V_W, BF16)]
    return pl.pallas_call(
        _proj0_kernel,
        grid=(n // tm,),
        in_specs=[row(D_MODEL), _full((1, D_MODEL)), _full((D_MODEL, MIX_COLS)), _full((1, LANES)),
                  _full((1, FOX_W)), _full((1, FOX_W)), _full((FOX_W, FOX_W)), tab, tab],
        out_specs=[row(wd) for wd, _ in outs] + [pl.BlockSpec((tm, FOX_HEADS, FOX_HD), lambda i: (i, 0, 0))] * 2,
        out_shape=[jax.ShapeDtypeStruct((n, wd), dt) for wd, dt in outs]
        + [jax.ShapeDtypeStruct((n, FOX_HEADS, FOX_HD), F32)] * 2,
        compiler_params=_cparams("parallel"),
        name="proj0",
    )(x, g, w, bf, qg, kg, gbd, cos, sin)


def _cum_kernel(lf_ref, ccol_ref, crow_ref, *, t, cb):
    r = lax.broadcasted_iota(jnp.int32, (cb, cb), 0)
    c = lax.broadcasted_iota(jnp.int32, (cb, cb), 1)
    tri = jnp.where(r >= c, 1.0, 0.0).astype(BF16)
    carry = jnp.zeros((1, LANES), F32)
    for blk in range(t // cb):
        a = lf_ref[0, blk * cb:(blk + 1) * cb, :]
        a1 = a.astype(BF16)
        r1 = a - a1.astype(F32)
        a2 = r1.astype(BF16)
        a3 = (r1 - a2.astype(F32)).astype(BF16)
        cc = (_dot(tri, a1) + _dot(tri, a2)) + _dot(tri, a3) + carry
        ccol_ref[0, blk * cb:(blk + 1) * cb, :] = cc
        crow_ref[0, :, blk * cb:(blk + 1) * cb] = cc.T[0:FOX_HEADS, :]
        carry = cc[cb - 1:cb, :]


def _cumsum_logf(lf, cb):
    b, t, _ = lf.shape
    return pl.pallas_call(
        functools.partial(_cum_kernel, t=t, cb=cb),
        grid=(b,),
        in_specs=[pl.BlockSpec((1, t, LANES), lambda i: (i, 0, 0))],
        out_specs=[pl.BlockSpec((1, t, LANES), lambda i: (i, 0, 0)),
                   pl.BlockSpec((1, FOX_HEADS, t), lambda i: (i, 0, 0))],
        out_shape=[jax.ShapeDtypeStruct((b, t, LANES), F32),
                   jax.ShapeDtypeStruct((b, FOX_HEADS, t), F32)],
        compiler_params=_cparams("parallel"),
        name="cumsum_logf",
    )(lf)


def _fox_kernel(q_ref, k_ref, v_ref, cq_ref, ck_ref, o_ref, *, tq, tk, q_off):
    hp = pl.program_id(1)
    i = pl.program_id(2)
    q0 = q_off + i * tq
    nfull = (q0 + 1) // tk
    nch = (q0 + tq + tk - 1) // tk
    lane = lax.broadcasted_iota(jnp.int32, (1, LANES), 1)
    q = q_ref[0]
    cq_all = cq_ref[0]
    qpos = q0 + lax.broadcasted_iota(jnp.int32, (tq, 1), 0)
    kiota = lax.broadcasted_iota(jnp.int32, (1, tk), 1)
    qms = [jnp.where((lane // FOX_HD) == h2, q, jnp.zeros_like(q)) for h2 in range(2)]
    cqs = [jnp.sum(jnp.where(lane == 2 * hp + h2, cq_all, 0.0), axis=-1, keepdims=True) for h2 in range(2)]

    def step(j, carry, masked):
        start = pl.multiple_of(j * tk, tk)
        kj = k_ref[0, pl.ds(start, tk), :].astype(BF16)
        vj = v_ref[0, pl.ds(start, tk), :].astype(BF16)
        new = []
        for h2 in range(2):
            m, l, acc = carry[h2]
            s = _dot_nt(qms[h2], kj)
            s = s + cqs[h2] - ck_ref[0, 2 * hp + h2, j]
            if masked:
                s = jnp.where(j * tk + kiota <= qpos, s, MASKED)
            m_new = jnp.maximum(m, jnp.max(s, axis=-1, keepdims=True))
            alpha = jnp.exp(m - m_new)
            p = jnp.exp(s - m_new)
            l = alpha * l + jnp.sum(p, axis=-1, keepdims=True)
            acc = alpha * acc + _dot(p.astype(BF16), vj)
            new.append((m_new, l, acc))
        return tuple(new)

    one = (jnp.full((tq, 1), MASKED, F32), jnp.zeros((tq, 1), F32), jnp.zeros((tq, LANES), F32))
    carry = lax.fori_loop(0, nfull, functools.partial(step, masked=False), (one, one))
    carry = lax.fori_loop(nfull, nch, functools.partial(step, masked=True), carry)
    outs = [acc / l for _, l, acc in carry]
    o_ref[0] = jnp.where(lane < FOX_HD, outs[0], outs[1]).astype(BF16)


def _fox_attention(q, k, v, ccol, crow, tq, tk, q_off):
    b, tqs, _ = q.shape
    tks = k.shape[1]
    nck = tks // tk
    crow5 = crow.reshape(b, FOX_HEADS, nck, 1, tk)
    qb0 = q_off // tq
    return pl.pallas_call(
        functools.partial(_fox_kernel, tq=tq, tk=tk, q_off=q_off),
        grid=(b, FOX_HEADS // 2, tqs // tq),
        in_specs=[pl.BlockSpec((1, tq, LANES), lambda bi, hp, i: (bi, i, hp)),
                  pl.BlockSpec((1, tks, LANES), lambda bi, hp, i: (bi, 0, hp)),
                  pl.BlockSpec((1, tks, LANES), lambda bi, hp, i: (bi, 0, hp)),
                  pl.BlockSpec((1, tq, LANES), lambda bi, hp, i: (bi, qb0 + i, 0)),
                  pl.BlockSpec((1, FOX_HEADS, nck, 1, tk), lambda bi, hp, i: (bi, 0, 0, 0, 0))],
        out_specs=pl.BlockSpec((1, tq, LANES), lambda bi, hp, i: (bi, i, hp)),
        out_shape=jax.ShapeDtypeStruct((b, tqs, FOX_W), BF16),
        compiler_params=_cparams("parallel", "parallel", "parallel"),
        name="fox_attention",
    )(q, k, v, ccol, crow5)


FOX_VROWS = 80


def _bf16_pieces(c):
    hi = c.astype(BF16).astype(F32)
    r = c - hi
    mid = r.astype(BF16).astype(F32)
    lo = (r - mid).astype(BF16).astype(F32)
    return hi, mid, lo


def _fox_prep_kernel(q_ref, k_ref, v_ref, c_ref, qa_ref, ka_ref, vt_ref, *, t, rc):
    hp = pl.program_id(1)
    lane = lax.broadcasted_iota(jnp.int32, (1, LANES), 1)
    src = lax.broadcasted_iota(jnp.int32, (LANES, 1), 0)
    one_lane = 3 * FOX_HEADS

    def placer(h, c_to, one_to, sign):
        m = jnp.zeros((LANES, LANES), F32)
        for piece in range(3):
            m = jnp.where((src == piece * FOX_HEADS + h) & (lane == c_to + piece), sign, m)
            m = jnp.where((src == one_lane) & (lane == one_to + piece), 1.0, m)
        return m.astype(BF16)

    place_q = [placer(2 * hp + h2, FOX_HD, FOX_HD + 3, 1.0) for h2 in range(2)]
    place_k = [placer(2 * hp + h2, FOX_HD + 3, FOX_HD, -1.0) for h2 in range(2)]
    extra = jnp.where(lax.broadcasted_iota(jnp.int32, (FOX_VROWS - FOX_HD, rc), 0) == 0, 1.0, 0.0)
    for ch in range(t // rc):
        rows = slice(ch * rc, (ch + 1) * rc)
        qf = q_ref[0, rows, :].astype(F32)
        kf = k_ref[0, rows, :]
        hi, mid, lo = _bf16_pieces(c_ref[0, rows, :])
        pieces = (hi + pltpu.roll(mid, FOX_HEADS, 1) + pltpu.roll(lo, 2 * FOX_HEADS, 1)
                  + jnp.where(lane == one_lane, 1.0, 0.0)).astype(BF16)
        vt = v_ref[0, rows, :].T
        for h2 in range(2):
            qh = qf if h2 == 0 else pltpu.roll(qf, FOX_HD, 1)
            kh = kf if h2 == 0 else pltpu.roll(kf, FOX_HD, 1)
            qa_ref[0, h2, rows, :] = jnp.where(lane < FOX_HD, qh, _dot(pieces, place_q[h2])).astype(BF16)
            ka_ref[0, h2, rows, :] = jnp.where(lane < FOX_HD, kh, _dot(pieces, place_k[h2])).astype(BF16)
            vt_ref[0, h2, ch] = jnp.concatenate([vt[h2 * FOX_HD:(h2 + 1) * FOX_HD, :], extra], axis=0).astype(BF16)


def _fox_prep(q, k, v, ccol, rc):
    b, t, _ = q.shape
    pair = pl.BlockSpec((1, t, LANES), lambda bi, hp: (bi, 0, hp))
    aug = pl.BlockSpec((1, 2, t, LANES), lambda bi, hp: (bi, hp, 0, 0))
    return pl.pallas_call(
        functools.partial(_fox_prep_kernel, t=t, rc=rc),
        grid=(b, FOX_HEADS // 2),
        in_specs=[pair, pair, pair, pl.BlockSpec((1, t, LANES), lambda bi, hp: (bi, 0, 0))],
        out_specs=[aug, aug, pl.BlockSpec((1, 2, t // rc, FOX_VROWS, rc), lambda bi, hp: (bi, hp, 0, 0, 0))],
        out_shape=[jax.ShapeDtypeStruct((b, FOX_HEADS, t, LANES), BF16),
                   jax.ShapeDtypeStruct((b, FOX_HEADS, t, LANES), BF16),
                   jax.ShapeDtypeStruct((b, FOX_HEADS, t // rc, FOX_VROWS, rc), BF16)],
        compiler_params=_cparams("parallel", "parallel"),
        name="fox_prep",
    )(q, k, v, ccol)


def _fox_t_kernel(qa_ref, ka_ref, vt_ref, o_ref, s_scr, acc_scr, *, tq, tk):
    i = pl.program_id(1)
    q0 = i * tq
    nfull = (q0 + 1) // tk
    nch = (q0 + tq + tk - 1) // tk
    qpos = q0 + lax.broadcasted_iota(jnp.int32, (1, tq), 1)
    kiota = lax.broadcasted_iota(jnp.int32, (tk, 1), 0)
    heads = range(FOX_HEADS)

    def scores(j, mx, masked):
        start = pl.multiple_of(j * tk, tk)
        out = []
        for h in heads:
            st = _dot_nt(ka_ref[0, h, pl.ds(start, tk), :], qa_ref[0, h])
            if masked:
                st = jnp.where(start + kiota <= qpos, st, MASKED)
            s_scr[h, j] = st
            out.append(jnp.maximum(mx[h], jnp.max(st, axis=0, keepdims=True)))
        return tuple(out)

    mx = tuple(jnp.full((1, tq), MASKED, F32) for _ in heads)
    mx = lax.fori_loop(0, nfull, functools.partial(scores, masked=False), mx)
    mx = lax.fori_loop(nfull, nch, functools.partial(scores, masked=True), mx)

    acc_scr[...] = jnp.zeros_like(acc_scr)

    def weigh(j, carry):
        for h in heads:
            p = jnp.exp(s_scr[h, j] - mx[h]).astype(BF16)
            acc_scr[h] += _dot(vt_ref[0, h, j], p)
        return carry

    lax.fori_loop(0, nch, weigh, 0)
    for hp in range(FOX_HEADS // 2):
        outs = []
        for h in (2 * hp, 2 * hp + 1):
            acc = acc_scr[h]
            outs.append(acc[0:FOX_HD, :] / acc[FOX_HD:FOX_HD + 1, :])
        o_ref[0, :, hp * LANES:(hp + 1) * LANES] = jnp.concatenate(outs, axis=0).T.astype(BF16)


def _fox_attention_t(qa, ka, vt, tq):
    b, _, t, _ = qa.shape
    nck, tk = vt.shape[2], vt.shape[4]
    return pl.pallas_call(
        functools.partial(_fox_t_kernel, tq=tq, tk=tk),
        grid=(b, t // tq),
        in_specs=[pl.BlockSpec((1, FOX_HEADS, tq, LANES), lambda bi, i: (bi, 0, i, 0)),
                  pl.BlockSpec((1, FOX_HEADS, t, LANES), lambda bi, i: (bi, 0, 0, 0)),
                  pl.BlockSpec((1, FOX_HEADS, nck, FOX_VROWS, tk), lambda bi, i: (bi, 0, 0, 0, 0))],
        out_specs=pl.BlockSpec((1, tq, FOX_W), lambda bi, i: (bi, i, 0)),
        out_shape=jax.ShapeDtypeStruct((b, t, FOX_W), BF16),
        scratch_shapes=[pltpu.VMEM((FOX_HEADS, nck, tk, tq), F32),
                        pltpu.VMEM((FOX_HEADS, FOX_VROWS, tq), F32)],
        compiler_params=_cparams("parallel", "arbitrary"),
        name="fox_attention_t",
    )(qa, ka, vt)


def _ret_kernel(lg_ref, rq_ref, rk_ref, rv_ref, rg_ref, s0_ref, gn_ref, o_ref, sfin_ref, s_scr, d_scr, *, c, nbs):
    first = (pl.program_id(0) == 0) & (pl.program_id(1) == 0)
    ci = pl.program_id(1)
    ii = lax.broadcasted_iota(jnp.int32, (c, 1), 0).astype(F32)
    lane = lax.broadcasted_iota(jnp.int32, (1, LANES), 1)

    @pl.when(first)
    def _():
        jj = lax.broadcasted_iota(jnp.int32, (1, c), 1).astype(F32)
        diff = ii - jj
        for h in range(RET_HEADS):
            lg = lg_ref[h][:, 0:1]
            d_scr[h] = jnp.where(diff >= 0.0, jnp.exp(lg * jnp.maximum(diff, 0.0)), 0.0)

    @pl.when(ci == 0)
    def _():
        zero = jnp.zeros((RET_DK, RET_DV), F32)
        for bb in range(nbs):
            for h in range(RET_HEADS):
                lo, hi = (s0_ref[bb, h], zero) if h % 2 == 0 else (zero, s0_ref[bb, h])
                s_scr[bb * RET_HEADS + h, 0:RET_DK, :] = lo
                s_scr[bb * RET_HEADS + h, RET_DK:2 * RET_DK, :] = hi

    for bb, h in [(bb, h) for bb in range(nbs) for h in range(RET_HEADS)]:
        lg = lg_ref[h][:, 0:1]
        st = bb * RET_HEADS + h
        pair = slice((h // 2) * LANES, (h // 2 + 1) * LANES)
        mine = slice(h * RET_DV, (h + 1) * RET_DV)
        inhead = (lane // RET_DK) == (h % 2)
        q = rq_ref[bb, :, pair]
        k = rk_ref[bb, :, pair]
        qm = jnp.where(inhead, q, jnp.zeros_like(q))
        km = jnp.where(inhead, k, jnp.zeros_like(k))
        v = rv_ref[bb, :, mine]
        scores = _dot_nt(qm, km) * d_scr[h]
        inner = _dot(scores.astype(BF16), v)
        s_prev = s_scr[st]
        cross = _dot(qm, s_prev.astype(BF16)) * jnp.exp(lg * (ii + 1.0))
        y = inner + cross
        k_dec = (km.astype(F32) * jnp.exp(lg * (c - 1.0 - ii))).astype(BF16)
        s_new = jnp.exp(lg * float(c)) * s_prev + _dot_tn(k_dec, v)
        s_scr[st] = s_new

        mu = jnp.mean(y, axis=-1, keepdims=True)
        yc = y - mu
        var = jnp.mean(yc * yc, axis=-1, keepdims=True)
        yn = yc * lax.rsqrt(var + EPS) * gn_ref[:, mine]
        g = rg_ref[bb, :, mine].astype(F32)
        o_ref[bb, :, mine] = ((g * _sigmoid(g)) * yn).astype(BF16)
        off = (h % 2) * RET_DK
        sfin_ref[bb, h] = s_new[off:off + RET_DK, :]


def _retention(lg_tab, rq, rk, rv, rg, s0, gn, c):
    b, t, _ = rq.shape
    nbs = 2 if b % 2 == 0 else 1
    qk_spec = pl.BlockSpec((nbs, c, RET_QK_W), lambda bi, ci: (bi, ci, 0))
    v_spec = pl.BlockSpec((nbs, c, RET_V_W), lambda bi, ci: (bi, ci, 0))
    st_spec = pl.BlockSpec((nbs, RET_HEADS, RET_DK, RET_DV), lambda bi, ci: (bi, 0, 0, 0))
    return pl.pallas_call(
        functools.partial(_ret_kernel, c=c, nbs=nbs),
        grid=(b // nbs, t // c),
        in_specs=[_full((RET_HEADS, 1, LANES)), qk_spec, qk_spec, v_spec, v_spec, st_spec,
                  _full((1, RET_V_W))],
        out_specs=[v_spec, st_spec],
        out_shape=[jax.ShapeDtypeStruct((b, t, RET_V_W), BF16),
                   jax.ShapeDtypeStruct((b, RET_HEADS, RET_DK, RET_DV), F32)],
        scratch_shapes=[pltpu.VMEM((nbs * RET_HEADS, 2 * RET_DK, RET_DV), F32), pltpu.VMEM((RET_HEADS, c, c), F32)],
        compiler_params=_cparams("arbitrary", "arbitrary"),
        name="retention",
    )(lg_tab, rq, rk, rv, rg, s0, gn)


def _out0_kernel(x_ref, of_ref, or_ref, w_ref, *rest):
    y_ref = rest[-4] if len(rest) > 1 else rest[0]
    mix = _dot(of_ref[...], w_ref[0:FOX_W, :]) + _dot(or_ref[...], w_ref[FOX_W:FOX_W + RET_V_W, :])
    y = x_ref[...] + mix
    y_ref[...] = y
    if len(rest) > 1:
        g_ref, wth_ref, wtl_ref, bt_ref, _, h_ref, route_ref, wcol_ref = rest
        _route_tokens(y, g_ref, wth_ref, wtl_ref, bt_ref, h_ref, route_ref, wcol_ref)


def _out0(x, o_fox, o_ret, w, tm, router=None):
    n = x.shape[0]
    row = lambda width: pl.BlockSpec((tm, width), lambda i: (i, 0))
    in_specs = [row(D_MODEL), row(FOX_W), row(RET_V_W), _full((FOX_W + RET_V_W, D_MODEL))]
    out_specs = [row(D_MODEL)]
    out_shape = [jax.ShapeDtypeStruct((n, D_MODEL), F32)]
    args = [x, o_fox, o_ret, w]
    if router is not None:
        in_specs += [_full((1, D_MODEL)), _full((ROUTE_ROWS, D_MODEL)), _full((ROUTE_ROWS, D_MODEL)),
                     _full((ROUTE_ROWS, 1))]
        out_specs += [pl.BlockSpec((tm * SLAB, LANES), lambda i: (i, 0)), pl.BlockSpec((8, tm), lambda i: (0, i)),
                      row(LANES)]
        out_shape += [jax.ShapeDtypeStruct((n * SLAB, LANES), F32), jax.ShapeDtypeStruct((8, n), F32),
                      jax.ShapeDtypeStruct((n, LANES), F32)]
        args += list(router)
    return pl.pallas_call(
        _out0_kernel,
        grid=(n // tm,),
        in_specs=in_specs,
        out_specs=out_specs,
        out_shape=out_shape,
        compiler_params=_cparams("parallel"),
        name="out_proj0",
    )(*args)


def _router_kernel(x_ref, g_ref, wh_ref, wl_ref, b_ref, h_ref, gates_ref):
    hf = _rms(x_ref[...], g_ref[...])
    hh = hf.astype(BF16)
    hl = (hf - hh.astype(F32)).astype(BF16)
    h_ref[...] = hh
    wh = wh_ref[...]
    logits = _dot(hh, wh) + (_dot(hl, wh) + _dot(hh, wl_ref[...])) + b_ref[...]

    lane = lax.broadcasted_iota(jnp.int32, (1, LANES), 1)
    lanef = lane.astype(F32)
    ninf = -jnp.inf
    is_grp = lane < N_GROUPS
    gl = jnp.where(is_grp, logits, ninf)
    gmax = jnp.max(gl, axis=-1, keepdims=True)
    grp = jnp.min(jnp.where(gl == gmax, lanef, 1e9), axis=-1, keepdims=True)
    p_grp = 1.0 / jnp.sum(jnp.exp(gl - gmax), axis=-1, keepdims=True)

    is_exp = (lane >= ROUTER_LANE0) & (lane < ROUTER_LANE0 + N_EXPERTS)
    lane_grp = ((lane - ROUTER_LANE0) // EXP_PER_GROUP).astype(F32)
    em = jnp.where(is_exp & (lane_grp == grp), logits, ninf)
    v1 = jnp.max(em, axis=-1, keepdims=True)
    i1 = jnp.min(jnp.where(em == v1, lanef, 1e9), axis=-1, keepdims=True)
    em2 = jnp.where(lanef == i1, ninf, em)
    v2 = jnp.max(em2, axis=-1, keepdims=True)
    i2 = jnp.min(jnp.where(em2 == v2, lanef, 1e9), axis=-1, keepdims=True)
    t = jnp.exp(v2 - v1)
    w1 = (1.0 / (1.0 + t)) * p_grp
    w2 = (t / (1.0 + t)) * p_grp
    gates_ref[...] = jnp.where(lanef == i1, w1, 0.0) + jnp.where(lanef == i2, w2, 0.0)


def _router(x, g, wh, wl, b, tm):
    n = x.shape[0]
    row = lambda width: pl.BlockSpec((tm, width), lambda i: (i, 0))
    return pl.pallas_call(
        _router_kernel,
        grid=(n // tm,),
        in_specs=[row(D_MODEL), _full((1, D_MODEL)), _full((D_MODEL, LANES)), _full((D_MODEL, LANES)),
                  _full((1, LANES))],
        out_specs=[row(D_MODEL), row(LANES)],
        out_shape=[jax.ShapeDtypeStruct((n, D_MODEL), BF16), jax.ShapeDtypeStruct((n, LANES), F32)],
        compiler_params=_cparams("parallel"),
        name="router",
    )(x, g, wh, wl, b)


def _moe_kernel(h_ref, gates_ref, x_ref, wg_ref, wu_ref, wd_ref, y_ref):
    e = pl.program_id(1)

    @pl.when(e == 0)
    def _():
        y_ref[...] = jnp.zeros_like(y_ref)

    h = h_ref[...]
    a = _dot(h, wg_ref[0].astype(BF16))
    u = _dot(h, wu_ref[0].astype(BF16))
    lane = lax.broadcasted_iota(jnp.int32, (1, LANES), 1)
    gate = jnp.sum(jnp.where(lane == e + ROUTER_LANE0, gates_ref[...], 0.0), axis=-1, keepdims=True)
    hid = (a * _sigmoid(a)) * u * gate
    y_ref[...] += _dot(hid.astype(BF16), wd_ref[0].astype(BF16))

    @pl.when(e == pl.num_programs(1) - 1)
    def _():
        y_ref[...] += x_ref[...]


def _moe(h, gates, x, wg, wu, wd, layer, tm):
    n = x.shape[0]
    first = layer * N_EXPERTS
    row = lambda width: pl.BlockSpec((tm, width), lambda i, e: (i, 0))
    return pl.pallas_call(
        _moe_kernel,
        grid=(n // tm, N_EXPERTS),
        in_specs=[row(D_MODEL), row(LANES), row(D_MODEL),
                  pl.BlockSpec((1, D_MODEL, D_EXPERT), lambda i, e: (first + e, 0, 0)),
                  pl.BlockSpec((1, D_MODEL, D_EXPERT), lambda i, e: (first + e, 0, 0)),
                  pl.BlockSpec((1, D_EXPERT, D_MODEL), lambda i, e: (first + e, 0, 0))],
        out_specs=row(D_MODEL),
        out_shape=jax.ShapeDtypeStruct((n, D_MODEL), F32),
        compiler_params=_cparams("parallel", "arbitrary"),
        name="moe_experts",
    )(h, gates, x, wg, wu, wd)


def _router_t_kernel(x_ref, g_ref, wth_ref, wtl_ref, bt_ref, h_ref, route_ref, wcol_ref):
    _route_tokens(x_ref[...], g_ref, wth_ref, wtl_ref, bt_ref, h_ref, route_ref, wcol_ref)


def _route_tokens(x, g_ref, wth_ref, wtl_ref, bt_ref, h_ref, route_ref, wcol_ref):
    hf = _rms(x, g_ref[...])
    _to_slabs(h_ref, hf)
    hh = hf.astype(BF16)
    hl = (hf - hh.astype(F32)).astype(BF16)
    wth = wth_ref[...]
    logits = _dot_nt(wth, hh) + (_dot_nt(wth, hl) + _dot_nt(wtl_ref[...], hh)) + bt_ref[...]

    row = lax.broadcasted_iota(jnp.int32, (ROUTE_ROWS, 1), 0)
    rowf = row.astype(F32)
    ninf = -jnp.inf
    is_grp = (row >= N_EXPERTS) & (row < N_EXPERTS + N_GROUPS)
    gl = jnp.where(is_grp, logits, ninf)
    gmax = jnp.max(gl, axis=0, keepdims=True)
    grp = jnp.min(jnp.where(gl == gmax, rowf, 1e9), axis=0, keepdims=True) - float(N_EXPERTS)
    p_grp = 1.0 / jnp.sum(jnp.exp(gl - gmax), axis=0, keepdims=True)

    row_grp = (row // EXP_PER_GROUP).astype(F32)
    em = jnp.where((row < N_EXPERTS) & (row_grp == grp), logits, ninf)
    v1 = jnp.max(em, axis=0, keepdims=True)
    i1 = jnp.min(jnp.where(em == v1, rowf, 1e9), axis=0, keepdims=True)
    em2 = jnp.where(rowf == i1, ninf, em)
    v2 = jnp.max(em2, axis=0, keepdims=True)
    i2 = jnp.min(jnp.where(em2 == v2, rowf, 1e9), axis=0, keepdims=True)
    t = jnp.exp(v2 - v1)
    w1 = (1.0 / (1.0 + t)) * p_grp
    w2 = (t / (1.0 + t)) * p_grp

    tm = logits.shape[1]
    r128 = lax.broadcasted_iota(jnp.int32, (LANES, 1), 0)
    rt = (jnp.where(r128 == 0, i1, 0.0) + jnp.where(r128 == 1, i2, 0.0)
          + jnp.where(r128 == 2, w1, 0.0) + jnp.where(r128 == 3, w2, 0.0))
    route_ref[...] = rt[0:8, :]
    wcol_ref[...] = rt.T


def _router_t(x, g, wth, wtl, bt, tm):
    n = x.shape[0]
    row = lambda width: pl.BlockSpec((tm, width), lambda i: (i, 0))
    return pl.pallas_call(
        _router_t_kernel,
        grid=(n // tm,),
        in_specs=[row(D_MODEL), _full((1, D_MODEL)), _full((ROUTE_ROWS, D_MODEL)),
                  _full((ROUTE_ROWS, D_MODEL)), _full((ROUTE_ROWS, 1))],
        out_specs=[pl.BlockSpec((tm * SLAB, LANES), lambda i: (i, 0)), pl.BlockSpec((8, tm), lambda i: (0, i)),
                   row(LANES)],
        out_shape=[jax.ShapeDtypeStruct((n * SLAB, LANES), F32), jax.ShapeDtypeStruct((8, n), F32),
                   jax.ShapeDtypeStruct((n, LANES), F32)],
        compiler_params=_cparams("parallel"),
        name="router_t",
    )(x, g, wth, wtl, bt)


def _plan_kernel(route_ref, pos_ref, te_ref, nt_ref, pad_ref, *, n, tb, tm):
    nb = n // tb
    erow = lax.broadcasted_iota(jnp.int32, (N_EXPERTS, 1), 0).astype(F32)

    def picks(blk):
        i1 = route_ref[0:1, blk * tb:(blk + 1) * tb]
        i2 = route_ref[1:2, blk * tb:(blk + 1) * tb]
        return erow == i1, erow == i2

    acc = jnp.zeros((N_EXPERTS, tb), F32)
    for blk in range(nb):
        e1, e2 = picks(blk)
        acc = acc + jnp.where(e1, 1.0, 0.0) + jnp.where(e2, 1.0, 0.0)
    counts = jnp.sum(acc, axis=1, keepdims=True).astype(jnp.int32)
    assert tm & (tm - 1) == 0
    ntile = jnp.right_shift(counts + (tm - 1), tm.bit_length() - 1).astype(F32)
    lane = lax.broadcasted_iota(jnp.int32, (1, LANES), 1).astype(F32)
    ntile_row = jnp.sum(jnp.where(lane == erow, ntile, 0.0), axis=0, keepdims=True)
    tend = jnp.sum(jnp.where(lane <= erow, ntile_row, 0.0), axis=1, keepdims=True)
    off = (tend - ntile) * float(tm)
    ntot = jnp.sum(ntile, axis=0, keepdims=True)
    kk = jnp.minimum(lax.broadcasted_iota(jnp.int32, (1, 2 * LANES), 1).astype(F32), ntot - 1.0)
    te_ref[...] = jnp.sum(jnp.where(tend <= kk, 1.0, 0.0), axis=0, keepdims=True).astype(jnp.int32)
    nt_ref[...] = jnp.broadcast_to(ntot, (1, LANES)).astype(jnp.int32)
    pad0 = off + counts.astype(F32)
    pad_ref[...] = jnp.sum(jnp.where(lane == erow, pad0, 0.0), axis=0, keepdims=True).astype(jnp.int32)

    r = lax.broadcasted_iota(jnp.int32, (tb, tb), 0)
    c = lax.broadcasted_iota(jnp.int32, (tb, tb), 1)
    before = jnp.where(r < c, 1.0, 0.0).astype(BF16)
    carry = jnp.zeros((N_EXPERTS, 1), F32)
    for blk in range(nb):
        e1, e2 = picks(blk)
        mt = jnp.where(e1, 1.0, 0.0) + jnp.where(e2, 1.0, 0.0)
        slot = _dot(mt.astype(BF16), before) + (carry + off)
        pos_ref[blk, 0:1, :] = jnp.sum(jnp.where(e1, slot, 0.0), axis=0, keepdims=True).astype(jnp.int32)
        pos_ref[blk, 1:2, :] = jnp.sum(jnp.where(e2, slot, 0.0), axis=0, keepdims=True).astype(jnp.int32)
        carry = carry + jnp.sum(mt, axis=1, keepdims=True)


def _plan(route, tb, tm):
    n = route.shape[1]
    vm = pl.BlockSpec(memory_space=pltpu.VMEM)
    return pl.pallas_call(
        functools.partial(_plan_kernel, n=n, tb=tb, tm=tm),
        in_specs=[vm],
        out_specs=[vm, vm, vm, vm],
        out_shape=[jax.ShapeDtypeStruct((n // tb, 2, tb), jnp.int32),
                   jax.ShapeDtypeStruct((1, 2 * LANES), jnp.int32),
                   jax.ShapeDtypeStruct((1, LANES), jnp.int32),
                   jax.ShapeDtypeStruct((1, LANES), jnp.int32)],
        compiler_params=pltpu.CompilerParams(vmem_limit_bytes=VMEM_LIMIT),
        name="moe_plan",
    )(route)


SLAB = D_MODEL // LANES


def _to_slabs(ref, y):
    m = y.shape[0]
    for s_ in range(SLAB):
        ref[pl.ds(s_, m, stride=SLAB), :] = y[:, s_ * LANES:(s_ + 1) * LANES]


def _slab_piece(ref, s_, m):
    return ref[pl.ds(s_, m, stride=SLAB), :]


def _from_slabs(ref, m):
    return jnp.concatenate([_slab_piece(ref, s_, m) for s_ in range(SLAB)], axis=1)


def _tok_copy(src, src_tok, dst, dst_tok, sem):
    return pltpu.make_async_copy(src.at[pl.ds(pl.multiple_of(src_tok * SLAB, SLAB), SLAB)],
                                 dst.at[pl.ds(pl.multiple_of(dst_tok * SLAB, SLAB), SLAB)], sem)


DMA_UNROLL = 8


def _dispatch_kernel(pos_ref, pad_ref, nt_ref, h_ref, xs_hbm, zero_scr, sem, zsem, *, ts, tm, ntmax):
    @pl.when(pl.program_id(0) == 0)
    def _():
        zero_scr[...] = jnp.zeros_like(zero_scr)

        def fill(first_slot):
            return pltpu.make_async_copy(
                zero_scr, xs_hbm.at[pl.ds(pl.multiple_of(first_slot * SLAB, SLAB), tm * SLAB)], zsem)

        for e in range(N_EXPERTS):
            fill(pad_ref[0, e]).start()
        for e in range(N_EXPERTS):
            fill(pad_ref[0, e]).wait()

        ntot = nt_ref[0, 0]

        def tail_start(k, carry):
            fill(k * tm).start()
            return carry

        def tail_wait(k, carry):
            fill(k * tm).wait()
            return carry

        lax.fori_loop(ntot, ntmax, tail_start, 0)
        lax.fori_loop(ntot, ntmax, tail_wait, 0)

    def issue(g, carry):
        for u in range(DMA_UNROLL):
            r = g * DMA_UNROLL + u
            for ch in range(2):
                _tok_copy(h_ref, r, xs_hbm, pos_ref[0, ch, r], sem).start(priority=ch)
        return carry

    def drain(g, carry):
        for _ in range(2 * DMA_UNROLL):
            _tok_copy(h_ref, 0, xs_hbm, 0, sem).wait()
        return carry

    lax.fori_loop(0, ts // DMA_UNROLL, issue, 0)
    lax.fori_loop(0, ts // DMA_UNROLL, drain, 0)


def _dispatch(pos, pad, nt, h, ts, tm):
    nb = pos.shape[0]
    n = h.shape[0] // SLAB
    ntmax = _sorted_rows(n) // tm + 1
    smem = pl.BlockSpec(memory_space=pltpu.SMEM)
    return pl.pallas_call(
        functools.partial(_dispatch_kernel, ts=ts, tm=tm, ntmax=ntmax),
        grid=(nb,),
        in_specs=[pl.BlockSpec((1, 2, ts), lambda i: (i, 0, 0), memory_space=pltpu.SMEM), smem, smem,
                  pl.BlockSpec((ts * SLAB, LANES), lambda i: (i, 0))],
        out_specs=pl.BlockSpec(memory_space=pl.ANY),
        out_shape=jax.ShapeDtypeStruct((ntmax * tm * SLAB, LANES), F32),
        scratch_shapes=[pltpu.VMEM((tm * SLAB, LANES), F32), pltpu.SemaphoreType.DMA,
                        pltpu.SemaphoreType.DMA],
        compiler_params=_cparams("arbitrary"),
        name="moe_dispatch",
    )(pos, pad, nt, h)


def _experts_kernel(te_ref, nt_ref, xs_ref, wg_ref, wu_ref, wd_ref, ys_ref, *, tm):
    @pl.when(pl.program_id(0) < nt_ref[0])
    def _():
        x = _from_slabs(xs_ref, tm).astype(BF16)
        a = _dot(x, wg_ref[0].astype(BF16))
        u = _dot(x, wu_ref[0].astype(BF16))
        hid = (a * _sigmoid(a)) * u
        _to_slabs(ys_ref, _dot(hid.astype(BF16), wd_ref[0].astype(BF16)))


def _experts(te, nt, xs, wg, wu, wd, layer, tm):
    ntmax = xs.shape[0] // (tm * SLAB)
    first = layer * N_EXPERTS
    tile = pl.BlockSpec((tm * SLAB, LANES), lambda k, te, nt: (jnp.minimum(k, nt[0] - 1), 0))
    out_tile = tile
    return pl.pallas_call(
        functools.partial(_experts_kernel, tm=tm),
        grid_spec=pltpu.PrefetchScalarGridSpec(
            num_scalar_prefetch=2,
            grid=(ntmax,),
            in_specs=[tile,
                      pl.BlockSpec((1, D_MODEL, D_EXPERT), lambda k, te, nt: (first + te[k], 0, 0)),
                      pl.BlockSpec((1, D_MODEL, D_EXPERT), lambda k, te, nt: (first + te[k], 0, 0)),
                      pl.BlockSpec((1, D_EXPERT, D_MODEL), lambda k, te, nt: (first + te[k], 0, 0))],
            out_specs=out_tile),
        out_shape=jax.ShapeDtypeStruct(xs.shape, F32),
        input_output_aliases={2: 0},
        compiler_params=_cparams("arbitrary"),
        name="moe_experts_sorted",
    )(te, nt, xs, wg, wu, wd)


def _combine_kernel(pos_ref, posn_ref, x_ref, w_ref, ys_hbm, y_ref, buf, sems, *, tc):
    i = pl.program_id(0)
    nb = pl.num_programs(0)

    def start(p_ref, slot):
        def body(g, carry):
            for u in range(DMA_UNROLL):
                r = g * DMA_UNROLL + u
                for ch in range(2):
                    _tok_copy(ys_hbm, p_ref[0, ch, r], buf.at[slot, ch], r, sems.at[slot]).start(priority=ch)
            return carry
        lax.fori_loop(0, tc // DMA_UNROLL, body, 0)

    def wait(slot):
        def body(g, carry):
            for _ in range(2 * DMA_UNROLL):
                _tok_copy(ys_hbm, 0, buf.at[slot, 0], 0, sems.at[slot]).wait()
            return carry
        lax.fori_loop(0, tc // DMA_UNROLL, body, 0)

    @pl.when(i == 0)
    def _():
        start(pos_ref, 0)

    w = w_ref[...]
    w1, w2 = w[:, 2:3], w[:, 3:4]
    for slot in range(2):
        @pl.when(i % 2 == slot)
        def _(slot=slot):
            @pl.when(i + 1 < nb)
            def _():
                start(posn_ref, 1 - slot)

            wait(slot)
            for s_ in range(SLAB):
                cols = slice(s_ * LANES, (s_ + 1) * LANES)
                y_ref[:, cols] = x_ref[:, cols] + (w1 * _slab_piece(buf.at[slot, 0], s_, tc)
                                                   + w2 * _slab_piece(buf.at[slot, 1], s_, tc))


def _combine(pos, x, wcol, ys, tc):
    n = x.shape[0]
    nb = n // tc
    row = lambda width: pl.BlockSpec((tc, width), lambda i: (i, 0))
    return pl.pallas_call(
        functools.partial(_combine_kernel, tc=tc),
        grid=(nb,),
        in_specs=[pl.BlockSpec((1, 2, tc), lambda i: (i, 0, 0), memory_space=pltpu.SMEM),
                  pl.BlockSpec((1, 2, tc), lambda i: (jnp.minimum(i + 1, nb - 1), 0, 0), memory_space=pltpu.SMEM),
                  row(D_MODEL), row(LANES), pl.BlockSpec(memory_space=pl.ANY)],
        out_specs=row(D_MODEL),
        out_shape=jax.ShapeDtypeStruct((n, D_MODEL), F32),
        scratch_shapes=[pltpu.VMEM((2, 2, tc * SLAB, LANES), F32), pltpu.SemaphoreType.DMA((2,))],
        compiler_params=_cparams("arbitrary"),
        name="moe_combine",
    )(pos, pos, x, wcol, ys)


def _pw1_kernel(x_ref, g_ref, w_ref, b_ref, u_ref):
    h = _rms(x_ref[...], g_ref[...]).astype(BF16)
    a = _dot(h, w_ref[:, 0:D_MODEL]) + b_ref[:, 0:D_MODEL]
    g = _dot(h, w_ref[:, D_MODEL:2 * D_MODEL]) + b_ref[:, D_MODEL:2 * D_MODEL]
    u_ref[...] = a * _sigmoid(g)


def _pw1(x, g, w, b, tm):
    n = x.shape[0]
    row = lambda width: pl.BlockSpec((tm, width), lambda i: (i, 0))
    return pl.pallas_call(
        _pw1_kernel,
        grid=(n // tm,),
        in_specs=[row(D_MODEL), _full((1, D_MODEL)), _full((D_MODEL, 2 * D_MODEL)), _full((1, 2 * D_MODEL))],
        out_specs=row(D_MODEL),
        out_shape=jax.ShapeDtypeStruct((n, D_MODEL), F32),
        compiler_params=_cparams("parallel"),
        name="conv_pw1_glu",
    )(x, g, w, b)


def _conv_kernel(u_ref, hist_ref, x_ref, wdw_ref, bdw_ref, lng_ref, lnb_ref, w2_ref, y_ref, ext_scr, sh_scr,
                 *, tt, rs):
    ti = pl.program_id(1)

    @pl.when(ti == 0)
    def _():
        ext_scr[0:HIST_ROWS, :] = hist_ref[0]

    ext_scr[HIST_ROWS:HIST_ROWS + tt, :] = u_ref[0]
    pad = HIST_ROWS - (CONV_W - 1)
    rows = tt + HIST_ROWS - 8
    for r in range(1, 8):
        sh_scr[r - 1, :, :] = ext_scr[r:r + rows, :]
    parts = []
    for r0 in range(0, tt, rs):
        acc = jnp.zeros((rs, D_MODEL), F32)
        for kk in range(CONV_W):
            a, r = divmod(kk + pad, 8)
            lo = r0 + 8 * a
            src = ext_scr[lo:lo + rs, :] if r == 0 else sh_scr[r - 1, lo:lo + rs, :]
            acc = acc + wdw_ref[kk:kk + 1, :] * src
        parts.append(acc)
    y = jnp.concatenate(parts, axis=0) + bdw_ref[...]
    mu = jnp.mean(y, axis=-1, keepdims=True)
    yc = y - mu
    var = jnp.mean(yc * yc, axis=-1, keepdims=True)
    yn = yc * lax.rsqrt(var + EPS) * lng_ref[...] + lnb_ref[...]
    z = yn * _sigmoid(yn)
    y_ref[0] = x_ref[0] + _dot(z.astype(BF16), w2_ref[...])
    if tt >= HIST_ROWS:
        ext_scr[0:HIST_ROWS, :] = ext_scr[tt:tt + HIST_ROWS, :]


def _conv_module(u, hist, x, wdw, bdw, lng, lnb, w2, tt):
    b, t, _ = u.shape
    rs = min(tt, 32)
    blk = pl.BlockSpec((1, tt, D_MODEL), lambda bi, ti: (bi, ti, 0))
    return pl.pallas_call(
        functools.partial(_conv_kernel, tt=tt, rs=rs),
        grid=(b, t // tt),
        in_specs=[blk, pl.BlockSpec((1, HIST_ROWS, D_MODEL), lambda bi, ti: (bi, 0, 0)), blk,
                  _full((HIST_ROWS, D_MODEL)), _full((1, D_MODEL)), _full((1, D_MODEL)),
                  _full((1, D_MODEL)), _full((D_MODEL, D_MODEL))],
        out_specs=blk,
        out_shape=jax.ShapeDtypeStruct((b, t, D_MODEL), F32),
        scratch_shapes=[pltpu.VMEM((HIST_ROWS + tt, D_MODEL), F32),
                        pltpu.VMEM((7, tt + HIST_ROWS - 8, D_MODEL), F32)],
        compiler_params=_cparams("parallel", "arbitrary"),
        name="conv_module",
    )(u, hist, x, wdw, bdw, lng, lnb, w2)


def _rope_tables(pos):
    half = RET_DK // 2
    inv_freq = ROPE_BASE ** (-jnp.arange(half, dtype=F32) / half)
    ang = pos.astype(F32)[:, None] * inv_freq[None, :]
    cos, sin = jnp.cos(ang), jnp.sin(ang)
    reps = LANES // RET_DK
    cos_t = jnp.tile(jnp.concatenate([cos, cos], axis=1), (1, reps))
    sin_t = jnp.tile(jnp.concatenate([-sin, sin], axis=1), (1, reps))
    return cos_t, sin_t


def _pad_lanes(v, width=LANES):
    v = v.reshape(1, -1)
    return jnp.pad(v, ((0, 0), (0, width - v.shape[1])))


def _moe_dense(x, p, l, tm):
    h, gates = _router(x, p["norm_ffn"][l], p["wr_hi"][l], p["wr_lo"][l], p["br"][l], tm)
    return _moe(h, gates, x, p["wg"], p["wu"], p["wd"], l, tm)


def _sorted_rows(n_tokens):
    return (2 * n_tokens // MOE_TM + N_EXPERTS) * MOE_TM


def _router_args(p, l):
    return p["norm_ffn"][l], p["wt_hi"][l], p["wt_lo"][l], p["bt"][l]


def _moe_sparse(x, p, l, tm, routed=None):
    h, route, wcol = routed if routed is not None else _router_t(x, *_router_args(p, l), tm)
    pos, te, nt, pad = _plan(route, PLAN_TB, MOE_TM)
    xs = _dispatch(pos, pad, nt, h, PLAN_TB, MOE_TM)
    ys = _experts(te.reshape(-1), nt[0, :1], xs, p["wg"], p["wu"], p["wd"], l, MOE_TM)
    return _combine(pos, x, wcol, ys, PLAN_TB)


def _run_group(x, p, *, sparse, seq, tm, fox_hist, ret_state, conv_hist, pos, tq, tk, ret_c, conv_tt):
    b = x.shape[0] // seq
    q_off = 0 if fox_hist is None else fox_hist[0].shape[1]

    cos_t, sin_t = _rope_tables(pos)
    q, k, v, lf, rq, rk, rv, rg, k5, v5 = _proj0(x, p["norm_mix"][0], p["w_in"], p["b_f"], p["q_gain"], p["k_gain"],
                                          p["gbd"], cos_t, sin_t, tm)
    k3 = k.reshape(b, seq, FOX_W)
    v3 = v.reshape(b, seq, FOX_W)
    lf3 = lf.reshape(b, seq, LANES)
    if fox_hist is None:
        k_all, v_all, lf_all = k3, v3, lf3
    else:
        ck_, cv_, clf_ = fox_hist
        tot = q_off + seq
        padded = -(-tot // tk) * tk
        tail = padded - tot
        k_all = jnp.concatenate([ck_, k3, jnp.zeros((b, tail, FOX_W), F32)], axis=1)
        v_all = jnp.concatenate([cv_, v3, jnp.zeros((b, tail, FOX_W), F32)], axis=1)
        clf_ = jnp.pad(clf_, ((0, 0), (0, 0), (0, LANES - FOX_HEADS)))
        lf_all = jnp.concatenate([clf_, lf3, jnp.zeros((b, tail, LANES), F32)], axis=1)
    ccol, crow = _cumsum_logf(lf_all, CUM_BLOCK)
    if fox_hist is None:
        qa, ka, vt = _fox_prep(q.reshape(b, seq, FOX_W), k_all, v_all, ccol, tk)
        o_fox = _fox_attention_t(qa, ka, vt, tq)
    else:
        o_fox = _fox_attention(q.reshape(b, seq, FOX_W), k_all, v_all, ccol, crow, tq, tk, q_off)
    o_ret, s_fin = _retention(p["lg_tab"], rq.reshape(b, seq, RET_QK_W), rk.reshape(b, seq, RET_QK_W),
                              rv.reshape(b, seq, RET_V_W), rg.reshape(b, seq, RET_V_W), ret_state,
                              p["gn_gain"], ret_c)
    o_fox2, o_ret2 = o_fox.reshape(-1, FOX_W), o_ret.reshape(-1, RET_V_W)
    if sparse:
        x, *routed = _out0(x, o_fox2, o_ret2, p["w_out"], tm, _router_args(p, 0))
        x = _moe_sparse(x, p, 0, tm, routed)
    else:
        x = _moe_dense(_out0(x, o_fox2, o_ret2, p["w_out"], tm)[0], p, 0, tm)

    u = _pw1(x, p["norm_mix"][1], p["w_pw1"], p["b_pw1"], tm)
    u3 = u.reshape(b, seq, D_MODEL)
    x = _conv_module(u3, conv_hist, x.reshape(b, seq, D_MODEL), p["w_dw"], p["b_dw"], p["ln_g"], p["ln_b"],
                     p["w_pw2"], conv_tt).reshape(-1, D_MODEL)
    x = _moe_sparse(x, p, 1, tm) if sparse else _moe_dense(x, p, 1, tm)

    fox_k = k5.reshape(1, b, seq, FOX_HEADS, FOX_HD)
    fox_v = v5.reshape(1, b, seq, FOX_HEADS, FOX_HD)
    fox_lf = lf3[:, :, :FOX_HEADS].reshape(1, b, seq, FOX_HEADS)
    return x.reshape(b, seq, D_MODEL), fox_k, fox_v, fox_lf, s_fin[None], u3


def kernel(x_prompt, x_sample, cache_fox_k, cache_fox_v, cache_fox_logf, state_ret, cache_conv, norm_mix, norm_ffn, w_in_mix, b_forget, fox_q_gain, fox_k_gain, ret_gn_gain, w_out_mix, w_pw1, b_pw1, w_dw, b_dw, conv_ln_g, conv_ln_b, w_pw2, w_router_group, b_router_group, w_router_expert, b_router_expert, w_exp_gate, w_exp_up, w_exp_down):
    bp, t, d = x_prompt.shape
    bs, l, _ = x_sample.shape
    past = cache_fox_k.shape[2]
    depth = norm_mix.shape[0]
    assert d == D_MODEL and depth == 2 and w_in_mix.shape[0] == 1 and w_pw1.shape[0] == 1

    w_in = w_in_mix[0]
    n_pre = 3 * FOX_W
    w_in_r = jnp.concatenate(
        [w_in[:, :n_pre], w_in[:, n_pre + FOX_HEADS:], w_in[:, n_pre:n_pre + FOX_HEADS],
         jnp.zeros((D_MODEL, LANES - FOX_HEADS), F32)], axis=1).astype(BF16)
    hid = jnp.arange(FOX_W) // FOX_HD
    gbd = jnp.where(hid[:, None] == hid[None, :], 1.0 / FOX_HD, 0.0).astype(BF16)
    w_r = jnp.concatenate([w_router_group, w_router_expert], axis=-1)
    w_r = jnp.pad(w_r, ((0, 0), (0, 0), (0, LANES - w_r.shape[-1])))
    wr_hi = w_r.astype(BF16)
    wr_lo = (w_r - wr_hi.astype(F32)).astype(BF16)
    b_r = jnp.concatenate([b_router_group, b_router_expert], axis=-1)
    b_r = jnp.pad(b_r, ((0, 0), (0, LANES - b_r.shape[-1])))
    w_t = jnp.swapaxes(jnp.concatenate([w_router_expert, w_router_group], axis=-1), 1, 2)
    w_t = jnp.pad(w_t, ((0, 0), (0, ROUTE_ROWS - w_t.shape[1]), (0, 0)))
    wt_hi = w_t.astype(BF16)
    wt_lo = (w_t - wt_hi.astype(F32)).astype(BF16)
    b_t = jnp.concatenate([b_router_expert, b_router_group], axis=-1)
    b_t = jnp.pad(b_t, ((0, 0), (0, ROUTE_ROWS - b_t.shape[-1])))[:, :, None]
    log_gamma = jnp.log(1.0 - 2.0 ** (-5.0 - jnp.arange(RET_HEADS, dtype=F32)))
    p = {
        "norm_mix": norm_mix.reshape(depth, 1, D_MODEL),
        "norm_ffn": norm_ffn.reshape(depth, 1, D_MODEL),
        "w_in": w_in_r,
        "b_f": _pad_lanes(b_forget[0]),
        "q_gain": jnp.tile(fox_q_gain[0], FOX_HEADS).reshape(1, FOX_W),
        "k_gain": jnp.tile(fox_k_gain[0], FOX_HEADS).reshape(1, FOX_W),
        "gbd": gbd,
        "gn_gain": ret_gn_gain[0].reshape(1, RET_V_W),
        "lg_tab": jnp.broadcast_to(log_gamma[:, None, None], (RET_HEADS, 1, LANES)),
        "w_out": w_out_mix[0].astype(BF16),
        "w_pw1": w_pw1[0].astype(BF16),
        "b_pw1": b_pw1[0].reshape(1, -1),
        "w_dw": jnp.pad(w_dw[0], ((0, HIST_ROWS - CONV_W), (0, 0))),
        "b_dw": b_dw[0].reshape(1, -1),
        "ln_g": conv_ln_g[0].reshape(1, -1),
        "ln_b": conv_ln_b[0].reshape(1, -1),
        "w_pw2": w_pw2[0].astype(BF16),
        "wr_hi": wr_hi,
        "wr_lo": wr_lo,
        "br": b_r.reshape(depth, 1, LANES),
        "wt_hi": wt_hi,
        "wt_lo": wt_lo,
        "bt": b_t,
        "wg": w_exp_gate.reshape(depth * N_EXPERTS, D_MODEL, D_EXPERT),
        "wu": w_exp_up.reshape(depth * N_EXPERTS, D_MODEL, D_EXPERT),
        "wd": w_exp_down.reshape(depth * N_EXPERTS, D_EXPERT, D_MODEL),
    }

    hist_pad = HIST_ROWS - (CONV_W - 1)
    yp, fk_p, fv_p, lf_p, rs_p, u_p = _run_group(
        x_prompt.reshape(bp * t, D_MODEL), p, sparse=True, seq=t, tm=512, fox_hist=None,
        ret_state=jnp.zeros((bp, RET_HEADS, RET_DK, RET_DV), F32),
        conv_hist=jnp.zeros((bp, HIST_ROWS, D_MODEL), F32),
        pos=jnp.arange(t), tq=256, tk=256, ret_c=256, conv_tt=256)

    ns = bs * l
    ys, fk_s, fv_s, lf_s, rs_s, u_s = _run_group(
        x_sample.reshape(ns, D_MODEL), p, sparse=False, seq=l, tm=ns,
        fox_hist=(cache_fox_k[0].reshape(bs, past, FOX_W), cache_fox_v[0].reshape(bs, past, FOX_W),
                  cache_fox_logf[0]),
        ret_state=state_ret[0],
        conv_hist=jnp.pad(cache_conv[0], ((0, 0), (hist_pad, 0), (0, 0))),
        pos=past + (jnp.arange(ns) % l), tq=l, tk=-(-(past + l) // CUM_BLOCK) * CUM_BLOCK, ret_c=l, conv_tt=l)

    conv_p = u_p[:, t - (CONV_W - 1):][None]
    conv_s = jnp.concatenate([cache_conv[0], u_s], axis=1)[:, l:][None]
    return (yp, ys, fk_p, fv_p, lf_p, rs_p, conv_p, fk_s, fv_s, lf_s, rs_s, conv_s)
```

```python
import functools
import math

import jax
import jax.numpy as jnp
import numpy as np
from jax import lax
from jax.experimental import pallas as pl
from jax.experimental.pallas import tpu as pltpu

F32 = jnp.float32
BF16 = jnp.bfloat16

D_MODEL = 1024
FOX_HEADS = 8
FOX_HD = 64
RET_HEADS = 4
RET_DK = 64
RET_DV = 128
ROPE_BASE = 10000.0
CONV_W = 31
N_GROUPS = 4
EXP_PER_GROUP = 8
N_EXPERTS = N_GROUPS * EXP_PER_GROUP
D_EXPERT = 256
EPS = 1e-6
FOX_W = FOX_HEADS * FOX_HD
RET_QK_W = RET_HEADS * RET_DK
RET_V_W = RET_HEADS * RET_DV

LANES = 128
HIST_ROWS = 32
ROUTER_LANE0 = N_GROUPS
MASKED = -1e30
VMEM_LIMIT = 56 * 1024 * 1024
CUM_BLOCK = 256
MOE_TM = 512
PLAN_TB = 512
ROUTE_ROWS = 48

C_FQ, C_FK, C_FV = 0, FOX_W, 2 * FOX_W
C_RQ = 3 * FOX_W
C_RK = C_RQ + RET_QK_W
C_RV = C_RK + RET_QK_W
C_RG = C_RV + RET_V_W
C_FF = C_RG + RET_V_W
MIX_COLS = C_FF + LANES


def _cparams(*sem):
    return pltpu.CompilerParams(dimension_semantics=sem, vmem_limit_bytes=VMEM_LIMIT)


def _full(shape):
    n = len(shape)
    return pl.BlockSpec(shape, lambda *_: (0,) * n)


def _rms(x, g):
    ms = jnp.mean(x * x, axis=-1, keepdims=True)
    return x * lax.rsqrt(ms + EPS) * g


def _sigmoid(x):
    return 1.0 / (1.0 + jnp.exp(-x))


def _dot(a, b):
    return jnp.dot(a, b, preferred_element_type=F32)


def _dot_nt(a, b):
    return lax.dot_general(a, b, (((1,), (1,)), ((), ())), preferred_element_type=F32)


def _dot_tn(a, b):
    return lax.dot_general(a, b, (((0,), (0,)), ((), ())), preferred_element_type=F32)


def _proj0_kernel(x_ref, g_ref, w_ref, bf_ref, qg_ref, kg_ref, gbd_ref, cos_ref, sin_ref,
                  q_ref, k_ref, v_ref, lf_ref, rq_ref, rk_ref, rv_ref, rg_ref, k5_ref, v5_ref):
    h = _rms(x_ref[...], g_ref[...]).astype(BF16)

    def seg(a, b):
        return _dot(h, w_ref[:, a:b])

    gbd = gbd_ref[...]

    def head_rms(y, gain):
        ms = _dot((y * y).astype(BF16), gbd)
        return y * lax.rsqrt(ms + EPS) * gain

    q_ref[...] = (head_rms(seg(C_FQ, C_FK), qg_ref[...]) * (FOX_HD ** -0.5)).astype(BF16)
    kk = head_rms(seg(C_FK, C_FV), kg_ref[...])
    vv = seg(C_FV, C_RQ)
    k_ref[...] = kk
    v_ref[...] = vv
    k5_ref[...] = kk.reshape(kk.shape[0], FOX_HEADS, FOX_HD)
    v5_ref[...] = vv.reshape(vv.shape[0], FOX_HEADS, FOX_HD)

    z = seg(C_FF, MIX_COLS) + bf_ref[...]
    logf = jnp.minimum(z, 0.0) - jnp.log(1.0 + jnp.exp(-jnp.abs(z)))
    lane = lax.broadcasted_iota(jnp.int32, (1, LANES), 1)
    lf_ref[...] = jnp.where(lane < FOX_HEADS, logf, 0.0)

    cos = cos_ref[...]
    sin = sin_ref[...]
    first_half = (lane % RET_DK) < (RET_DK // 2)

    def rotary(y):
        outs = []
        for s in range(y.shape[1] // LANES):
            ys = y[:, s * LANES:(s + 1) * LANES]
            rot = jnp.where(first_half, pltpu.roll(ys, LANES - RET_DK // 2, 1),
                            pltpu.roll(ys, RET_DK // 2, 1))
            outs.append(ys * cos + rot * sin)
        return jnp.concatenate(outs, axis=1)

    rq_ref[...] = rotary(seg(C_RQ, C_RK)).astype(BF16)
    rk_ref[...] = (rotary(seg(C_RK, C_RV)) * (RET_DK ** -0.5)).astype(BF16)
    rv_ref[...] = seg(C_RV, C_RG).astype(BF16)
    rg_ref[...] = seg(C_RG, C_FF).astype(BF16)


def _proj0(x, g, w, bf, qg, kg, gbd, cos, sin, tm):
    n = x.shape[0]
    nper = cos.shape[0] // tm
    row = lambda width: pl.BlockSpec((tm, width), lambda i: (i, 0))
    tab = pl.BlockSpec((tm, LANES), lambda i: (i % nper, 0))
    outs = [(FOX_W, BF16), (FOX_W, F32), (FOX_W, F32), (LANES, F32),
            (RET_QK_W, BF16), (RET_QK_W, BF16), (RET_V_W, BF16), (RET_V_W, BF16)]
    return pl.pallas_call(
        _proj0_kernel,
        grid=(n // tm,),
        in_specs=[row(D_MODEL), _full((1, D_MODEL)), _full((D_MODEL, MIX_COLS)), _full((1, LANES)),
                  _full((1, FOX_W)), _full((1, FOX_W)), _full((FOX_W, FOX_W)), tab, tab],
        out_specs=[row(wd) for wd, _ in outs] + [pl.BlockSpec((tm, FOX_HEADS, FOX_HD), lambda i: (i, 0, 0))] * 2,
        out_shape=[jax.ShapeDtypeStruct((n, wd), dt) for wd, dt in outs]
        + [jax.ShapeDtypeStruct((n, FOX_HEADS, FOX_HD), F32)] * 2,
        compiler_params=_cparams("parallel"),
        name="proj0",
    )(x, g, w, bf, qg, kg, gbd, cos, sin)


def _cum_kernel(lf_ref, ccol_ref, crow_ref, *, t, cb):
    r = lax.broadcasted_iota(jnp.int32, (cb, cb), 0)
    c = lax.broadcasted_iota(jnp.int32, (cb, cb), 1)
    tri = jnp.where(r >= c, 1.0, 0.0).astype(BF16)
    carry = jnp.zeros((1, LANES), F32)
    for blk in range(t // cb):
        a = lf_ref[0, blk * cb:(blk + 1) * cb, :]
        a1 = a.astype(BF16)
        r1 = a - a1.astype(F32)
        a2 = r1.astype(BF16)
        a3 = (r1 - a2.astype(F32)).astype(BF16)
        cc = (_dot(tri, a1) + _dot(tri, a2)) + _dot(tri, a3) + carry
        ccol_ref[0, blk * cb:(blk + 1) * cb, :] = cc
        crow_ref[0, :, blk * cb:(blk + 1) * cb] = cc.T[0:FOX_HEADS, :]
        carry = cc[cb - 1:cb, :]


def _cumsum_logf(lf, cb):
    b, t, _ = lf.shape
    return pl.pallas_call(
        functools.partial(_cum_kernel, t=t, cb=cb),
        grid=(b,),
        in_specs=[pl.BlockSpec((1, t, LANES), lambda i: (i, 0, 0))],
        out_specs=[pl.BlockSpec((1, t, LANES), lambda i: (i, 0, 0)),
                   pl.BlockSpec((1, FOX_HEADS, t), lambda i: (i, 0, 0))],
        out_shape=[jax.ShapeDtypeStruct((b, t, LANES), F32),
                   jax.ShapeDtypeStruct((b, FOX_HEADS, t), F32)],
        compiler_params=_cparams("parallel"),
        name="cumsum_logf",
    )(lf)


def _fox_kernel(q_ref, k_ref, v_ref, cq_ref, ck_ref, o_ref, *, tq, tk, q_off):
    hp = pl.program_id(1)
    i = pl.program_id(2)
    q0 = q_off + i * tq
    nfull = (q0 + 1) // tk
    nch = (q0 + tq + tk - 1) // tk
    lane = lax.broadcasted_iota(jnp.int32, (1, LANES), 1)
    q = q_ref[0]
    cq_all = cq_ref[0]
    qpos = q0 + lax.broadcasted_iota(jnp.int32, (tq, 1), 0)
    kiota = lax.broadcasted_iota(jnp.int32, (1, tk), 1)
    qms = [jnp.where((lane // FOX_HD) == h2, q, jnp.zeros_like(q)) for h2 in range(2)]
    cqs = [jnp.sum(jnp.where(lane == 2 * hp + h2, cq_all, 0.0), axis=-1, keepdims=True) for h2 in range(2)]

    def step(j, carry, masked):
        start = pl.multiple_of(j * tk, tk)
        kj = k_ref[0, pl.ds(start, tk), :].astype(BF16)
        vj = v_ref[0, pl.ds(start, tk), :].astype(BF16)
        new = []
        for h2 in range(2):
            m, l, acc = carry[h2]
            s = _dot_nt(qms[h2], kj)
            s = s + cqs[h2] - ck_ref[0, 2 * hp + h2, j]
            if masked:
                s = jnp.where(j * tk + kiota <= qpos, s, MASKED)
            m_new = jnp.maximum(m, jnp.max(s, axis=-1, keepdims=True))
            alpha = jnp.exp(m - m_new)
            p = jnp.exp(s - m_new)
            l = alpha * l + jnp.sum(p, axis=-1, keepdims=True)
            acc = alpha * acc + _dot(p.astype(BF16), vj)
            new.append((m_new, l, acc))
        return tuple(new)

    one = (jnp.full((tq, 1), MASKED, F32), jnp.zeros((tq, 1), F32), jnp.zeros((tq, LANES), F32))
    carry = lax.fori_loop(0, nfull, functools.partial(step, masked=False), (one, one))
    carry = lax.fori_loop(nfull, nch, functools.partial(step, masked=True), carry)
    outs = [acc / l for _, l, acc in carry]
    o_ref[0] = jnp.where(lane < FOX_HD, outs[0], outs[1]).astype(BF16)


def _fox_attention(q, k, v, ccol, crow, tq, tk, q_off):
    b, tqs, _ = q.shape
    tks = k.shape[1]
    nck = tks // tk
    crow5 = crow.reshape(b, FOX_HEADS, nck, 1, tk)
    qb0 = q_off // tq
    return pl.pallas_call(
        functools.partial(_fox_kernel, tq=tq, tk=tk, q_off=q_off),
        grid=(b, FOX_HEADS // 2, tqs // tq),
        in_specs=[pl.BlockSpec((1, tq, LANES), lambda bi, hp, i: (bi, i, hp)),
                  pl.BlockSpec((1, tks, LANES), lambda bi, hp, i: (bi, 0, hp)),
                  pl.BlockSpec((1, tks, LANES), lambda bi, hp, i: (bi, 0, hp)),
                  pl.BlockSpec((1, tq, LANES), lambda bi, hp, i: (bi, qb0 + i, 0)),
                  pl.BlockSpec((1, FOX_HEADS, nck, 1, tk), lambda bi, hp, i: (bi, 0, 0, 0, 0))],
        out_specs=pl.BlockSpec((1, tq, LANES), lambda bi, hp, i: (bi, i, hp)),
        out_shape=jax.ShapeDtypeStruct((b, tqs, FOX_W), BF16),
        compiler_params=_cparams("parallel", "parallel", "parallel"),
        name="fox_attention",
    )(q, k, v, ccol, crow5)


FOX_VROWS = 80


def _bf16_pieces(c):
    hi = c.astype(BF16).astype(F32)
    r = c - hi
    mid = r.astype(BF16).astype(F32)
    lo = (r - mid).astype(BF16).astype(F32)
    return hi, mid, lo


def _fox_prep_kernel(q_ref, k_ref, v_ref, c_ref, qa_ref, ka_ref, vt_ref, *, t, rc):
    hp = pl.program_id(1)
    lane = lax.broadcasted_iota(jnp.int32, (1, LANES), 1)
    src = lax.broadcasted_iota(jnp.int32, (LANES, 1), 0)
    one_lane = 3 * FOX_HEADS

    def placer(h, c_to, one_to, sign):
        m = jnp.zeros((LANES, LANES), F32)
        for piece in range(3):
            m = jnp.where((src == piece * FOX_HEADS + h) & (lane == c_to + piece), sign, m)
            m = jnp.where((src == one_lane) & (lane == one_to + piece), 1.0, m)
        return m.astype(BF16)

    place_q = [placer(2 * hp + h2, FOX_HD, FOX_HD + 3, 1.0) for h2 in range(2)]
    place_k = [placer(2 * hp + h2, FOX_HD + 3, FOX_HD, -1.0) for h2 in range(2)]
    extra = jnp.where(lax.broadcasted_iota(jnp.int32, (FOX_VROWS - FOX_HD, rc), 0) == 0, 1.0, 0.0)
    for ch in range(t // rc):
        rows = slice(ch * rc, (ch + 1) * rc)
        qf = q_ref[0, rows, :].astype(F32)
        kf = k_ref[0, rows, :]
        hi, mid, lo = _bf16_pieces(c_ref[0, rows, :])
        pieces = (hi + pltpu.roll(mid, FOX_HEADS, 1) + pltpu.roll(lo, 2 * FOX_HEADS, 1)
                  + jnp.where(lane == one_lane, 1.0, 0.0)).astype(BF16)
        vt = v_ref[0, rows, :].T
        for h2 in range(2):
            qh = qf if h2 == 0 else pltpu.roll(qf, FOX_HD, 1)
            kh = kf if h2 == 0 else pltpu.roll(kf, FOX_HD, 1)
            qa_ref[0, h2, rows, :] = jnp.where(lane < FOX_HD, qh, _dot(pieces, place_q[h2])).astype(BF16)
            ka_ref[0, h2, rows, :] = jnp.where(lane < FOX_HD, kh, _dot(pieces, place_k[h2])).astype(BF16)
            vt_ref[0, h2, ch] = jnp.concatenate([vt[h2 * FOX_HD:(h2 + 1) * FOX_HD, :], extra], axis=0).astype(BF16)


def _fox_prep(q, k, v, ccol, rc):
    b, t, _ = q.shape
    pair = pl.BlockSpec((1, t, LANES), lambda bi, hp: (bi, 0, hp))
    aug = pl.BlockSpec((1, 2, t, LANES), lambda bi, hp: (bi, hp, 0, 0))
    return pl.pallas_call(
        functools.partial(_fox_prep_kernel, t=t, rc=rc),
        grid=(b, FOX_HEADS // 2),
        in_specs=[pair, pair, pair, pl.BlockSpec((1, t, LANES), lambda bi, hp: (bi, 0, 0))],
        out_specs=[aug, aug, pl.BlockSpec((1, 2, t // rc, FOX_VROWS, rc), lambda bi, hp: (bi, hp, 0, 0, 0))],
        out_shape=[jax.ShapeDtypeStruct((b, FOX_HEADS, t, LANES), BF16),
                   jax.ShapeDtypeStruct((b, FOX_HEADS, t, LANES), BF16),
                   jax.ShapeDtypeStruct((b, FOX_HEADS, t // rc, FOX_VROWS, rc), BF16)],
        compiler_params=_cparams("parallel", "parallel"),
        name="fox_prep",
    )(q, k, v, ccol)


def _fox_t_kernel(qa_ref, ka_ref, vt_ref, o_ref, s_scr, acc_scr, *, tq, tk):
    i = pl.program_id(1)
    q0 = i * tq
    nfull = (q0 + 1) // tk
    nch = (q0 + tq + tk - 1) // tk
    qpos = q0 + lax.broadcasted_iota(jnp.int32, (1, tq), 1)
    kiota = lax.broadcasted_iota(jnp.int32, (tk, 1), 0)
    heads = range(FOX_HEADS)

    def scores(j, mx, masked):
        start = pl.multiple_of(j * tk, tk)
        out = []
        for h in heads:
            st = _dot_nt(ka_ref[0, h, pl.ds(start, tk), :], qa_ref[0, h])
            if masked:
                st = jnp.where(start + kiota <= qpos, st, MASKED)
            s_scr[h, j] = st
            out.append(jnp.maximum(mx[h], jnp.max(st, axis=0, keepdims=True)))
        return tuple(out)

    mx = tuple(jnp.full((1, tq), MASKED, F32) for _ in heads)
    mx = lax.fori_loop(0, nfull, functools.partial(scores, masked=False), mx)
    mx = lax.fori_loop(nfull, nch, functools.partial(scores, masked=True), mx)

    acc_scr[...] = jnp.zeros_like(acc_scr)

    def weigh(j, carry):
        for h in heads:
            p = jnp.exp(s_scr[h, j] - mx[h]).astype(BF16)
            acc_scr[h] += _dot(vt_ref[0, h, j], p)
        return carry

    lax.fori_loop(0, nch, weigh, 0)
    for hp in range(FOX_HEADS // 2):
        outs = []
        for h in (2 * hp, 2 * hp + 1):
            acc = acc_scr[h]
            outs.append(acc[0:FOX_HD, :] / acc[FOX_HD:FOX_HD + 1, :])
        o_ref[0, :, hp * LANES:(hp + 1) * LANES] = jnp.concatenate(outs, axis=0).T.astype(BF16)


def _fox_attention_t(qa, ka, vt, tq):
    b, _, t, _ = qa.shape
    nck, tk = vt.shape[2], vt.shape[4]
    return pl.pallas_call(
        functools.partial(_fox_t_kernel, tq=tq, tk=tk),
        grid=(b, t // tq),
        in_specs=[pl.BlockSpec((1, FOX_HEADS, tq, LANES), lambda bi, i: (bi, 0, i, 0)),
                  pl.BlockSpec((1, FOX_HEADS, t, LANES), lambda bi, i: (bi, 0, 0, 0)),
                  pl.BlockSpec((1, FOX_HEADS, nck, FOX_VROWS, tk), lambda bi, i: (bi, 0, 0, 0, 0))],
        out_specs=pl.BlockSpec((1, tq, FOX_W), lambda bi, i: (bi, i, 0)),
        out_shape=jax.ShapeDtypeStruct((b, t, FOX_W), BF16),
        scratch_shapes=[pltpu.VMEM((FOX_HEADS, nck, tk, tq), F32),
                        pltpu.VMEM((FOX_HEADS, FOX_VROWS, tq), F32)],
        compiler_params=_cparams("parallel", "arbitrary"),
        name="fox_attention_t",
    )(qa, ka, vt)


def _ret_kernel(lg_ref, rq_ref, rk_ref, rv_ref, rg_ref, s0_ref, gn_ref, o_ref, sfin_ref, s_scr, d_scr, *, c, nbs):
    first = (pl.program_id(0) == 0) & (pl.program_id(1) == 0)
    ci = pl.program_id(1)
    ii = lax.broadcasted_iota(jnp.int32, (c, 1), 0).astype(F32)
    lane = lax.broadcasted_iota(jnp.int32, (1, LANES), 1)

    @pl.when(first)
    def _():
        jj = lax.broadcasted_iota(jnp.int32, (1, c), 1).astype(F32)
        diff = ii - jj
        for h in range(RET_HEADS):
            lg = lg_ref[h][:, 0:1]
            d_scr[h] = jnp.where(diff >= 0.0, jnp.exp(lg * jnp.maximum(diff, 0.0)), 0.0)

    @pl.when(ci == 0)
    def _():
        zero = jnp.zeros((RET_DK, RET_DV), F32)
        for bb in range(nbs):
            for h in range(RET_HEADS):
                lo, hi = (s0_ref[bb, h], zero) if h % 2 == 0 else (zero, s0_ref[bb, h])
                s_scr[bb * RET_HEADS + h, 0:RET_DK, :] = lo
                s_scr[bb * RET_HEADS + h, RET_DK:2 * RET_DK, :] = hi

    for bb, h in [(bb, h) for bb in range(nbs) for h in range(RET_HEADS)]:
        lg = lg_ref[h][:, 0:1]
        st = bb * RET_HEADS + h
        pair = slice((h // 2) * LANES, (h // 2 + 1) * LANES)
        mine = slice(h * RET_DV, (h + 1) * RET_DV)
        inhead = (lane // RET_DK) == (h % 2)
        q = rq_ref[bb, :, pair]
        k = rk_ref[bb, :, pair]
        qm = jnp.where(inhead, q, jnp.zeros_like(q))
        km = jnp.where(inhead, k, jnp.zeros_like(k))
        v = rv_ref[bb, :, mine]
        scores = _dot_nt(qm, km) * d_scr[h]
        inner = _dot(scores.astype(BF16), v)
        s_prev = s_scr[st]
        cross = _dot(qm, s_prev.astype(BF16)) * jnp.exp(lg * (ii + 1.0))
        y = inner + cross
        k_dec = (km.astype(F32) * jnp.exp(lg * (c - 1.0 - ii))).astype(BF16)
        s_new = jnp.exp(lg * float(c)) * s_prev + _dot_tn(k_dec, v)
        s_scr[st] = s_new

        mu = jnp.mean(y, axis=-1, keepdims=True)
        yc = y - mu
        var = jnp.mean(yc * yc, axis=-1, keepdims=True)
        yn = yc * lax.rsqrt(var + EPS) * gn_ref[:, mine]
        g = rg_ref[bb, :, mine].astype(F32)
        o_ref[bb, :, mine] = ((g * _sigmoid(g)) * yn).astype(BF16)
        off = (h % 2) * RET_DK
        sfin_ref[bb, h] = s_new[off:off + RET_DK, :]


def _retention(lg_tab, rq, rk, rv, rg, s0, gn, c):
    b, t, _ = rq.shape
    nbs = 2 if b % 2 == 0 else 1
    qk_spec = pl.BlockSpec((nbs, c, RET_QK_W), lambda bi, ci: (bi, ci, 0))
    v_spec = pl.BlockSpec((nbs, c, RET_V_W), lambda bi, ci: (bi, ci, 0))
    st_spec = pl.BlockSpec((nbs, RET_HEADS, RET_DK, RET_DV), lambda bi, ci: (bi, 0, 0, 0))
    return pl.pallas_call(
        functools.partial(_ret_kernel, c=c, nbs=nbs),
        grid=(b // nbs, t // c),
        in_specs=[_full((RET_HEADS, 1, LANES)), qk_spec, qk_spec, v_spec, v_spec, st_spec,
                  _full((1, RET_V_W))],
        out_specs=[v_spec, st_spec],
        out_shape=[jax.ShapeDtypeStruct((b, t, RET_V_W), BF16),
                   jax.ShapeDtypeStruct((b, RET_HEADS, RET_DK, RET_DV), F32)],
        scratch_shapes=[pltpu.VMEM((nbs * RET_HEADS, 2 * RET_DK, RET_DV), F32), pltpu.VMEM((RET_HEADS, c, c), F32)],
        compiler_params=_cparams("arbitrary", "arbitrary"),
        name="retention",
    )(lg_tab, rq, rk, rv, rg, s0, gn)


def _out0_kernel(x_ref, of_ref, or_ref, w_ref, *rest):
    y_ref = rest[-4] if len(rest) > 1 else rest[0]
    mix = _dot(of_ref[...], w_ref[0:FOX_W, :]) + _dot(or_ref[...], w_ref[FOX_W:FOX_W + RET_V_W, :])
    y = x_ref[...] + mix
    y_ref[...] = y
    if len(rest) > 1:
        g_ref, wth_ref, wtl_ref, bt_ref, _, h_ref, route_ref, wcol_ref = rest
        _route_tokens(y, g_ref, wth_ref, wtl_ref, bt_ref, h_ref, route_ref, wcol_ref)


def _out0(x, o_fox, o_ret, w, tm, router=None):
    n = x.shape[0]
    row = lambda width: pl.BlockSpec((tm, width), lambda i: (i, 0))
    in_specs = [row(D_MODEL), row(FOX_W), row(RET_V_W), _full((FOX_W + RET_V_W, D_MODEL))]
    out_specs = [row(D_MODEL)]
    out_shape = [jax.ShapeDtypeStruct((n, D_MODEL), F32)]
    args = [x, o_fox, o_ret, w]
    if router is not None:
        in_specs += [_full((1, D_MODEL)), _full((ROUTE_ROWS, D_MODEL)), _full((ROUTE_ROWS, D_MODEL)),
                     _full((ROUTE_ROWS, 1))]
        out_specs += [pl.BlockSpec((tm * SLAB, LANES), lambda i: (i, 0)), pl.BlockSpec((8, tm), lambda i: (0, i)),
                      row(LANES)]
        out_shape += [jax.ShapeDtypeStruct((n * SLAB, LANES), F32), jax.ShapeDtypeStruct((8, n), F32),
                      jax.ShapeDtypeStruct((n, LANES), F32)]
        args += list(router)
    return pl.pallas_call(
        _out0_kernel,
        grid=(n // tm,),
        in_specs=in_specs,
        out_specs=out_specs,
        out_shape=out_shape,
        compiler_params=_cparams("parallel"),
        name="out_proj0",
    )(*args)


def _router_kernel(x_ref, g_ref, wh_ref, wl_ref, b_ref, h_ref, gates_ref):
    hf = _rms(x_ref[...], g_ref[...])
    hh = hf.astype(BF16)
    hl = (hf - hh.astype(F32)).astype(BF16)
    h_ref[...] = hh
    wh = wh_ref[...]
    logits = _dot(hh, wh) + (_dot(hl, wh) + _dot(hh, wl_ref[...])) + b_ref[...]

    lane = lax.broadcasted_iota(jnp.int32, (1, LANES), 1)
    lanef = lane.astype(F32)
    ninf = -jnp.inf
    is_grp = lane < N_GROUPS
    gl = jnp.where(is_grp, logits, ninf)
    gmax = jnp.max(gl, axis=-1, keepdims=True)
    grp = jnp.min(jnp.where(gl == gmax, lanef, 1e9), axis=-1, keepdims=True)
    p_grp = 1.0 / jnp.sum(jnp.exp(gl - gmax), axis=-1, keepdims=True)

    is_exp = (lane >= ROUTER_LANE0) & (lane < ROUTER_LANE0 + N_EXPERTS)
    lane_grp = ((lane - ROUTER_LANE0) // EXP_PER_GROUP).astype(F32)
    em = jnp.where(is_exp & (lane_grp == grp), logits, ninf)
    v1 = jnp.max(em, axis=-1, keepdims=True)
    i1 = jnp.min(jnp.where(em == v1, lanef, 1e9), axis=-1, keepdims=True)
    em2 = jnp.where(lanef == i1, ninf, em)
    v2 = jnp.max(em2, axis=-1, keepdims=True)
    i2 = jnp.min(jnp.where(em2 == v2, lanef, 1e9), axis=-1, keepdims=True)
    t = jnp.exp(v2 - v1)
    w1 = (1.0 / (1.0 + t)) * p_grp
    w2 = (t / (1.0 + t)) * p_grp
    gates_ref[...] = jnp.where(lanef == i1, w1, 0.0) + jnp.where(lanef == i2, w2, 0.0)


def _router(x, g, wh, wl, b, tm):
    n = x.shape[0]
    row = lambda width: pl.BlockSpec((tm, width), lambda i: (i, 0))
    return pl.pallas_call(
        _router_kernel,
        grid=(n // tm,),
        in_specs=[row(D_MODEL), _full((1, D_MODEL)), _full((D_MODEL, LANES)), _full((D_MODEL, LANES)),
                  _full((1, LANES))],
        out_specs=[row(D_MODEL), row(LANES)],
        out_shape=[jax.ShapeDtypeStruct((n, D_MODEL), BF16), jax.ShapeDtypeStruct((n, LANES), F32)],
        compiler_params=_cparams("parallel"),
        name="router",
    )(x, g, wh, wl, b)


def _moe_kernel(h_ref, gates_ref, x_ref, wg_ref, wu_ref, wd_ref, y_ref):
    e = pl.program_id(1)

    @pl.when(e == 0)
    def _():
        y_ref[...] = jnp.zeros_like(y_ref)

    h = h_ref[...]
    a = _dot(h, wg_ref[0].astype(BF16))
    u = _dot(h, wu_ref[0].astype(BF16))
    lane = lax.broadcasted_iota(jnp.int32, (1, LANES), 1)
    gate = jnp.sum(jnp.where(lane == e + ROUTER_LANE0, gates_ref[...], 0.0), axis=-1, keepdims=True)
    hid = (a * _sigmoid(a)) * u * gate
    y_ref[...] += _dot(hid.astype(BF16), wd_ref[0].astype(BF16))

    @pl.when(e == pl.num_programs(1) - 1)
    def _():
        y_ref[...] += x_ref[...]


def _moe(h, gates, x, wg, wu, wd, layer, tm):
    n = x.shape[0]
    first = layer * N_EXPERTS
    row = lambda width: pl.BlockSpec((tm, width), lambda i, e: (i, 0))
    return pl.pallas_call(
        _moe_kernel,
        grid=(n // tm, N_EXPERTS),
        in_specs=[row(D_MODEL), row(LANES), row(D_MODEL),
                  pl.BlockSpec((1, D_MODEL, D_EXPERT), lambda i, e: (first + e, 0, 0)),
                  pl.BlockSpec((1, D_MODEL, D_EXPERT), lambda i, e: (first + e, 0, 0)),
                  pl.BlockSpec((1, D_EXPERT, D_MODEL), lambda i, e: (first + e, 0, 0))],
        out_specs=row(D_MODEL),
        out_shape=jax.ShapeDtypeStruct((n, D_MODEL), F32),
        compiler_params=_cparams("parallel", "arbitrary"),
        name="moe_experts",
    )(h, gates, x, wg, wu, wd)


def _router_t_kernel(x_ref, g_ref, wth_ref, wtl_ref, bt_ref, h_ref, route_ref, wcol_ref):
    _route_tokens(x_ref[...], g_ref, wth_ref, wtl_ref, bt_ref, h_ref, route_ref, wcol_ref)


def _route_tokens(x, g_ref, wth_ref, wtl_ref, bt_ref, h_ref, route_ref, wcol_ref):
    hf = _rms(x, g_ref[...])
    _to_slabs(h_ref, hf)
    hh = hf.astype(BF16)
    hl = (hf - hh.astype(F32)).astype(BF16)
    wth = wth_ref[...]
    logits = _dot_nt(wth, hh) + (_dot_nt(wth, hl) + _dot_nt(wtl_ref[...], hh)) + bt_ref[...]

    row = lax.broadcasted_iota(jnp.int32, (ROUTE_ROWS, 1), 0)
    rowf = row.astype(F32)
    ninf = -jnp.inf
    is_grp = (row >= N_EXPERTS) & (row < N_EXPERTS + N_GROUPS)
    gl = jnp.where(is_grp, logits, ninf)
    gmax = jnp.max(gl, axis=0, keepdims=True)
    grp = jnp.min(jnp.where(gl == gmax, rowf, 1e9), axis=0, keepdims=True) - float(N_EXPERTS)
    p_grp = 1.0 / jnp.sum(jnp.exp(gl - gmax), axis=0, keepdims=True)

    row_grp = (row // EXP_PER_GROUP).astype(F32)
    em = jnp.where((row < N_EXPERTS) & (row_grp == grp), logits, ninf)
    v1 = jnp.max(em, axis=0, keepdims=True)
    i1 = jnp.min(jnp.where(em == v1, rowf, 1e9), axis=0, keepdims=True)
    em2 = jnp.where(rowf == i1, ninf, em)
    v2 = jnp.max(em2, axis=0, keepdims=True)
    i2 = jnp.min(jnp.where(em2 == v2, rowf, 1e9), axis=0, keepdims=True)
    t = jnp.exp(v2 - v1)
    w1 = (1.0 / (1.0 + t)) * p_grp
    w2 = (t / (1.0 + t)) * p_grp

    tm = logits.shape[1]
    r128 = lax.broadcasted_iota(jnp.int32, (LANES, 1), 0)
    rt = (jnp.where(r128 == 0, i1, 0.0) + jnp.where(r128 == 1, i2, 0.0)
          + jnp.where(r128 == 2, w1, 0.0) + jnp.where(r128 == 3, w2, 0.0))
    route_ref[...] = rt[0:8, :]
    wcol_ref[...] = rt.T


def _router_t(x, g, wth, wtl, bt, tm):
    n = x.shape[0]
    row = lambda width: pl.BlockSpec((tm, width), lambda i: (i, 0))
    return pl.pallas_call(
        _router_t_kernel,
        grid=(n // tm,),
        in_specs=[row(D_MODEL), _full((1, D_MODEL)), _full((ROUTE_ROWS, D_MODEL)),
                  _full((ROUTE_ROWS, D_MODEL)), _full((ROUTE_ROWS, 1))],
        out_specs=[pl.BlockSpec((tm * SLAB, LANES), lambda i: (i, 0)), pl.BlockSpec((8, tm), lambda i: (0, i)),
                   row(LANES)],
        out_shape=[jax.ShapeDtypeStruct((n * SLAB, LANES), F32), jax.ShapeDtypeStruct((8, n), F32),
                   jax.ShapeDtypeStruct((n, LANES), F32)],
        compiler_params=_cparams("parallel"),
        name="router_t",
    )(x, g, wth, wtl, bt)


def _plan_kernel(route_ref, pos_ref, te_ref, nt_ref, pad_ref, *, n, tb, tm):
    nb = n // tb
    erow = lax.broadcasted_iota(jnp.int32, (N_EXPERTS, 1), 0).astype(F32)

    def picks(blk):
        i1 = route_ref[0:1, blk * tb:(blk + 1) * tb]
        i2 = route_ref[1:2, blk * tb:(blk + 1) * tb]
        return erow == i1, erow == i2

    acc = jnp.zeros((N_EXPERTS, tb), F32)
    for blk in range(nb):
        e1, e2 = picks(blk)
        acc = acc + jnp.where(e1, 1.0, 0.0) + jnp.where(e2, 1.0, 0.0)
    counts = jnp.sum(acc, axis=1, keepdims=True).astype(jnp.int32)
    assert tm & (tm - 1) == 0
    ntile = jnp.right_shift(counts + (tm - 1), tm.bit_length() - 1).astype(F32)
    lane = lax.broadcasted_iota(jnp.int32, (1, LANES), 1).astype(F32)
    ntile_row = jnp.sum(jnp.where(lane == erow, ntile, 0.0), axis=0, keepdims=True)
    tend = jnp.sum(jnp.where(lane <= erow, ntile_row, 0.0), axis=1, keepdims=True)
    off = (tend - ntile) * float(tm)
    ntot = jnp.sum(ntile, axis=0, keepdims=True)
    kk = jnp.minimum(lax.broadcasted_iota(jnp.int32, (1, 2 * LANES), 1).astype(F32), ntot - 1.0)
    te_ref[...] = jnp.sum(jnp.where(tend <= kk, 1.0, 0.0), axis=0, keepdims=True).astype(jnp.int32)
    nt_ref[...] = jnp.broadcast_to(ntot, (1, LANES)).astype(jnp.int32)
    pad0 = off + counts.astype(F32)
    pad_ref[...] = jnp.sum(jnp.where(lane == erow, pad0, 0.0), axis=0, keepdims=True).astype(jnp.int32)

    r = lax.broadcasted_iota(jnp.int32, (tb, tb), 0)
    c = lax.broadcasted_iota(jnp.int32, (tb, tb), 1)
    before = jnp.where(r < c, 1.0, 0.0).astype(BF16)
    carry = jnp.zeros((N_EXPERTS, 1), F32)
    for blk in range(nb):
        e1, e2 = picks(blk)
        mt = jnp.where(e1, 1.0, 0.0) + jnp.where(e2, 1.0, 0.0)
        slot = _dot(mt.astype(BF16), before) + (carry + off)
        pos_ref[blk, 0:1, :] = jnp.sum(jnp.where(e1, slot, 0.0), axis=0, keepdims=True).astype(jnp.int32)
        pos_ref[blk, 1:2, :] = jnp.sum(jnp.where(e2, slot, 0.0), axis=0, keepdims=True).astype(jnp.int32)
        carry = carry + jnp.sum(mt, axis=1, keepdims=True)


def _plan(route, tb, tm):
    n = route.shape[1]
    vm = pl.BlockSpec(memory_space=pltpu.VMEM)
    return pl.pallas_call(
        functools.partial(_plan_kernel, n=n, tb=tb, tm=tm),
        in_specs=[vm],
        out_specs=[vm, vm, vm, vm],
        out_shape=[jax.ShapeDtypeStruct((n // tb, 2, tb), jnp.int32),
                   jax.ShapeDtypeStruct((1, 2 * LANES), jnp.int32),
                   jax.ShapeDtypeStruct((1, LANES), jnp.int32),
                   jax.ShapeDtypeStruct((1, LANES), jnp.int32)],
        compiler_params=pltpu.CompilerParams(vmem_limit_bytes=VMEM_LIMIT),
        name="moe_plan",
    )(route)


SLAB = D_MODEL // LANES


def _to_slabs(ref, y):
    m = y.shape[0]
    for s_ in range(SLAB):
        ref[pl.ds(s_, m, stride=SLAB), :] = y[:, s_ * LANES:(s_ + 1) * LANES]


def _slab_piece(ref, s_, m):
    return ref[pl.ds(s_, m, stride=SLAB), :]


def _from_slabs(ref, m):
    return jnp.concatenate([_slab_piece(ref, s_, m) for s_ in range(SLAB)], axis=1)


def _tok_copy(src, src_tok, dst, dst_tok, sem):
    return pltpu.make_async_copy(src.at[pl.ds(pl.multiple_of(src_tok * SLAB, SLAB), SLAB)],
                                 dst.at[pl.ds(pl.multiple_of(dst_tok * SLAB, SLAB), SLAB)], sem)


DMA_UNROLL = 8


def _dispatch_kernel(pos_ref, pad_ref, nt_ref, h_ref, xs_hbm, zero_scr, sem, zsem, *, ts, tm, ntmax):
    @pl.when(pl.program_id(0) == 0)
    def _():
        zero_scr[...] = jnp.zeros_like(zero_scr)

        def fill(first_slot):
            return pltpu.make_async_copy(
                zero_scr, xs_hbm.at[pl.ds(pl.multiple_of(first_slot * SLAB, SLAB), tm * SLAB)], zsem)

        for e in range(N_EXPERTS):
            fill(pad_ref[0, e]).start()
        for e in range(N_EXPERTS):
            fill(pad_ref[0, e]).wait()

        ntot = nt_ref[0, 0]

        def tail_start(k, carry):
            fill(k * tm).start()
            return carry

        def tail_wait(k, carry):
            fill(k * tm).wait()
            return carry

        lax.fori_loop(ntot, ntmax, tail_start, 0)
        lax.fori_loop(ntot, ntmax, tail_wait, 0)

    def issue(g, carry):
        for u in range(DMA_UNROLL):
            r = g * DMA_UNROLL + u
            for ch in range(2):
                _tok_copy(h_ref, r, xs_hbm, pos_ref[0, ch, r], sem).start(priority=ch)
        return carry

    def drain(g, carry):
        for _ in range(2 * DMA_UNROLL):
            _tok_copy(h_ref, 0, xs_hbm, 0, sem).wait()
        return carry

    lax.fori_loop(0, ts // DMA_UNROLL, issue, 0)
    lax.fori_loop(0, ts // DMA_UNROLL, drain, 0)


def _dispatch(pos, pad, nt, h, ts, tm):
    nb = pos.shape[0]
    n = h.shape[0] // SLAB
    ntmax = _sorted_rows(n) // tm + 1
    smem = pl.BlockSpec(memory_space=pltpu.SMEM)
    return pl.pallas_call(
        functools.partial(_dispatch_kernel, ts=ts, tm=tm, ntmax=ntmax),
        grid=(nb,),
        in_specs=[pl.BlockSpec((1, 2, ts), lambda i: (i, 0, 0), memory_space=pltpu.SMEM), smem, smem,
                  pl.BlockSpec((ts * SLAB, LANES), lambda i: (i, 0))],
        out_specs=pl.BlockSpec(memory_space=pl.ANY),
        out_shape=jax.ShapeDtypeStruct((ntmax * tm * SLAB, LANES), F32),
        scratch_shapes=[pltpu.VMEM((tm * SLAB, LANES), F32), pltpu.SemaphoreType.DMA,
                        pltpu.SemaphoreType.DMA],
        compiler_params=_cparams("arbitrary"),
        name="moe_dispatch",
    )(pos, pad, nt, h)


def _experts_kernel(te_ref, nt_ref, xs_ref, wg_ref, wu_ref, wd_ref, ys_ref, *, tm):
    @pl.when(pl.program_id(0) < nt_ref[0])
    def _():
        x = _from_slabs(xs_ref, tm).astype(BF16)
        a = _dot(x, wg_ref[0].astype(BF16))
        u = _dot(x, wu_ref[0].astype(BF16))
        hid = (a * _sigmoid(a)) * u
        _to_slabs(ys_ref, _dot(hid.astype(BF16), wd_ref[0].astype(BF16)))


def _experts(te, nt, xs, wg, wu, wd, layer, tm):
    ntmax = xs.shape[0] // (tm * SLAB)
    first = layer * N_EXPERTS
    tile = pl.BlockSpec((tm * SLAB, LANES), lambda k, te, nt: (jnp.minimum(k, nt[0] - 1), 0))
    out_tile = tile
    return pl.pallas_call(
        functools.partial(_experts_kernel, tm=tm),
        grid_spec=pltpu.PrefetchScalarGridSpec(
            num_scalar_prefetch=2,
            grid=(ntmax,),
            in_specs=[tile,
                      pl.BlockSpec((1, D_MODEL, D_EXPERT), lambda k, te, nt: (first + te[k], 0, 0)),
                      pl.BlockSpec((1, D_MODEL, D_EXPERT), lambda k, te, nt: (first + te[k], 0, 0)),
                      pl.BlockSpec((1, D_EXPERT, D_MODEL), lambda k, te, nt: (first + te[k], 0, 0))],
            out_specs=out_tile),
        out_shape=jax.ShapeDtypeStruct(xs.shape, F32),
        input_output_aliases={2: 0},
        compiler_params=_cparams("arbitrary"),
        name="moe_experts_sorted",
    )(te, nt, xs, wg, wu, wd)


def _combine_kernel(pos_ref, posn_ref, x_ref, w_ref, ys_hbm, y_ref, buf, sems, *, tc):
    i = pl.program_id(0)
    nb = pl.num_programs(0)

    def start(p_ref, slot):
        def body(g, carry):
            for u in range(DMA_UNROLL):
                r = g * DMA_UNROLL + u
                for ch in range(2):
                    _tok_copy(ys_hbm, p_ref[0, ch, r], buf.at[slot, ch], r, sems.at[slot]).start(priority=ch)
            return carry
        lax.fori_loop(0, tc // DMA_UNROLL, body, 0)

    def wait(slot):
        def body(g, carry):
            for _ in range(2 * DMA_UNROLL):
                _tok_copy(ys_hbm, 0, buf.at[slot, 0], 0, sems.at[slot]).wait()
            return carry
        lax.fori_loop(0, tc // DMA_UNROLL, body, 0)

    @pl.when(i == 0)
    def _():
        start(pos_ref, 0)

    w = w_ref[...]
    w1, w2 = w[:, 2:3], w[:, 3:4]
    for slot in range(2):
        @pl.when(i % 2 == slot)
        def _(slot=slot):
            @pl.when(i + 1 < nb)
            def _():
                start(posn_ref, 1 - slot)

            wait(slot)
            for s_ in range(SLAB):
                cols = slice(s_ * LANES, (s_ + 1) * LANES)
                y_ref[:, cols] = x_ref[:, cols] + (w1 * _slab_piece(buf.at[slot, 0], s_, tc)
                                                   + w2 * _slab_piece(buf.at[slot, 1], s_, tc))


def _combine(pos, x, wcol, ys, tc):
    n = x.shape[0]
    nb = n // tc
    row = lambda width: pl.BlockSpec((tc, width), lambda i: (i, 0))
    return pl.pallas_call(
        functools.partial(_combine_kernel, tc=tc),
        grid=(nb,),
        in_specs=[pl.BlockSpec((1, 2, tc), lambda i: (i, 0, 0), memory_space=pltpu.SMEM),
                  pl.BlockSpec((1, 2, tc), lambda i: (jnp.minimum(i + 1, nb - 1), 0, 0), memory_space=pltpu.SMEM),
                  row(D_MODEL), row(LANES), pl.BlockSpec(memory_space=pl.ANY)],
        out_specs=row(D_MODEL),
        out_shape=jax.ShapeDtypeStruct((n, D_MODEL), F32),
        scratch_shapes=[pltpu.VMEM((2, 2, tc * SLAB, LANES), F32), pltpu.SemaphoreType.DMA((2,))],
        compiler_params=_cparams("arbitrary"),
        name="moe_combine",
    )(pos, pos, x, wcol, ys)


def _pw1_kernel(x_ref, g_ref, w_ref, b_ref, u_ref):
    h = _rms(x_ref[...], g_ref[...]).astype(BF16)
    a = _dot(h, w_ref[:, 0:D_MODEL]) + b_ref[:, 0:D_MODEL]
    g = _dot(h, w_ref[:, D_MODEL:2 * D_MODEL]) + b_ref[:, D_MODEL:2 * D_MODEL]
    u_ref[...] = a * _sigmoid(g)


def _pw1(x, g, w, b, tm):
    n = x.shape[0]
    row = lambda width: pl.BlockSpec((tm, width), lambda i: (i, 0))
    return pl.pallas_call(
        _pw1_kernel,
        grid=(n // tm,),
        in_specs=[row(D_MODEL), _full((1, D_MODEL)), _full((D_MODEL, 2 * D_MODEL)), _full((1, 2 * D_MODEL))],
        out_specs=row(D_MODEL),
        out_shape=jax.ShapeDtypeStruct((n, D_MODEL), F32),
        compiler_params=_cparams("parallel"),
        name="conv_pw1_glu",
    )(x, g, w, b)


def _conv_kernel(u_ref, hist_ref, x_ref, wdw_ref, bdw_ref, lng_ref, lnb_ref, w2_ref, y_ref, ext_scr, sh_scr,
                 *, tt, rs):
    ti = pl.program_id(1)

    @pl.when(ti == 0)
    def _():
        ext_scr[0:HIST_ROWS, :] = hist_ref[0]

    ext_scr[HIST_ROWS:HIST_ROWS + tt, :] = u_ref[0]
    pad = HIST_ROWS - (CONV_W - 1)
    rows = tt + HIST_ROWS - 8
    for r in range(1, 8):
        sh_scr[r - 1, :, :] = ext_scr[r:r + rows, :]
    parts = []
    for r0 in range(0, tt, rs):
        acc = jnp.zeros((rs, D_MODEL), F32)
        for kk in range(CONV_W):
            a, r = divmod(kk + pad, 8)
            lo = r0 + 8 * a
            src = ext_scr[lo:lo + rs, :] if r == 0 else sh_scr[r - 1, lo:lo + rs, :]
            acc = acc + wdw_ref[kk:kk + 1, :] * src
        parts.append(acc)
    y = jnp.concatenate(parts, axis=0) + bdw_ref[...]
    mu = jnp.mean(y, axis=-1, keepdims=True)
    yc = y - mu
    var = jnp.mean(yc * yc, axis=-1, keepdims=True)
    yn = yc * lax.rsqrt(var + EPS) * lng_ref[...] + lnb_ref[...]
    z = yn * _sigmoid(yn)
    y_ref[0] = x_ref[0] + _dot(z.astype(BF16), w2_ref[...])
    if tt >= HIST_ROWS:
        ext_scr[0:HIST_ROWS, :] = ext_scr[tt:tt + HIST_ROWS, :]


def _conv_module(u, hist, x, wdw, bdw, lng, lnb, w2, tt):
    b, t, _ = u.shape
    rs = min(tt, 32)
    blk = pl.BlockSpec((1, tt, D_MODEL), lambda bi, ti: (bi, ti, 0))
    return pl.pallas_call(
        functools.partial(_conv_kernel, tt=tt, rs=rs),
        grid=(b, t // tt),
        in_specs=[blk, pl.BlockSpec((1, HIST_ROWS, D_MODEL), lambda bi, ti: (bi, 0, 0)), blk,
                  _full((HIST_ROWS, D_MODEL)), _full((1, D_MODEL)), _full((1, D_MODEL)),
                  _full((1, D_MODEL)), _full((D_MODEL, D_MODEL))],
        out_specs=blk,
        out_shape=jax.ShapeDtypeStruct((b, t, D_MODEL), F32),
        scratch_shapes=[pltpu.VMEM((HIST_ROWS + tt, D_MODEL), F32),
                        pltpu.VMEM((7, tt + HIST_ROWS - 8, D_MODEL), F32)],
        compiler_params=_cparams("parallel", "arbitrary"),
        name="conv_module",
    )(u, hist, x, wdw, bdw, lng, lnb, w2)


def _rope_tables(pos):
    half = RET_DK // 2
    inv_freq = ROPE_BASE ** (-jnp.arange(half, dtype=F32) / half)
    ang = pos.astype(F32)[:, None] * inv_freq[None, :]
    cos, sin = jnp.cos(ang), jnp.sin(ang)
    reps = LANES // RET_DK
    cos_t = jnp.tile(jnp.concatenate([cos, cos], axis=1), (1, reps))
    sin_t = jnp.tile(jnp.concatenate([-sin, sin], axis=1), (1, reps))
    return cos_t, sin_t


def _pad_lanes(v, width=LANES):
    v = v.reshape(1, -1)
    return jnp.pad(v, ((0, 0), (0, width - v.shape[1])))


def _moe_dense(x, p, l, tm):
    h, gates = _router(x, p["norm_ffn"][l], p["wr_hi"][l], p["wr_lo"][l], p["br"][l], tm)
    return _moe(h, gates, x, p["wg"], p["wu"], p["wd"], l, tm)


def _sorted_rows(n_tokens):
    return (2 * n_tokens // MOE_TM + N_EXPERTS) * MOE_TM


def _router_args(p, l):
    return p["norm_ffn"][l], p["wt_hi"][l], p["wt_lo"][l], p["bt"][l]


def _moe_sparse(x, p, l, tm, routed=None):
    h, route, wcol = routed if routed is not None else _router_t(x, *_router_args(p, l), tm)
    pos, te, nt, pad = _plan(route, PLAN_TB, MOE_TM)
    xs = _dispatch(pos, pad, nt, h, PLAN_TB, MOE_TM)
    ys = _experts(te.reshape(-1), nt[0, :1], xs, p["wg"], p["wu"], p["wd"], l, MOE_TM)
    return _combine(pos, x, wcol, ys, PLAN_TB)


def _run_group(x, p, *, sparse, seq, tm, fox_hist, ret_state, conv_hist, pos, tq, tk, ret_c, conv_tt):
    b = x.shape[0] // seq
    q_off = 0 if fox_hist is None else fox_hist[0].shape[1]

    cos_t, sin_t = _rope_tables(pos)
    q, k, v, lf, rq, rk, rv, rg, k5, v5 = _proj0(x, p["norm_mix"][0], p["w_in"], p["b_f"], p["q_gain"], p["k_gain"],
                                          p["gbd"], cos_t, sin_t, tm)
    k3 = k.reshape(b, seq, FOX_W)
    v3 = v.reshape(b, seq, FOX_W)
    lf3 = lf.reshape(b, seq, LANES)
    if fox_hist is None:
        k_all, v_all, lf_all = k3, v3, lf3
    else:
        ck_, cv_, clf_ = fox_hist
        tot = q_off + seq
        padded = -(-tot // tk) * tk
        tail = padded - tot
        k_all = jnp.concatenate([ck_, k3, jnp.zeros((b, tail, FOX_W), F32)], axis=1)
        v_all = jnp.concatenate([cv_, v3, jnp.zeros((b, tail, FOX_W), F32)], axis=1)
        clf_ = jnp.pad(clf_, ((0, 0), (0, 0), (0, LANES - FOX_HEADS)))
        lf_all = jnp.concatenate([clf_, lf3, jnp.zeros((b, tail, LANES), F32)], axis=1)
    ccol, crow = _cumsum_logf(lf_all, CUM_BLOCK)
    if fox_hist is None:
        qa, ka, vt = _fox_prep(q.reshape(b, seq, FOX_W), k_all, v_all, ccol, tk)
        o_fox = _fox_attention_t(qa, ka, vt, tq)
    else:
        o_fox = _fox_attention(q.reshape(b, seq, FOX_W), k_all, v_all, ccol, crow, tq, tk, q_off)
    o_ret, s_fin = _retention(p["lg_tab"], rq.reshape(b, seq, RET_QK_W), rk.reshape(b, seq, RET_QK_W),
                              rv.reshape(b, seq, RET_V_W), rg.reshape(b, seq, RET_V_W), ret_state,
                              p["gn_gain"], ret_c)
    o_fox2, o_ret2 = o_fox.reshape(-1, FOX_W), o_ret.reshape(-1, RET_V_W)
    if sparse:
        x, *routed = _out0(x, o_fox2, o_ret2, p["w_out"], tm, _router_args(p, 0))
        x = _moe_sparse(x, p, 0, tm, routed)
    else:
        x = _moe_dense(_out0(x, o_fox2, o_ret2, p["w_out"], tm)[0], p, 0, tm)

    u = _pw1(x, p["norm_mix"][1], p["w_pw1"], p["b_pw1"], tm)
    u3 = u.reshape(b, seq, D_MODEL)
    x = _conv_module(u3, conv_hist, x.reshape(b, seq, D_MODEL), p["w_dw"], p["b_dw"], p["ln_g"], p["ln_b"],
                     p["w_pw2"], conv_tt).reshape(-1, D_MODEL)
    x = _moe_sparse(x, p, 1, tm) if sparse else _moe_dense(x, p, 1, tm)

    fox_k = k5.reshape(1, b, seq, FOX_HEADS, FOX_HD)
    fox_v = v5.reshape(1, b, seq, FOX_HEADS, FOX_HD)
    fox_lf = lf3[:, :, :FOX_HEADS].reshape(1, b, seq, FOX_HEADS)
    return x.reshape(b, seq, D_MODEL), fox_k, fox_v, fox_lf, s_fin[None], u3


def kernel(x_prompt, x_sample, cache_fox_k, cache_fox_v, cache_fox_logf, state_ret, cache_conv, norm_mix, norm_ffn, w_in_mix, b_forget, fox_q_gain, fox_k_gain, ret_gn_gain, w_out_mix, w_pw1, b_pw1, w_dw, b_dw, conv_ln_g, conv_ln_b, w_pw2, w_router_group, b_router_group, w_router_expert, b_router_expert, w_exp_gate, w_exp_up, w_exp_down):
    bp, t, d = x_prompt.shape
    bs, l, _ = x_sample.shape
    past = cache_fox_k.shape[2]
    depth = norm_mix.shape[0]
    assert d == D_MODEL and depth == 2 and w_in_mix.shape[0] == 1 and w_pw1.shape[0] == 1

    w_in = w_in_mix[0]
    n_pre = 3 * FOX_W
    w_in_r = jnp.concatenate(
        [w_in[:, :n_pre], w_in[:, n_pre + FOX_HEADS:], w_in[:, n_pre:n_pre + FOX_HEADS],
         jnp.zeros((D_MODEL, LANES - FOX_HEADS), F32)], axis=1).astype(BF16)
    hid = jnp.arange(FOX_W) // FOX_HD
    gbd = jnp.where(hid[:, None] == hid[None, :], 1.0 / FOX_HD, 0.0).astype(BF16)
    w_r = jnp.concatenate([w_router_group, w_router_expert], axis=-1)
    w_r = jnp.pad(w_r, ((0, 0), (0, 0), (0, LANES - w_r.shape[-1])))
    wr_hi = w_r.astype(BF16)
    wr_lo = (w_r - wr_hi.astype(F32)).astype(BF16)
    b_r = jnp.concatenate([b_router_group, b_router_expert], axis=-1)
    b_r = jnp.pad(b_r, ((0, 0), (0, LANES - b_r.shape[-1])))
    w_t = jnp.swapaxes(jnp.concatenate([w_router_expert, w_router_group], axis=-1), 1, 2)
    w_t = jnp.pad(w_t, ((0, 0), (0, ROUTE_ROWS - w_t.shape[1]), (0, 0)))
    wt_hi = w_t.astype(BF16)
    wt_lo = (w_t - wt_hi.astype(F32)).astype(BF16)
    b_t = jnp.concatenate([b_router_expert, b_router_group], axis=-1)
    b_t = jnp.pad(b_t, ((0, 0), (0, ROUTE_ROWS - b_t.shape[-1])))[:, :, None]
    log_gamma = jnp.log(1.0 - 2.0 ** (-5.0 - jnp.arange(RET_HEADS, dtype=F32)))
    p = {
        "norm_mix": norm_mix.reshape(depth, 1, D_MODEL),
        "norm_ffn": norm_ffn.reshape(depth, 1, D_MODEL),
        "w_in": w_in_r,
        "b_f": _pad_lanes(b_forget[0]),
        "q_gain": jnp.tile(fox_q_gain[0], FOX_HEADS).reshape(1, FOX_W),
        "k_gain": jnp.tile(fox_k_gain[0], FOX_HEADS).reshape(1, FOX_W),
        "gbd": gbd,
        "gn_gain": ret_gn_gain[0].reshape(1, RET_V_W),
        "lg_tab": jnp.broadcast_to(log_gamma[:, None, None], (RET_HEADS, 1, LANES)),
        "w_out": w_out_mix[0].astype(BF16),
        "w_pw1": w_pw1[0].astype(BF16),
        "b_pw1": b_pw1[0].reshape(1, -1),
        "w_dw": jnp.pad(w_dw[0], ((0, HIST_ROWS - CONV_W), (0, 0))),
        "b_dw": b_dw[0].reshape(1, -1),
        "ln_g": conv_ln_g[0].reshape(1, -1),
        "ln_b": conv_ln_b[0].reshape(1, -1),
        "w_pw2": w_pw2[0].astype(BF16),
        "wr_hi": wr_hi,
        "wr_lo": wr_lo,
        "br": b_r.reshape(depth, 1, LANES),
        "wt_hi": wt_hi,
        "wt_lo": wt_lo,
        "bt": b_t,
        "wg": w_exp_gate.reshape(depth * N_EXPERTS, D_MODEL, D_EXPERT),
        "wu": w_exp_up.reshape(depth * N_EXPERTS, D_MODEL, D_EXPERT),
        "wd": w_exp_down.reshape(depth * N_EXPERTS, D_EXPERT, D_MODEL),
    }

    hist_pad = HIST_ROWS - (CONV_W - 1)
    yp, fk_p, fv_p, lf_p, rs_p, u_p = _run_group(
        x_prompt.reshape(bp * t, D_MODEL), p, sparse=True, seq=t, tm=512, fox_hist=None,
        ret_state=jnp.zeros((bp, RET_HEADS, RET_DK, RET_DV), F32),
        conv_hist=jnp.zeros((bp, HIST_ROWS, D_MODEL), F32),
        pos=jnp.arange(t), tq=256, tk=256, ret_c=256, conv_tt=256)

    ns = bs * l
    ys, fk_s, fv_s, lf_s, rs_s, u_s = _run_group(
        x_sample.reshape(ns, D_MODEL), p, sparse=False, seq=l, tm=ns,
        fox_hist=(cache_fox_k[0].reshape(bs, past, FOX_W), cache_fox_v[0].reshape(bs, past, FOX_W),
                  cache_fox_logf[0]),
        ret_state=state_ret[0],
        conv_hist=jnp.pad(cache_conv[0], ((0, 0), (hist_pad, 0), (0, 0))),
        pos=past + (jnp.arange(ns) % l), tq=l, tk=-(-(past + l) // CUM_BLOCK) * CUM_BLOCK, ret_c=l, conv_tt=l)

    conv_p = u_p[:, t - (CONV_W - 1):][None]
    conv_s = jnp.concatenate([cache_conv[0], u_s], axis=1)[:, l:][None]
    return (yp, ys, fk_p, fv_p, lf_p, rs_p, conv_p, fk_s, fv_s, lf_s, rs_s, conv_s)
```
